```python
import math
import jax, jax.numpy as jnp
from jax import lax
import numpy as np

D_MODEL = 1024
BATCH = 16
SEQ = 2048
DEPTH = 4

CTX_LEN = 256
GRID_W = 64
EPS = 1e-6
NEG_INF = -1e30
D_FF = 4 * D_MODEL

GLA_WIDTH = 3 * D_MODEL // 8
ATT_WIDTH = 3 * D_MODEL // 8
S5_WIDTH = D_MODEL - GLA_WIDTH - ATT_WIDTH

GLA_HEADS = 4
GLA_DV = GLA_WIDTH // GLA_HEADS
GLA_DK = GLA_DV // 2
GLA_RANK = 16
GLA_TAU = 16.0
GLA_CHUNK = 64

ATT_HD = 64
ATT_HEADS = ATT_WIDTH // ATT_HD
ATT_KV_HEADS = 2
ATT_GROUP = ATT_HEADS // ATT_KV_HEADS
WINDOW = 128
ATT_BLOCK = 128
ROPE_BASE = 10000.0

S5_GROUP = 16
S5_GROUPS = S5_WIDTH // S5_GROUP
S5_STATE = 64

IN_SPLITS = (GLA_HEADS * GLA_DK, GLA_HEADS * GLA_DK, GLA_WIDTH, GLA_WIDTH, GLA_RANK, GLA_RANK,
             ATT_WIDTH, ATT_KV_HEADS * ATT_HD, ATT_KV_HEADS * ATT_HD, S5_WIDTH)
IN_WIDTH = sum(IN_SPLITS)

kernel_name = 'hybrid_gla_swa_s5_prefix_dit'


def rmsnorm(x, w):
    xf = x.astype(jnp.float32)
    y = xf * lax.rsqrt(jnp.mean(xf * xf, axis=-1, keepdims=True) + EPS)
    return (y * w.astype(jnp.float32)).astype(x.dtype)


def split_columns(p):
    offsets = [int(o) for o in np.cumsum(IN_SPLITS)[:-1]]
    return jnp.split(p, offsets, axis=-1)


def to_heads(t, n_heads):
    b, n, _ = t.shape
    return t.reshape(b, n, n_heads, -1).transpose(0, 2, 1, 3)


def axial_rope_tables(rows):
    row = jnp.repeat(jnp.arange(rows, dtype=jnp.float32), GRID_W)
    col = jnp.tile(jnp.arange(GRID_W, dtype=jnp.float32), rows)
    n_freq = ATT_HD // 4
    inv_freq = ROPE_BASE ** (-jnp.arange(n_freq, dtype=jnp.float32) / n_freq)
    ang_r = row[:, None] * inv_freq
    ang_c = col[:, None] * inv_freq
    ang = jnp.concatenate([ang_r, ang_r, ang_c, ang_c], axis=-1)
    return jnp.cos(ang), jnp.sin(ang)


def apply_axial_rope(t, cos, sin):
    t1, t2, t3, t4 = jnp.split(t, 4, axis=-1)
    rot = jnp.concatenate([-t2, t1, -t4, t3], axis=-1)
    return t * cos.astype(t.dtype) + rot * sin.astype(t.dtype)


def gla_scan(q, k, v, log_a, h0, with_output):
    bsz, nh, t_len, dk = k.shape
    dv = v.shape[-1]
    nc = t_len // GLA_CHUNK
    chunked = lambda t: t.reshape(bsz, nh, nc, GLA_CHUNK, t.shape[-1])
    k, v, log_a = chunked(k), chunked(v), chunked(log_a)
    b = jnp.cumsum(log_a, axis=3)
    b_last = b[:, :, :, -1:, :]
    d_state = jnp.einsum('bhncd,bhnce->nbhde', k * jnp.exp(b_last - b), v)
    chunk_decay = jnp.exp(b_last[:, :, :, 0, :]).transpose(2, 0, 1, 3)

    def step(state, inp):
        decay, ds = inp
        return decay[..., None] * state + ds, state

    h_final, s_in = lax.scan(step, h0, (chunk_decay, d_state))
    if not with_output:
        return None, h_final
    qb = chunked(q) * jnp.exp(b)
    kb = k * jnp.exp(-b)
    o_inter = jnp.einsum('bhncd,nbhde->bhnce', qb, s_in)
    scores = jnp.einsum('bhnid,bhnjd->bhnij', qb, kb)
    lower = jnp.tril(jnp.ones((GLA_CHUNK, GLA_CHUNK), dtype=bool))
    o_intra = jnp.einsum('bhnij,bhnje->bhnie', jnp.where(lower, scores, 0.0), v)
    return (o_inter + o_intra).reshape(bsz, nh, t_len, dv), h_final


def gla_prepare(q, k, v, zf, zb, wa_f, ba_f, wa_b, ba_b):
    f = lambda t: t.astype(jnp.float32)
    q = to_heads(f(q), GLA_HEADS) * (GLA_DK ** -0.5)
    k = to_heads(f(k), GLA_HEADS)
    v = to_heads(f(v), GLA_HEADS)
    la_f = to_heads(jax.nn.log_sigmoid(f(zf) @ f(wa_f) + f(ba_f)) / GLA_TAU, GLA_HEADS)
    la_b = to_heads(jax.nn.log_sigmoid(f(zb) @ f(wa_b) + f(ba_b)) / GLA_TAU, GLA_HEADS)
    return q, k, v, la_f, la_b


def gla_readout(o, g, norm_w):
    bsz, nh, t_len, dv = o.shape
    o = rmsnorm(o.transpose(0, 2, 1, 3), norm_w).reshape(bsz, t_len, nh * dv)
    return (o * jax.nn.silu(g.astype(jnp.float32))).astype(g.dtype)


def gla_mixer(lat, cpx, wa_f, ba_f, wa_b, ba_b, norm_w, ctx_out):
    ql, kl, vl, laf_l, lab_l = gla_prepare(lat[0], lat[1], lat[2], lat[4], lat[5], wa_f, ba_f, wa_b, ba_b)
    qc, kc, vc, laf_c, lab_c = gla_prepare(cpx[0], cpx[1], cpx[2], cpx[4], cpx[5], wa_f, ba_f, wa_b, ba_b)
    flip = lambda t: jnp.flip(t, axis=2)
    h0 = jnp.zeros((kl.shape[0], GLA_HEADS, GLA_DK, GLA_DV), jnp.float32)
    oc_f, hc_f = gla_scan(qc, kc, vc, laf_c, h0, ctx_out)
    oc_b, hc_b = gla_scan(flip(qc), flip(kc), flip(vc), flip(lab_c), h0, ctx_out)
    ol_f, _ = gla_scan(ql, kl, vl, laf_l, hc_f, True)
    ol_b, _ = gla_scan(flip(ql), flip(kl), flip(vl), flip(lab_l), hc_b, True)
    out_lat = gla_readout(ol_f + flip(ol_b), lat[3], norm_w)
    out_ctx = gla_readout(oc_f + flip(oc_b), cpx[3], norm_w) if ctx_out else None
    return out_lat, out_ctx


def swa_latent(q, k, v, kc, vc, sink):
    bsz, t_len = q.shape[:2]
    nb = t_len // ATT_BLOCK
    qb = q.reshape(bsz, nb, ATT_BLOCK, ATT_KV_HEADS, ATT_GROUP, ATT_HD)

    def band(t):
        tp = jnp.pad(t, ((0, 0), (ATT_BLOCK, ATT_BLOCK), (0, 0), (0, 0)))
        tp = tp.reshape(bsz, nb + 2, ATT_BLOCK, ATT_KV_HEADS, ATT_HD)
        return jnp.concatenate([tp[:, :-2], tp[:, 1:-1], tp[:, 2:]], axis=2)

    kw, vw = band(k), band(v)
    scale = ATT_HD ** -0.5
    s_win = jnp.einsum('bnqkgd,bnjkd->bkgnqj', qb, kw).astype(jnp.float32) * scale
    qpos = jnp.arange(nb)[:, None] * ATT_BLOCK + jnp.arange(ATT_BLOCK)[None, :]
    kpos = (jnp.arange(nb)[:, None] - 1) * ATT_BLOCK + jnp.arange(3 * ATT_BLOCK)[None, :]
    valid = ((jnp.abs(qpos[:, :, None] - kpos[:, None, :]) <= WINDOW)
             & (kpos[:, None, :] >= 0) & (kpos[:, None, :] < t_len))
    s_win = jnp.where(valid, s_win, NEG_INF)
    s_ctx = jnp.einsum('bnqkgd,bjkd->bkgnqj', qb, kc).astype(jnp.float32) * scale
    s_sink = jnp.broadcast_to(sink[None, :, :, None, None, None], s_ctx.shape[:-1] + (1,))
    p = jax.nn.softmax(jnp.concatenate([s_win, s_ctx, s_sink], axis=-1), axis=-1).astype(v.dtype)
    n_win, n_ctx = kw.shape[2], kc.shape[1]
    o = (jnp.einsum('bkgnqj,bnjkd->bnqkgd', p[..., :n_win], vw)
         + jnp.einsum('bkgnqj,bjkd->bnqkgd', p[..., n_win:n_win + n_ctx], vc))
    return o.reshape(bsz, t_len, ATT_WIDTH)


def attend_context_only(qc, kc, vc, sink):
    bsz, n_ctx = qc.shape[:2]
    s = jnp.einsum('bqkgd,bjkd->bkgqj', qc, kc).astype(jnp.float32) * (ATT_HD ** -0.5)
    s_sink = jnp.broadcast_to(sink[None, :, :, None, None], s.shape[:-1] + (1,))
    p = jax.nn.softmax(jnp.concatenate([s, s_sink], axis=-1), axis=-1).astype(vc.dtype)
    o = jnp.einsum('bkgqj,bjkd->bqkgd', p[..., :n_ctx], vc)
    return o.reshape(bsz, n_ctx, ATT_WIDTH)


def swa_mixer(ql, kl, vl, qc, kc, vc, sink, cos, sin, ctx_out):
    bsz, t_len, _ = ql.shape
    n_ctx = kc.shape[1]
    ql = ql.reshape(bsz, t_len, ATT_KV_HEADS, ATT_GROUP, ATT_HD)
    kl = kl.reshape(bsz, t_len, ATT_KV_HEADS, ATT_HD)
    vl = vl.reshape(bsz, t_len, ATT_KV_HEADS, ATT_HD)
    kc = kc.reshape(bsz, n_ctx, ATT_KV_HEADS, ATT_HD)
    vc = vc.reshape(bsz, n_ctx, ATT_KV_HEADS, ATT_HD)
    ql = apply_axial_rope(ql, cos[:, None, None, :], sin[:, None, None, :])
    kl = apply_axial_rope(kl, cos[:, None, :], sin[:, None, :])
    sink = sink.astype(jnp.float32).reshape(ATT_KV_HEADS, ATT_GROUP)
    out_lat = swa_latent(ql, kl, vl, kc, vc, sink)
    out_ctx = None
    if ctx_out:
        qc = qc.reshape(bsz, n_ctx, ATT_KV_HEADS, ATT_GROUP, ATT_HD)
        out_ctx = attend_context_only(qc, kc, vc, sink)
    return out_lat, out_ctx


def s5_discretize(lam_re, lam_im, log_step, b_re, b_im):
    f = lambda t: t.astype(jnp.float32)
    lr = jnp.minimum(f(lam_re), -1e-4)
    li = f(lam_im)
    dt = jnp.exp(f(log_step))[:, None]
    mag = jnp.exp(lr * dt)
    ar, ai = mag * jnp.cos(li * dt), mag * jnp.sin(li * dt)
    den = lr * lr + li * li
    fr = ((ar - 1.0) * lr + ai * li) / den
    fi = (ai * lr - (ar - 1.0) * li) / den
    bbr = fr[..., None] * f(b_re) - fi[..., None] * f(b_im)
    bbi = fr[..., None] * f(b_im) + fi[..., None] * f(b_re)
    return ar, ai, bbr, bbi


def complex_linear_combine(e1, e2):
    a1r, a1i, b1r, b1i = e1
    a2r, a2i, b2r, b2i = e2
    return (a1r * a2r - a1i * a2i, a1r * a2i + a1i * a2r,
            a2r * b1r - a2i * b1i + b2r, a2r * b1i + a2i * b1r + b2i)


def s5_scan(u, ar, ai, bbr, bbi, h0=None):
    t_len = u.shape[1]
    ur = jnp.einsum('gph,btgh->btgp', bbr, u)
    ui = jnp.einsum('gph,btgh->btgp', bbi, u)
    a_r = jnp.broadcast_to(ar[None, None], (1, t_len) + ar.shape)
    a_i = jnp.broadcast_to(ai[None, None], (1, t_len) + ai.shape)
    pr, pi, xr, xi = lax.associative_scan(complex_linear_combine, (a_r, a_i, ur, ui), axis=1)
    if h0 is not None:
        h0r, h0i = h0
        xr = xr + pr * h0r[:, None] - pi * h0i[:, None]
        xi = xi + pr * h0i[:, None] + pi * h0r[:, None]
    return xr, xi


def s5_readout(c_re, c_im, xr, xi):
    return (jnp.einsum('ghp,btgp->btgh', c_re.astype(jnp.float32), xr)
            - jnp.einsum('ghp,btgp->btgh', c_im.astype(jnp.float32), xi))


def s5_direction(uc, ul, params, ctx_out):
    lam_re, lam_im, log_step, b_re, b_im, c_re, c_im = params
    ar, ai, bbr, bbi = s5_discretize(lam_re, lam_im, log_step, b_re, b_im)
    xcr, xci = s5_scan(uc, ar, ai, bbr, bbi)
    xlr, xli = s5_scan(ul, ar, ai, bbr, bbi, (xcr[:, -1], xci[:, -1]))
    y_lat = s5_readout(c_re, c_im, xlr, xli)
    y_ctx = s5_readout(c_re, c_im, xcr, xci) if ctx_out else None
    return y_lat, y_ctx


def s5_glu(y, glu_w, glu_b):
    bsz, t_len = y.shape[:2]
    y = jax.nn.gelu(y.reshape(bsz, t_len, S5_WIDTH))
    z = y @ glu_w.astype(jnp.float32) + glu_b.astype(jnp.float32)
    a, gate = jnp.split(z, 2, axis=-1)
    return a * jax.nn.sigmoid(gate)


def s5_mixer(ul, uc, fwd, bwd, d, glu_w, glu_b, ctx_out):
    groups = lambda t: t.astype(jnp.float32).reshape(t.shape[0], t.shape[1], S5_GROUPS, S5_GROUP)
    ul4, uc4 = groups(ul), groups(uc)
    flip = lambda t: jnp.flip(t, axis=1)
    yl_f, yc_f = s5_direction(uc4, ul4, fwd, ctx_out)
    yl_b, yc_b = s5_direction(flip(uc4), flip(ul4), bwd, ctx_out)
    d4 = d.astype(jnp.float32).reshape(S5_GROUPS, S5_GROUP)
    out_lat = s5_glu(yl_f + flip(yl_b) + d4 * ul4, glu_w, glu_b).astype(ul.dtype)
    out_ctx = s5_glu(yc_f + flip(yc_b) + d4 * uc4, glu_w, glu_b).astype(uc.dtype) if ctx_out else None
    return out_lat, out_ctx


def sq_relu_mlp(h, w1, w2):
    return jnp.square(jax.nn.relu(h @ w1)) @ w2


def setup_inputs(seed: int = 0) -> dict:
    key = jax.random.key(seed)
    keys = iter(jax.random.split(key, 48))
    f32 = jnp.float32

    def normal(shape, std):
        return jax.random.normal(next(keys), shape, f32) * std

    def gain(shape):
        return 1.0 + normal(shape, 0.02)

    L, G, P, H = DEPTH, S5_GROUPS, S5_STATE, S5_GROUP
    inputs = {}
    inputs['x'] = normal((BATCH, SEQ, D_MODEL), 1.0)
    inputs['c'] = normal((BATCH, D_MODEL), 1.0)
    inputs['ctx'] = normal((BATCH, CTX_LEN, D_MODEL), 1.0)
    inputs['c_ctx'] = normal((D_MODEL,), 1.0)
    inputs['w_mod'] = normal((L, D_MODEL, 6 * D_MODEL), 0.5 * D_MODEL ** -0.5)
    inputs['b_mod'] = normal((L, 6 * D_MODEL), 0.01)
    inputs['norm1_w'] = gain((L, D_MODEL))
    inputs['norm2_w'] = gain((L, D_MODEL))
    inputs['w_in'] = normal((L, D_MODEL, IN_WIDTH), D_MODEL ** -0.5)
    inputs['gla_wa_f'] = normal((L, GLA_RANK, GLA_HEADS * GLA_DK), GLA_RANK ** -0.5)
    inputs['gla_ba_f'] = normal((L, GLA_HEADS * GLA_DK), 0.1)
    inputs['gla_wa_b'] = normal((L, GLA_RANK, GLA_HEADS * GLA_DK), GLA_RANK ** -0.5)
    inputs['gla_ba_b'] = normal((L, GLA_HEADS * GLA_DK), 0.1)
    inputs['gla_norm_w'] = gain((L, GLA_DV))
    inputs['attn_sink'] = normal((L, ATT_HEADS), 0.5)
    for tag in ('f', 'b'):
        inputs['s5_lam_re_' + tag] = -0.5 + normal((L, G, P), 0.01)
        inputs['s5_lam_im_' + tag] = jnp.pi * jnp.arange(P, dtype=f32) + normal((L, G, P), 0.01)
        inputs['s5_log_step_' + tag] = jax.random.uniform(next(keys), (L, G), f32, math.log(1e-3), math.log(1e-1))
        inputs['s5_b_re_' + tag] = normal((L, G, P, H), 0.5)
        inputs['s5_b_im_' + tag] = normal((L, G, P, H), 0.5)
        inputs['s5_c_re_' + tag] = normal((L, G, H, P), P ** -0.5)
        inputs['s5_c_im_' + tag] = normal((L, G, H, P), P ** -0.5)
    inputs['s5_d'] = normal((L, S5_WIDTH), 0.5)
    inputs['glu_w'] = normal((L, S5_WIDTH, 2 * S5_WIDTH), S5_WIDTH ** -0.5)
    inputs['glu_b'] = normal((L, 2 * S5_WIDTH), 0.01)
    inputs['w_out'] = normal((L, D_MODEL, D_MODEL), D_MODEL ** -0.5)
    inputs['mlp_w1'] = normal((L, D_MODEL, D_FF), D_MODEL ** -0.5)
    inputs['mlp_w2'] = normal((L, D_FF, D_MODEL), D_FF ** -0.5)
    inputs['final_norm_w'] = gain((D_MODEL,))
    return inputs


def reference(x, c, ctx, c_ctx, w_mod, b_mod, norm1_w, norm2_w, w_in,
              gla_wa_f, gla_ba_f, gla_wa_b, gla_ba_b, gla_norm_w, attn_sink,
              s5_lam_re_f, s5_lam_im_f, s5_log_step_f, s5_b_re_f, s5_b_im_f, s5_c_re_f, s5_c_im_f,
              s5_lam_re_b, s5_lam_im_b, s5_log_step_b, s5_b_re_b, s5_b_im_b, s5_c_re_b, s5_c_im_b,
              s5_d, glu_w, glu_b, w_out, mlp_w1, mlp_w2, final_norm_w):
    rows = x.shape[1] // GRID_W
    cos, sin = axial_rope_tables(rows)
    c_act = jax.nn.silu(c)
    cc_act = jax.nn.silu(c_ctx)
    xc = ctx
    for l in range(DEPTH):
        last = l == DEPTH - 1
        mod = (c_act @ w_mod[l] + b_mod[l])[:, None, :]
        mod_c = (cc_act @ w_mod[l] + b_mod[l])[None, None, :]
        sh1, sc1, g1, sh2, sc2, g2 = jnp.split(mod, 6, axis=-1)
        csh1, csc1, cg1, csh2, csc2, cg2 = jnp.split(mod_c, 6, axis=-1)

        h = rmsnorm(x, norm1_w[l]) * (1 + sc1) + sh1
        hc = rmsnorm(xc, norm1_w[l]) * (1 + csc1) + csh1
        lat = split_columns(h @ w_in[l])
        cpx = split_columns(hc @ w_in[l])

        a_lat, a_ctx = gla_mixer(lat[0:6], cpx[0:6], gla_wa_f[l], gla_ba_f[l], gla_wa_b[l], gla_ba_b[l],
                                 gla_norm_w[l], not last)
        b_lat, b_ctx = swa_mixer(lat[6], lat[7], lat[8], cpx[6], cpx[7], cpx[8], attn_sink[l], cos, sin, not last)
        fwd = (s5_lam_re_f[l], s5_lam_im_f[l], s5_log_step_f[l], s5_b_re_f[l], s5_b_im_f[l], s5_c_re_f[l], s5_c_im_f[l])
        bwd = (s5_lam_re_b[l], s5_lam_im_b[l], s5_log_step_b[l], s5_b_re_b[l], s5_b_im_b[l], s5_c_re_b[l], s5_c_im_b[l])
        s_lat, s_ctx = s5_mixer(lat[9], cpx[9], fwd, bwd, s5_d[l], glu_w[l], glu_b[l], not last)

        x = x + g1 * (jnp.concatenate([a_lat, b_lat, s_lat], axis=-1) @ w_out[l])
        x = x + g2 * sq_relu_mlp(rmsnorm(x, norm2_w[l]) * (1 + sc2) + sh2, mlp_w1[l], mlp_w2[l])
        if not last:
            xc = xc + cg1 * (jnp.concatenate([a_ctx, b_ctx, s_ctx], axis=-1) @ w_out[l])
            xc = xc + cg2 * sq_relu_mlp(rmsnorm(xc, norm2_w[l]) * (1 + csc2) + csh2, mlp_w1[l], mlp_w2[l])
    return rmsnorm(x, final_norm_w)
```

```python
import functools
import math

import jax
import jax.numpy as jnp
import numpy as np
from jax import lax
from jax.experimental import pallas as pl
from jax.experimental.pallas import tpu as pltpu

F32 = jnp.float32
BF16 = jnp.bfloat16

D_MODEL = 1024
D_FF = 4 * D_MODEL
CTX_LEN = 256
GRID_W = 64
EPS = 1e-6
NEG_INF = -1e30

GLA_HEADS = 4
GLA_DV = 96
GLA_DK = 48
GLA_WIDTH = GLA_HEADS * GLA_DV
GLA_KW = GLA_HEADS * GLA_DK
GLA_KPAD = 256
GLA_RANK = 16
GLA_TAU = 16.0
GLA_CHUNK = 64

ATT_HD = 64
ATT_HEADS = 6
ATT_KV_HEADS = 2
ATT_WIDTH = ATT_HEADS * ATT_HD
ATT_KVW = ATT_KV_HEADS * ATT_HD
WINDOW = 128
ATT_BLOCK = 128
ROPE_BASE = 10000.0

S5_WIDTH = 256
S5_GROUP = 16
S5_GROUPS = 16
S5_STATE = 64
S5_NS = S5_GROUPS * S5_STATE
S5_TC = 64
S5_COLS = 512

TOK_BLOCK = 256

C_V, C_G, C_Q, C_K, C_Z, C_AQ, C_AK, C_AV, C_U = 0, 384, 768, 1024, 1280, 1408, 1792, 1920, 2048
IN_PAD = 2304
GI_V, GI_G, GI_Q, GI_K, GI_LF, GI_LB, GI_W = 0, 384, 768, 1024, 1280, 1536, 1792

VMEM_LIMIT = 56 * 1024 * 1024


def _sigmoid(x):
    return 1.0 / (1.0 + jnp.exp(-x))


def _silu(x):
    return x * _sigmoid(x)


def _dot(a, b):
    return jnp.dot(a, b, preferred_element_type=F32)


def _dot_nt(a, b):
    return lax.dot_general(a, b, (((1,), (1,)), ((), ())), preferred_element_type=F32)


def _dot_tn(a, b):
    return lax.dot_general(a, b, (((0,), (0,)), ((), ())), preferred_element_type=F32)


def _split3(x):
    hi = x.astype(BF16)
    r1 = x - hi.astype(F32)
    mid = r1.astype(BF16)
    lo = (r1 - mid.astype(F32)).astype(BF16)
    return hi, mid, lo


def _mod_kernel(c_ref, w_ref, b_ref, o_ref):
    a = _silu(c_ref[...]).astype(BF16)
    o_ref[...] = _dot(a, w_ref[...].astype(BF16)) + b_ref[...]


def _modulation(cvec, w_mod, b_mod):
    n_layers = w_mod.shape[0]
    rows = cvec.shape[0]
    tn = 1536
    return pl.pallas_call(
        _mod_kernel,
        grid=(n_layers, 6 * D_MODEL // tn),
        in_specs=[
            pl.BlockSpec((rows, D_MODEL), lambda l, n: (0, 0)),
            pl.BlockSpec((None, D_MODEL, tn), lambda l, n: (l, 0, n)),
            pl.BlockSpec((None, 1, tn), lambda l, n: (l, 0, n)),
        ],
        out_specs=pl.BlockSpec((None, rows, tn), lambda l, n: (l, 0, n)),
        out_shape=jax.ShapeDtypeStruct((n_layers, rows, 6 * D_MODEL), F32),
        compiler_params=pltpu.CompilerParams(
            dimension_semantics=("arbitrary", "arbitrary"), vmem_limit_bytes=VMEM_LIMIT),
        name="modulation",
    )(cvec, w_mod, b_mod.reshape(n_layers, 1, 6 * D_MODEL))


def _inproj_kernel(x_ref, mod_ref, n1_ref, w_ref, wa_ref, ba_ref, qs_ref, cos_ref, sa_ref, sb_ref,
                   gin_ref, att_ref, u_ref):
    j = pl.program_id(1)
    xf = x_ref[...]
    y = xf * lax.rsqrt(jnp.mean(xf * xf, axis=-1, keepdims=True) + EPS) * n1_ref[...]
    m = mod_ref[...]
    mrow = jnp.where(j > 0, m[1:2, :], m[0:1, :])
    sh1 = mrow[:, 0:D_MODEL]
    sc1 = mrow[:, D_MODEL:2 * D_MODEL]
    h = y * (1.0 + sc1) + sh1
    p = _dot(h.astype(BF16), w_ref[...])

    gin_ref[:, 0:GI_LF] = p[:, 0:GI_LF] * qs_ref[...]
    z = p[:, C_Z:C_Z + 128].astype(BF16)
    zg = _dot(z, wa_ref[...]) + ba_ref[...]
    la = (jnp.minimum(zg, 0.0) - jnp.log1p(jnp.exp(-jnp.abs(zg)))) * (1.0 / GLA_TAU)
    gin_ref[:, GI_LF:GI_W] = la

    cos = cos_ref[...]
    sa = sa_ref[...]
    sb = sb_ref[...]

    def rope(t, reps):
        w = t.shape[-1]
        c3 = jnp.concatenate([cos] * reps, axis=-1) if reps > 1 else cos
        a3 = jnp.concatenate([sa] * reps, axis=-1) if reps > 1 else sa
        b3 = jnp.concatenate([sb] * reps, axis=-1) if reps > 1 else sb
        up = pltpu.roll(t, w - 16, 1)
        dn = pltpu.roll(t, 16, 1)
        return t * c3 + up * a3 + dn * b3

    aq = rope(p[:, C_AQ:C_AQ + ATT_WIDTH], 3) * (ATT_HD ** -0.5)
    ak = rope(p[:, C_AK:C_AK + ATT_KVW], 1)
    av = p[:, C_AV:C_AV + ATT_KVW]
    left = lax.broadcasted_iota(jnp.int32, ak.shape, 1) < ATT_HD

    def expand(t):
        sw = pltpu.roll(t, ATT_HD, 1)
        return [jnp.where(left, t, sw), t, jnp.where(left, sw, t)]

    att = jnp.concatenate([aq] + expand(ak) + expand(av), axis=-1)
    att_ref[...] = att.astype(BF16)
    u_ref[...] = p[:, C_U:C_U + S5_WIDTH]


def _inproj(xs, modsel, n1w, w_in_p, wa_cat, ba_cat, qscale, cos_t, sa_t, sb_t):
    bsz, tt, _ = xs.shape
    nb = tt // TOK_BLOCK
    const = lambda b, j: (0, 0)
    return pl.pallas_call(
        _inproj_kernel,
        grid=(bsz, nb),
        in_specs=[
            pl.BlockSpec((None, TOK_BLOCK, D_MODEL), lambda b, j: (b, j, 0)),
            pl.BlockSpec((None, 2, 6 * D_MODEL), lambda b, j: (b, 0, 0)),
            pl.BlockSpec((1, D_MODEL), const),
            pl.BlockSpec((D_MODEL, IN_PAD), const),
            pl.BlockSpec((128, 2 * GLA_KPAD), const),
            pl.BlockSpec((1, 2 * GLA_KPAD), const),
            pl.BlockSpec((1, GI_LF), const),
            pl.BlockSpec((TOK_BLOCK, 128), lambda b, j: (j, 0)),
            pl.BlockSpec((TOK_BLOCK, 128), lambda b, j: (j, 0)),
            pl.BlockSpec((TOK_BLOCK, 128), lambda b, j: (j, 0)),
        ],
        out_specs=[
            pl.BlockSpec((None, TOK_BLOCK, GI_W), lambda b, j: (b, j, 0)),
            pl.BlockSpec((None, TOK_BLOCK, 3 * ATT_WIDTH), lambda b, j: (b, j, 0)),
            pl.BlockSpec((TOK_BLOCK, S5_WIDTH), lambda b, j: (j, b)),
        ],
        out_shape=[
            jax.ShapeDtypeStruct((bsz, tt, GI_W), F32),
            jax.ShapeDtypeStruct((bsz, tt, 3 * ATT_WIDTH), BF16),
            jax.ShapeDtypeStruct((tt, bsz * S5_WIDTH), F32),
        ],
        compiler_params=pltpu.CompilerParams(
            dimension_semantics=("arbitrary", "arbitrary"), vmem_limit_bytes=VMEM_LIMIT),
        name="inproj",
    )(xs, modsel, n1w, w_in_p, wa_cat, ba_cat, qscale, cos_t, sa_t, sb_t)


def _gla_chunk(gin_ref, st_ref, o_ref, c, la_col, tri, tri4, last_row, hmask, vmask, bdmask):
    r0 = c * GLA_CHUNK
    rows = pl.ds(r0, GLA_CHUNK)
    v = gin_ref[rows, GI_V:GI_V + GLA_WIDTH]
    q = gin_ref[rows, GI_Q:GI_Q + GLA_KPAD]
    k = gin_ref[rows, GI_K:GI_K + GLA_KPAD]
    la = gin_ref[rows, la_col:la_col + GLA_KPAD]
    hi, mid, lo = _split3(la)
    b = _dot(tri, hi) + _dot(tri, mid) + _dot(tri, lo)
    bl = b[last_row:last_row + 1, :]
    qb = q * jnp.exp(b)
    kb = (k * jnp.exp(-b)).astype(BF16)
    kd = (k * jnp.exp(bl - b)).astype(BF16)
    vb = v.astype(BF16)
    qs = jnp.concatenate([qb * hmask[h] for h in range(GLA_HEADS)], axis=0).astype(BF16)
    s = _dot_nt(qs, kb)
    s = jnp.where(tri4, s, 0.0).astype(BF16)
    pv = _dot(s, vb)
    o = _dot_nt(qb.astype(BF16), st_ref[...].astype(BF16))
    for h in range(GLA_HEADS):
        o = o + pv[h * GLA_CHUNK:(h + 1) * GLA_CHUNK, :] * vmask[h]
    o_ref[rows, :] = o
    st_ref[...] = st_ref[...] * jnp.exp(bl) + _dot_tn(vb, kd) * bdmask


def _gla_kernel(gin_ref, tri_ref, hm_ref, vm_ref, bd_ref, o_ref, st_ref):
    d = pl.program_id(1)
    j = pl.program_id(2)

    @pl.when(j == 0)
    def _():
        st_ref[...] = jnp.zeros_like(st_ref)

    hmask = [hm_ref[h:h + 1, :] for h in range(GLA_HEADS)]
    vmask = [vm_ref[h:h + 1, :] for h in range(GLA_HEADS)]
    bdmask = bd_ref[...]
    n_chunks = TOK_BLOCK // GLA_CHUNK

    def run(direction, order, la_col, last_row):
        tri = tri_ref[direction]
        tri4 = jnp.concatenate([tri] * GLA_HEADS, axis=0) > 0
        for c in order:
            _gla_chunk(gin_ref, st_ref, o_ref, c, la_col, tri, tri4, last_row, hmask, vmask, bdmask)

    @pl.when(d == 0)
    def _():
        run(0, range(n_chunks), GI_LF, GLA_CHUNK - 1)

    @pl.when(d == 1)
    def _():
        run(1, reversed(range(n_chunks)), GI_LB, 0)


def _gla_masks():
    r = np.arange(GLA_CHUNK)
    lower = (r[None, :] <= r[:, None]).astype(np.float32)
    tri = np.stack([lower, lower.T])
    klane = np.arange(GLA_KPAD)
    vlane = np.arange(GLA_WIDTH)
    hm = np.zeros((8, GLA_KPAD), np.float32)
    vm = np.zeros((8, GLA_WIDTH), np.float32)
    for h in range(GLA_HEADS):
        hm[h] = (klane // GLA_DK) == h
        vm[h] = (vlane // GLA_DV) == h
    bd = ((vlane[:, None] // GLA_DV) == (klane[None, :] // GLA_DK)).astype(np.float32)
    return jnp.asarray(tri, BF16), jnp.asarray(hm), jnp.asarray(vm), jnp.asarray(bd)


def _scan_block(d, j, n_ctx, n_all):
    bwd = jnp.where(j < n_ctx, n_ctx - 1 - j, n_all + n_ctx - 1 - j)
    return jnp.where(d == 0, j, bwd)


def _gla(gin):
    bsz, tt, _ = gin.shape
    nb = tt // TOK_BLOCK
    n_ctx = CTX_LEN // TOK_BLOCK
    tri, hm, vm, bd = _gla_masks()
    const2 = lambda b, d, j: (0, 0)
    return pl.pallas_call(
        _gla_kernel,
        grid=(bsz, 2, nb),
        in_specs=[
            pl.BlockSpec((None, TOK_BLOCK, GI_W), lambda b, d, j: (b, _scan_block(d, j, n_ctx, nb), 0)),
            pl.BlockSpec((2, GLA_CHUNK, GLA_CHUNK), lambda b, d, j: (0, 0, 0)),
            pl.BlockSpec((8, GLA_KPAD), const2),
            pl.BlockSpec((8, GLA_WIDTH), const2),
            pl.BlockSpec((GLA_WIDTH, GLA_KPAD), const2),
        ],
        out_specs=pl.BlockSpec((None, None, TOK_BLOCK, GLA_WIDTH),
                               lambda b, d, j: (d, b, _scan_block(d, j, n_ctx, nb), 0)),
        out_shape=jax.ShapeDtypeStruct((2, bsz, tt, GLA_WIDTH), F32),
        scratch_shapes=[pltpu.VMEM((GLA_WIDTH, GLA_KPAD), F32)],
        compiler_params=pltpu.CompilerParams(
            dimension_semantics=("arbitrary", "arbitrary", "arbitrary"), vmem_limit_bytes=VMEM_LIMIT),
        name="gla",
    )(gin, tri, hm, vm, bd)


def _swa_pair(q2, kw, vw, kc, vc, valid2, sink_col):
    s_c = _dot_nt(q2, kc)
    m = jnp.maximum(jnp.max(s_c, axis=-1, keepdims=True), sink_col)
    if kw is not None:
        s_w = jnp.where(valid2, _dot_nt(q2, kw), NEG_INF)
        m = jnp.maximum(m, jnp.max(s_w, axis=-1, keepdims=True))
    e_c = jnp.exp(s_c - m)
    den = jnp.sum(e_c, axis=-1, keepdims=True) + jnp.exp(sink_col - m)
    pv = _dot(e_c.astype(BF16), vc)
    if kw is not None:
        e_w = jnp.exp(s_w - m)
        den = den + jnp.sum(e_w, axis=-1, keepdims=True)
        pv = pv + _dot(e_w.astype(BF16), vw)
    return pv / den


def _swa_kernel(sink_ref, att_ref, o_ref, *, tt):
    i = pl.program_id(1)
    n_ctx_blocks = CTX_LEN // ATT_BLOCK
    q0 = pl.multiple_of(i * ATT_BLOCK, ATT_BLOCK)
    lane = lax.broadcasted_iota(jnp.int32, (ATT_BLOCK, 2 * ATT_HD), 1)
    left = lane < ATT_HD
    row2 = lax.broadcasted_iota(jnp.int32, (2 * ATT_BLOCK, 1), 0)

    def run(windowed):
        if windowed:
            start = jnp.minimum((i - 1) * ATT_BLOCK, tt - 3 * ATT_BLOCK)
            start = pl.multiple_of(start, ATT_BLOCK)
            qpos = q0 - CTX_LEN + lax.broadcasted_iota(jnp.int32, (2 * ATT_BLOCK, 3 * ATT_BLOCK), 0) % ATT_BLOCK
            kpos = start - CTX_LEN + lax.broadcasted_iota(jnp.int32, (2 * ATT_BLOCK, 3 * ATT_BLOCK), 1)
            valid2 = (jnp.abs(qpos - kpos) <= WINDOW) & (kpos >= 0)
        for p in range(ATT_HEADS // 2):
            qp = att_ref[pl.ds(q0, ATT_BLOCK), p * 128:(p + 1) * 128]
            zero = jnp.zeros_like(qp)
            q2 = jnp.concatenate([jnp.where(left, qp, zero), jnp.where(left, zero, qp)], axis=0)
            kcol = ATT_WIDTH + p * 128
            vcol = 2 * ATT_WIDTH + p * 128
            kc = att_ref[0:CTX_LEN, kcol:kcol + 128]
            vc = att_ref[0:CTX_LEN, vcol:vcol + 128]
            sink_col = jnp.where(row2 < ATT_BLOCK, sink_ref[2 * p], sink_ref[2 * p + 1])
            if windowed:
                kw = att_ref[pl.ds(start, 3 * ATT_BLOCK), kcol:kcol + 128]
                vw = att_ref[pl.ds(start, 3 * ATT_BLOCK), vcol:vcol + 128]
                o2 = _swa_pair(q2, kw, vw, kc, vc, valid2, sink_col)
            else:
                o2 = _swa_pair(q2, None, None, kc, vc, None, sink_col)
            o = jnp.where(left, o2[0:ATT_BLOCK], o2[ATT_BLOCK:2 * ATT_BLOCK])
            o_ref[:, p * 128:(p + 1) * 128] = o.astype(BF16)

    @pl.when(i < n_ctx_blocks)
    def _():
        run(False)

    @pl.when(i >= n_ctx_blocks)
    def _():
        run(True)


def _swa(att, sink):
    bsz, tt, _ = att.shape
    nq = tt // ATT_BLOCK
    return pl.pallas_call(
        functools.partial(_swa_kernel, tt=tt),
        grid=(bsz, nq),
        in_specs=[
            pl.BlockSpec(memory_space=pltpu.SMEM),
            pl.BlockSpec((None, tt, 3 * ATT_WIDTH), lambda b, i: (b, 0, 0)),
        ],
        out_specs=pl.BlockSpec((None, ATT_BLOCK, ATT_WIDTH), lambda b, i: (b, i, 0)),
        out_shape=jax.ShapeDtypeStruct((bsz, tt, ATT_WIDTH), BF16),
        compiler_params=pltpu.CompilerParams(
            dimension_semantics=("arbitrary", "arbitrary"), vmem_limit_bytes=VMEM_LIMIT),
        name="swa",
    )(sink, att)


def _s5_prep_kernel(lr_ref, li_ref, ls_ref, br_ref, bi_ref, a_ref, bbr_ref, bbi_ref):
    lr = jnp.minimum(lr_ref[...], -1e-4)
    li = li_ref[...]
    dt = jnp.exp(ls_ref[...])
    mag = jnp.exp(lr * dt)
    ar = mag * jnp.cos(li * dt)
    ai = mag * jnp.sin(li * dt)
    den = lr * lr + li * li
    fr = ((ar - 1.0) * lr + ai * li) / den
    fi = (ai * lr - (ar - 1.0) * li) / den
    a_ref[0:1, :] = ar
    a_ref[1:2, :] = ai
    br = br_ref[...]
    bi = bi_ref[...]
    bbr_ref[...] = fr * br - fi * bi
    bbi_ref[...] = fr * bi + fi * br


def _s5_prep(lam_re, lam_im, log_step, b_re, b_im):
    n_slots = lam_re.shape[0]
    vec = pl.BlockSpec((None, 1, S5_NS), lambda s: (s, 0, 0))
    mat = pl.BlockSpec((None, S5_GROUP, S5_NS), lambda s: (s, 0, 0))
    return pl.pallas_call(
        _s5_prep_kernel,
        grid=(n_slots,),
        in_specs=[vec, vec, vec, mat, mat],
        out_specs=[pl.BlockSpec((None, 2, S5_NS), lambda s: (s, 0, 0)), mat, mat],
        out_shape=[
            jax.ShapeDtypeStruct((n_slots, 2, S5_NS), F32),
            jax.ShapeDtypeStruct((n_slots, S5_GROUP, S5_NS), F32),
            jax.ShapeDtypeStruct((n_slots, S5_GROUP, S5_NS), F32),
        ],
        name="s5_prep",
    )(lam_re, lam_im, log_step, b_re, b_im)


def _s5_kernel(u_ref, a_ref, bre_ref, bim_ref, cre_ref, cim_ref, y_ref, xr_ref, xi_ref, sr_ref, si_ref, *, bsz):
    d = pl.program_id(0)
    j = pl.program_id(1)

    @pl.when(j == 0)
    def _():
        sr_ref[...] = jnp.zeros_like(sr_ref)
        si_ref[...] = jnp.zeros_like(si_ref)

    ub = u_ref[...].astype(BF16)
    xr_ref[...] = _dot(ub, bre_ref[...])
    xi_ref[...] = _dot(ub, bim_ref[...])

    for cb in range(S5_NS // S5_COLS):
        cols = slice(cb * S5_COLS, (cb + 1) * S5_COLS)
        ar = a_ref[0:1, cols]
        ai = a_ref[1:2, cols]

        def step(tstep, carry):
            sr, si = carry
            t = jnp.where(d == 0, tstep, S5_TC - 1 - tstep)
            rows = pl.ds(pl.multiple_of(t * bsz, bsz), bsz)
            nr = ar * sr - ai * si + xr_ref[rows, cols]
            ni = ar * si + ai * sr + xi_ref[rows, cols]
            xr_ref[rows, cols] = nr
            xi_ref[rows, cols] = ni
            return nr, ni

        sr, si = lax.fori_loop(0, S5_TC, step, (sr_ref[:, cols], si_ref[:, cols]))
        sr_ref[:, cols] = sr
        si_ref[:, cols] = si

    y_ref[...] = _dot(xr_ref[...].astype(BF16), cre_ref[...]) - _dot(xi_ref[...].astype(BF16), cim_ref[...])


def _s5(u_tm, avec, bre, bim, cre, cim, layer, bsz):
    rows_total = u_tm.shape[0]
    rows = S5_TC * bsz
    nchunks = rows_total // rows
    n_ctx = CTX_LEN // S5_TC
    slot = lambda d: 2 * layer + d
    blk = lambda d, j: _scan_block(d, j, n_ctx, nchunks)
    return pl.pallas_call(
        functools.partial(_s5_kernel, bsz=bsz),
        grid=(2, nchunks),
        in_specs=[
            pl.BlockSpec((rows, S5_WIDTH), lambda d, j: (blk(d, j), 0)),
            pl.BlockSpec((None, 2, S5_NS), lambda d, j: (slot(d), 0, 0)),
            pl.BlockSpec((None, S5_WIDTH, S5_NS), lambda d, j: (slot(d), 0, 0)),
            pl.BlockSpec((None, S5_WIDTH, S5_NS), lambda d, j: (slot(d), 0, 0)),
            pl.BlockSpec((None, S5_NS, S5_WIDTH), lambda d, j: (slot(d), 0, 0)),
            pl.BlockSpec((None, S5_NS, S5_WIDTH), lambda d, j: (slot(d), 0, 0)),
        ],
        out_specs=pl.BlockSpec((None, rows, S5_WIDTH), lambda d, j: (d, blk(d, j), 0)),
        out_shape=jax.ShapeDtypeStruct((2, rows_total, S5_WIDTH), F32),
        scratch_shapes=[
            pltpu.VMEM((rows, S5_NS), F32),
            pltpu.VMEM((rows, S5_NS), F32),
            pltpu.VMEM((bsz, S5_NS), F32),
            pltpu.VMEM((bsz, S5_NS), F32),
        ],
        compiler_params=pltpu.CompilerParams(
            dimension_semantics=("arbitrary", "arbitrary"), vmem_limit_bytes=VMEM_LIMIT),
        name="s5",
    )(u_tm, avec, bre, bim, cre, cim)


def _out_kernel(x_ref, mod_ref, go_ref, g_ref, att_ref, y_ref, u_ref, gnw_ref, hm_ref, d_ref, gluw_ref,
                glub_ref, wo_ref, n2_ref, w1_ref, w2_ref, fn_ref, o_ref, *, first_block, final):
    j = pl.program_id(1) + first_block
    m = mod_ref[...]
    mrow = jnp.where(j > 0, m[1:2, :], m[0:1, :])
    g1 = mrow[:, 2 * D_MODEL:3 * D_MODEL]
    sh2 = mrow[:, 3 * D_MODEL:4 * D_MODEL]
    sc2 = mrow[:, 4 * D_MODEL:5 * D_MODEL]
    g2 = mrow[:, 5 * D_MODEL:6 * D_MODEL]

    o = go_ref[0] + go_ref[1]
    o2 = o * o
    hi = o2.astype(BF16)
    lo = (o2 - hi.astype(F32)).astype(BF16)
    ms = (_dot(hi, hm_ref[...]) + _dot(lo, hm_ref[...])) * (1.0 / GLA_DV)
    a = o * lax.rsqrt(ms + EPS) * gnw_ref[...] * _silu(g_ref[...])

    yy = y_ref[0] + y_ref[1] + d_ref[...] * u_ref[...]
    ge = 0.5 * yy * (1.0 + jnp.tanh(math.sqrt(2.0 / math.pi) * (yy + 0.044715 * (yy * yy * yy))))
    z = _dot(ge.astype(BF16), gluw_ref[...]) + glub_ref[...]
    s = z[:, 0:S5_WIDTH] * _sigmoid(z[:, S5_WIDTH:2 * S5_WIDTH])

    proj = (_dot(a.astype(BF16), wo_ref[0:GLA_WIDTH, :])
            + _dot(att_ref[...], wo_ref[GLA_WIDTH:GLA_WIDTH + ATT_WIDTH, :])
            + _dot(s.astype(BF16), wo_ref[GLA_WIDTH + ATT_WIDTH:D_MODEL, :]))
    x1 = x_ref[...] + g1 * proj
    y2 = x1 * lax.rsqrt(jnp.mean(x1 * x1, axis=-1, keepdims=True) + EPS) * n2_ref[...]
    h2 = y2 * (1.0 + sc2) + sh2
    hid = jnp.maximum(_dot(h2.astype(BF16), w1_ref[...]), 0.0)
    mlp = _dot((hid * hid).astype(BF16), w2_ref[...])
    x2 = x1 + g2 * mlp
    if final:
        x2 = x2 * lax.rsqrt(jnp.mean(x2 * x2, axis=-1, keepdims=True) + EPS) * fn_ref[...]
    o_ref[...] = x2


def _outproj(xs, modsel, gla_o, gin, att_o, y_tm, u_tm, gnw, hm, s5d, gluw, glub, wo, n2w, w1, w2, fnw,
             layer, final):
    bsz, tt, _ = xs.shape
    nb = tt // TOK_BLOCK
    first = CTX_LEN // TOK_BLOCK if final else 0
    nsteps = nb - first
    const2 = lambda b, j: (0, 0)
    lyr3 = lambda b, j: (layer, 0, 0)
    single = pl.Buffered(1)
    return pl.pallas_call(
        functools.partial(_out_kernel, first_block=first, final=final),
        grid=(bsz, nsteps),
        in_specs=[
            pl.BlockSpec((None, TOK_BLOCK, D_MODEL), lambda b, j: (b, j + first, 0)),
            pl.BlockSpec((None, 2, 6 * D_MODEL), lambda b, j: (b, 0, 0)),
            pl.BlockSpec((2, None, TOK_BLOCK, GLA_WIDTH), lambda b, j: (0, b, j + first, 0)),
            pl.BlockSpec((None, TOK_BLOCK, GLA_WIDTH), lambda b, j: (b, j + first, GI_G // GLA_WIDTH)),
            pl.BlockSpec((None, TOK_BLOCK, ATT_WIDTH), lambda b, j: (b, j + first, 0)),
            pl.BlockSpec((2, TOK_BLOCK, S5_WIDTH), lambda b, j: (0, j + first, b)),
            pl.BlockSpec((TOK_BLOCK, S5_WIDTH), lambda b, j: (j + first, b)),
            pl.BlockSpec((1, GLA_WIDTH), const2),
            pl.BlockSpec((GLA_WIDTH, GLA_WIDTH), const2),
            pl.BlockSpec((1, S5_WIDTH), const2),
            pl.BlockSpec((None, S5_WIDTH, 2 * S5_WIDTH), lyr3, pipeline_mode=single),
            pl.BlockSpec((1, 2 * S5_WIDTH), const2),
            pl.BlockSpec((None, D_MODEL, D_MODEL), lyr3, pipeline_mode=single),
            pl.BlockSpec((1, D_MODEL), const2),
            pl.BlockSpec((None, D_MODEL, D_FF), lyr3, pipeline_mode=single),
            pl.BlockSpec((None, D_FF, D_MODEL), lyr3, pipeline_mode=single),
            pl.BlockSpec((1, D_MODEL), const2),
        ],
        out_specs=pl.BlockSpec((None, TOK_BLOCK, D_MODEL), lambda b, j: (b, j, 0)),
        out_shape=jax.ShapeDtypeStruct((bsz, nsteps * TOK_BLOCK, D_MODEL), F32),
        compiler_params=pltpu.CompilerParams(
            dimension_semantics=("arbitrary", "arbitrary"), vmem_limit_bytes=VMEM_LIMIT),
        name="outproj_mlp",
    )(xs, modsel, gla_o, gin, att_o, y_tm, u_tm, gnw, hm, s5d, gluw, glub, wo, n2w, w1, w2, fnw)


def _rope_tables(tt):
    n_lat = tt - CTX_LEN
    rows = n_lat // GRID_W
    row = jnp.repeat(jnp.arange(rows, dtype=F32), GRID_W)
    col = jnp.tile(jnp.arange(GRID_W, dtype=F32), rows)
    n_freq = ATT_HD // 4
    inv_freq = ROPE_BASE ** (-jnp.arange(n_freq, dtype=F32) / n_freq)
    ang_r = row[:, None] * inv_freq
    ang_c = col[:, None] * inv_freq
    ang = jnp.concatenate([ang_r, ang_r, ang_c, ang_c], axis=-1)
    cos = jnp.concatenate([jnp.ones((CTX_LEN, ATT_HD), F32), jnp.cos(ang)], axis=0)
    sin = jnp.concatenate([jnp.zeros((CTX_LEN, ATT_HD), F32), jnp.sin(ang)], axis=0)
    up_quarter = (np.arange(ATT_HD) // 16) % 2 == 0
    sa = jnp.where(up_quarter, -sin, 0.0)
    sb = jnp.where(up_quarter, 0.0, sin)
    two = lambda t: jnp.concatenate([t, t], axis=-1)
    return two(cos), two(sa), two(sb)


def _pad_cols(w, width):
    return jnp.pad(w, ((0, 0), (0, 0), (0, width - w.shape[-1])))


def _layout_w_in(w_in):
    offs = np.cumsum([0, GLA_KW, GLA_KW, GLA_WIDTH, GLA_WIDTH, GLA_RANK, GLA_RANK, ATT_WIDTH, ATT_KVW, ATT_KVW,
                      S5_WIDTH])
    q, k, v, g, zf, zb, aq, ak, av, u = [w_in[:, :, offs[i]:offs[i + 1]] for i in range(10)]
    z = jnp.concatenate([zf, zb], axis=-1)
    cols = [v, g, _pad_cols(q, GLA_KPAD), _pad_cols(k, GLA_KPAD), _pad_cols(z, 128), aq, ak, av, u]
    return jnp.concatenate(cols, axis=-1).astype(BF16)


def _block_diag(t):
    s, g, a, b = t.shape
    eye = jnp.eye(g, dtype=t.dtype)
    return jnp.einsum('sgab,gk->sgakb', t, eye).reshape(s, g * a, g * b)


def kernel(x, c, ctx, c_ctx, w_mod, b_mod, norm1_w, norm2_w, w_in, gla_wa_f, gla_ba_f, gla_wa_b, gla_ba_b,
           gla_norm_w, attn_sink, s5_lam_re_f, s5_lam_im_f, s5_log_step_f, s5_b_re_f, s5_b_im_f, s5_c_re_f,
           s5_c_im_f, s5_lam_re_b, s5_lam_im_b, s5_log_step_b, s5_b_re_b, s5_b_im_b, s5_c_re_b, s5_c_im_b,
           s5_d, glu_w, glu_b, w_out, mlp_w1, mlp_w2, final_norm_w):
    bsz, seq, _ = x.shape
    n_layers = w_mod.shape[0]
    tt = CTX_LEN + seq
    assert bsz % 8 == 0 and seq % TOK_BLOCK == 0 and seq >= 3 * ATT_BLOCK

    xs = jnp.concatenate([ctx, x], axis=1)

    mod_rows = -(-(bsz + 1) // 8) * 8
    cvec = jnp.zeros((mod_rows, D_MODEL), F32).at[:bsz].set(c).at[bsz].set(c_ctx)
    mod = _modulation(cvec, w_mod, b_mod)
    mod_ctx = jnp.broadcast_to(mod[:, bsz][:, None], (n_layers, bsz, 6 * D_MODEL))
    modsel = jnp.stack([mod_ctx, mod[:, :bsz]], axis=2)

    w_in_p = _layout_w_in(w_in)
    wa_cat = jnp.zeros((n_layers, 128, 2 * GLA_KPAD), F32)
    wa_cat = wa_cat.at[:, 0:GLA_RANK, 0:GLA_KW].set(gla_wa_f)
    wa_cat = wa_cat.at[:, GLA_RANK:2 * GLA_RANK, GLA_KPAD:GLA_KPAD + GLA_KW].set(gla_wa_b).astype(BF16)
    ba_cat = jnp.zeros((n_layers, 1, 2 * GLA_KPAD), F32)
    ba_cat = ba_cat.at[:, 0, 0:GLA_KW].set(gla_ba_f).at[:, 0, GLA_KPAD:GLA_KPAD + GLA_KW].set(gla_ba_b)
    qscale = jnp.ones((1, GI_LF), F32).at[:, GI_Q:GI_Q + GLA_KW].set(GLA_DK ** -0.5)
    cos_t, sa_t, sb_t = _rope_tables(tt)
    gnw = jnp.tile(gla_norm_w, (1, GLA_HEADS))[:, None, :]
    head = np.arange(GLA_WIDTH) // GLA_DV
    hm = jnp.asarray(head[:, None] == head[None, :], BF16)
    wo_b = w_out.astype(BF16)
    w1_b = mlp_w1.astype(BF16)
    w2_b = mlp_w2.astype(BF16)
    gluw_b = glu_w.astype(BF16)

    def slots(f, b):
        return jnp.stack([f, b], axis=1).reshape((2 * n_layers,) + f.shape[1:])

    lam_re = slots(s5_lam_re_f, s5_lam_re_b).reshape(-1, 1, S5_NS)
    lam_im = slots(s5_lam_im_f, s5_lam_im_b).reshape(-1, 1, S5_NS)
    log_step = jnp.repeat(slots(s5_log_step_f, s5_log_step_b), S5_STATE, axis=-1).reshape(-1, 1, S5_NS)
    to_hn = lambda t: t.transpose(0, 3, 1, 2).reshape(-1, S5_GROUP, S5_NS)
    avec, bbr, bbi = _s5_prep(lam_re, lam_im, log_step, to_hn(slots(s5_b_re_f, s5_b_re_b)),
                              to_hn(slots(s5_b_im_f, s5_b_im_b)))
    from_hn = lambda t: t.reshape(-1, S5_GROUP, S5_GROUPS, S5_STATE).transpose(0, 2, 1, 3)
    bre = _block_diag(from_hn(bbr)).astype(BF16)
    bim = _block_diag(from_hn(bbi)).astype(BF16)
    to_ph = lambda t: _block_diag(t.transpose(0, 1, 3, 2)).astype(BF16)
    cre = to_ph(slots(s5_c_re_f, s5_c_re_b))
    cim = to_ph(slots(s5_c_im_f, s5_c_im_b))

    for l in range(n_layers):
        final = l == n_layers - 1
        gin, att, u_tm = _inproj(xs, modsel[l], norm1_w[l][None], w_in_p[l], wa_cat[l], ba_cat[l], qscale,
                                 cos_t, sa_t, sb_t)
        gla_o = _gla(gin)
        att_o = _swa(att, attn_sink[l])
        y_tm = _s5(u_tm.reshape(tt * bsz, S5_WIDTH), avec, bre, bim, cre, cim, l, bsz)
        xs = _outproj(xs, modsel[l], gla_o, gin, att_o, y_tm.reshape(2, tt, bsz * S5_WIDTH), u_tm,
                      gnw[l], hm, s5_d[l][None], gluw_b, glu_b[l][None], wo_b, norm2_w[l][None], w1_b, w2_b,
                      final_norm_w[None], l, final)
    return xs
```

```python
import functools
import math

import jax
import jax.numpy as jnp
import numpy as np
from jax import lax
from jax.experimental import pallas as pl
from jax.experimental.pallas import tpu as pltpu

F32 = jnp.float32
BF16 = jnp.bfloat16

D_MODEL = 1024
D_FF = 4 * D_MODEL
CTX_LEN = 256
GRID_W = 64
EPS = 1e-6
NEG_INF = -1e30

GLA_HEADS = 4
GLA_DV = 96
GLA_DK = 48
GLA_WIDTH = GLA_HEADS * GLA_DV
GLA_KW = GLA_HEADS * GLA_DK
GLA_KPAD = 256
GLA_RANK = 16
GLA_TAU = 16.0
GLA_CHUNK = 64

ATT_HD = 64
ATT_HEADS = 6
ATT_KV_HEADS = 2
ATT_WIDTH = ATT_HEADS * ATT_HD
ATT_KVW = ATT_KV_HEADS * ATT_HD
WINDOW = 128
ATT_BLOCK = 128
ROPE_BASE = 10000.0

S5_WIDTH = 256
S5_GROUP = 16
S5_GROUPS = 16
S5_STATE = 64
S5_NS = S5_GROUPS * S5_STATE
S5_TC = 64
S5_COLS = 512

TOK_BLOCK = 256

C_V, C_G, C_Q, C_K, C_Z, C_AQ, C_AK, C_AV, C_U = 0, 384, 768, 1024, 1280, 1408, 1792, 1920, 2048
IN_PAD = 2304
GI_V, GI_G, GI_Q, GI_K, GI_LF, GI_LB, GI_W = 0, 384, 768, 1024, 1280, 1536, 1792

VMEM_LIMIT = 56 * 1024 * 1024


def _sigmoid(x):
    return 1.0 / (1.0 + jnp.exp(-x))


def _silu(x):
    return x * _sigmoid(x)


def _dot(a, b):
    return jnp.dot(a, b, preferred_element_type=F32)


def _dot_nt(a, b):
    return lax.dot_general(a, b, (((1,), (1,)), ((), ())), preferred_element_type=F32)


def _dot_tn(a, b):
    return lax.dot_general(a, b, (((0,), (0,)), ((), ())), preferred_element_type=F32)


def _split3(x):
    hi = x.astype(BF16)
    r1 = x - hi.astype(F32)
    mid = r1.astype(BF16)
    lo = (r1 - mid.astype(F32)).astype(BF16)
    return hi, mid, lo


def _mod_kernel(c_ref, w_ref, b_ref, o_ref):
    a = _silu(c_ref[...]).astype(BF16)
    o_ref[...] = _dot(a, w_ref[...].astype(BF16)) + b_ref[...]


def _modulation(cvec, w_mod, b_mod):
    n_layers = w_mod.shape[0]
    rows = cvec.shape[0]
    tn = 1536
    return pl.pallas_call(
        _mod_kernel,
        grid=(n_layers, 6 * D_MODEL // tn),
        in_specs=[
            pl.BlockSpec((rows, D_MODEL), lambda l, n: (0, 0)),
            pl.BlockSpec((None, D_MODEL, tn), lambda l, n: (l, 0, n)),
            pl.BlockSpec((None, 1, tn), lambda l, n: (l, 0, n)),
        ],
        out_specs=pl.BlockSpec((None, rows, tn), lambda l, n: (l, 0, n)),
        out_shape=jax.ShapeDtypeStruct((n_layers, rows, 6 * D_MODEL), F32),
        compiler_params=pltpu.CompilerParams(
            dimension_semantics=("arbitrary", "arbitrary"), vmem_limit_bytes=VMEM_LIMIT),
        name="modulation",
    )(cvec, w_mod, b_mod.reshape(n_layers, 1, 6 * D_MODEL))


def _inproj_kernel(x_ref, mod_ref, n1_ref, w_ref, wa_ref, ba_ref, qs_ref, cos_ref, sa_ref, sb_ref,
                   gin_ref, att_ref, u_ref):
    j = pl.program_id(1)
    xf = x_ref[...]
    y = xf * lax.rsqrt(jnp.mean(xf * xf, axis=-1, keepdims=True) + EPS) * n1_ref[...]
    m = mod_ref[...]
    mrow = jnp.where(j > 0, m[1:2, :], m[0:1, :])
    sh1 = mrow[:, 0:D_MODEL]
    sc1 = mrow[:, D_MODEL:2 * D_MODEL]
    h = y * (1.0 + sc1) + sh1
    p = _dot(h.astype(BF16), w_ref[...])

    gin_ref[:, 0:GI_LF] = p[:, 0:GI_LF] * qs_ref[...]
    z = p[:, C_Z:C_Z + 128].astype(BF16)
    zg = _dot(z, wa_ref[...]) + ba_ref[...]
    la = (jnp.minimum(zg, 0.0) - jnp.log1p(jnp.exp(-jnp.abs(zg)))) * (1.0 / GLA_TAU)
    gin_ref[:, GI_LF:GI_W] = la

    cos = cos_ref[...]
    sa = sa_ref[...]
    sb = sb_ref[...]

    def rope(t, reps):
        w = t.shape[-1]
        c3 = jnp.concatenate([cos] * reps, axis=-1) if reps > 1 else cos
        a3 = jnp.concatenate([sa] * reps, axis=-1) if reps > 1 else sa
        b3 = jnp.concatenate([sb] * reps, axis=-1) if reps > 1 else sb
        up = pltpu.roll(t, w - 16, 1)
        dn = pltpu.roll(t, 16, 1)
        return t * c3 + up * a3 + dn * b3

    aq = rope(p[:, C_AQ:C_AQ + ATT_WIDTH], 3) * (ATT_HD ** -0.5)
    ak = rope(p[:, C_AK:C_AK + ATT_KVW], 1)
    av = p[:, C_AV:C_AV + ATT_KVW]
    left = lax.broadcasted_iota(jnp.int32, ak.shape, 1) < ATT_HD

    def expand(t):
        sw = pltpu.roll(t, ATT_HD, 1)
        return [jnp.where(left, t, sw), t, jnp.where(left, sw, t)]

    att = jnp.concatenate([aq] + expand(ak) + expand(av), axis=-1)
    att_ref[...] = att.astype(BF16)
    u_ref[...] = p[:, C_U:C_U + S5_WIDTH]


def _inproj(xs, modsel, n1w, w_in_p, wa_cat, ba_cat, qscale, cos_t, sa_t, sb_t):
    bsz, tt, _ = xs.shape
    nb = tt // TOK_BLOCK
    const = lambda b, j: (0, 0)
    return pl.pallas_call(
        _inproj_kernel,
        grid=(bsz, nb),
        in_specs=[
            pl.BlockSpec((None, TOK_BLOCK, D_MODEL), lambda b, j: (b, j, 0)),
            pl.BlockSpec((None, 2, 6 * D_MODEL), lambda b, j: (b, 0, 0)),
            pl.BlockSpec((1, D_MODEL), const),
            pl.BlockSpec((D_MODEL, IN_PAD), const),
            pl.BlockSpec((128, 2 * GLA_KPAD), const),
            pl.BlockSpec((1, 2 * GLA_KPAD), const),
            pl.BlockSpec((1, GI_LF), const),
            pl.BlockSpec((TOK_BLOCK, 128), lambda b, j: (j, 0)),
            pl.BlockSpec((TOK_BLOCK, 128), lambda b, j: (j, 0)),
            pl.BlockSpec((TOK_BLOCK, 128), lambda b, j: (j, 0)),
        ],
        out_specs=[
            pl.BlockSpec((None, TOK_BLOCK, GI_W), lambda b, j: (b, j, 0)),
            pl.BlockSpec((None, TOK_BLOCK, 3 * ATT_WIDTH), lambda b, j: (b, j, 0)),
            pl.BlockSpec((TOK_BLOCK, S5_WIDTH), lambda b, j: (j, b)),
        ],
        out_shape=[
            jax.ShapeDtypeStruct((bsz, tt, GI_W), F32),
            jax.ShapeDtypeStruct((bsz, tt, 3 * ATT_WIDTH), BF16),
            jax.ShapeDtypeStruct((tt, bsz * S5_WIDTH), F32),
        ],
        compiler_params=pltpu.CompilerParams(
            dimension_semantics=("arbitrary", "arbitrary"), vmem_limit_bytes=VMEM_LIMIT),
        name="inproj",
    )(xs, modsel, n1w, w_in_p, wa_cat, ba_cat, qscale, cos_t, sa_t, sb_t)


def _gla_kernel(gf_ref, gb_ref, tri_ref, tril_ref, hm_ref, vm_ref, bd_ref, of_ref, ob_ref, stf_ref, stb_ref):
    j = pl.program_id(1)

    @pl.when(j == 0)
    def _():
        stf_ref[...] = jnp.zeros_like(stf_ref)
        stb_ref[...] = jnp.zeros_like(stb_ref)

    n_chunks = TOK_BLOCK // GLA_CHUNK
    chunk = lambda t, c: t[c * GLA_CHUNK:(c + 1) * GLA_CHUNK]
    dirs = ((gf_ref, GI_LF, 0, GLA_CHUNK - 1, of_ref, stf_ref, list(range(n_chunks))),
            (gb_ref, GI_LB, 1, 0, ob_ref, stb_ref, list(reversed(range(n_chunks)))))

    cum = []
    for g_ref, la_col, d, _, _, _, _ in dirs:
        la = g_ref[:, la_col:la_col + GLA_KPAD]
        hi = la.astype(BF16)
        lo = (la - hi.astype(F32)).astype(BF16)
        cum.append(_dot(tri_ref[d], hi) + _dot(tri_ref[d], lo))

    ops = []
    for (g_ref, _, d, last_row, _, _, _), b in zip(dirs, cum):
        q = g_ref[:, GI_Q:GI_Q + GLA_KPAD]
        k = g_ref[:, GI_K:GI_K + GLA_KPAD]
        vb = g_ref[:, GI_V:GI_V + GLA_WIDTH].astype(BF16)
        bl = [chunk(b, c)[last_row:last_row + 1, :] for c in range(n_chunks)]
        blx = jnp.concatenate([jnp.broadcast_to(t, (GLA_CHUNK, GLA_KPAD)) for t in bl], axis=0)
        qb = (q * jnp.exp(b)).astype(BF16)
        kb = (k * jnp.exp(-b)).astype(BF16)
        kd = (k * jnp.exp(blx - b)).astype(BF16)
        ops.append((qb, kb, kd, vb, bl))

    scores = []
    for (qb, kb, _, _, _), (_, _, d, _, _, _, _) in zip(ops, dirs):
        keep = tril_ref[d] > 0
        per = []
        for c in range(n_chunks):
            kst = jnp.concatenate([chunk(kb, c) * hm_ref[h] for h in range(GLA_HEADS)], axis=0)
            per.append(jnp.where(keep, _dot_nt(chunk(qb, c), kst), 0.0).astype(BF16))
        scores.append(per)

    bdmask = bd_ref[...]
    intra, inc = [], []
    for (_, _, kd, vb, _), per in zip(ops, scores):
        oi, ds = [], []
        for c in range(n_chunks):
            vc = chunk(vb, c)
            vbd = jnp.concatenate([vc * vm_ref[h] for h in range(GLA_HEADS)], axis=0)
            oi.append(_dot(per[c], vbd))
            ds.append(_dot_tn(vc, chunk(kd, c)) * bdmask)
        intra.append(oi)
        inc.append(ds)

    for (_, _, _, _, o_ref, st_ref, order), (qb, _, _, _, bl), oi, ds in zip(dirs, ops, intra, inc):
        st = st_ref[...]
        for c in order:
            o_ref[c * GLA_CHUNK:(c + 1) * GLA_CHUNK, :] = oi[c] + _dot_nt(chunk(qb, c), st.astype(BF16))
            st = st * jnp.exp(bl[c]) + ds[c]
        st_ref[...] = st


def _gla_masks():
    r = np.arange(GLA_CHUNK)
    lower = (r[None, :] <= r[:, None]).astype(np.float32)
    tri1 = np.stack([lower, lower.T])
    n_chunks = TOK_BLOCK // GLA_CHUNK
    tri = np.stack([np.kron(np.eye(n_chunks, dtype=np.float32), t) for t in tri1])
    tril = np.tile(tri1, (1, 1, GLA_HEADS))
    klane = np.arange(GLA_KPAD)
    vlane = np.arange(GLA_WIDTH)
    hm = np.stack([np.broadcast_to((klane // GLA_DK) == h, (GLA_CHUNK, GLA_KPAD)) for h in range(GLA_HEADS)])
    vm = np.stack([np.broadcast_to((vlane // GLA_DV) == h, (GLA_CHUNK, GLA_WIDTH)) for h in range(GLA_HEADS)])
    bd = ((vlane[:, None] // GLA_DV) == (klane[None, :] // GLA_DK)).astype(np.float32)
    return (jnp.asarray(tri, BF16), jnp.asarray(tril, F32), jnp.asarray(hm, BF16), jnp.asarray(vm, BF16),
            jnp.asarray(bd))


def _scan_block(d, j, n_ctx, n_all):
    bwd = jnp.where(j < n_ctx, n_ctx - 1 - j, n_all + n_ctx - 1 - j)
    return jnp.where(d == 0, j, bwd)


def _gla(gin):
    bsz, tt, _ = gin.shape
    nb = tt // TOK_BLOCK
    n_ctx = CTX_LEN // TOK_BLOCK
    tri, tril, hm, vm, bd = _gla_masks()
    const2 = lambda b, j: (0, 0)
    const3 = lambda b, j: (0, 0, 0)
    fwd = lambda b, j: (b, j, 0)
    bwd = lambda b, j: (b, _scan_block(1, j, n_ctx, nb), 0)
    return pl.pallas_call(
        _gla_kernel,
        grid=(bsz, nb),
        in_specs=[
            pl.BlockSpec((None, TOK_BLOCK, GI_W), fwd),
            pl.BlockSpec((None, TOK_BLOCK, GI_W), bwd),
            pl.BlockSpec((2, TOK_BLOCK, TOK_BLOCK), const3),
            pl.BlockSpec((2, GLA_CHUNK, GLA_HEADS * GLA_CHUNK), const3),
            pl.BlockSpec((GLA_HEADS, GLA_CHUNK, GLA_KPAD), const3),
            pl.BlockSpec((GLA_HEADS, GLA_CHUNK, GLA_WIDTH), const3),
            pl.BlockSpec((GLA_WIDTH, GLA_KPAD), const2),
        ],
        out_specs=[
            pl.BlockSpec((None, TOK_BLOCK, GLA_WIDTH), fwd),
            pl.BlockSpec((None, TOK_BLOCK, GLA_WIDTH), bwd),
        ],
        out_shape=[jax.ShapeDtypeStruct((bsz, tt, GLA_WIDTH), F32)] * 2,
        scratch_shapes=[pltpu.VMEM((GLA_WIDTH, GLA_KPAD), F32)] * 2,
        compiler_params=pltpu.CompilerParams(
            dimension_semantics=("arbitrary", "arbitrary"), vmem_limit_bytes=VMEM_LIMIT),
        name="gla",
    )(gin, gin, tri, tril, hm, vm, bd)


def _swa_pair(q2, kw, vw, kc, vc, valid2, sink_col):
    s_c = _dot_nt(q2, kc)
    m = jnp.maximum(jnp.max(s_c, axis=-1, keepdims=True), sink_col)
    if kw is not None:
        s_w = jnp.where(valid2, _dot_nt(q2, kw), NEG_INF)
        m = jnp.maximum(m, jnp.max(s_w, axis=-1, keepdims=True))
    e_c = jnp.exp(s_c - m)
    den = jnp.sum(e_c, axis=-1, keepdims=True) + jnp.exp(sink_col - m)
    pv = _dot(e_c.astype(BF16), vc)
    if kw is not None:
        e_w = jnp.exp(s_w - m)
        den = den + jnp.sum(e_w, axis=-1, keepdims=True)
        pv = pv + _dot(e_w.astype(BF16), vw)
    return pv / den


def _swa_kernel(sink_ref, att_ref, o_ref, *, tt):
    i = pl.program_id(1)
    n_ctx_blocks = CTX_LEN // ATT_BLOCK
    q0 = pl.multiple_of(i * ATT_BLOCK, ATT_BLOCK)
    lane = lax.broadcasted_iota(jnp.int32, (ATT_BLOCK, 2 * ATT_HD), 1)
    left = lane < ATT_HD
    row2 = lax.broadcasted_iota(jnp.int32, (2 * ATT_BLOCK, 1), 0)

    def run(windowed):
        if windowed:
            start = jnp.minimum((i - 1) * ATT_BLOCK, tt - 3 * ATT_BLOCK)
            start = pl.multiple_of(start, ATT_BLOCK)
            qpos = q0 - CTX_LEN + lax.broadcasted_iota(jnp.int32, (2 * ATT_BLOCK, 3 * ATT_BLOCK), 0) % ATT_BLOCK
            kpos = start - CTX_LEN + lax.broadcasted_iota(jnp.int32, (2 * ATT_BLOCK, 3 * ATT_BLOCK), 1)
            valid2 = (jnp.abs(qpos - kpos) <= WINDOW) & (kpos >= 0)
        for p in range(ATT_HEADS // 2):
            qp = att_ref[pl.ds(q0, ATT_BLOCK), p * 128:(p + 1) * 128]
            zero = jnp.zeros_like(qp)
            q2 = jnp.concatenate([jnp.where(left, qp, zero), jnp.where(left, zero, qp)], axis=0)
            kcol = ATT_WIDTH + p * 128
            vcol = 2 * ATT_WIDTH + p * 128
            kc = att_ref[0:CTX_LEN, kcol:kcol + 128]
            vc = att_ref[0:CTX_LEN, vcol:vcol + 128]
            sink_col = jnp.where(row2 < ATT_BLOCK, sink_ref[2 * p], sink_ref[2 * p + 1])
            if windowed:
                kw = att_ref[pl.ds(start, 3 * ATT_BLOCK), kcol:kcol + 128]
                vw = att_ref[pl.ds(start, 3 * ATT_BLOCK), vcol:vcol + 128]
                o2 = _swa_pair(q2, kw, vw, kc, vc, valid2, sink_col)
            else:
                o2 = _swa_pair(q2, None, None, kc, vc, None, sink_col)
            o = jnp.where(left, o2[0:ATT_BLOCK], o2[ATT_BLOCK:2 * ATT_BLOCK])
            o_ref[:, p * 128:(p + 1) * 128] = o.astype(BF16)

    @pl.when(i < n_ctx_blocks)
    def _():
        run(False)

    @pl.when(i >= n_ctx_blocks)
    def _():
        run(True)


def _swa(att, sink):
    bsz, tt, _ = att.shape
    nq = tt // ATT_BLOCK
    return pl.pallas_call(
        functools.partial(_swa_kernel, tt=tt),
        grid=(bsz, nq),
        in_specs=[
            pl.BlockSpec(memory_space=pltpu.SMEM),
            pl.BlockSpec((None, tt, 3 * ATT_WIDTH), lambda b, i: (b, 0, 0)),
        ],
        out_specs=pl.BlockSpec((None, ATT_BLOCK, ATT_WIDTH), lambda b, i: (b, i, 0)),
        out_shape=jax.ShapeDtypeStruct((bsz, tt, ATT_WIDTH), BF16),
        compiler_params=pltpu.CompilerParams(
            dimension_semantics=("arbitrary", "arbitrary"), vmem_limit_bytes=VMEM_LIMIT),
        name="swa",
    )(sink, att)


def _s5_prep_kernel(lr_ref, li_ref, ls_ref, br_ref, bi_ref, a_ref, bbr_ref, bbi_ref):
    lr = jnp.minimum(lr_ref[...], -1e-4)
    li = li_ref[...]
    dt = jnp.exp(ls_ref[...])
    mag = jnp.exp(lr * dt)
    ar = mag * jnp.cos(li * dt)
    ai = mag * jnp.sin(li * dt)
    den = lr * lr + li * li
    fr = ((ar - 1.0) * lr + ai * li) / den
    fi = (ai * lr - (ar - 1.0) * li) / den
    a_ref[0:1, :] = ar
    a_ref[1:2, :] = ai
    br = br_ref[...]
    bi = bi_ref[...]
    bbr_ref[...] = fr * br - fi * bi
    bbi_ref[...] = fr * bi + fi * br


def _s5_prep(lam_re, lam_im, log_step, b_re, b_im):
    n_slots = lam_re.shape[0]
    vec = pl.BlockSpec((None, 1, S5_NS), lambda s: (s, 0, 0))
    mat = pl.BlockSpec((None, S5_GROUP, S5_NS), lambda s: (s, 0, 0))
    return pl.pallas_call(
        _s5_prep_kernel,
        grid=(n_slots,),
        in_specs=[vec, vec, vec, mat, mat],
        out_specs=[pl.BlockSpec((None, 2, S5_NS), lambda s: (s, 0, 0)), mat, mat],
        out_shape=[
            jax.ShapeDtypeStruct((n_slots, 2, S5_NS), F32),
            jax.ShapeDtypeStruct((n_slots, S5_GROUP, S5_NS), F32),
            jax.ShapeDtypeStruct((n_slots, S5_GROUP, S5_NS), F32),
        ],
        name="s5_prep",
    )(lam_re, lam_im, log_step, b_re, b_im)


def _s5_kernel(u_ref, a_ref, bre_ref, bim_ref, cre_ref, cim_ref, y_ref, xr_ref, xi_ref, sr_ref, si_ref, *, bsz):
    d = pl.program_id(0)
    j = pl.program_id(1)

    @pl.when(j == 0)
    def _():
        sr_ref[...] = jnp.zeros_like(sr_ref)
        si_ref[...] = jnp.zeros_like(si_ref)

    ub = u_ref[...].astype(BF16)
    xr_ref[...] = _dot(ub, bre_ref[...])
    xi_ref[...] = _dot(ub, bim_ref[...])

    for cb in range(S5_NS // S5_COLS):
        cols = slice(cb * S5_COLS, (cb + 1) * S5_COLS)
        ar = a_ref[0:1, cols]
        ai = a_ref[1:2, cols]

        def step(tstep, carry):
            sr, si = carry
            t = jnp.where(d == 0, tstep, S5_TC - 1 - tstep)
            rows = pl.ds(pl.multiple_of(t * bsz, bsz), bsz)
            nr = ar * sr - ai * si + xr_ref[rows, cols]
            ni = ar * si + ai * sr + xi_ref[rows, cols]
            xr_ref[rows, cols] = nr
            xi_ref[rows, cols] = ni
            return nr, ni

        sr, si = lax.fori_loop(0, S5_TC, step, (sr_ref[:, cols], si_ref[:, cols]))
        sr_ref[:, cols] = sr
        si_ref[:, cols] = si

    y_ref[...] = _dot(xr_ref[...].astype(BF16), cre_ref[...]) - _dot(xi_ref[...].astype(BF16), cim_ref[...])


def _s5(u_tm, avec, bre, bim, cre, cim, layer, bsz):
    rows_total = u_tm.shape[0]
    rows = S5_TC * bsz
    nchunks = rows_total // rows
    n_ctx = CTX_LEN // S5_TC
    slot = lambda d: 2 * layer + d
    blk = lambda d, j: _scan_block(d, j, n_ctx, nchunks)
    return pl.pallas_call(
        functools.partial(_s5_kernel, bsz=bsz),
        grid=(2, nchunks),
        in_specs=[
            pl.BlockSpec((rows, S5_WIDTH), lambda d, j: (blk(d, j), 0)),
            pl.BlockSpec((None, 2, S5_NS), lambda d, j: (slot(d), 0, 0)),
            pl.BlockSpec((None, S5_WIDTH, S5_NS), lambda d, j: (slot(d), 0, 0)),
            pl.BlockSpec((None, S5_WIDTH, S5_NS), lambda d, j: (slot(d), 0, 0)),
            pl.BlockSpec((None, S5_NS, S5_WIDTH), lambda d, j: (slot(d), 0, 0)),
            pl.BlockSpec((None, S5_NS, S5_WIDTH), lambda d, j: (slot(d), 0, 0)),
        ],
        out_specs=pl.BlockSpec((None, rows, S5_WIDTH), lambda d, j: (d, blk(d, j), 0)),
        out_shape=jax.ShapeDtypeStruct((2, rows_total, S5_WIDTH), F32),
        scratch_shapes=[
            pltpu.VMEM((rows, S5_NS), F32),
            pltpu.VMEM((rows, S5_NS), F32),
            pltpu.VMEM((bsz, S5_NS), F32),
            pltpu.VMEM((bsz, S5_NS), F32),
        ],
        compiler_params=pltpu.CompilerParams(
            dimension_semantics=("arbitrary", "arbitrary"), vmem_limit_bytes=VMEM_LIMIT),
        name="s5",
    )(u_tm, avec, bre, bim, cre, cim)


def _out_kernel(x_ref, mod_ref, gof_ref, gob_ref, g_ref, att_ref, y_ref, u_ref, gnw_ref, hm_ref, d_ref, gluw_ref,
                glub_ref, wo_ref, n2_ref, w1_ref, w2_ref, fn_ref, o_ref, *, first_block, final):
    j = pl.program_id(1) + first_block
    m = mod_ref[...]
    mrow = jnp.where(j > 0, m[1:2, :], m[0:1, :])
    g1 = mrow[:, 2 * D_MODEL:3 * D_MODEL]
    sh2 = mrow[:, 3 * D_MODEL:4 * D_MODEL]
    sc2 = mrow[:, 4 * D_MODEL:5 * D_MODEL]
    g2 = mrow[:, 5 * D_MODEL:6 * D_MODEL]

    o = gof_ref[...] + gob_ref[...]
    o2 = o * o
    hi = o2.astype(BF16)
    lo = (o2 - hi.astype(F32)).astype(BF16)
    ms = (_dot(hi, hm_ref[...]) + _dot(lo, hm_ref[...])) * (1.0 / GLA_DV)
    a = o * lax.rsqrt(ms + EPS) * gnw_ref[...] * _silu(g_ref[...])

    yy = y_ref[0] + y_ref[1] + d_ref[...] * u_ref[...]
    ge = 0.5 * yy * (1.0 + jnp.tanh(math.sqrt(2.0 / math.pi) * (yy + 0.044715 * (yy * yy * yy))))
    z = _dot(ge.astype(BF16), gluw_ref[...]) + glub_ref[...]
    s = z[:, 0:S5_WIDTH] * _sigmoid(z[:, S5_WIDTH:2 * S5_WIDTH])

    proj = (_dot(a.astype(BF16), wo_ref[0:GLA_WIDTH, :])
            + _dot(att_ref[...], wo_ref[GLA_WIDTH:GLA_WIDTH + ATT_WIDTH, :])
            + _dot(s.astype(BF16), wo_ref[GLA_WIDTH + ATT_WIDTH:D_MODEL, :]))
    x1 = x_ref[...] + g1 * proj
    y2 = x1 * lax.rsqrt(jnp.mean(x1 * x1, axis=-1, keepdims=True) + EPS) * n2_ref[...]
    h2 = y2 * (1.0 + sc2) + sh2
    hid = jnp.maximum(_dot(h2.astype(BF16), w1_ref[...]), 0.0)
    mlp = _dot((hid * hid).astype(BF16), w2_ref[...])
    x2 = x1 + g2 * mlp
    if final:
        x2 = x2 * lax.rsqrt(jnp.mean(x2 * x2, axis=-1, keepdims=True) + EPS) * fn_ref[...]
    o_ref[...] = x2


def _outproj(xs, modsel, gla_f, gla_b, gin, att_o, y_tm, u_tm, gnw, hm, s5d, gluw, glub, wo, n2w, w1, w2, fnw,
             layer, final):
    bsz, tt, _ = xs.shape
    nb = tt // TOK_BLOCK
    first = CTX_LEN // TOK_BLOCK if final else 0
    nsteps = nb - first
    const2 = lambda b, j: (0, 0)
    lyr3 = lambda b, j: (layer, 0, 0)
    single = pl.Buffered(1)
    return pl.pallas_call(
        functools.partial(_out_kernel, first_block=first, final=final),
        grid=(bsz, nsteps),
        in_specs=[
            pl.BlockSpec((None, TOK_BLOCK, D_MODEL), lambda b, j: (b, j + first, 0)),
            pl.BlockSpec((None, 2, 6 * D_MODEL), lambda b, j: (b, 0, 0)),
            pl.BlockSpec((None, TOK_BLOCK, GLA_WIDTH), lambda b, j: (b, j + first, 0)),
            pl.BlockSpec((None, TOK_BLOCK, GLA_WIDTH), lambda b, j: (b, j + first, 0)),
            pl.BlockSpec((None, TOK_BLOCK, GLA_WIDTH), lambda b, j: (b, j + first, GI_G // GLA_WIDTH)),
            pl.BlockSpec((None, TOK_BLOCK, ATT_WIDTH), lambda b, j: (b, j + first, 0)),
            pl.BlockSpec((2, TOK_BLOCK, S5_WIDTH), lambda b, j: (0, j + first, b)),
            pl.BlockSpec((TOK_BLOCK, S5_WIDTH), lambda b, j: (j + first, b)),
            pl.BlockSpec((1, GLA_WIDTH), const2),
            pl.BlockSpec((GLA_WIDTH, GLA_WIDTH), const2),
            pl.BlockSpec((1, S5_WIDTH), const2),
            pl.BlockSpec((None, S5_WIDTH, 2 * S5_WIDTH), lyr3, pipeline_mode=single),
            pl.BlockSpec((1, 2 * S5_WIDTH), const2),
            pl.BlockSpec((None, D_MODEL, D_MODEL), lyr3, pipeline_mode=single),
            pl.BlockSpec((1, D_MODEL), const2),
            pl.BlockSpec((None, D_MODEL, D_FF), lyr3, pipeline_mode=single),
            pl.BlockSpec((None, D_FF, D_MODEL), lyr3, pipeline_mode=single),
            pl.BlockSpec((1, D_MODEL), const2),
        ],
        out_specs=pl.BlockSpec((None, TOK_BLOCK, D_MODEL), lambda b, j: (b, j, 0)),
        out_shape=jax.ShapeDtypeStruct((bsz, nsteps * TOK_BLOCK, D_MODEL), F32),
        compiler_params=pltpu.CompilerParams(
            dimension_semantics=("arbitrary", "arbitrary"), vmem_limit_bytes=VMEM_LIMIT),
        name="outproj_mlp",
    )(xs, modsel, gla_f, gla_b, gin, att_o, y_tm, u_tm, gnw, hm, s5d, gluw, glub, wo, n2w, w1, w2, fnw)


def _rope_tables(tt):
    n_lat = tt - CTX_LEN
    rows = n_lat // GRID_W
    row = jnp.repeat(jnp.arange(rows, dtype=F32), GRID_W)
    col = jnp.tile(jnp.arange(GRID_W, dtype=F32), rows)
    n_freq = ATT_HD // 4
    inv_freq = ROPE_BASE ** (-jnp.arange(n_freq, dtype=F32) / n_freq)
    ang_r = row[:, None] * inv_freq
    ang_c = col[:, None] * inv_freq
    ang = jnp.concatenate([ang_r, ang_r, ang_c, ang_c], axis=-1)
    cos = jnp.concatenate([jnp.ones((CTX_LEN, ATT_HD), F32), jnp.cos(ang)], axis=0)
    sin = jnp.concatenate([jnp.zeros((CTX_LEN, ATT_HD), F32), jnp.sin(ang)], axis=0)
    up_quarter = (np.arange(ATT_HD) // 16) % 2 == 0
    sa = jnp.where(up_quarter, -sin, 0.0)
    sb = jnp.where(up_quarter, 0.0, sin)
    two = lambda t: jnp.concatenate([t, t], axis=-1)
    return two(cos), two(sa), two(sb)


def _pad_cols(w, width):
    return jnp.pad(w, ((0, 0), (0, 0), (0, width - w.shape[-1])))


def _layout_w_in(w_in):
    offs = np.cumsum([0, GLA_KW, GLA_KW, GLA_WIDTH, GLA_WIDTH, GLA_RANK, GLA_RANK, ATT_WIDTH, ATT_KVW, ATT_KVW,
                      S5_WIDTH])
    q, k, v, g, zf, zb, aq, ak, av, u = [w_in[:, :, offs[i]:offs[i + 1]] for i in range(10)]
    z = jnp.concatenate([zf, zb], axis=-1)
    cols = [v, g, _pad_cols(q, GLA_KPAD), _pad_cols(k, GLA_KPAD), _pad_cols(z, 128), aq, ak, av, u]
    return jnp.concatenate(cols, axis=-1).astype(BF16)


def _block_diag(t):
    s, g, a, b = t.shape
    eye = jnp.eye(g, dtype=t.dtype)
    return jnp.einsum('sgab,gk->sgakb', t, eye).reshape(s, g * a, g * b)


def kernel(x, c, ctx, c_ctx, w_mod, b_mod, norm1_w, norm2_w, w_in, gla_wa_f, gla_ba_f, gla_wa_b, gla_ba_b,
           gla_norm_w, attn_sink, s5_lam_re_f, s5_lam_im_f, s5_log_step_f, s5_b_re_f, s5_b_im_f, s5_c_re_f,
           s5_c_im_f, s5_lam_re_b, s5_lam_im_b, s5_log_step_b, s5_b_re_b, s5_b_im_b, s5_c_re_b, s5_c_im_b,
           s5_d, glu_w, glu_b, w_out, mlp_w1, mlp_w2, final_norm_w):
    bsz, seq, _ = x.shape
    n_layers = w_mod.shape[0]
    tt = CTX_LEN + seq
    assert bsz % 8 == 0 and seq % TOK_BLOCK == 0 and seq >= 3 * ATT_BLOCK

    xs = jnp.concatenate([ctx, x], axis=1)

    mod_rows = -(-(bsz + 1) // 8) * 8
    cvec = jnp.zeros((mod_rows, D_MODEL), F32).at[:bsz].set(c).at[bsz].set(c_ctx)
    mod = _modulation(cvec, w_mod, b_mod)
    mod_ctx = jnp.broadcast_to(mod[:, bsz][:, None], (n_layers, bsz, 6 * D_MODEL))
    modsel = jnp.stack([mod_ctx, mod[:, :bsz]], axis=2)

    w_in_p = _layout_w_in(w_in)
    wa_cat = jnp.zeros((n_layers, 128, 2 * GLA_KPAD), F32)
    wa_cat = wa_cat.at[:, 0:GLA_RANK, 0:GLA_KW].set(gla_wa_f)
    wa_cat = wa_cat.at[:, GLA_RANK:2 * GLA_RANK, GLA_KPAD:GLA_KPAD + GLA_KW].set(gla_wa_b).astype(BF16)
    ba_cat = jnp.zeros((n_layers, 1, 2 * GLA_KPAD), F32)
    ba_cat = ba_cat.at[:, 0, 0:GLA_KW].set(gla_ba_f).at[:, 0, GLA_KPAD:GLA_KPAD + GLA_KW].set(gla_ba_b)
    qscale = jnp.ones((1, GI_LF), F32).at[:, GI_Q:GI_Q + GLA_KW].set(GLA_DK ** -0.5)
    cos_t, sa_t, sb_t = _rope_tables(tt)
    gnw = jnp.tile(gla_norm_w, (1, GLA_HEADS))[:, None, :]
    head = np.arange(GLA_WIDTH) // GLA_DV
    hm = jnp.asarray(head[:, None] == head[None, :], BF16)
    wo_b = w_out.astype(BF16)
    w1_b = mlp_w1.astype(BF16)
    w2_b = mlp_w2.astype(BF16)
    gluw_b = glu_w.astype(BF16)

    def slots(f, b):
        return jnp.stack([f, b], axis=1).reshape((2 * n_layers,) + f.shape[1:])

    lam_re = slots(s5_lam_re_f, s5_lam_re_b).reshape(-1, 1, S5_NS)
    lam_im = slots(s5_lam_im_f, s5_lam_im_b).reshape(-1, 1, S5_NS)
    log_step = jnp.repeat(slots(s5_log_step_f, s5_log_step_b), S5_STATE, axis=-1).reshape(-1, 1, S5_NS)
    to_hn = lambda t: t.transpose(0, 3, 1, 2).reshape(-1, S5_GROUP, S5_NS)
    avec, bbr, bbi = _s5_prep(lam_re, lam_im, log_step, to_hn(slots(s5_b_re_f, s5_b_re_b)),
                              to_hn(slots(s5_b_im_f, s5_b_im_b)))
    from_hn = lambda t: t.reshape(-1, S5_GROUP, S5_GROUPS, S5_STATE).transpose(0, 2, 1, 3)
    bre = _block_diag(from_hn(bbr)).astype(BF16)
    bim = _block_diag(from_hn(bbi)).astype(BF16)
    to_ph = lambda t: _block_diag(t.transpose(0, 1, 3, 2)).astype(BF16)
    cre = to_ph(slots(s5_c_re_f, s5_c_re_b))
    cim = to_ph(slots(s5_c_im_f, s5_c_im_b))

    for l in range(n_layers):
        final = l == n_layers - 1
        gin, att, u_tm = _inproj(xs, modsel[l], norm1_w[l][None], w_in_p[l], wa_cat[l], ba_cat[l], qscale,
                                 cos_t, sa_t, sb_t)
        gla_f, gla_b = _gla(gin)
        att_o = _swa(att, attn_sink[l])
        y_tm = _s5(u_tm.reshape(tt * bsz, S5_WIDTH), avec, bre, bim, cre, cim, l, bsz)
        xs = _outproj(xs, modsel[l], gla_f, gla_b, gin, att_o, y_tm.reshape(2, tt, bsz * S5_WIDTH), u_tm,
                      gnw[l], hm, s5_d[l][None], gluw_b, glu_b[l][None], wo_b, norm2_w[l][None], w1_b, w2_b,
                      final_norm_w[None], l, final)
    return xs
```

```python
import functools
import math

import jax
import jax.numpy as jnp
import numpy as np
from jax import lax
from jax.experimental import pallas as pl
from jax.experimental.pallas import tpu as pltpu

F32 = jnp.float32
BF16 = jnp.bfloat16

D_MODEL = 1024
D_FF = 4 * D_MODEL
CTX_LEN = 256
GRID_W = 64
EPS = 1e-6
NEG_INF = -1e30

GLA_HEADS = 4
GLA_DV = 96
GLA_DK = 48
GLA_WIDTH = GLA_HEADS * GLA_DV
GLA_KW = GLA_HEADS * GLA_DK
GLA_KPAD = 256
GLA_RANK = 16
GLA_TAU = 16.0
GLA_CHUNK = 64

ATT_HD = 64
ATT_HEADS = 6
ATT_KV_HEADS = 2
ATT_WIDTH = ATT_HEADS * ATT_HD
ATT_KVW = ATT_KV_HEADS * ATT_HD
WINDOW = 128
ATT_BLOCK = 128
ROPE_BASE = 10000.0

S5_WIDTH = 256
S5_GROUP = 16
S5_GROUPS = 16
S5_STATE = 64
S5_NS = S5_GROUPS * S5_STATE
S5_LC = 16
S5_CW = S5_LC * S5_WIDTH

TOK_BLOCK = 256
CH_BLOCK = TOK_BLOCK // S5_LC

C_V, C_G, C_Q, C_K, C_Z, C_AQ, C_AK, C_AV, C_U = 0, 384, 768, 1024, 1280, 1408, 1792, 1920, 2048
IN_PAD = 2304
GI_V, GI_G, GI_Q, GI_K, GI_LF, GI_LB, GI_W = 0, 384, 768, 1024, 1280, 1536, 1792

VMEM_LIMIT = 56 * 1024 * 1024


def _sigmoid(x):
    return 1.0 / (1.0 + jnp.exp(-x))


def _silu(x):
    return x * _sigmoid(x)


def _dot(a, b):
    return jnp.dot(a, b, preferred_element_type=F32)


def _dot_nt(a, b):
    return lax.dot_general(a, b, (((1,), (1,)), ((), ())), preferred_element_type=F32)


def _dot_tn(a, b):
    return lax.dot_general(a, b, (((0,), (0,)), ((), ())), preferred_element_type=F32)


def _split3(x):
    hi = x.astype(BF16)
    r1 = x - hi.astype(F32)
    mid = r1.astype(BF16)
    lo = (r1 - mid.astype(F32)).astype(BF16)
    return hi, mid, lo


def _mod_kernel(c_ref, w_ref, b_ref, o_ref):
    a = _silu(c_ref[...]).astype(BF16)
    o_ref[...] = _dot(a, w_ref[...].astype(BF16)) + b_ref[...]


def _modulation(cvec, w_mod, b_mod):
    n_layers = w_mod.shape[0]
    rows = cvec.shape[0]
    tn = 1536
    return pl.pallas_call(
        _mod_kernel,
        grid=(n_layers, 6 * D_MODEL // tn),
        in_specs=[
            pl.BlockSpec((rows, D_MODEL), lambda l, n: (0, 0)),
            pl.BlockSpec((None, D_MODEL, tn), lambda l, n: (l, 0, n)),
            pl.BlockSpec((None, 1, tn), lambda l, n: (l, 0, n)),
        ],
        out_specs=pl.BlockSpec((None, rows, tn), lambda l, n: (l, 0, n)),
        out_shape=jax.ShapeDtypeStruct((n_layers, rows, 6 * D_MODEL), F32),
        compiler_params=pltpu.CompilerParams(
            dimension_semantics=("arbitrary", "arbitrary"), vmem_limit_bytes=VMEM_LIMIT),
        name="modulation",
    )(cvec, w_mod, b_mod.reshape(n_layers, 1, 6 * D_MODEL))


def _inproj_kernel(x_ref, mod_ref, n1_ref, w_ref, wa_ref, ba_ref, qs_ref, cos_ref, sa_ref, sb_ref,
                   gin_ref, att_ref, uc_ref, us_ref):
    j = pl.program_id(1)
    xf = x_ref[...]
    y = xf * lax.rsqrt(jnp.mean(xf * xf, axis=-1, keepdims=True) + EPS) * n1_ref[...]
    m = mod_ref[...]
    mrow = jnp.where(j > 0, m[1:2, :], m[0:1, :])
    sh1 = mrow[:, 0:D_MODEL]
    sc1 = mrow[:, D_MODEL:2 * D_MODEL]
    h = y * (1.0 + sc1) + sh1
    p = _dot(h.astype(BF16), w_ref[...])

    gin_ref[:, 0:GI_LF] = p[:, 0:GI_LF] * qs_ref[...]
    z = p[:, C_Z:C_Z + 128].astype(BF16)
    zg = _dot(z, wa_ref[...]) + ba_ref[...]
    la = (jnp.minimum(zg, 0.0) - jnp.log1p(jnp.exp(-jnp.abs(zg)))) * (1.0 / GLA_TAU)
    gin_ref[:, GI_LF:GI_W] = la

    cos = cos_ref[...]
    sa = sa_ref[...]
    sb = sb_ref[...]

    def rope(t, reps):
        w = t.shape[-1]
        c3 = jnp.concatenate([cos] * reps, axis=-1) if reps > 1 else cos
        a3 = jnp.concatenate([sa] * reps, axis=-1) if reps > 1 else sa
        b3 = jnp.concatenate([sb] * reps, axis=-1) if reps > 1 else sb
        up = pltpu.roll(t, w - 16, 1)
        dn = pltpu.roll(t, 16, 1)
        return t * c3 + up * a3 + dn * b3

    aq = rope(p[:, C_AQ:C_AQ + ATT_WIDTH], 3) * (ATT_HD ** -0.5)
    ak = rope(p[:, C_AK:C_AK + ATT_KVW], 1)
    av = p[:, C_AV:C_AV + ATT_KVW]
    left = lax.broadcasted_iota(jnp.int32, ak.shape, 1) < ATT_HD

    def expand(t):
        sw = pltpu.roll(t, ATT_HD, 1)
        return [jnp.where(left, t, sw), t, jnp.where(left, sw, t)]

    att = jnp.concatenate([aq] + expand(ak) + expand(av), axis=-1)
    att_ref[...] = att.astype(BF16)
    for half in range(S5_WIDTH // 128):
        us_ref[half] = p[:, C_U + half * 128:C_U + (half + 1) * 128]
    for st in range(S5_LC):
        for half in range(S5_WIDTH // 128):
            lane0 = st * S5_WIDTH + half * 128
            uc_ref[:, lane0:lane0 + 128] = us_ref[half, pl.ds(st, CH_BLOCK, stride=S5_LC), :]


def _inproj(xs, modsel, n1w, w_in_p, wa_cat, ba_cat, qscale, cos_t, sa_t, sb_t):
    bsz, tt, _ = xs.shape
    nb = tt // TOK_BLOCK
    const = lambda b, j: (0, 0)
    return pl.pallas_call(
        _inproj_kernel,
        grid=(bsz, nb),
        in_specs=[
            pl.BlockSpec((None, TOK_BLOCK, D_MODEL), lambda b, j: (b, j, 0)),
            pl.BlockSpec((None, 2, 6 * D_MODEL), lambda b, j: (b, 0, 0)),
            pl.BlockSpec((1, D_MODEL), const),
            pl.BlockSpec((D_MODEL, IN_PAD), const),
            pl.BlockSpec((128, 2 * GLA_KPAD), const),
            pl.BlockSpec((1, 2 * GLA_KPAD), const),
            pl.BlockSpec((1, GI_LF), const),
            pl.BlockSpec((TOK_BLOCK, 128), lambda b, j: (j, 0)),
            pl.BlockSpec((TOK_BLOCK, 128), lambda b, j: (j, 0)),
            pl.BlockSpec((TOK_BLOCK, 128), lambda b, j: (j, 0)),
        ],
        out_specs=[
            pl.BlockSpec((None, TOK_BLOCK, GI_W), lambda b, j: (b, j, 0)),
            pl.BlockSpec((None, TOK_BLOCK, 3 * ATT_WIDTH), lambda b, j: (b, j, 0)),
            pl.BlockSpec((None, CH_BLOCK, S5_CW), lambda b, j: (b, j, 0)),
        ],
        out_shape=[
            jax.ShapeDtypeStruct((bsz, tt, GI_W), F32),
            jax.ShapeDtypeStruct((bsz, tt, 3 * ATT_WIDTH), BF16),
            jax.ShapeDtypeStruct((bsz, tt // S5_LC, S5_CW), F32),
        ],
        scratch_shapes=[pltpu.VMEM((S5_WIDTH // 128, TOK_BLOCK, 128), F32)],
        compiler_params=pltpu.CompilerParams(
            dimension_semantics=("arbitrary", "arbitrary"), vmem_limit_bytes=VMEM_LIMIT),
        name="inproj",
    )(xs, modsel, n1w, w_in_p, wa_cat, ba_cat, qscale, cos_t, sa_t, sb_t)


def _gla_kernel(gf_ref, gb_ref, tri_ref, tril_ref, hm_ref, vm_ref, bd_ref, of_ref, ob_ref, stf_ref, stb_ref):
    j = pl.program_id(1)

    @pl.when(j == 0)
    def _():
        stf_ref[...] = jnp.zeros_like(stf_ref)
        stb_ref[...] = jnp.zeros_like(stb_ref)

    n_chunks = TOK_BLOCK // GLA_CHUNK
    chunk = lambda t, c: t[c * GLA_CHUNK:(c + 1) * GLA_CHUNK]
    dirs = ((gf_ref, GI_LF, 0, GLA_CHUNK - 1, of_ref, stf_ref, list(range(n_chunks))),
            (gb_ref, GI_LB, 1, 0, ob_ref, stb_ref, list(reversed(range(n_chunks)))))

    cum = []
    for g_ref, la_col, d, _, _, _, _ in dirs:
        la = g_ref[:, la_col:la_col + GLA_KPAD]
        hi = la.astype(BF16)
        lo = (la - hi.astype(F32)).astype(BF16)
        cum.append(_dot(tri_ref[d], hi) + _dot(tri_ref[d], lo))

    ops = []
    for (g_ref, _, d, last_row, _, _, _), b in zip(dirs, cum):
        q = g_ref[:, GI_Q:GI_Q + GLA_KPAD]
        k = g_ref[:, GI_K:GI_K + GLA_KPAD]
        vb = g_ref[:, GI_V:GI_V + GLA_WIDTH].astype(BF16)
        bl = [chunk(b, c)[last_row:last_row + 1, :] for c in range(n_chunks)]
        blx = jnp.concatenate([jnp.broadcast_to(t, (GLA_CHUNK, GLA_KPAD)) for t in bl], axis=0)
        qb = (q * jnp.exp(b)).astype(BF16)
        kb = (k * jnp.exp(-b)).astype(BF16)
        kd = (k * jnp.exp(blx - b)).astype(BF16)
        ops.append((qb, kb, kd, vb, bl))

    scores = []
    for (qb, kb, _, _, _), (_, _, d, _, _, _, _) in zip(ops, dirs):
        keep = tril_ref[d] > 0
        per = []
        for c in range(n_chunks):
            kst = jnp.concatenate([chunk(kb, c) * hm_ref[h] for h in range(GLA_HEADS)], axis=0)
            per.append(jnp.where(keep, _dot_nt(chunk(qb, c), kst), 0.0).astype(BF16))
        scores.append(per)

    bdmask = bd_ref[...]
    intra, inc = [], []
    for (_, _, kd, vb, _), per in zip(ops, scores):
        oi, ds = [], []
        for c in range(n_chunks):
            vc = chunk(vb, c)
            vbd = jnp.concatenate([vc * vm_ref[h] for h in range(GLA_HEADS)], axis=0)
            oi.append(_dot(per[c], vbd))
            ds.append(_dot_tn(vc, chunk(kd, c)) * bdmask)
        intra.append(oi)
        inc.append(ds)

    for (_, _, _, _, o_ref, st_ref, order), (qb, _, _, _, bl), oi, ds in zip(dirs, ops, intra, inc):
        st = st_ref[...]
        for c in order:
            o_ref[c * GLA_CHUNK:(c + 1) * GLA_CHUNK, :] = oi[c] + _dot_nt(chunk(qb, c), st.astype(BF16))
            st = st * jnp.exp(bl[c]) + ds[c]
        st_ref[...] = st


def _gla_masks():
    r = np.arange(GLA_CHUNK)
    lower = (r[None, :] <= r[:, None]).astype(np.float32)
    tri1 = np.stack([lower, lower.T])
    n_chunks = TOK_BLOCK // GLA_CHUNK
    tri = np.stack([np.kron(np.eye(n_chunks, dtype=np.float32), t) for t in tri1])
    tril = np.tile(tri1, (1, 1, GLA_HEADS))
    klane = np.arange(GLA_KPAD)
    vlane = np.arange(GLA_WIDTH)
    hm = np.stack([np.broadcast_to((klane // GLA_DK) == h, (GLA_CHUNK, GLA_KPAD)) for h in range(GLA_HEADS)])
    vm = np.stack([np.broadcast_to((vlane // GLA_DV) == h, (GLA_CHUNK, GLA_WIDTH)) for h in range(GLA_HEADS)])
    bd = ((vlane[:, None] // GLA_DV) == (klane[None, :] // GLA_DK)).astype(np.float32)
    return (jnp.asarray(tri, BF16), jnp.asarray(tril, F32), jnp.asarray(hm, BF16), jnp.asarray(vm, BF16),
            jnp.asarray(bd))


def _scan_block(d, j, n_ctx, n_all):
    bwd = jnp.where(j < n_ctx, n_ctx - 1 - j, n_all + n_ctx - 1 - j)
    return jnp.where(d == 0, j, bwd)


def _gla(gin):
    bsz, tt, _ = gin.shape
    nb = tt // TOK_BLOCK
    n_ctx = CTX_LEN // TOK_BLOCK
    tri, tril, hm, vm, bd = _gla_masks()
    const2 = lambda b, j: (0, 0)
    const3 = lambda b, j: (0, 0, 0)
    fwd = lambda b, j: (b, j, 0)
    bwd = lambda b, j: (b, _scan_block(1, j, n_ctx, nb), 0)
    return pl.pallas_call(
        _gla_kernel,
        grid=(bsz, nb),
        in_specs=[
            pl.BlockSpec((None, TOK_BLOCK, GI_W), fwd),
            pl.BlockSpec((None, TOK_BLOCK, GI_W), bwd),
            pl.BlockSpec((2, TOK_BLOCK, TOK_BLOCK), const3),
            pl.BlockSpec((2, GLA_CHUNK, GLA_HEADS * GLA_CHUNK), const3),
            pl.BlockSpec((GLA_HEADS, GLA_CHUNK, GLA_KPAD), const3),
            pl.BlockSpec((GLA_HEADS, GLA_CHUNK, GLA_WIDTH), const3),
            pl.BlockSpec((GLA_WIDTH, GLA_KPAD), const2),
        ],
        out_specs=[
            pl.BlockSpec((None, TOK_BLOCK, GLA_WIDTH), fwd),
            pl.BlockSpec((None, TOK_BLOCK, GLA_WIDTH), bwd),
        ],
        out_shape=[jax.ShapeDtypeStruct((bsz, tt, GLA_WIDTH), F32)] * 2,
        scratch_shapes=[pltpu.VMEM((GLA_WIDTH, GLA_KPAD), F32)] * 2,
        compiler_params=pltpu.CompilerParams(
            dimension_semantics=("arbitrary", "arbitrary"), vmem_limit_bytes=VMEM_LIMIT),
        name="gla",
    )(gin, gin, tri, tril, hm, vm, bd)


def _swa_pair(q2, kw, vw, kc, vc, valid2, sink_col):
    s_c = _dot_nt(q2, kc)
    m = jnp.maximum(jnp.max(s_c, axis=-1, keepdims=True), sink_col)
    if kw is not None:
        s_w = jnp.where(valid2, _dot_nt(q2, kw), NEG_INF)
        m = jnp.maximum(m, jnp.max(s_w, axis=-1, keepdims=True))
    e_c = jnp.exp(s_c - m)
    den = jnp.sum(e_c, axis=-1, keepdims=True) + jnp.exp(sink_col - m)
    pv = _dot(e_c.astype(BF16), vc)
    if kw is not None:
        e_w = jnp.exp(s_w - m)
        den = den + jnp.sum(e_w, axis=-1, keepdims=True)
        pv = pv + _dot(e_w.astype(BF16), vw)
    return pv / den


def _swa_kernel(sink_ref, att_ref, o_ref, *, tt):
    i = pl.program_id(1)
    n_ctx_blocks = CTX_LEN // ATT_BLOCK
    q0 = pl.multiple_of(i * ATT_BLOCK, ATT_BLOCK)
    lane = lax.broadcasted_iota(jnp.int32, (ATT_BLOCK, 2 * ATT_HD), 1)
    left = lane < ATT_HD
    row2 = lax.broadcasted_iota(jnp.int32, (2 * ATT_BLOCK, 1), 0)

    def run(windowed):
        if windowed:
            start = jnp.minimum((i - 1) * ATT_BLOCK, tt - 3 * ATT_BLOCK)
            start = pl.multiple_of(start, ATT_BLOCK)
            qpos = q0 - CTX_LEN + lax.broadcasted_iota(jnp.int32, (2 * ATT_BLOCK, 3 * ATT_BLOCK), 0) % ATT_BLOCK
            kpos = start - CTX_LEN + lax.broadcasted_iota(jnp.int32, (2 * ATT_BLOCK, 3 * ATT_BLOCK), 1)
            valid2 = (jnp.abs(qpos - kpos) <= WINDOW) & (kpos >= 0)
        for p in range(ATT_HEADS // 2):
            qp = att_ref[pl.ds(q0, ATT_BLOCK), p * 128:(p + 1) * 128]
            zero = jnp.zeros_like(qp)
            q2 = jnp.concatenate([jnp.where(left, qp, zero), jnp.where(left, zero, qp)], axis=0)
            kcol = ATT_WIDTH + p * 128
            vcol = 2 * ATT_WIDTH + p * 128
            kc = att_ref[0:CTX_LEN, kcol:kcol + 128]
            vc = att_ref[0:CTX_LEN, vcol:vcol + 128]
            sink_col = jnp.where(row2 < ATT_BLOCK, sink_ref[2 * p], sink_ref[2 * p + 1])
            if windowed:
                kw = att_ref[pl.ds(start, 3 * ATT_BLOCK), kcol:kcol + 128]
                vw = att_ref[pl.ds(start, 3 * ATT_BLOCK), vcol:vcol + 128]
                o2 = _swa_pair(q2, kw, vw, kc, vc, valid2, sink_col)
            else:
                o2 = _swa_pair(q2, None, None, kc, vc, None, sink_col)
            o = jnp.where(left, o2[0:ATT_BLOCK], o2[ATT_BLOCK:2 * ATT_BLOCK])
            o_ref[:, p * 128:(p + 1) * 128] = o.astype(BF16)

    @pl.when(i < n_ctx_blocks)
    def _():
        run(False)

    @pl.when(i >= n_ctx_blocks)
    def _():
        run(True)


def _swa(att, sink):
    bsz, tt, _ = att.shape
    nq = tt // ATT_BLOCK
    return pl.pallas_call(
        functools.partial(_swa_kernel, tt=tt),
        grid=(bsz, nq),
        in_specs=[
            pl.BlockSpec(memory_space=pltpu.SMEM),
            pl.BlockSpec((None, tt, 3 * ATT_WIDTH), lambda b, i: (b, 0, 0)),
        ],
        out_specs=pl.BlockSpec((None, ATT_BLOCK, ATT_WIDTH), lambda b, i: (b, i, 0)),
        out_shape=jax.ShapeDtypeStruct((bsz, tt, ATT_WIDTH), BF16),
        compiler_params=pltpu.CompilerParams(
            dimension_semantics=("arbitrary", "arbitrary"), vmem_limit_bytes=VMEM_LIMIT),
        name="swa",
    )(sink, att)


def _s5_prep_kernel(lr_ref, li_ref, ls_ref, br_ref, bi_ref, cr_ref, ci_ref,
                    a16_ref, mre_ref, mim_ref, rre_ref, rim_ref, k_ref, kd_ref):
    rows = S5_LC * S5_GROUP
    rowg = lax.broadcasted_iota(jnp.int32, (rows, S5_NS), 0) >> 4
    colg = lax.broadcasted_iota(jnp.int32, (rows, S5_NS), 1) >> 6
    same_group = rowg == colg
    kdiag = None
    for d in range(2):
        lr = jnp.minimum(lr_ref[d], -1e-4)
        li = li_ref[d]
        dt = jnp.exp(ls_ref[d])
        mag = jnp.exp(lr * dt)
        ar = mag * jnp.cos(li * dt)
        ai = mag * jnp.sin(li * dt)
        den = lr * lr + li * li
        fr = ((ar - 1.0) * lr + ai * li) / den
        fi = (ai * lr - (ar - 1.0) * li) / den
        br = br_ref[d]
        bi = bi_ref[d]
        bbr = fr * br - fi * bi
        bbi = fr * bi + fi * br
        cr = cr_ref[d]
        ci = ci_ref[d]
        pr = jnp.ones_like(ar)
        pi = jnp.zeros_like(ar)
        cp_re, cp_im = [], []
        for tau in range(S5_LC + 1):
            cpr = cr * pr - ci * pi
            cpi = cr * pi + ci * pr
            if tau < S5_LC:
                mre_ref[d, tau] = pr * bbr - pi * bbi
                mim_ref[d, tau] = pr * bbi + pi * bbr
                cp_re.append(cpr)
                cp_im.append(cpi)
            if tau >= 1:
                rre_ref[d, tau - 1] = cpr
                rim_ref[d, tau - 1] = -cpi
            if tau == S5_LC:
                a16_ref[d, 0:1, :] = pr
                a16_ref[d, 1:2, :] = pi
            pr, pi = pr * ar - pi * ai, pr * ai + pi * ar
        bd_re = jnp.where(same_group, jnp.concatenate([bbr] * S5_GROUPS, axis=0), 0.0)
        bd_im = jnp.where(same_group, jnp.concatenate([bbi] * S5_GROUPS, axis=0), 0.0)
        hp = lax.Precision.HIGHEST
        dims = (((1,), (1,)), ((), ()))
        k = (lax.dot_general(jnp.concatenate(cp_re, axis=0), bd_re, dims, precision=hp, preferred_element_type=F32)
             - lax.dot_general(jnp.concatenate(cp_im, axis=0), bd_im, dims, precision=hp,
                               preferred_element_type=F32))
        k_ref[d] = k
        kdiag = k[0:S5_GROUP] if kdiag is None else kdiag + k[0:S5_GROUP]
    kd_ref[...] = kdiag


def _pair_block_diag(t):
    n, g, a, b = t.shape
    t = t.reshape(n, g // 2, 2, a, b)
    return jnp.einsum('npgab,gk->npgakb', t, jnp.eye(2, dtype=t.dtype)).reshape(n, g // 2, 2 * a, 2 * b)


def _s5_tables(fwd, bwd):
    n_layers = fwd[0].shape[0]
    both = lambda i: jnp.stack([fwd[i], bwd[i]], axis=1)
    lam_re = both(0).reshape(n_layers, 2, 1, S5_NS)
    lam_im = both(1).reshape(n_layers, 2, 1, S5_NS)
    log_step = jnp.repeat(both(2), S5_STATE, axis=-1).reshape(n_layers, 2, 1, S5_NS)
    b_hn = lambda t: t.transpose(0, 1, 4, 2, 3).reshape(n_layers, 2, S5_GROUP, S5_NS)
    c_hn = lambda t: t.transpose(0, 1, 3, 2, 4).reshape(n_layers, 2, S5_GROUP, S5_NS)
    vec = pl.BlockSpec((None, 2, 1, S5_NS), lambda l: (l, 0, 0, 0))
    mat = pl.BlockSpec((None, 2, S5_GROUP, S5_NS), lambda l: (l, 0, 0, 0))
    tab = pl.BlockSpec((None, 2, S5_LC, S5_GROUP, S5_NS), lambda l: (l, 0, 0, 0, 0))
    taps = S5_LC * S5_GROUP
    a16, mre, mim, rre, rim, k, kd = pl.pallas_call(
        _s5_prep_kernel,
        grid=(n_layers,),
        in_specs=[vec, vec, vec, mat, mat, mat, mat],
        out_specs=[
            pl.BlockSpec((None, 2, 2, S5_NS), lambda l: (l, 0, 0, 0)),
            tab, tab, tab, tab,
            pl.BlockSpec((None, 2, taps, S5_WIDTH), lambda l: (l, 0, 0, 0)),
            pl.BlockSpec((None, S5_GROUP, S5_WIDTH), lambda l: (l, 0, 0)),
        ],
        out_shape=[
            jax.ShapeDtypeStruct((n_layers, 2, 2, S5_NS), F32),
            jax.ShapeDtypeStruct((n_layers, 2, S5_LC, S5_GROUP, S5_NS), F32),
            jax.ShapeDtypeStruct((n_layers, 2, S5_LC, S5_GROUP, S5_NS), F32),
            jax.ShapeDtypeStruct((n_layers, 2, S5_LC, S5_GROUP, S5_NS), F32),
            jax.ShapeDtypeStruct((n_layers, 2, S5_LC, S5_GROUP, S5_NS), F32),
            jax.ShapeDtypeStruct((n_layers, 2, taps, S5_WIDTH), F32),
            jax.ShapeDtypeStruct((n_layers, S5_GROUP, S5_WIDTH), F32),
        ],
        compiler_params=pltpu.CompilerParams(dimension_semantics=("arbitrary",), vmem_limit_bytes=VMEM_LIMIT),
        name="s5_prep",
    )(lam_re, lam_im, log_step, b_hn(both(3)), b_hn(both(4)), c_hn(both(5)), c_hn(both(6)))

    def inc(m, flip):
        m = m.reshape(n_layers, S5_LC, S5_GROUP, S5_GROUPS, S5_STATE)
        m = m[:, ::-1] if flip else m
        m = m.transpose(0, 3, 1, 2, 4).reshape(n_layers, S5_GROUPS, taps, S5_STATE)
        return _pair_block_diag(m)

    m_f = jnp.concatenate([inc(mre[:, 0], True), inc(mim[:, 0], True)], axis=-1).astype(BF16)
    m_b = jnp.concatenate([inc(mre[:, 1], False), inc(mim[:, 1], False)], axis=-1).astype(BF16)

    def car(r, flip):
        r = r.reshape(n_layers, S5_LC, S5_GROUP, S5_GROUPS, S5_STATE)
        r = r[:, ::-1] if flip else r
        r = r.transpose(0, 3, 4, 1, 2).reshape(n_layers, S5_GROUPS, S5_STATE, taps)
        return _pair_block_diag(r)

    r_all = jnp.concatenate([car(rre[:, 0], False), car(rim[:, 0], False),
                             car(rre[:, 1], True), car(rim[:, 1], True)], axis=2).astype(BF16)

    k = k.reshape(n_layers, 2, S5_LC, S5_GROUP, S5_GROUPS, S5_GROUP)
    step = np.arange(S5_LC)
    lag = step[None, :] - step[:, None]
    t_f = k[:, 0][:, np.clip(lag, 0, S5_LC - 1)]
    t_b = k[:, 1][:, np.clip(-lag, 0, S5_LC - 1)]
    kd = kd.reshape(n_layers, 1, 1, S5_GROUP, S5_GROUPS, S5_GROUP)
    sel = lambda m: jnp.asarray(m)[None, :, :, None, None, None]
    toe = jnp.where(sel(lag > 0), t_f, jnp.where(sel(lag < 0), t_b, kd))
    toe = toe.transpose(0, 4, 1, 5, 2, 3).reshape(n_layers, S5_GROUPS, taps, taps).astype(BF16)
    return a16, m_f, m_b, r_all, toe


def _lane_block_transpose(cols):
    lane = lax.broadcasted_iota(jnp.int32, cols[0].shape, 1)
    out = [None] * 32
    for ah in range(2):
        for bh in range(2):
            v = [cols[(ah * 8 + al) * 2 + bh] for al in range(8)]
            for kbit in range(3):
                width = 16 << kbit
                low = ((lane >> (4 + kbit)) & 1) == 0
                nxt = list(v)
                for i in range(8):
                    if i & (1 << kbit):
                        continue
                    lo_v, hi_v = v[i], v[i | (1 << kbit)]
                    nxt[i] = jnp.where(low, lo_v, pltpu.roll(hi_v, width, 1))
                    nxt[i | (1 << kbit)] = jnp.where(low, pltpu.roll(lo_v, 128 - width, 1), hi_v)
                v = nxt
            for bl in range(8):
                out[(bh * 8 + bl) * 2 + ah] = v[bl]
    return out


def _s5_increments(ush, m_ref, d_re, d_im):
    for p in range(S5_GROUPS // 2):
        dp = _dot(ush[:, p * 512:(p + 1) * 512], m_ref[p])
        d_re[p] = dp[:, 0:128]
        d_im[p] = dp[:, 128:256]


def _s5_recurrence(a16_ref, d_re, d_im, x_re, x_im, sr_ref, si_ref, order, bsz):
    ar = a16_ref[0:1, :]
    ai = a16_ref[1:2, :]
    sr = sr_ref[...]
    si = si_ref[...]
    n_slabs = S5_NS // 128
    gather = lambda ref, rows: jnp.concatenate([ref[p, rows, :] for p in range(n_slabs)], axis=1)
    for c in order:
        rows = pl.ds(c, bsz, stride=CH_BLOCK)
        for p in range(n_slabs):
            x_re[p, rows, :] = sr[:, p * 128:(p + 1) * 128]
            x_im[p, rows, :] = si[:, p * 128:(p + 1) * 128]
        sr, si = ar * sr - ai * si + gather(d_re, rows), ar * si + ai * sr + gather(d_im, rows)
    sr_ref[...] = sr
    si_ref[...] = si


def _s5_fwd_kernel(uc_ref, a16_ref, m_ref, ush_ref, xin_ref, d_re, d_im, x_re, x_im, sr_ref, si_ref, *, bsz):
    @pl.when(pl.program_id(0) == 0)
    def _():
        sr_ref[...] = jnp.zeros_like(sr_ref)
        si_ref[...] = jnp.zeros_like(si_ref)

    rows = bsz * CH_BLOCK
    ub = uc_ref[...].reshape(rows, S5_CW).astype(BF16)
    packed = pltpu.bitcast(ub, jnp.uint32)
    cols = _lane_block_transpose([packed[:, v * 128:(v + 1) * 128] for v in range(32)])
    ush = pltpu.bitcast(jnp.concatenate(cols, axis=1), BF16)
    ush_ref[...] = ush.reshape(bsz, CH_BLOCK, S5_CW)
    _s5_increments(ush, m_ref, d_re, d_im)
    _s5_recurrence(a16_ref, d_re, d_im, x_re, x_im, sr_ref, si_ref, range(CH_BLOCK), bsz)
    for p in range(S5_NS // 128):
        xin_ref[:, :, p * 128:(p + 1) * 128] = x_re[p].astype(BF16).reshape(bsz, CH_BLOCK, 128)
        xin_ref[:, :, S5_NS + p * 128:S5_NS + (p + 1) * 128] = x_im[p].astype(BF16).reshape(bsz, CH_BLOCK, 128)


def _s5_bwd_kernel(ush_ref, xf_ref, a16_ref, m_ref, toe_ref, r_ref, y_ref, d_re, d_im, x_re, x_im, sr_ref, si_ref,
                   *, bsz):
    @pl.when(pl.program_id(0) == 0)
    def _():
        sr_ref[...] = jnp.zeros_like(sr_ref)
        si_ref[...] = jnp.zeros_like(si_ref)

    rows = bsz * CH_BLOCK
    ush = ush_ref[...].reshape(rows, S5_CW)
    _s5_increments(ush, m_ref, d_re, d_im)
    _s5_recurrence(a16_ref, d_re, d_im, x_re, x_im, sr_ref, si_ref, reversed(range(CH_BLOCK)), bsz)
    xf = xf_ref[...].reshape(rows, 2 * S5_NS)
    ycols = []
    for p in range(S5_GROUPS // 2):
        lanes = slice(p * 128, (p + 1) * 128)
        xcat = jnp.concatenate([xf[:, lanes], xf[:, S5_NS + p * 128:S5_NS + (p + 1) * 128],
                                x_re[p].astype(BF16), x_im[p].astype(BF16)], axis=1)
        carry = _dot(xcat, r_ref[p])
        for g2 in range(2):
            g = 2 * p + g2
            yg = carry[:, g2 * 256:(g2 + 1) * 256] + _dot(ush[:, g * 256:(g + 1) * 256], toe_ref[g])
            ycols += [yg[:, 0:128], yg[:, 128:256]]
    ycols = _lane_block_transpose(ycols)
    y_ref[...] = jnp.concatenate(ycols, axis=1).reshape(bsz, CH_BLOCK, S5_CW)


def _s5(u_c, tables, layer):
    a16, m_f, m_b, r_all, toe = tables
    bsz, n_rows, _ = u_c.shape
    nb = n_rows // CH_BLOCK
    n_ctx = CTX_LEN // TOK_BLOCK
    rows = bsz * CH_BLOCK
    blk3 = lambda w: (bsz, CH_BLOCK, w)
    scratch = [pltpu.VMEM((S5_NS // 128, rows, 128), F32)] * 4 + [pltpu.VMEM((bsz, S5_NS), F32)] * 2
    params = pltpu.CompilerParams(dimension_semantics=("arbitrary",), vmem_limit_bytes=VMEM_LIMIT)
    ush, xin_f = pl.pallas_call(
        functools.partial(_s5_fwd_kernel, bsz=bsz),
        grid=(nb,),
        in_specs=[
            pl.BlockSpec(blk3(S5_CW), lambda j: (0, j, 0)),
            pl.BlockSpec((None, None, 2, S5_NS), lambda j: (layer, 0, 0, 0)),
            pl.BlockSpec((None, S5_GROUPS // 2, 512, 256), lambda j: (layer, 0, 0, 0)),
        ],
        out_specs=[
            pl.BlockSpec(blk3(S5_CW), lambda j: (0, j, 0)),
            pl.BlockSpec(blk3(2 * S5_NS), lambda j: (0, j, 0)),
        ],
        out_shape=[
            jax.ShapeDtypeStruct((bsz, n_rows, S5_CW), BF16),
            jax.ShapeDtypeStruct((bsz, n_rows, 2 * S5_NS), BF16),
        ],
        scratch_shapes=scratch,
        compiler_params=params,
        name="s5_fwd",
    )(u_c, a16, m_f)
    back = lambda j: (0, _scan_block(1, j, n_ctx, nb), 0)
    return pl.pallas_call(
        functools.partial(_s5_bwd_kernel, bsz=bsz),
        grid=(nb,),
        in_specs=[
            pl.BlockSpec(blk3(S5_CW), back),
            pl.BlockSpec(blk3(2 * S5_NS), back),
            pl.BlockSpec((None, None, 2, S5_NS), lambda j: (layer, 1, 0, 0)),
            pl.BlockSpec((None, S5_GROUPS // 2, 512, 256), lambda j: (layer, 0, 0, 0)),
            pl.BlockSpec((None, S5_GROUPS, 256, 256), lambda j: (layer, 0, 0, 0)),
            pl.BlockSpec((None, S5_GROUPS // 2, 512, 512), lambda j: (layer, 0, 0, 0)),
        ],
        out_specs=pl.BlockSpec(blk3(S5_CW), back),
        out_shape=jax.ShapeDtypeStruct((bsz, n_rows, S5_CW), F32),
        scratch_shapes=scratch,
        compiler_params=params,
        name="s5_bwd",
    )(ush, xin_f, a16, m_b, toe, r_all)


def _out_kernel(x_ref, mod_ref, gof_ref, gob_ref, g_ref, att_ref, y_ref, uc_ref, gnw_ref, hm_ref, d_ref, gluw_ref,
                glub_ref, wo_ref, n2_ref, w1_ref, w2_ref, fn_ref, o_ref, ys_ref, *, first_block, final):
    j = pl.program_id(1) + first_block
    m = mod_ref[...]
    mrow = jnp.where(j > 0, m[1:2, :], m[0:1, :])
    g1 = mrow[:, 2 * D_MODEL:3 * D_MODEL]
    sh2 = mrow[:, 3 * D_MODEL:4 * D_MODEL]
    sc2 = mrow[:, 4 * D_MODEL:5 * D_MODEL]
    g2 = mrow[:, 5 * D_MODEL:6 * D_MODEL]

    o = gof_ref[...] + gob_ref[...]
    o2 = o * o
    hi = o2.astype(BF16)
    lo = (o2 - hi.astype(F32)).astype(BF16)
    ms = (_dot(hi, hm_ref[...]) + _dot(lo, hm_ref[...])) * (1.0 / GLA_DV)
    a = o * lax.rsqrt(ms + EPS) * gnw_ref[...] * _silu(g_ref[...])

    yc = y_ref[...] + d_ref[...] * uc_ref[...]
    for st in range(S5_LC):
        for half in range(S5_WIDTH // 128):
            lane0 = st * S5_WIDTH + half * 128
            ys_ref[half, pl.ds(st, CH_BLOCK, stride=S5_LC), :] = yc[:, lane0:lane0 + 128]
    yy = jnp.concatenate([ys_ref[half] for half in range(S5_WIDTH // 128)], axis=1)
    ge = 0.5 * yy * (1.0 + jnp.tanh(math.sqrt(2.0 / math.pi) * (yy + 0.044715 * (yy * yy * yy))))
    z = _dot(ge.astype(BF16), gluw_ref[...]) + glub_ref[...]
    s = z[:, 0:S5_WIDTH] * _sigmoid(z[:, S5_WIDTH:2 * S5_WIDTH])

    proj = (_dot(a.astype(BF16), wo_ref[0:GLA_WIDTH, :])
            + _dot(att_ref[...], wo_ref[GLA_WIDTH:GLA_WIDTH + ATT_WIDTH, :])
            + _dot(s.astype(BF16), wo_ref[GLA_WIDTH + ATT_WIDTH:D_MODEL, :]))
    x1 = x_ref[...] + g1 * proj
    y2 = x1 * lax.rsqrt(jnp.mean(x1 * x1, axis=-1, keepdims=True) + EPS) * n2_ref[...]
    h2 = y2 * (1.0 + sc2) + sh2
    hid = jnp.maximum(_dot(h2.astype(BF16), w1_ref[...]), 0.0)
    mlp = _dot((hid * hid).astype(BF16), w2_ref[...])
    x2 = x1 + g2 * mlp
    if final:
        x2 = x2 * lax.rsqrt(jnp.mean(x2 * x2, axis=-1, keepdims=True) + EPS) * fn_ref[...]
    o_ref[...] = x2


def _outproj(xs, modsel, gla_f, gla_b, gin, att_o, y_c, u_c, gnw, hm, s5d, gluw, glub, wo, n2w, w1, w2, fnw,
             layer, final):
    bsz, tt, _ = xs.shape
    nb = tt // TOK_BLOCK
    first = CTX_LEN // TOK_BLOCK if final else 0
    nsteps = nb - first
    const2 = lambda b, j: (0, 0)
    lyr3 = lambda b, j: (layer, 0, 0)
    single = pl.Buffered(1)
    return pl.pallas_call(
        functools.partial(_out_kernel, first_block=first, final=final),
        grid=(bsz, nsteps),
        in_specs=[
            pl.BlockSpec((None, TOK_BLOCK, D_MODEL), lambda b, j: (b, j + first, 0)),
            pl.BlockSpec((None, 2, 6 * D_MODEL), lambda b, j: (b, 0, 0)),
            pl.BlockSpec((None, TOK_BLOCK, GLA_WIDTH), lambda b, j: (b, j + first, 0)),
            pl.BlockSpec((None, TOK_BLOCK, GLA_WIDTH), lambda b, j: (b, j + first, 0)),
            pl.BlockSpec((None, TOK_BLOCK, GLA_WIDTH), lambda b, j: (b, j + first, GI_G // GLA_WIDTH)),
            pl.BlockSpec((None, TOK_BLOCK, ATT_WIDTH), lambda b, j: (b, j + first, 0)),
            pl.BlockSpec((None, CH_BLOCK, S5_CW), lambda b, j: (b, j + first, 0)),
            pl.BlockSpec((None, CH_BLOCK, S5_CW), lambda b, j: (b, j + first, 0)),
            pl.BlockSpec((1, GLA_WIDTH), const2),
            pl.BlockSpec((GLA_WIDTH, GLA_WIDTH), const2),
            pl.BlockSpec((1, S5_CW), const2),
            pl.BlockSpec((None, S5_WIDTH, 2 * S5_WIDTH), lyr3, pipeline_mode=single),
            pl.BlockSpec((1, 2 * S5_WIDTH), const2),
            pl.BlockSpec((None, D_MODEL, D_MODEL), lyr3, pipeline_mode=single),
            pl.BlockSpec((1, D_MODEL), const2),
            pl.BlockSpec((None, D_MODEL, D_FF), lyr3, pipeline_mode=single),
            pl.BlockSpec((None, D_FF, D_MODEL), lyr3, pipeline_mode=single),
            pl.BlockSpec((1, D_MODEL), const2),
        ],
        out_specs=pl.BlockSpec((None, TOK_BLOCK, D_MODEL), lambda b, j: (b, j, 0)),
        out_shape=jax.ShapeDtypeStruct((bsz, nsteps * TOK_BLOCK, D_MODEL), F32),
        scratch_shapes=[pltpu.VMEM((S5_WIDTH // 128, TOK_BLOCK, 128), F32)],
        compiler_params=pltpu.CompilerParams(
            dimension_semantics=("arbitrary", "arbitrary"), vmem_limit_bytes=VMEM_LIMIT),
        name="outproj_mlp",
    )(xs, modsel, gla_f, gla_b, gin, att_o, y_c, u_c, gnw, hm, s5d, gluw, glub, wo, n2w, w1, w2, fnw)


def _rope_tables(tt):
    n_lat = tt - CTX_LEN
    rows = n_lat // GRID_W
    row = jnp.repeat(jnp.arange(rows, dtype=F32), GRID_W)
    col = jnp.tile(jnp.arange(GRID_W, dtype=F32), rows)
    n_freq = ATT_HD // 4
    inv_freq = ROPE_BASE ** (-jnp.arange(n_freq, dtype=F32) / n_freq)
    ang_r = row[:, None] * inv_freq
    ang_c = col[:, None] * inv_freq
    ang = jnp.concatenate([ang_r, ang_r, ang_c, ang_c], axis=-1)
    cos = jnp.concatenate([jnp.ones((CTX_LEN, ATT_HD), F32), jnp.cos(ang)], axis=0)
    sin = jnp.concatenate([jnp.zeros((CTX_LEN, ATT_HD), F32), jnp.sin(ang)], axis=0)
    up_quarter = (np.arange(ATT_HD) // 16) % 2 == 0
    sa = jnp.where(up_quarter, -sin, 0.0)
    sb = jnp.where(up_quarter, 0.0, sin)
    two = lambda t: jnp.concatenate([t, t], axis=-1)
    return two(cos), two(sa), two(sb)


def _pad_cols(w, width):
    return jnp.pad(w, ((0, 0), (0, 0), (0, width - w.shape[-1])))


def _layout_w_in(w_in):
    offs = np.cumsum([0, GLA_KW, GLA_KW, GLA_WIDTH, GLA_WIDTH, GLA_RANK, GLA_RANK, ATT_WIDTH, ATT_KVW, ATT_KVW,
                      S5_WIDTH])
    q, k, v, g, zf, zb, aq, ak, av, u = [w_in[:, :, offs[i]:offs[i + 1]] for i in range(10)]
    z = jnp.concatenate([zf, zb], axis=-1)
    cols = [v, g, _pad_cols(q, GLA_KPAD), _pad_cols(k, GLA_KPAD), _pad_cols(z, 128), aq, ak, av, u]
    return jnp.concatenate(cols, axis=-1).astype(BF16)


def _block_diag(t):
    s, g, a, b = t.shape
    eye = jnp.eye(g, dtype=t.dtype)
    return jnp.einsum('sgab,gk->sgakb', t, eye).reshape(s, g * a, g * b)


def kernel(x, c, ctx, c_ctx, w_mod, b_mod, norm1_w, norm2_w, w_in, gla_wa_f, gla_ba_f, gla_wa_b, gla_ba_b,
           gla_norm_w, attn_sink, s5_lam_re_f, s5_lam_im_f, s5_log_step_f, s5_b_re_f, s5_b_im_f, s5_c_re_f,
           s5_c_im_f, s5_lam_re_b, s5_lam_im_b, s5_log_step_b, s5_b_re_b, s5_b_im_b, s5_c_re_b, s5_c_im_b,
           s5_d, glu_w, glu_b, w_out, mlp_w1, mlp_w2, final_norm_w):
    bsz, seq, _ = x.shape
    n_layers = w_mod.shape[0]
    tt = CTX_LEN + seq
    assert bsz % 8 == 0 and seq % TOK_BLOCK == 0 and seq >= 3 * ATT_BLOCK

    xs = jnp.concatenate([ctx, x], axis=1)

    mod_rows = -(-(bsz + 1) // 8) * 8
    cvec = jnp.zeros((mod_rows, D_MODEL), F32).at[:bsz].set(c).at[bsz].set(c_ctx)
    mod = _modulation(cvec, w_mod, b_mod)
    mod_ctx = jnp.broadcast_to(mod[:, bsz][:, None], (n_layers, bsz, 6 * D_MODEL))
    modsel = jnp.stack([mod_ctx, mod[:, :bsz]], axis=2)

    w_in_p = _layout_w_in(w_in)
    wa_cat = jnp.zeros((n_layers, 128, 2 * GLA_KPAD), F32)
    wa_cat = wa_cat.at[:, 0:GLA_RANK, 0:GLA_KW].set(gla_wa_f)
    wa_cat = wa_cat.at[:, GLA_RANK:2 * GLA_RANK, GLA_KPAD:GLA_KPAD + GLA_KW].set(gla_wa_b).astype(BF16)
    ba_cat = jnp.zeros((n_layers, 1, 2 * GLA_KPAD), F32)
    ba_cat = ba_cat.at[:, 0, 0:GLA_KW].set(gla_ba_f).at[:, 0, GLA_KPAD:GLA_KPAD + GLA_KW].set(gla_ba_b)
    qscale = jnp.ones((1, GI_LF), F32).at[:, GI_Q:GI_Q + GLA_KW].set(GLA_DK ** -0.5)
    cos_t, sa_t, sb_t = _rope_tables(tt)
    gnw = jnp.tile(gla_norm_w, (1, GLA_HEADS))[:, None, :]
    head = np.arange(GLA_WIDTH) // GLA_DV
    hm = jnp.asarray(head[:, None] == head[None, :], BF16)
    wo_b = w_out.astype(BF16)
    w1_b = mlp_w1.astype(BF16)
    w2_b = mlp_w2.astype(BF16)
    gluw_b = glu_w.astype(BF16)

    s5_tab = _s5_tables(
        (s5_lam_re_f, s5_lam_im_f, s5_log_step_f, s5_b_re_f, s5_b_im_f, s5_c_re_f, s5_c_im_f),
        (s5_lam_re_b, s5_lam_im_b, s5_log_step_b, s5_b_re_b, s5_b_im_b, s5_c_re_b, s5_c_im_b))
    s5_dt = jnp.tile(s5_d, (1, S5_LC))[:, None, :]

    for l in range(n_layers):
        final = l == n_layers - 1
        gin, att, u_c = _inproj(xs, modsel[l], norm1_w[l][None], w_in_p[l], wa_cat[l], ba_cat[l], qscale,
                                cos_t, sa_t, sb_t)
        gla_f, gla_b = _gla(gin)
        att_o = _swa(att, attn_sink[l])
        y_c = _s5(u_c, s5_tab, l)
        xs = _outproj(xs, modsel[l], gla_f, gla_b, gin, att_o, y_c, u_c,
                      gnw[l], hm, s5_dt[l], gluw_b, glu_b[l][None], wo_b, norm2_w[l][None], w1_b, w2_b,
                      final_norm_w[None], l, final)
    return xs
```

```python
import functools
import math

import jax
import jax.numpy as jnp
import numpy as np
from jax import lax
from jax.experimental import pallas as pl
from jax.experimental.pallas import tpu as pltpu

F32 = jnp.float32
BF16 = jnp.bfloat16

D_MODEL = 1024
D_FF = 4 * D_MODEL
CTX_LEN = 256
GRID_W = 64
EPS = 1e-6
NEG_INF = -1e30

GLA_HEADS = 4
GLA_DV = 96
GLA_DK = 48
GLA_WIDTH = GLA_HEADS * GLA_DV
GLA_KW = GLA_HEADS * GLA_DK
GLA_KPAD = 256
GLA_RANK = 16
GLA_TAU = 16.0
GLA_CHUNK = 64

ATT_HD = 64
ATT_HEADS = 6
ATT_KV_HEADS = 2
ATT_WIDTH = ATT_HEADS * ATT_HD
ATT_KVW = ATT_KV_HEADS * ATT_HD
WINDOW = 128
ATT_BLOCK = 128
ROPE_BASE = 10000.0

S5_WIDTH = 256
S5_GROUP = 16
S5_GROUPS = 16
S5_STATE = 64
S5_NS = S5_GROUPS * S5_STATE
S5_LC = 16
S5_CW = S5_LC * S5_WIDTH

TOK_BLOCK = 256
CH_BLOCK = TOK_BLOCK // S5_LC

C_V, C_G, C_Q, C_K, C_Z, C_AQ, C_AK, C_AV, C_U = 0, 384, 768, 1024, 1280, 1408, 1792, 1920, 2048
IN_PAD = 2304
C_QK_END = C_K + GLA_KPAD

VMEM_LIMIT = 56 * 1024 * 1024


def _sigmoid(x):
    return 1.0 / (1.0 + jnp.exp(-x))


def _silu(x):
    return x * _sigmoid(x)


def _dot(a, b):
    return jnp.dot(a, b, preferred_element_type=F32)


def _dot_nt(a, b):
    return lax.dot_general(a, b, (((1,), (1,)), ((), ())), preferred_element_type=F32)


def _dot_tn(a, b):
    return lax.dot_general(a, b, (((0,), (0,)), ((), ())), preferred_element_type=F32)


def _split3(x):
    hi = x.astype(BF16)
    r1 = x - hi.astype(F32)
    mid = r1.astype(BF16)
    lo = (r1 - mid.astype(F32)).astype(BF16)
    return hi, mid, lo


def _mod_kernel(c_ref, w_ref, b_ref, o_ref):
    a = _silu(c_ref[...]).astype(BF16)
    o_ref[...] = _dot(a, w_ref[...].astype(BF16)) + b_ref[...]


def _modulation(cvec, w_mod, b_mod):
    n_layers = w_mod.shape[0]
    rows = cvec.shape[0]
    tn = 1536
    return pl.pallas_call(
        _mod_kernel,
        grid=(n_layers, 6 * D_MODEL // tn),
        in_specs=[
            pl.BlockSpec((rows, D_MODEL), lambda l, n: (0, 0)),
            pl.BlockSpec((None, D_MODEL, tn), lambda l, n: (l, 0, n)),
            pl.BlockSpec((None, 1, tn), lambda l, n: (l, 0, n)),
        ],
        out_specs=pl.BlockSpec((None, rows, tn), lambda l, n: (l, 0, n)),
        out_shape=jax.ShapeDtypeStruct((n_layers, rows, 6 * D_MODEL), F32),
        compiler_params=pltpu.CompilerParams(
            dimension_semantics=("arbitrary", "arbitrary"), vmem_limit_bytes=VMEM_LIMIT),
        name="modulation",
    )(cvec, w_mod, b_mod.reshape(n_layers, 1, 6 * D_MODEL))


def _inproj_kernel(x_ref, mod_ref, n1_ref, w_ref, wa_ref, ba_ref, qs_ref, cos_ref, sa_ref, sb_ref,
                   gv_ref, gg_ref, gqk_ref, gla_ref, att_ref, uc_ref, us_ref):
    j = pl.program_id(1)
    xf = x_ref[...]
    y = xf * lax.rsqrt(jnp.mean(xf * xf, axis=-1, keepdims=True) + EPS) * n1_ref[...]
    m = mod_ref[...]
    mrow = jnp.where(j > 0, m[1:2, :], m[0:1, :])
    sh1 = mrow[:, 0:D_MODEL]
    sc1 = mrow[:, D_MODEL:2 * D_MODEL]
    h = y * (1.0 + sc1) + sh1
    p = _dot(h.astype(BF16), w_ref[...])

    gv_ref[...] = p[:, C_V:C_V + GLA_WIDTH].astype(BF16)
    gg_ref[...] = p[:, C_G:C_G + GLA_WIDTH]
    gqk_ref[...] = p[:, C_Q:C_QK_END] * qs_ref[...]
    z = p[:, C_Z:C_Z + 128].astype(BF16)
    zg = _dot(z, wa_ref[...]) + ba_ref[...]
    la = (jnp.minimum(zg, 0.0) - jnp.log1p(jnp.exp(-jnp.abs(zg)))) * (1.0 / GLA_TAU)
    gla_ref[...] = la

    cos = cos_ref[...]
    sa = sa_ref[...]
    sb = sb_ref[...]

    def rope(t, reps):
        w = t.shape[-1]
        c3 = jnp.concatenate([cos] * reps, axis=-1) if reps > 1 else cos
        a3 = jnp.concatenate([sa] * reps, axis=-1) if reps > 1 else sa
        b3 = jnp.concatenate([sb] * reps, axis=-1) if reps > 1 else sb
        up = pltpu.roll(t, w - 16, 1)
        dn = pltpu.roll(t, 16, 1)
        return t * c3 + up * a3 + dn * b3

    aq = rope(p[:, C_AQ:C_AQ + ATT_WIDTH], 3) * (ATT_HD ** -0.5)
    ak = rope(p[:, C_AK:C_AK + ATT_KVW], 1)
    av = p[:, C_AV:C_AV + ATT_KVW]
    left = lax.broadcasted_iota(jnp.int32, ak.shape, 1) < ATT_HD

    def expand(t):
        sw = pltpu.roll(t, ATT_HD, 1)
        return [jnp.where(left, t, sw), t, jnp.where(left, sw, t)]

    att = jnp.concatenate([aq] + expand(ak) + expand(av), axis=-1)
    att_ref[...] = att.astype(BF16)
    for half in range(S5_WIDTH // 128):
        us_ref[half] = p[:, C_U + half * 128:C_U + (half + 1) * 128]
    for st in range(S5_LC):
        for half in range(S5_WIDTH // 128):
            lane0 = st * S5_WIDTH + half * 128
            uc_ref[:, lane0:lane0 + 128] = us_ref[half, pl.ds(st, CH_BLOCK, stride=S5_LC), :]


def _inproj(xs, modsel, n1w, w_in_p, wa_cat, ba_cat, qscale, cos_t, sa_t, sb_t, layer):
    bsz, tt, _ = xs.shape
    nb = tt // TOK_BLOCK
    const = lambda b, j: (0, 0)
    lyr3 = lambda b, j: (layer, 0, 0)
    return pl.pallas_call(
        _inproj_kernel,
        grid=(bsz, nb),
        in_specs=[
            pl.BlockSpec((None, TOK_BLOCK, D_MODEL), lambda b, j: (b, j, 0)),
            pl.BlockSpec((None, 2, 6 * D_MODEL), lambda b, j: (b, 0, 0)),
            pl.BlockSpec((1, D_MODEL), const),
            pl.BlockSpec((None, D_MODEL, IN_PAD), lyr3),
            pl.BlockSpec((None, 128, 2 * GLA_KPAD), lyr3),
            pl.BlockSpec((1, 2 * GLA_KPAD), const),
            pl.BlockSpec((1, 2 * GLA_KPAD), const),
            pl.BlockSpec((TOK_BLOCK, 128), lambda b, j: (j, 0)),
            pl.BlockSpec((TOK_BLOCK, 128), lambda b, j: (j, 0)),
            pl.BlockSpec((TOK_BLOCK, 128), lambda b, j: (j, 0)),
        ],
        out_specs=[
            pl.BlockSpec((None, TOK_BLOCK, GLA_WIDTH), lambda b, j: (b, j, 0)),
            pl.BlockSpec((None, TOK_BLOCK, GLA_WIDTH), lambda b, j: (b, j, 0)),
            pl.BlockSpec((None, TOK_BLOCK, 2 * GLA_KPAD), lambda b, j: (b, j, 0)),
            pl.BlockSpec((None, TOK_BLOCK, 2 * GLA_KPAD), lambda b, j: (b, j, 0)),
            pl.BlockSpec((None, TOK_BLOCK, 3 * ATT_WIDTH), lambda b, j: (b, j, 0)),
            pl.BlockSpec((None, CH_BLOCK, S5_CW), lambda b, j: (b, j, 0)),
        ],
        out_shape=[
            jax.ShapeDtypeStruct((bsz, tt, GLA_WIDTH), BF16),
            jax.ShapeDtypeStruct((bsz, tt, GLA_WIDTH), F32),
            jax.ShapeDtypeStruct((bsz, tt, 2 * GLA_KPAD), F32),
            jax.ShapeDtypeStruct((bsz, tt, 2 * GLA_KPAD), F32),
            jax.ShapeDtypeStruct((bsz, tt, 3 * ATT_WIDTH), BF16),
            jax.ShapeDtypeStruct((bsz, tt // S5_LC, S5_CW), F32),
        ],
        scratch_shapes=[pltpu.VMEM((S5_WIDTH // 128, TOK_BLOCK, 128), F32)],
        compiler_params=pltpu.CompilerParams(
            dimension_semantics=("arbitrary", "arbitrary"), vmem_limit_bytes=VMEM_LIMIT),
        name="inproj",
    )(xs, modsel, n1w, w_in_p, wa_cat, ba_cat, qscale, cos_t, sa_t, sb_t)


def _gla_kernel(qkf_ref, vf_ref, laf_ref, qkb_ref, vb_ref, lab_ref, tri_ref, tril_ref, hm_ref, vm_ref, bd_ref, of_ref, ob_ref, stf_ref, stb_ref):
    j = pl.program_id(1)

    @pl.when(j == 0)
    def _():
        stf_ref[...] = jnp.zeros_like(stf_ref)
        stb_ref[...] = jnp.zeros_like(stb_ref)

    n_chunks = TOK_BLOCK // GLA_CHUNK
    chunk = lambda t, c: t[c * GLA_CHUNK:(c + 1) * GLA_CHUNK]
    dirs = (((qkf_ref, vf_ref, laf_ref), None, 0, GLA_CHUNK - 1, of_ref, stf_ref, list(range(n_chunks))),
            ((qkb_ref, vb_ref, lab_ref), None, 1, 0, ob_ref, stb_ref, list(reversed(range(n_chunks)))))

    cum = []
    for (_, _, la_ref), _, d, _, _, _, _ in dirs:
        la = la_ref[...]
        hi = la.astype(BF16)
        lo = (la - hi.astype(F32)).astype(BF16)
        cum.append(_dot(tri_ref[d], hi) + _dot(tri_ref[d], lo))

    ops = []
    for ((qk_ref, v_ref, _), _, d, last_row, _, _, _), b in zip(dirs, cum):
        q = qk_ref[:, 0:GLA_KPAD]
        k = qk_ref[:, GLA_KPAD:2 * GLA_KPAD]
        vb = v_ref[...]
        bl = [chunk(b, c)[last_row:last_row + 1, :] for c in range(n_chunks)]
        blx = jnp.concatenate([jnp.broadcast_to(t, (GLA_CHUNK, GLA_KPAD)) for t in bl], axis=0)
        qb = (q * jnp.exp(b)).astype(BF16)
        kb = (k * jnp.exp(-b)).astype(BF16)
        kd = (k * jnp.exp(blx - b)).astype(BF16)
        ops.append((qb, kb, kd, vb, bl))

    scores = []
    for (qb, kb, _, _, _), (_, _, d, _, _, _, _) in zip(ops, dirs):
        keep = tril_ref[d] > 0
        per = []
        for c in range(n_chunks):
            kst = jnp.concatenate([chunk(kb, c) * hm_ref[h] for h in range(GLA_HEADS)], axis=0)
            per.append(jnp.where(keep, _dot_nt(chunk(qb, c), kst), 0.0).astype(BF16))
        scores.append(per)

    bdmask = bd_ref[...]
    intra, inc = [], []
    for (_, _, kd, vb, _), per in zip(ops, scores):
        oi, ds = [], []
        for c in range(n_chunks):
            vc = chunk(vb, c)
            vbd = jnp.concatenate([vc * vm_ref[h] for h in range(GLA_HEADS)], axis=0)
            oi.append(_dot(per[c], vbd))
            ds.append(_dot_tn(vc, chunk(kd, c)) * bdmask)
        intra.append(oi)
        inc.append(ds)

    for (_, _, _, _, o_ref, st_ref, order), (qb, _, _, _, bl), oi, ds in zip(dirs, ops, intra, inc):
        st = st_ref[...]
        for c in order:
            o_ref[c * GLA_CHUNK:(c + 1) * GLA_CHUNK, :] = oi[c] + _dot_nt(chunk(qb, c), st.astype(BF16))
            st = st * jnp.exp(bl[c]) + ds[c]
        st_ref[...] = st


def _gla_masks():
    r = np.arange(GLA_CHUNK)
    lower = (r[None, :] <= r[:, None]).astype(np.float32)
    tri1 = np.stack([lower, lower.T])
    n_chunks = TOK_BLOCK // GLA_CHUNK
    tri = np.stack([np.kron(np.eye(n_chunks, dtype=np.float32), t) for t in tri1])
    tril = np.tile(tri1, (1, 1, GLA_HEADS))
    klane = np.arange(GLA_KPAD)
    vlane = np.arange(GLA_WIDTH)
    hm = np.stack([np.broadcast_to((klane // GLA_DK) == h, (GLA_CHUNK, GLA_KPAD)) for h in range(GLA_HEADS)])
    vm = np.stack([np.broadcast_to((vlane // GLA_DV) == h, (GLA_CHUNK, GLA_WIDTH)) for h in range(GLA_HEADS)])
    bd = ((vlane[:, None] // GLA_DV) == (klane[None, :] // GLA_DK)).astype(np.float32)
    return (jnp.asarray(tri, BF16), jnp.asarray(tril, F32), jnp.asarray(hm, BF16), jnp.asarray(vm, BF16),
            jnp.asarray(bd))


def _scan_block(d, j, n_ctx, n_all):
    bwd = jnp.where(j < n_ctx, n_ctx - 1 - j, n_all + n_ctx - 1 - j)
    return jnp.where(d == 0, j, bwd)


def _gla(gqk, gv, gla):
    bsz, tt, _ = gv.shape
    nb = tt // TOK_BLOCK
    n_ctx = CTX_LEN // TOK_BLOCK
    tri, tril, hm, vm, bd = _gla_masks()
    const2 = lambda b, j: (0, 0)
    const3 = lambda b, j: (0, 0, 0)
    fwd = lambda b, j: (b, j, 0)
    bwd = lambda b, j: (b, _scan_block(1, j, n_ctx, nb), 0)
    bwd_gate = lambda b, j: (b, _scan_block(1, j, n_ctx, nb), 1)
    return pl.pallas_call(
        _gla_kernel,
        grid=(bsz, nb),
        in_specs=[
            pl.BlockSpec((None, TOK_BLOCK, 2 * GLA_KPAD), fwd),
            pl.BlockSpec((None, TOK_BLOCK, GLA_WIDTH), fwd),
            pl.BlockSpec((None, TOK_BLOCK, GLA_KPAD), fwd),
            pl.BlockSpec((None, TOK_BLOCK, 2 * GLA_KPAD), bwd),
            pl.BlockSpec((None, TOK_BLOCK, GLA_WIDTH), bwd),
            pl.BlockSpec((None, TOK_BLOCK, GLA_KPAD), bwd_gate),
            pl.BlockSpec((2, TOK_BLOCK, TOK_BLOCK), const3),
            pl.BlockSpec((2, GLA_CHUNK, GLA_HEADS * GLA_CHUNK), const3),
            pl.BlockSpec((GLA_HEADS, GLA_CHUNK, GLA_KPAD), const3),
            pl.BlockSpec((GLA_HEADS, GLA_CHUNK, GLA_WIDTH), const3),
            pl.BlockSpec((GLA_WIDTH, GLA_KPAD), const2),
        ],
        out_specs=[
            pl.BlockSpec((None, TOK_BLOCK, GLA_WIDTH), fwd),
            pl.BlockSpec((None, TOK_BLOCK, GLA_WIDTH), bwd),
        ],
        out_shape=[jax.ShapeDtypeStruct((bsz, tt, GLA_WIDTH), F32)] * 2,
        scratch_shapes=[pltpu.VMEM((GLA_WIDTH, GLA_KPAD), F32)] * 2,
        compiler_params=pltpu.CompilerParams(
            dimension_semantics=("arbitrary", "arbitrary"), vmem_limit_bytes=VMEM_LIMIT),
        name="gla",
    )(gqk, gv, gla, gqk, gv, gla, tri, tril, hm, vm, bd)


def _swa_kernel(sink_ref, mask_ref, att_ref, o_ref, *, tt):
    j = pl.program_id(1)
    n_sub = TOK_BLOCK // ATT_BLOCK
    n_ctx_blocks = CTX_LEN // TOK_BLOCK
    last_q = tt // ATT_BLOCK - 1
    lane = lax.broadcasted_iota(jnp.int32, (ATT_BLOCK, 2 * ATT_HD), 1)
    left = lane < ATT_HD
    row2 = lax.broadcasted_iota(jnp.int32, (2 * ATT_BLOCK, 1), 0)
    units = [(sub, p) for sub in range(n_sub) for p in range(ATT_HEADS // 2)]

    def run(windowed):
        scores = []
        for sub, p in units:
            q0 = pl.multiple_of(j * TOK_BLOCK + sub * ATT_BLOCK, ATT_BLOCK)
            qp = att_ref[pl.ds(q0, ATT_BLOCK), p * 128:(p + 1) * 128]
            zero = jnp.zeros_like(qp)
            q2 = jnp.concatenate([jnp.where(left, qp, zero), jnp.where(left, zero, qp)], axis=0)
            kcol = ATT_WIDTH + p * 128
            s_c = _dot_nt(q2, att_ref[0:CTX_LEN, kcol:kcol + 128])
            s_w = None
            if windowed:
                qi = j * n_sub + sub
                start = pl.multiple_of(jnp.minimum(q0 - ATT_BLOCK, tt - 3 * ATT_BLOCK), ATT_BLOCK)
                kind = jnp.where(qi == CTX_LEN // ATT_BLOCK, 0, jnp.where(qi == last_q, 2, 1))
                keep = mask_ref[kind] > 0
                s_w = jnp.where(keep, _dot_nt(q2, att_ref[pl.ds(start, 3 * ATT_BLOCK), kcol:kcol + 128]), NEG_INF)
            scores.append((s_c, s_w, start if windowed else None))

        probs = []
        for (sub, p), (s_c, s_w, start) in zip(units, scores):
            sink_col = jnp.where(row2 < ATT_BLOCK, sink_ref[2 * p], sink_ref[2 * p + 1])
            m = jnp.maximum(jnp.max(s_c, axis=-1, keepdims=True), sink_col)
            if windowed:
                m = jnp.maximum(m, jnp.max(s_w, axis=-1, keepdims=True))
            e_c = jnp.exp(s_c - m)
            den = jnp.sum(e_c, axis=-1, keepdims=True) + jnp.exp(sink_col - m)
            e_w = None
            if windowed:
                e_w = jnp.exp(s_w - m)
                den = den + jnp.sum(e_w, axis=-1, keepdims=True)
                e_w = e_w.astype(BF16)
            probs.append((e_c.astype(BF16), e_w, den, start))

        for (sub, p), (e_c, e_w, den, start) in zip(units, probs):
            vcol = 2 * ATT_WIDTH + p * 128
            pv = _dot(e_c, att_ref[0:CTX_LEN, vcol:vcol + 128])
            if windowed:
                pv = pv + _dot(e_w, att_ref[pl.ds(start, 3 * ATT_BLOCK), vcol:vcol + 128])
            o2 = pv / den
            o = jnp.where(left, o2[0:ATT_BLOCK], o2[ATT_BLOCK:2 * ATT_BLOCK])
            o_ref[sub * ATT_BLOCK:(sub + 1) * ATT_BLOCK, p * 128:(p + 1) * 128] = o.astype(BF16)

    @pl.when(j < n_ctx_blocks)
    def _():
        run(False)

    @pl.when(j >= n_ctx_blocks)
    def _():
        run(True)


def _swa_masks():
    r = np.arange(2 * ATT_BLOCK)[:, None] % ATT_BLOCK
    c = np.arange(3 * ATT_BLOCK)[None, :]
    near = lambda delta: np.abs(r - c + delta) <= WINDOW
    first = near(ATT_BLOCK) & (c >= ATT_BLOCK)
    return jnp.asarray(np.stack([first, near(ATT_BLOCK), near(2 * ATT_BLOCK)]), F32)


def _swa(att, sink):
    bsz, tt, _ = att.shape
    nb = tt // TOK_BLOCK
    return pl.pallas_call(
        functools.partial(_swa_kernel, tt=tt),
        grid=(bsz, nb),
        in_specs=[
            pl.BlockSpec(memory_space=pltpu.SMEM),
            pl.BlockSpec((3, 2 * ATT_BLOCK, 3 * ATT_BLOCK), lambda b, j: (0, 0, 0)),
            pl.BlockSpec((None, tt, 3 * ATT_WIDTH), lambda b, j: (b, 0, 0)),
        ],
        out_specs=pl.BlockSpec((None, TOK_BLOCK, ATT_WIDTH), lambda b, j: (b, j, 0)),
        out_shape=jax.ShapeDtypeStruct((bsz, tt, ATT_WIDTH), BF16),
        compiler_params=pltpu.CompilerParams(
            dimension_semantics=("arbitrary", "arbitrary"), vmem_limit_bytes=VMEM_LIMIT),
        name="swa",
    )(sink, _swa_masks(), att)


def _s5_prep_kernel(lr_ref, li_ref, ls_ref, br_ref, bi_ref, cr_ref, ci_ref,
                    a16_ref, mre_ref, mim_ref, rre_ref, rim_ref, k_ref, kd_ref):
    rows = S5_LC * S5_GROUP
    rowg = lax.broadcasted_iota(jnp.int32, (rows, S5_NS), 0) >> 4
    colg = lax.broadcasted_iota(jnp.int32, (rows, S5_NS), 1) >> 6
    same_group = rowg == colg
    kdiag = None
    for d in range(2):
        lr = jnp.minimum(lr_ref[d], -1e-4)
        li = li_ref[d]
        dt = jnp.exp(ls_ref[d])
        mag = jnp.exp(lr * dt)
        ar = mag * jnp.cos(li * dt)
        ai = mag * jnp.sin(li * dt)
        den = lr * lr + li * li
        fr = ((ar - 1.0) * lr + ai * li) / den
        fi = (ai * lr - (ar - 1.0) * li) / den
        br = br_ref[d]
        bi = bi_ref[d]
        bbr = fr * br - fi * bi
        bbi = fr * bi + fi * br
        cr = cr_ref[d]
        ci = ci_ref[d]
        pr = jnp.ones_like(ar)
        pi = jnp.zeros_like(ar)
        cp_re, cp_im = [], []
        for tau in range(S5_LC + 1):
            cpr = cr * pr - ci * pi
            cpi = cr * pi + ci * pr
            if tau < S5_LC:
                mre_ref[d, tau] = pr * bbr - pi * bbi
                mim_ref[d, tau] = pr * bbi + pi * bbr
                cp_re.append(cpr)
                cp_im.append(cpi)
            if tau >= 1:
                rre_ref[d, tau - 1] = cpr
                rim_ref[d, tau - 1] = -cpi
            if tau == S5_LC:
                a16_ref[d, 0:1, :] = pr
                a16_ref[d, 1:2, :] = pi
            pr, pi = pr * ar - pi * ai, pr * ai + pi * ar
        bd_re = jnp.where(same_group, jnp.concatenate([bbr] * S5_GROUPS, axis=0), 0.0)
        bd_im = jnp.where(same_group, jnp.concatenate([bbi] * S5_GROUPS, axis=0), 0.0)
        hp = lax.Precision.HIGHEST
        dims = (((1,), (1,)), ((), ()))
        k = (lax.dot_general(jnp.concatenate(cp_re, axis=0), bd_re, dims, precision=hp, preferred_element_type=F32)
             - lax.dot_general(jnp.concatenate(cp_im, axis=0), bd_im, dims, precision=hp,
                               preferred_element_type=F32))
        k_ref[d] = k
        kdiag = k[0:S5_GROUP] if kdiag is None else kdiag + k[0:S5_GROUP]
    kd_ref[...] = kdiag


def _pair_block_diag(t):
    n, g, a, b = t.shape
    t = t.reshape(n, g // 2, 2, a, b)
    return jnp.einsum('npgab,gk->npgakb', t, jnp.eye(2, dtype=t.dtype)).reshape(n, g // 2, 2 * a, 2 * b)


def _s5_tables(fwd, bwd):
    n_layers = fwd[0].shape[0]
    both = lambda i: jnp.stack([fwd[i], bwd[i]], axis=1)
    lam_re = both(0).reshape(n_layers, 2, 1, S5_NS)
    lam_im = both(1).reshape(n_layers, 2, 1, S5_NS)
    log_step = jnp.repeat(both(2), S5_STATE, axis=-1).reshape(n_layers, 2, 1, S5_NS)
    b_hn = lambda t: t.transpose(0, 1, 4, 2, 3).reshape(n_layers, 2, S5_GROUP, S5_NS)
    c_hn = lambda t: t.transpose(0, 1, 3, 2, 4).reshape(n_layers, 2, S5_GROUP, S5_NS)
    vec = pl.BlockSpec((None, 2, 1, S5_NS), lambda l: (l, 0, 0, 0))
    mat = pl.BlockSpec((None, 2, S5_GROUP, S5_NS), lambda l: (l, 0, 0, 0))
    tab = pl.BlockSpec((None, 2, S5_LC, S5_GROUP, S5_NS), lambda l: (l, 0, 0, 0, 0))
    taps = S5_LC * S5_GROUP
    a16, mre, mim, rre, rim, k, kd = pl.pallas_call(
        _s5_prep_kernel,
        grid=(n_layers,),
        in_specs=[vec, vec, vec, mat, mat, mat, mat],
        out_specs=[
            pl.BlockSpec((None, 2, 2, S5_NS), lambda l: (l, 0, 0, 0)),
            tab, tab, tab, tab,
            pl.BlockSpec((None, 2, taps, S5_WIDTH), lambda l: (l, 0, 0, 0)),
            pl.BlockSpec((None, S5_GROUP, S5_WIDTH), lambda l: (l, 0, 0)),
        ],
        out_shape=[
            jax.ShapeDtypeStruct((n_layers, 2, 2, S5_NS), F32),
            jax.ShapeDtypeStruct((n_layers, 2, S5_LC, S5_GROUP, S5_NS), F32),
            jax.ShapeDtypeStruct((n_layers, 2, S5_LC, S5_GROUP, S5_NS), F32),
            jax.ShapeDtypeStruct((n_layers, 2, S5_LC, S5_GROUP, S5_NS), F32),
            jax.ShapeDtypeStruct((n_layers, 2, S5_LC, S5_GROUP, S5_NS), F32),
            jax.ShapeDtypeStruct((n_layers, 2, taps, S5_WIDTH), F32),
            jax.ShapeDtypeStruct((n_layers, S5_GROUP, S5_WIDTH), F32),
        ],
        compiler_params=pltpu.CompilerParams(dimension_semantics=("arbitrary",), vmem_limit_bytes=VMEM_LIMIT),
        name="s5_prep",
    )(lam_re, lam_im, log_step, b_hn(both(3)), b_hn(both(4)), c_hn(both(5)), c_hn(both(6)))

    def inc(m, flip):
        m = m.reshape(n_layers, S5_LC, S5_GROUP, S5_GROUPS, S5_STATE)
        m = m[:, ::-1] if flip else m
        m = m.transpose(0, 3, 1, 2, 4).reshape(n_layers, S5_GROUPS, taps, S5_STATE)
        return _pair_block_diag(m)

    m_f = jnp.concatenate([inc(mre[:, 0], True), inc(mim[:, 0], True)], axis=-1).astype(BF16)
    m_b = jnp.concatenate([inc(mre[:, 1], False), inc(mim[:, 1], False)], axis=-1).astype(BF16)

    def car(r, flip):
        r = r.reshape(n_layers, S5_LC, S5_GROUP, S5_GROUPS, S5_STATE)
        r = r[:, ::-1] if flip else r
        r = r.transpose(0, 3, 4, 1, 2).reshape(n_layers, S5_GROUPS, S5_STATE, taps)
        return _pair_block_diag(r)

    r_all = jnp.concatenate([car(rre[:, 0], False), car(rim[:, 0], False),
                             car(rre[:, 1], True), car(rim[:, 1], True)], axis=2).astype(BF16)

    k = k.reshape(n_layers, 2, S5_LC, S5_GROUP, S5_GROUPS, S5_GROUP)
    step = np.arange(S5_LC)
    lag = step[None, :] - step[:, None]
    t_f = k[:, 0][:, np.clip(lag, 0, S5_LC - 1)]
    t_b = k[:, 1][:, np.clip(-lag, 0, S5_LC - 1)]
    kd = kd.reshape(n_layers, 1, 1, S5_GROUP, S5_GROUPS, S5_GROUP)
    sel = lambda m: jnp.asarray(m)[None, :, :, None, None, None]
    toe = jnp.where(sel(lag > 0), t_f, jnp.where(sel(lag < 0), t_b, kd))
    toe = toe.transpose(0, 4, 1, 5, 2, 3).reshape(n_layers, S5_GROUPS, taps, taps).astype(BF16)
    return a16, m_f, m_b, r_all, toe


def _lane_block_transpose(cols):
    lane = lax.broadcasted_iota(jnp.int32, cols[0].shape, 1)
    out = [None] * 32
    for ah in range(2):
        for bh in range(2):
            v = [cols[(ah * 8 + al) * 2 + bh] for al in range(8)]
            for kbit in range(3):
                width = 16 << kbit
                low = ((lane >> (4 + kbit)) & 1) == 0
                nxt = list(v)
                for i in range(8):
                    if i & (1 << kbit):
                        continue
                    lo_v, hi_v = v[i], v[i | (1 << kbit)]
                    nxt[i] = jnp.where(low, lo_v, pltpu.roll(hi_v, width, 1))
                    nxt[i | (1 << kbit)] = jnp.where(low, pltpu.roll(lo_v, 128 - width, 1), hi_v)
                v = nxt
            for bl in range(8):
                out[(bh * 8 + bl) * 2 + ah] = v[bl]
    return out


def _s5_increments(ush, m_ref, d_re, d_im):
    for p in range(S5_GROUPS // 2):
        dp = _dot(ush[:, p * 512:(p + 1) * 512], m_ref[p])
        d_re[p] = dp[:, 0:128]
        d_im[p] = dp[:, 128:256]


def _s5_recurrence(a16_ref, d_re, d_im, x_re, x_im, sr_ref, si_ref, order, bsz):
    ar = a16_ref[0:1, :]
    ai = a16_ref[1:2, :]
    sr = sr_ref[...]
    si = si_ref[...]
    n_slabs = S5_NS // 128
    gather = lambda ref, rows: jnp.concatenate([ref[p, rows, :] for p in range(n_slabs)], axis=1)
    for c in order:
        rows = pl.ds(c, bsz, stride=CH_BLOCK)
        for p in range(n_slabs):
            x_re[p, rows, :] = sr[:, p * 128:(p + 1) * 128]
            x_im[p, rows, :] = si[:, p * 128:(p + 1) * 128]
        sr, si = ar * sr - ai * si + gather(d_re, rows), ar * si + ai * sr + gather(d_im, rows)
    sr_ref[...] = sr
    si_ref[...] = si


def _s5_fwd_kernel(uc_ref, a16_ref, m_ref, ush_ref, xin_ref, d_re, d_im, x_re, x_im, sr_ref, si_ref, *, bsz):
    @pl.when(pl.program_id(0) == 0)
    def _():
        sr_ref[...] = jnp.zeros_like(sr_ref)
        si_ref[...] = jnp.zeros_like(si_ref)

    rows = bsz * CH_BLOCK
    ub = uc_ref[...].reshape(rows, S5_CW).astype(BF16)
    packed = pltpu.bitcast(ub, jnp.uint32)
    cols = _lane_block_transpose([packed[:, v * 128:(v + 1) * 128] for v in range(32)])
    ush = pltpu.bitcast(jnp.concatenate(cols, axis=1), BF16)
    ush_ref[...] = ush.reshape(bsz, CH_BLOCK, S5_CW)
    _s5_increments(ush, m_ref, d_re, d_im)
    _s5_recurrence(a16_ref, d_re, d_im, x_re, x_im, sr_ref, si_ref, range(CH_BLOCK), bsz)
    for p in range(S5_NS // 128):
        xin_ref[:, :, p * 128:(p + 1) * 128] = x_re[p].astype(BF16).reshape(bsz, CH_BLOCK, 128)
        xin_ref[:, :, S5_NS + p * 128:S5_NS + (p + 1) * 128] = x_im[p].astype(BF16).reshape(bsz, CH_BLOCK, 128)


def _s5_bwd_kernel(ush_ref, xf_ref, a16_ref, m_ref, toe_ref, r_ref, y_ref, d_re, d_im, x_re, x_im, sr_ref, si_ref,
                   *, bsz):
    @pl.when(pl.program_id(0) == 0)
    def _():
        sr_ref[...] = jnp.zeros_like(sr_ref)
        si_ref[...] = jnp.zeros_like(si_ref)

    rows = bsz * CH_BLOCK
    ush = ush_ref[...].reshape(rows, S5_CW)
    _s5_increments(ush, m_ref, d_re, d_im)
    _s5_recurrence(a16_ref, d_re, d_im, x_re, x_im, sr_ref, si_ref, reversed(range(CH_BLOCK)), bsz)
    xf = xf_ref[...].reshape(rows, 2 * S5_NS)
    ycols = []
    for p in range(S5_GROUPS // 2):
        lanes = slice(p * 128, (p + 1) * 128)
        xcat = jnp.concatenate([xf[:, lanes], xf[:, S5_NS + p * 128:S5_NS + (p + 1) * 128],
                                x_re[p].astype(BF16), x_im[p].astype(BF16)], axis=1)
        carry = _dot(xcat, r_ref[p])
        for g2 in range(2):
            g = 2 * p + g2
            yg = carry[:, g2 * 256:(g2 + 1) * 256] + _dot(ush[:, g * 256:(g + 1) * 256], toe_ref[g])
            ycols += [yg[:, 0:128], yg[:, 128:256]]
    ycols = _lane_block_transpose(ycols)
    y_ref[...] = jnp.concatenate(ycols, axis=1).reshape(bsz, CH_BLOCK, S5_CW)


def _s5(u_c, tables, layer):
    a16, m_f, m_b, r_all, toe = tables
    bsz, n_rows, _ = u_c.shape
    nb = n_rows // CH_BLOCK
    n_ctx = CTX_LEN // TOK_BLOCK
    rows = bsz * CH_BLOCK
    blk3 = lambda w: (bsz, CH_BLOCK, w)
    scratch = [pltpu.VMEM((S5_NS // 128, rows, 128), F32)] * 4 + [pltpu.VMEM((bsz, S5_NS), F32)] * 2
    params = pltpu.CompilerParams(dimension_semantics=("arbitrary",), vmem_limit_bytes=VMEM_LIMIT)
    ush, xin_f = pl.pallas_call(
        functools.partial(_s5_fwd_kernel, bsz=bsz),
        grid=(nb,),
        in_specs=[
            pl.BlockSpec(blk3(S5_CW), lambda j: (0, j, 0)),
            pl.BlockSpec((None, None, 2, S5_NS), lambda j: (layer, 0, 0, 0)),
            pl.BlockSpec((None, S5_GROUPS // 2, 512, 256), lambda j: (layer, 0, 0, 0)),
        ],
        out_specs=[
            pl.BlockSpec(blk3(S5_CW), lambda j: (0, j, 0)),
            pl.BlockSpec(blk3(2 * S5_NS), lambda j: (0, j, 0)),
        ],
        out_shape=[
            jax.ShapeDtypeStruct((bsz, n_rows, S5_CW), BF16),
            jax.ShapeDtypeStruct((bsz, n_rows, 2 * S5_NS), BF16),
        ],
        scratch_shapes=scratch,
        compiler_params=params,
        name="s5_fwd",
    )(u_c, a16, m_f)
    back = lambda j: (0, _scan_block(1, j, n_ctx, nb), 0)
    return pl.pallas_call(
        functools.partial(_s5_bwd_kernel, bsz=bsz),
        grid=(nb,),
        in_specs=[
            pl.BlockSpec(blk3(S5_CW), back),
            pl.BlockSpec(blk3(2 * S5_NS), back),
            pl.BlockSpec((None, None, 2, S5_NS), lambda j: (layer, 1, 0, 0)),
            pl.BlockSpec((None, S5_GROUPS // 2, 512, 256), lambda j: (layer, 0, 0, 0)),
            pl.BlockSpec((None, S5_GROUPS, 256, 256), lambda j: (layer, 0, 0, 0)),
            pl.BlockSpec((None, S5_GROUPS // 2, 512, 512), lambda j: (layer, 0, 0, 0)),
        ],
        out_specs=pl.BlockSpec(blk3(S5_CW), back),
        out_shape=jax.ShapeDtypeStruct((bsz, n_rows, S5_CW), F32),
        scratch_shapes=scratch,
        compiler_params=params,
        name="s5_bwd",
    )(ush, xin_f, a16, m_b, toe, r_all)


def _out_kernel(x_ref, mod_ref, gof_ref, gob_ref, g_ref, att_ref, y_ref, uc_ref, gnw_ref, hm_ref, d_ref, gluw_ref,
                glub_ref, wo_ref, n2_ref, w1_ref, w2_ref, fn_ref, o_ref, ys_ref, *, first_block, final):
    j = pl.program_id(1) + first_block
    m = mod_ref[...]
    mrow = jnp.where(j > 0, m[1:2, :], m[0:1, :])
    g1 = mrow[:, 2 * D_MODEL:3 * D_MODEL]
    sh2 = mrow[:, 3 * D_MODEL:4 * D_MODEL]
    sc2 = mrow[:, 4 * D_MODEL:5 * D_MODEL]
    g2 = mrow[:, 5 * D_MODEL:6 * D_MODEL]

    o = gof_ref[...] + gob_ref[...]
    o2 = o * o
    hi = o2.astype(BF16)
    lo = (o2 - hi.astype(F32)).astype(BF16)
    ms = (_dot(hi, hm_ref[...]) + _dot(lo, hm_ref[...])) * (1.0 / GLA_DV)
    a = o * lax.rsqrt(ms + EPS) * gnw_ref[...] * _silu(g_ref[...])

    yc = y_ref[...] + d_ref[...] * uc_ref[...]
    for st in range(S5_LC):
        for half in range(S5_WIDTH // 128):
            lane0 = st * S5_WIDTH + half * 128
            ys_ref[half, pl.ds(st, CH_BLOCK, stride=S5_LC), :] = yc[:, lane0:lane0 + 128]
    yy = jnp.concatenate([ys_ref[half] for half in range(S5_WIDTH // 128)], axis=1)
    ge = 0.5 * yy * (1.0 + jnp.tanh(math.sqrt(2.0 / math.pi) * (yy + 0.044715 * (yy * yy * yy))))
    z = _dot(ge.astype(BF16), gluw_ref[...]) + glub_ref[...]
    s = z[:, 0:S5_WIDTH] * _sigmoid(z[:, S5_WIDTH:2 * S5_WIDTH])

    proj = (_dot(a.astype(BF16), wo_ref[0:GLA_WIDTH, :])
            + _dot(att_ref[...], wo_ref[GLA_WIDTH:GLA_WIDTH + ATT_WIDTH, :])
            + _dot(s.astype(BF16), wo_ref[GLA_WIDTH + ATT_WIDTH:D_MODEL, :]))
    x1 = x_ref[...] + g1 * proj
    y2 = x1 * lax.rsqrt(jnp.mean(x1 * x1, axis=-1, keepdims=True) + EPS) * n2_ref[...]
    h2 = y2 * (1.0 + sc2) + sh2
    hid = jnp.maximum(_dot(h2.astype(BF16), w1_ref[...]), 0.0)
    mlp = _dot((hid * hid).astype(BF16), w2_ref[...])
    x2 = x1 + g2 * mlp
    if final:
        x2 = x2 * lax.rsqrt(jnp.mean(x2 * x2, axis=-1, keepdims=True) + EPS) * fn_ref[...]
    o_ref[...] = x2


def _outproj(xs, modsel, gla_f, gla_b, gg, att_o, y_c, u_c, gnw, hm, s5d, gluw, glub, wo, n2w, w1, w2, fnw,
             layer, final):
    bsz, tt, _ = xs.shape
    nb = tt // TOK_BLOCK
    first = CTX_LEN // TOK_BLOCK if final else 0
    nsteps = nb - first
    const2 = lambda b, j: (0, 0)
    lyr3 = lambda b, j: (layer, 0, 0)
    single = pl.Buffered(1)
    return pl.pallas_call(
        functools.partial(_out_kernel, first_block=first, final=final),
        grid=(bsz, nsteps),
        in_specs=[
            pl.BlockSpec((None, TOK_BLOCK, D_MODEL), lambda b, j: (b, j + first, 0)),
            pl.BlockSpec((None, 2, 6 * D_MODEL), lambda b, j: (b, 0, 0)),
            pl.BlockSpec((None, TOK_BLOCK, GLA_WIDTH), lambda b, j: (b, j + first, 0)),
            pl.BlockSpec((None, TOK_BLOCK, GLA_WIDTH), lambda b, j: (b, j + first, 0)),
            pl.BlockSpec((None, TOK_BLOCK, GLA_WIDTH), lambda b, j: (b, j + first, 0)),
            pl.BlockSpec((None, TOK_BLOCK, ATT_WIDTH), lambda b, j: (b, j + first, 0)),
            pl.BlockSpec((None, CH_BLOCK, S5_CW), lambda b, j: (b, j + first, 0)),
            pl.BlockSpec((None, CH_BLOCK, S5_CW), lambda b, j: (b, j + first, 0)),
            pl.BlockSpec((1, GLA_WIDTH), const2),
            pl.BlockSpec((GLA_WIDTH, GLA_WIDTH), const2),
            pl.BlockSpec((1, S5_CW), const2),
            pl.BlockSpec((None, S5_WIDTH, 2 * S5_WIDTH), lyr3, pipeline_mode=single),
            pl.BlockSpec((1, 2 * S5_WIDTH), const2),
            pl.BlockSpec((None, D_MODEL, D_MODEL), lyr3, pipeline_mode=single),
            pl.BlockSpec((1, D_MODEL), const2),
            pl.BlockSpec((None, D_MODEL, D_FF), lyr3, pipeline_mode=single),
            pl.BlockSpec((None, D_FF, D_MODEL), lyr3, pipeline_mode=single),
            pl.BlockSpec((1, D_MODEL), const2),
        ],
        out_specs=pl.BlockSpec((None, TOK_BLOCK, D_MODEL), lambda b, j: (b, j, 0)),
        out_shape=jax.ShapeDtypeStruct((bsz, nsteps * TOK_BLOCK, D_MODEL), F32),
        scratch_shapes=[pltpu.VMEM((S5_WIDTH // 128, TOK_BLOCK, 128), F32)],
        compiler_params=pltpu.CompilerParams(
            dimension_semantics=("arbitrary", "arbitrary"), vmem_limit_bytes=VMEM_LIMIT),
        name="outproj_mlp",
    )(xs, modsel, gla_f, gla_b, gg, att_o, y_c, u_c, gnw, hm, s5d, gluw, glub, wo, n2w, w1, w2, fnw)


def _rope_tables(tt):
    n_lat = tt - CTX_LEN
    rows = n_lat // GRID_W
    row = jnp.repeat(jnp.arange(rows, dtype=F32), GRID_W)
    col = jnp.tile(jnp.arange(GRID_W, dtype=F32), rows)
    n_freq = ATT_HD // 4
    inv_freq = ROPE_BASE ** (-jnp.arange(n_freq, dtype=F32) / n_freq)
    ang_r = row[:, None] * inv_freq
    ang_c = col[:, None] * inv_freq
    ang = jnp.concatenate([ang_r, ang_r, ang_c, ang_c], axis=-1)
    cos = jnp.concatenate([jnp.ones((CTX_LEN, ATT_HD), F32), jnp.cos(ang)], axis=0)
    sin = jnp.concatenate([jnp.zeros((CTX_LEN, ATT_HD), F32), jnp.sin(ang)], axis=0)
    up_quarter = (np.arange(ATT_HD) // 16) % 2 == 0
    sa = jnp.where(up_quarter, -sin, 0.0)
    sb = jnp.where(up_quarter, 0.0, sin)
    two = lambda t: jnp.concatenate([t, t], axis=-1)
    return two(cos), two(sa), two(sb)


def _pad_cols(w, width):
    return jnp.pad(w, ((0, 0), (0, 0), (0, width - w.shape[-1])))


def _layout_w_in(w_in):
    offs = np.cumsum([0, GLA_KW, GLA_KW, GLA_WIDTH, GLA_WIDTH, GLA_RANK, GLA_RANK, ATT_WIDTH, ATT_KVW, ATT_KVW,
                      S5_WIDTH])
    q, k, v, g, zf, zb, aq, ak, av, u = [w_in[:, :, offs[i]:offs[i + 1]] for i in range(10)]
    z = jnp.concatenate([zf, zb], axis=-1)
    cols = [v, g, _pad_cols(q, GLA_KPAD), _pad_cols(k, GLA_KPAD), _pad_cols(z, 128), aq, ak, av, u]
    return jnp.concatenate(cols, axis=-1).astype(BF16)


def _block_diag(t):
    s, g, a, b = t.shape
    eye = jnp.eye(g, dtype=t.dtype)
    return jnp.einsum('sgab,gk->sgakb', t, eye).reshape(s, g * a, g * b)


def kernel(x, c, ctx, c_ctx, w_mod, b_mod, norm1_w, norm2_w, w_in, gla_wa_f, gla_ba_f, gla_wa_b, gla_ba_b,
           gla_norm_w, attn_sink, s5_lam_re_f, s5_lam_im_f, s5_log_step_f, s5_b_re_f, s5_b_im_f, s5_c_re_f,
           s5_c_im_f, s5_lam_re_b, s5_lam_im_b, s5_log_step_b, s5_b_re_b, s5_b_im_b, s5_c_re_b, s5_c_im_b,
           s5_d, glu_w, glu_b, w_out, mlp_w1, mlp_w2, final_norm_w):
    bsz, seq, _ = x.shape
    n_layers = w_mod.shape[0]
    tt = CTX_LEN + seq
    assert bsz % 8 == 0 and seq % TOK_BLOCK == 0 and seq >= 2 * TOK_BLOCK

    xs = jnp.concatenate([ctx, x], axis=1)

    mod_rows = -(-(bsz + 1) // 8) * 8
    cvec = jnp.zeros((mod_rows, D_MODEL), F32).at[:bsz].set(c).at[bsz].set(c_ctx)
    mod = _modulation(cvec, w_mod, b_mod)
    mod_ctx = jnp.broadcast_to(mod[:, bsz][:, None], (n_layers, bsz, 6 * D_MODEL))
    modsel = jnp.stack([mod_ctx, mod[:, :bsz]], axis=2)

    w_in_p = _layout_w_in(w_in)
    wa_cat = jnp.zeros((n_layers, 128, 2 * GLA_KPAD), F32)
    wa_cat = wa_cat.at[:, 0:GLA_RANK, 0:GLA_KW].set(gla_wa_f)
    wa_cat = wa_cat.at[:, GLA_RANK:2 * GLA_RANK, GLA_KPAD:GLA_KPAD + GLA_KW].set(gla_wa_b).astype(BF16)
    ba_cat = jnp.zeros((n_layers, 1, 2 * GLA_KPAD), F32)
    ba_cat = ba_cat.at[:, 0, 0:GLA_KW].set(gla_ba_f).at[:, 0, GLA_KPAD:GLA_KPAD + GLA_KW].set(gla_ba_b)
    qscale = jnp.ones((1, 2 * GLA_KPAD), F32).at[:, 0:GLA_KW].set(GLA_DK ** -0.5)
    cos_t, sa_t, sb_t = _rope_tables(tt)
    gnw = jnp.tile(gla_norm_w, (1, GLA_HEADS))[:, None, :]
    head = np.arange(GLA_WIDTH) // GLA_DV
    hm = jnp.asarray(head[:, None] == head[None, :], BF16)
    wo_b = w_out.astype(BF16)
    w1_b = mlp_w1.astype(BF16)
    w2_b = mlp_w2.astype(BF16)
    gluw_b = glu_w.astype(BF16)

    s5_tab = _s5_tables(
        (s5_lam_re_f, s5_lam_im_f, s5_log_step_f, s5_b_re_f, s5_b_im_f, s5_c_re_f, s5_c_im_f),
        (s5_lam_re_b, s5_lam_im_b, s5_log_step_b, s5_b_re_b, s5_b_im_b, s5_c_re_b, s5_c_im_b))
    s5_dt = jnp.tile(s5_d, (1, S5_LC))[:, None, :]

    for l in range(n_layers):
        final = l == n_layers - 1
        gv, gg, gqk, gla, att, u_c = _inproj(xs, modsel[l], norm1_w[l][None], w_in_p, wa_cat, ba_cat[l], qscale,
                                             cos_t, sa_t, sb_t, l)
        gla_f, gla_b = _gla(gqk, gv, gla)
        att_o = _swa(att, attn_sink[l])
        y_c = _s5(u_c, s5_tab, l)
        xs = _outproj(xs, modsel[l], gla_f, gla_b, gg, att_o, y_c, u_c,
                      gnw[l], hm, s5_dt[l], gluw_b, glu_b[l][None], wo_b, norm2_w[l][None], w1_b, w2_b,
                      final_norm_w[None], l, final)
    return xs
```

```python
import functools
import math

import jax
import jax.numpy as jnp
import numpy as np
from jax import lax
from jax.experimental import pallas as pl
from jax.experimental.pallas import tpu as pltpu

F32 = jnp.float32
BF16 = jnp.bfloat16

D_MODEL = 1024
D_FF = 4 * D_MODEL
CTX_LEN = 256
GRID_W = 64
EPS = 1e-6
NEG_INF = -1e30

GLA_HEADS = 4
GLA_DV = 96
GLA_DK = 48
GLA_WIDTH = GLA_HEADS * GLA_DV
GLA_KW = GLA_HEADS * GLA_DK
GLA_KPAD = 256
GLA_RANK = 16
GLA_TAU = 16.0
GLA_CHUNK = 64

ATT_HD = 64
ATT_HEADS = 6
ATT_KV_HEADS = 2
ATT_WIDTH = ATT_HEADS * ATT_HD
ATT_KVW = ATT_KV_HEADS * ATT_HD
WINDOW = 128
ATT_BLOCK = 128
ROPE_BASE = 10000.0

S5_WIDTH = 256
S5_GROUP = 16
S5_GROUPS = 16
S5_STATE = 64
S5_NS = S5_GROUPS * S5_STATE
S5_LC = 16
S5_CW = S5_LC * S5_WIDTH

TOK_BLOCK = 256
CH_BLOCK = TOK_BLOCK // S5_LC

C_V, C_G, C_Q, C_K, C_Z, C_AQ, C_AK, C_AV, C_U = 0, 384, 768, 1024, 1280, 1408, 1792, 1920, 2048
IN_PAD = 2304
C_QK_END = C_K + GLA_KPAD

VMEM_LIMIT = 56 * 1024 * 1024


def _sigmoid(x):
    return 1.0 / (1.0 + jnp.exp(-x))


def _silu(x):
    return x * _sigmoid(x)


def _dot(a, b):
    return jnp.dot(a, b, preferred_element_type=F32)


def _dot_nt(a, b):
    return lax.dot_general(a, b, (((1,), (1,)), ((), ())), preferred_element_type=F32)


def _dot_tn(a, b):
    return lax.dot_general(a, b, (((0,), (0,)), ((), ())), preferred_element_type=F32)


def _split3(x):
    hi = x.astype(BF16)
    r1 = x - hi.astype(F32)
    mid = r1.astype(BF16)
    lo = (r1 - mid.astype(F32)).astype(BF16)
    return hi, mid, lo


def _mod_kernel(c_ref, w_ref, b_ref, o_ref):
    a = _silu(c_ref[...]).astype(BF16)
    o_ref[...] = _dot(a, w_ref[...].astype(BF16)) + b_ref[...]


def _modulation(cvec, w_mod, b_mod):
    n_layers = w_mod.shape[0]
    rows = cvec.shape[0]
    tn = 1536
    return pl.pallas_call(
        _mod_kernel,
        grid=(n_layers, 6 * D_MODEL // tn),
        in_specs=[
            pl.BlockSpec((rows, D_MODEL), lambda l, n: (0, 0)),
            pl.BlockSpec((None, D_MODEL, tn), lambda l, n: (l, 0, n)),
            pl.BlockSpec((None, 1, tn), lambda l, n: (l, 0, n)),
        ],
        out_specs=pl.BlockSpec((None, rows, tn), lambda l, n: (l, 0, n)),
        out_shape=jax.ShapeDtypeStruct((n_layers, rows, 6 * D_MODEL), F32),
        compiler_params=pltpu.CompilerParams(
            dimension_semantics=("arbitrary", "arbitrary"), vmem_limit_bytes=VMEM_LIMIT),
        name="modulation",
    )(cvec, w_mod, b_mod.reshape(n_layers, 1, 6 * D_MODEL))


def _inproj_kernel(xc_ref, x_ref, mod_ref, n1_ref, w_ref, wa_ref, ba_ref, qs_ref, cos_ref, sa_ref, sb_ref,
                   gv_ref, gg_ref, gqk_ref, gla_ref, att_ref, uc_ref, us_ref, *, split):
    j = pl.program_id(1)
    xf = jnp.where(j == 0, xc_ref[...], x_ref[...]) if split else x_ref[...]
    y = xf * lax.rsqrt(jnp.mean(xf * xf, axis=-1, keepdims=True) + EPS) * n1_ref[...]
    m = mod_ref[...]
    mrow = jnp.where(j > 0, m[1:2, :], m[0:1, :])
    sh1 = mrow[:, 0:D_MODEL]
    sc1 = mrow[:, D_MODEL:2 * D_MODEL]
    h = y * (1.0 + sc1) + sh1
    p = _dot(h.astype(BF16), w_ref[...])

    gv_ref[...] = p[:, C_V:C_V + GLA_WIDTH].astype(BF16)
    gg_ref[...] = p[:, C_G:C_G + GLA_WIDTH]
    gqk_ref[...] = p[:, C_Q:C_QK_END] * qs_ref[...]
    z = p[:, C_Z:C_Z + 128].astype(BF16)
    zg = _dot(z, wa_ref[...]) + ba_ref[...]
    la = (jnp.minimum(zg, 0.0) - jnp.log1p(jnp.exp(-jnp.abs(zg)))) * (1.0 / GLA_TAU)
    gla_ref[...] = la

    cos = cos_ref[...]
    sa = sa_ref[...]
    sb = sb_ref[...]

    def rope(t, reps):
        w = t.shape[-1]
        c3 = jnp.concatenate([cos] * reps, axis=-1) if reps > 1 else cos
        a3 = jnp.concatenate([sa] * reps, axis=-1) if reps > 1 else sa
        b3 = jnp.concatenate([sb] * reps, axis=-1) if reps > 1 else sb
        up = pltpu.roll(t, w - 16, 1)
        dn = pltpu.roll(t, 16, 1)
        return t * c3 + up * a3 + dn * b3

    aq = rope(p[:, C_AQ:C_AQ + ATT_WIDTH], 3) * (ATT_HD ** -0.5)
    ak = rope(p[:, C_AK:C_AK + ATT_KVW], 1)
    av = p[:, C_AV:C_AV + ATT_KVW]
    left = lax.broadcasted_iota(jnp.int32, ak.shape, 1) < ATT_HD

    def expand(t):
        sw = pltpu.roll(t, ATT_HD, 1)
        return [jnp.where(left, t, sw), t, jnp.where(left, sw, t)]

    att = jnp.concatenate([aq] + expand(ak) + expand(av), axis=-1)
    att_ref[...] = att.astype(BF16)
    for half in range(S5_WIDTH // 128):
        us_ref[half] = p[:, C_U + half * 128:C_U + (half + 1) * 128]
    for st in range(S5_LC):
        for half in range(S5_WIDTH // 128):
            lane0 = st * S5_WIDTH + half * 128
            uc_ref[:, lane0:lane0 + 128] = us_ref[half, pl.ds(st, CH_BLOCK, stride=S5_LC), :]


def _stream_specs(lat_skip, first=0):
    return [pl.BlockSpec((None, TOK_BLOCK, D_MODEL), lambda b, j: (b, 0, 0)),
            pl.BlockSpec((None, TOK_BLOCK, D_MODEL), lambda b, j: (b, jnp.maximum(j + first - lat_skip, 0), 0))]


def _inproj(x_ctx, x_rest, lat_skip, modsel, n1w, w_in_p, wa_cat, ba_cat, qscale, cos_t, sa_t, sb_t, layer):
    bsz = x_ctx.shape[0]
    tt = cos_t.shape[0]
    nb = tt // TOK_BLOCK
    const = lambda b, j: (0, 0)
    lyr3 = lambda b, j: (layer, 0, 0)
    return pl.pallas_call(
        functools.partial(_inproj_kernel, split=lat_skip > 0),
        grid=(bsz, nb),
        in_specs=_stream_specs(lat_skip) + [
            pl.BlockSpec((None, 2, 6 * D_MODEL), lambda b, j: (b, 0, 0)),
            pl.BlockSpec((1, D_MODEL), const),
            pl.BlockSpec((None, D_MODEL, IN_PAD), lyr3),
            pl.BlockSpec((None, 128, 2 * GLA_KPAD), lyr3),
            pl.BlockSpec((1, 2 * GLA_KPAD), const),
            pl.BlockSpec((1, 2 * GLA_KPAD), const),
            pl.BlockSpec((TOK_BLOCK, 128), lambda b, j: (j, 0)),
            pl.BlockSpec((TOK_BLOCK, 128), lambda b, j: (j, 0)),
            pl.BlockSpec((TOK_BLOCK, 128), lambda b, j: (j, 0)),
        ],
        out_specs=[
            pl.BlockSpec((None, TOK_BLOCK, GLA_WIDTH), lambda b, j: (b, j, 0)),
            pl.BlockSpec((None, TOK_BLOCK, GLA_WIDTH), lambda b, j: (b, j, 0)),
            pl.BlockSpec((None, TOK_BLOCK, 2 * GLA_KPAD), lambda b, j: (b, j, 0)),
            pl.BlockSpec((None, TOK_BLOCK, 2 * GLA_KPAD), lambda b, j: (b, j, 0)),
            pl.BlockSpec((None, TOK_BLOCK, 3 * ATT_WIDTH), lambda b, j: (b, j, 0)),
            pl.BlockSpec((None, CH_BLOCK, S5_CW), lambda b, j: (b, j, 0)),
        ],
        out_shape=[
            jax.ShapeDtypeStruct((bsz, tt, GLA_WIDTH), BF16),
            jax.ShapeDtypeStruct((bsz, tt, GLA_WIDTH), F32),
            jax.ShapeDtypeStruct((bsz, tt, 2 * GLA_KPAD), F32),
            jax.ShapeDtypeStruct((bsz, tt, 2 * GLA_KPAD), F32),
            jax.ShapeDtypeStruct((bsz, tt, 3 * ATT_WIDTH), BF16),
            jax.ShapeDtypeStruct((bsz, tt // S5_LC, S5_CW), F32),
        ],
        scratch_shapes=[pltpu.VMEM((S5_WIDTH // 128, TOK_BLOCK, 128), F32)],
        compiler_params=pltpu.CompilerParams(
            dimension_semantics=("arbitrary", "arbitrary"), vmem_limit_bytes=VMEM_LIMIT),
        name="inproj",
    )(x_ctx, x_rest, modsel, n1w, w_in_p, wa_cat, ba_cat, qscale, cos_t, sa_t, sb_t)


def _gla_kernel(qkf_ref, vf_ref, laf_ref, qkb_ref, vb_ref, lab_ref, tri_ref, tril_ref, hm_ref, vm_ref, bd_ref, of_ref, ob_ref, stf_ref, stb_ref):
    j = pl.program_id(1)

    @pl.when(j == 0)
    def _():
        stf_ref[...] = jnp.zeros_like(stf_ref)
        stb_ref[...] = jnp.zeros_like(stb_ref)

    n_chunks = TOK_BLOCK // GLA_CHUNK
    chunk = lambda t, c: t[c * GLA_CHUNK:(c + 1) * GLA_CHUNK]
    dirs = (((qkf_ref, vf_ref, laf_ref), None, 0, GLA_CHUNK - 1, of_ref, stf_ref, list(range(n_chunks))),
            ((qkb_ref, vb_ref, lab_ref), None, 1, 0, ob_ref, stb_ref, list(reversed(range(n_chunks)))))

    cum = []
    for (_, _, la_ref), _, d, _, _, _, _ in dirs:
        la = la_ref[...]
        hi = la.astype(BF16)
        lo = (la - hi.astype(F32)).astype(BF16)
        cum.append(_dot(tri_ref[d], hi) + _dot(tri_ref[d], lo))

    ops = []
    for ((qk_ref, v_ref, _), _, d, last_row, _, _, _), b in zip(dirs, cum):
        q = qk_ref[:, 0:GLA_KPAD]
        k = qk_ref[:, GLA_KPAD:2 * GLA_KPAD]
        vb = v_ref[...]
        bl = [chunk(b, c)[last_row:last_row + 1, :] for c in range(n_chunks)]
        blx = jnp.concatenate([jnp.broadcast_to(t, (GLA_CHUNK, GLA_KPAD)) for t in bl], axis=0)
        qb = (q * jnp.exp(b)).astype(BF16)
        kb = (k * jnp.exp(-b)).astype(BF16)
        kd = (k * jnp.exp(blx - b)).astype(BF16)
        ops.append((qb, kb, kd, vb, bl))

    scores = []
    for (qb, kb, _, _, _), (_, _, d, _, _, _, _) in zip(ops, dirs):
        keep = tril_ref[d] > 0
        per = []
        for c in range(n_chunks):
            kst = jnp.concatenate([chunk(kb, c) * hm_ref[h] for h in range(GLA_HEADS)], axis=0)
            per.append(jnp.where(keep, _dot_nt(chunk(qb, c), kst), 0.0).astype(BF16))
        scores.append(per)

    bdmask = bd_ref[...]
    intra, inc = [], []
    for (_, _, kd, vb, _), per in zip(ops, scores):
        oi, ds = [], []
        for c in range(n_chunks):
            vc = chunk(vb, c)
            vbd = jnp.concatenate([vc * vm_ref[h] for h in range(GLA_HEADS)], axis=0)
            oi.append(_dot(per[c], vbd))
            ds.append(_dot_tn(vc, chunk(kd, c)) * bdmask)
        intra.append(oi)
        inc.append(ds)

    for (_, _, _, _, o_ref, st_ref, order), (qb, _, _, _, bl), oi, ds in zip(dirs, ops, intra, inc):
        st = st_ref[...]
        for c in order:
            o_ref[c * GLA_CHUNK:(c + 1) * GLA_CHUNK, :] = oi[c] + _dot_nt(chunk(qb, c), st.astype(BF16))
            st = st * jnp.exp(bl[c]) + ds[c]
        st_ref[...] = st


def _gla_masks():
    r = np.arange(GLA_CHUNK)
    lower = (r[None, :] <= r[:, None]).astype(np.float32)
    tri1 = np.stack([lower, lower.T])
    n_chunks = TOK_BLOCK // GLA_CHUNK
    tri = np.stack([np.kron(np.eye(n_chunks, dtype=np.float32), t) for t in tri1])
    tril = np.tile(tri1, (1, 1, GLA_HEADS))
    klane = np.arange(GLA_KPAD)
    vlane = np.arange(GLA_WIDTH)
    hm = np.stack([np.broadcast_to((klane // GLA_DK) == h, (GLA_CHUNK, GLA_KPAD)) for h in range(GLA_HEADS)])
    vm = np.stack([np.broadcast_to((vlane // GLA_DV) == h, (GLA_CHUNK, GLA_WIDTH)) for h in range(GLA_HEADS)])
    bd = ((vlane[:, None] // GLA_DV) == (klane[None, :] // GLA_DK)).astype(np.float32)
    return (jnp.asarray(tri, BF16), jnp.asarray(tril, F32), jnp.asarray(hm, BF16), jnp.asarray(vm, BF16),
            jnp.asarray(bd))


def _scan_block(d, j, n_ctx, n_all):
    bwd = jnp.where(j < n_ctx, n_ctx - 1 - j, n_all + n_ctx - 1 - j)
    return jnp.where(d == 0, j, bwd)


def _gla(gqk, gv, gla):
    bsz, tt, _ = gv.shape
    nb = tt // TOK_BLOCK
    n_ctx = CTX_LEN // TOK_BLOCK
    tri, tril, hm, vm, bd = _gla_masks()
    const2 = lambda b, j: (0, 0)
    const3 = lambda b, j: (0, 0, 0)
    fwd = lambda b, j: (b, j, 0)
    bwd = lambda b, j: (b, _scan_block(1, j, n_ctx, nb), 0)
    bwd_gate = lambda b, j: (b, _scan_block(1, j, n_ctx, nb), 1)
    return pl.pallas_call(
        _gla_kernel,
        grid=(bsz, nb),
        in_specs=[
            pl.BlockSpec((None, TOK_BLOCK, 2 * GLA_KPAD), fwd),
            pl.BlockSpec((None, TOK_BLOCK, GLA_WIDTH), fwd),
            pl.BlockSpec((None, TOK_BLOCK, GLA_KPAD), fwd),
            pl.BlockSpec((None, TOK_BLOCK, 2 * GLA_KPAD), bwd),
            pl.BlockSpec((None, TOK_BLOCK, GLA_WIDTH), bwd),
            pl.BlockSpec((None, TOK_BLOCK, GLA_KPAD), bwd_gate),
            pl.BlockSpec((2, TOK_BLOCK, TOK_BLOCK), const3),
            pl.BlockSpec((2, GLA_CHUNK, GLA_HEADS * GLA_CHUNK), const3),
            pl.BlockSpec((GLA_HEADS, GLA_CHUNK, GLA_KPAD), const3),
            pl.BlockSpec((GLA_HEADS, GLA_CHUNK, GLA_WIDTH), const3),
            pl.BlockSpec((GLA_WIDTH, GLA_KPAD), const2),
        ],
        out_specs=[
            pl.BlockSpec((None, TOK_BLOCK, GLA_WIDTH), fwd),
            pl.BlockSpec((None, TOK_BLOCK, GLA_WIDTH), bwd),
        ],
        out_shape=[jax.ShapeDtypeStruct((bsz, tt, GLA_WIDTH), F32)] * 2,
        scratch_shapes=[pltpu.VMEM((GLA_WIDTH, GLA_KPAD), F32)] * 2,
        compiler_params=pltpu.CompilerParams(
            dimension_semantics=("arbitrary", "arbitrary"), vmem_limit_bytes=VMEM_LIMIT),
        name="gla",
    )(gqk, gv, gla, gqk, gv, gla, tri, tril, hm, vm, bd)


def _swa_kernel(sink_ref, mask_ref, att_ref, o_ref, *, tt):
    j = pl.program_id(1)
    n_sub = TOK_BLOCK // ATT_BLOCK
    n_ctx_blocks = CTX_LEN // TOK_BLOCK
    last_q = tt // ATT_BLOCK - 1
    lane = lax.broadcasted_iota(jnp.int32, (ATT_BLOCK, 2 * ATT_HD), 1)
    left = lane < ATT_HD
    row2 = lax.broadcasted_iota(jnp.int32, (2 * ATT_BLOCK, 1), 0)
    units = [(sub, p) for sub in range(n_sub) for p in range(ATT_HEADS // 2)]

    def run(windowed):
        scores = []
        for sub, p in units:
            q0 = pl.multiple_of(j * TOK_BLOCK + sub * ATT_BLOCK, ATT_BLOCK)
            qp = att_ref[pl.ds(q0, ATT_BLOCK), p * 128:(p + 1) * 128]
            zero = jnp.zeros_like(qp)
            q2 = jnp.concatenate([jnp.where(left, qp, zero), jnp.where(left, zero, qp)], axis=0)
            kcol = ATT_WIDTH + p * 128
            s_c = _dot_nt(q2, att_ref[0:CTX_LEN, kcol:kcol + 128])
            s_w = None
            if windowed:
                qi = j * n_sub + sub
                start = pl.multiple_of(jnp.minimum(q0 - ATT_BLOCK, tt - 3 * ATT_BLOCK), ATT_BLOCK)
                kind = jnp.where(qi == CTX_LEN // ATT_BLOCK, 0, jnp.where(qi == last_q, 2, 1))
                keep = mask_ref[kind] > 0
                s_w = jnp.where(keep, _dot_nt(q2, att_ref[pl.ds(start, 3 * ATT_BLOCK), kcol:kcol + 128]), NEG_INF)
            scores.append((s_c, s_w, start if windowed else None))

        probs = []
        for (sub, p), (s_c, s_w, start) in zip(units, scores):
            sink_col = jnp.where(row2 < ATT_BLOCK, sink_ref[2 * p], sink_ref[2 * p + 1])
            m = jnp.maximum(jnp.max(s_c, axis=-1, keepdims=True), sink_col)
            if windowed:
                m = jnp.maximum(m, jnp.max(s_w, axis=-1, keepdims=True))
            e_c = jnp.exp(s_c - m)
            den = jnp.sum(e_c, axis=-1, keepdims=True) + jnp.exp(sink_col - m)
            e_w = None
            if windowed:
                e_w = jnp.exp(s_w - m)
                den = den + jnp.sum(e_w, axis=-1, keepdims=True)
                e_w = e_w.astype(BF16)
            probs.append((e_c.astype(BF16), e_w, den, start))

        for (sub, p), (e_c, e_w, den, start) in zip(units, probs):
            vcol = 2 * ATT_WIDTH + p * 128
            pv = _dot(e_c, att_ref[0:CTX_LEN, vcol:vcol + 128])
            if windowed:
                pv = pv + _dot(e_w, att_ref[pl.ds(start, 3 * ATT_BLOCK), vcol:vcol + 128])
            o2 = pv / den
            o = jnp.where(left, o2[0:ATT_BLOCK], o2[ATT_BLOCK:2 * ATT_BLOCK])
            o_ref[sub * ATT_BLOCK:(sub + 1) * ATT_BLOCK, p * 128:(p + 1) * 128] = o.astype(BF16)

    @pl.when(j < n_ctx_blocks)
    def _():
        run(False)

    @pl.when(j >= n_ctx_blocks)
    def _():
        run(True)


def _swa_masks():
    r = np.arange(2 * ATT_BLOCK)[:, None] % ATT_BLOCK
    c = np.arange(3 * ATT_BLOCK)[None, :]
    near = lambda delta: np.abs(r - c + delta) <= WINDOW
    first = near(ATT_BLOCK) & (c >= ATT_BLOCK)
    return jnp.asarray(np.stack([first, near(ATT_BLOCK), near(2 * ATT_BLOCK)]), F32)


def _swa(att, sink):
    bsz, tt, _ = att.shape
    nb = tt // TOK_BLOCK
    return pl.pallas_call(
        functools.partial(_swa_kernel, tt=tt),
        grid=(bsz, nb),
        in_specs=[
            pl.BlockSpec(memory_space=pltpu.SMEM),
            pl.BlockSpec((3, 2 * ATT_BLOCK, 3 * ATT_BLOCK), lambda b, j: (0, 0, 0)),
            pl.BlockSpec((None, tt, 3 * ATT_WIDTH), lambda b, j: (b, 0, 0)),
        ],
        out_specs=pl.BlockSpec((None, TOK_BLOCK, ATT_WIDTH), lambda b, j: (b, j, 0)),
        out_shape=jax.ShapeDtypeStruct((bsz, tt, ATT_WIDTH), BF16),
        compiler_params=pltpu.CompilerParams(
            dimension_semantics=("arbitrary", "arbitrary"), vmem_limit_bytes=VMEM_LIMIT),
        name="swa",
    )(sink, _swa_masks(), att)


def _s5_prep_kernel(lr_ref, li_ref, ls_ref, br_ref, bi_ref, cr_ref, ci_ref,
                    a16_ref, mre_ref, mim_ref, rre_ref, rim_ref, k_ref):
    rows = S5_LC * S5_GROUP
    rowg = lax.broadcasted_iota(jnp.int32, (rows, S5_NS), 0) >> 4
    colg = lax.broadcasted_iota(jnp.int32, (rows, S5_NS), 1) >> 6
    same_group = rowg == colg
    taps_t = []
    for d in range(2):
        lr = jnp.minimum(lr_ref[d], -1e-4)
        li = li_ref[d]
        dt = jnp.exp(ls_ref[d])
        mag = jnp.exp(lr * dt)
        ar = mag * jnp.cos(li * dt)
        ai = mag * jnp.sin(li * dt)
        den = lr * lr + li * li
        fr = ((ar - 1.0) * lr + ai * li) / den
        fi = (ai * lr - (ar - 1.0) * li) / den
        br = br_ref[d]
        bi = bi_ref[d]
        bbr = fr * br - fi * bi
        bbi = fr * bi + fi * br
        cr = cr_ref[d]
        ci = ci_ref[d]
        pr = jnp.ones_like(ar)
        pi = jnp.zeros_like(ar)
        cp_re, cp_im = [], []
        for tau in range(S5_LC + 1):
            cpr = cr * pr - ci * pi
            cpi = cr * pi + ci * pr
            if tau < S5_LC:
                mre_ref[d, tau] = pr * bbr - pi * bbi
                mim_ref[d, tau] = pr * bbi + pi * bbr
                cp_re.append(cpr)
                cp_im.append(cpi)
            if tau >= 1:
                rre_ref[d, tau - 1] = cpr
                rim_ref[d, tau - 1] = -cpi
            if tau == S5_LC:
                a16_ref[d, 0:1, :] = pr
                a16_ref[d, 1:2, :] = pi
            pr, pi = pr * ar - pi * ai, pr * ai + pi * ar
        bd_re = jnp.where(same_group, jnp.concatenate([bbr] * S5_GROUPS, axis=0), 0.0)
        bd_im = jnp.where(same_group, jnp.concatenate([bbi] * S5_GROUPS, axis=0), 0.0)
        if d == 1:
            cp_re, cp_im = cp_re[::-1], cp_im[::-1]
        hp = lax.Precision.HIGHEST
        dims = (((1,), (1,)), ((), ()))
        taps_t.append(
            lax.dot_general(bd_re, jnp.concatenate(cp_re, axis=0), dims, precision=hp, preferred_element_type=F32)
            - lax.dot_general(bd_im, jnp.concatenate(cp_im, axis=0), dims, precision=hp,
                              preferred_element_type=F32))
    kf, kb_rev = taps_t
    tap0 = lax.broadcasted_iota(jnp.int32, kf.shape, 1) < S5_GROUP
    k_ref[0] = jnp.where(tap0, kf + pltpu.roll(kb_rev, S5_GROUP, 1), kf)
    k_ref[1] = kb_rev


def _s5_tables(fwd, bwd):
    n_layers = fwd[0].shape[0]
    both = lambda i: jnp.stack([fwd[i], bwd[i]], axis=1)
    lam_re = both(0).reshape(n_layers, 2, 1, S5_NS)
    lam_im = both(1).reshape(n_layers, 2, 1, S5_NS)
    log_step = jnp.repeat(both(2), S5_STATE, axis=-1).reshape(n_layers, 2, 1, S5_NS)
    b_hn = lambda t: t.transpose(0, 1, 4, 2, 3).reshape(n_layers, 2, S5_GROUP, S5_NS)
    c_hn = lambda t: t.transpose(0, 1, 3, 2, 4).reshape(n_layers, 2, S5_GROUP, S5_NS)
    vec = pl.BlockSpec((None, 2, 1, S5_NS), lambda l: (l, 0, 0, 0))
    mat = pl.BlockSpec((None, 2, S5_GROUP, S5_NS), lambda l: (l, 0, 0, 0))
    tab = pl.BlockSpec((None, 2, S5_LC, S5_GROUP, S5_NS), lambda l: (l, 0, 0, 0, 0))
    taps = S5_LC * S5_GROUP
    a16, mre, mim, rre, rim, k = pl.pallas_call(
        _s5_prep_kernel,
        grid=(n_layers,),
        in_specs=[vec, vec, vec, mat, mat, mat, mat],
        out_specs=[
            pl.BlockSpec((None, 2, 2, S5_NS), lambda l: (l, 0, 0, 0)),
            tab, tab, tab, tab,
            pl.BlockSpec((None, 2, S5_WIDTH, taps), lambda l: (l, 0, 0, 0)),
        ],
        out_shape=[
            jax.ShapeDtypeStruct((n_layers, 2, 2, S5_NS), F32),
            jax.ShapeDtypeStruct((n_layers, 2, S5_LC, S5_GROUP, S5_NS), F32),
            jax.ShapeDtypeStruct((n_layers, 2, S5_LC, S5_GROUP, S5_NS), F32),
            jax.ShapeDtypeStruct((n_layers, 2, S5_LC, S5_GROUP, S5_NS), F32),
            jax.ShapeDtypeStruct((n_layers, 2, S5_LC, S5_GROUP, S5_NS), F32),
            jax.ShapeDtypeStruct((n_layers, 2, S5_WIDTH, taps), F32),
        ],
        compiler_params=pltpu.CompilerParams(dimension_semantics=("arbitrary",), vmem_limit_bytes=VMEM_LIMIT),
        name="s5_prep",
    )(lam_re, lam_im, log_step, b_hn(both(3)), b_hn(both(4)), c_hn(both(5)), c_hn(both(6)))

    n_pairs = S5_GROUPS // 2
    lane_group = np.arange(128) // S5_STATE

    def pair_rows(t, flip):
        t = t[:, ::-1] if flip else t
        t = t.reshape(n_layers, S5_LC, S5_GROUP, n_pairs, 128).transpose(0, 3, 1, 2, 4)
        own = jnp.asarray(lane_group[None, :] == np.arange(2)[:, None])
        t = jnp.where(own[None, None, :, None, None, :], t[:, :, None], 0.0)
        return t.reshape(n_layers, n_pairs, 2 * taps, 128)

    m_f = jnp.concatenate([pair_rows(mre[:, 0], True), pair_rows(mim[:, 0], True)], axis=-1).astype(BF16)
    m_b = jnp.concatenate([pair_rows(mre[:, 1], False), pair_rows(mim[:, 1], False)], axis=-1).astype(BF16)

    r_t = jnp.concatenate([pair_rows(rre[:, 0], False), pair_rows(rim[:, 0], False),
                           pair_rows(rre[:, 1], True), pair_rows(rim[:, 1], True)], axis=-1).astype(BF16)

    zeros = jnp.zeros((n_layers, S5_WIDTH, taps), F32)
    f2 = jnp.concatenate([zeros, k[:, 0]], axis=-1)
    b2 = jnp.concatenate([k[:, 1], zeros], axis=-1)
    t_f = jnp.stack([f2[..., taps - S5_GROUP * st:2 * taps - S5_GROUP * st] for st in range(S5_LC)], axis=2)
    t_b = jnp.stack([b2[..., S5_GROUP * (S5_LC - 1 - st):S5_GROUP * (S5_LC - 1 - st) + taps]
                     for st in range(S5_LC)], axis=2)
    later = jnp.asarray(np.arange(taps)[None, :] // S5_GROUP >= np.arange(S5_LC)[:, None])
    toe = jnp.where(later[None, None], t_f, t_b)
    toe = toe.reshape(n_layers, S5_GROUPS, S5_GROUP, S5_LC, taps).transpose(0, 1, 3, 2, 4)
    toe = toe.reshape(n_layers, S5_GROUPS, taps, taps).astype(BF16)
    return a16, m_f, m_b, r_t, toe


def _lane_block_transpose(cols):
    lane = lax.broadcasted_iota(jnp.int32, cols[0].shape, 1)
    out = [None] * 32
    for ah in range(2):
        for bh in range(2):
            v = [cols[(ah * 8 + al) * 2 + bh] for al in range(8)]
            for kbit in range(3):
                width = 16 << kbit
                low = ((lane >> (4 + kbit)) & 1) == 0
                nxt = list(v)
                for i in range(8):
                    if i & (1 << kbit):
                        continue
                    lo_v, hi_v = v[i], v[i | (1 << kbit)]
                    nxt[i] = jnp.where(low, lo_v, pltpu.roll(hi_v, width, 1))
                    nxt[i | (1 << kbit)] = jnp.where(low, pltpu.roll(lo_v, 128 - width, 1), hi_v)
                v = nxt
            for bl in range(8):
                out[(bh * 8 + bl) * 2 + ah] = v[bl]
    return out


def _s5_increments(ush, m_ref, d_re, d_im):
    for p in range(S5_GROUPS // 2):
        dp = _dot(ush[:, p * 512:(p + 1) * 512], m_ref[p])
        d_re[p] = dp[:, 0:128]
        d_im[p] = dp[:, 128:256]


def _s5_recurrence(a16_ref, d_re, d_im, x_re, x_im, sr_ref, si_ref, order, bsz):
    ar = a16_ref[0:1, :]
    ai = a16_ref[1:2, :]
    sr = sr_ref[...]
    si = si_ref[...]
    n_slabs = S5_NS // 128
    gather = lambda ref, rows: jnp.concatenate([ref[p, rows, :] for p in range(n_slabs)], axis=1)
    for c in order:
        rows = pl.ds(c, bsz, stride=CH_BLOCK)
        for p in range(n_slabs):
            x_re[p, rows, :] = sr[:, p * 128:(p + 1) * 128]
            x_im[p, rows, :] = si[:, p * 128:(p + 1) * 128]
        sr, si = ar * sr - ai * si + gather(d_re, rows), ar * si + ai * sr + gather(d_im, rows)
    sr_ref[...] = sr
    si_ref[...] = si


def _s5_fwd_kernel(uc_ref, a16_ref, m_ref, ush_ref, xin_ref, d_re, d_im, x_re, x_im, sr_ref, si_ref, *, bsz):
    @pl.when(pl.program_id(0) == 0)
    def _():
        sr_ref[...] = jnp.zeros_like(sr_ref)
        si_ref[...] = jnp.zeros_like(si_ref)

    rows = bsz * CH_BLOCK
    ub = uc_ref[...].reshape(rows, S5_CW).astype(BF16)
    packed = pltpu.bitcast(ub, jnp.uint32)
    cols = _lane_block_transpose([packed[:, v * 128:(v + 1) * 128] for v in range(32)])
    ush = pltpu.bitcast(jnp.concatenate(cols, axis=1), BF16)
    ush_ref[...] = ush.reshape(bsz, CH_BLOCK, S5_CW)
    _s5_increments(ush, m_ref, d_re, d_im)
    _s5_recurrence(a16_ref, d_re, d_im, x_re, x_im, sr_ref, si_ref, range(CH_BLOCK), bsz)
    for p in range(S5_NS // 128):
        xin_ref[:, :, p * 128:(p + 1) * 128] = x_re[p].astype(BF16).reshape(bsz, CH_BLOCK, 128)
        xin_ref[:, :, S5_NS + p * 128:S5_NS + (p + 1) * 128] = x_im[p].astype(BF16).reshape(bsz, CH_BLOCK, 128)


def _s5_bwd_kernel(ush_ref, xf_ref, a16_ref, m_ref, toe_ref, r_ref, y_ref, d_re, d_im, x_re, x_im, sr_ref, si_ref,
                   *, bsz):
    @pl.when(pl.program_id(0) == 0)
    def _():
        sr_ref[...] = jnp.zeros_like(sr_ref)
        si_ref[...] = jnp.zeros_like(si_ref)

    rows = bsz * CH_BLOCK
    ush = ush_ref[...].reshape(rows, S5_CW)
    _s5_increments(ush, m_ref, d_re, d_im)
    _s5_recurrence(a16_ref, d_re, d_im, x_re, x_im, sr_ref, si_ref, reversed(range(CH_BLOCK)), bsz)
    xf = xf_ref[...].reshape(rows, 2 * S5_NS)
    ycols = []
    for p in range(S5_GROUPS // 2):
        lanes = slice(p * 128, (p + 1) * 128)
        xcat = jnp.concatenate([xf[:, lanes], xf[:, S5_NS + p * 128:S5_NS + (p + 1) * 128],
                                x_re[p].astype(BF16), x_im[p].astype(BF16)], axis=1)
        carry = _dot_nt(xcat, r_ref[p])
        for g2 in range(2):
            g = 2 * p + g2
            yg = carry[:, g2 * 256:(g2 + 1) * 256] + _dot(ush[:, g * 256:(g + 1) * 256], toe_ref[g])
            ycols += [yg[:, 0:128], yg[:, 128:256]]
    ycols = _lane_block_transpose(ycols)
    y_ref[...] = jnp.concatenate(ycols, axis=1).reshape(bsz, CH_BLOCK, S5_CW)


def _s5(u_c, tables, layer):
    a16, m_f, m_b, r_t, toe = tables
    bsz, n_rows, _ = u_c.shape
    nb = n_rows // CH_BLOCK
    n_ctx = CTX_LEN // TOK_BLOCK
    rows = bsz * CH_BLOCK
    blk3 = lambda w: (bsz, CH_BLOCK, w)
    scratch = [pltpu.VMEM((S5_NS // 128, rows, 128), F32)] * 4 + [pltpu.VMEM((bsz, S5_NS), F32)] * 2
    params = pltpu.CompilerParams(dimension_semantics=("arbitrary",), vmem_limit_bytes=VMEM_LIMIT)
    ush, xin_f = pl.pallas_call(
        functools.partial(_s5_fwd_kernel, bsz=bsz),
        grid=(nb,),
        in_specs=[
            pl.BlockSpec(blk3(S5_CW), lambda j: (0, j, 0)),
            pl.BlockSpec((None, None, 2, S5_NS), lambda j: (layer, 0, 0, 0)),
            pl.BlockSpec((None, S5_GROUPS // 2, 512, 256), lambda j: (layer, 0, 0, 0)),
        ],
        out_specs=[
            pl.BlockSpec(blk3(S5_CW), lambda j: (0, j, 0)),
            pl.BlockSpec(blk3(2 * S5_NS), lambda j: (0, j, 0)),
        ],
        out_shape=[
            jax.ShapeDtypeStruct((bsz, n_rows, S5_CW), BF16),
            jax.ShapeDtypeStruct((bsz, n_rows, 2 * S5_NS), BF16),
        ],
        scratch_shapes=scratch,
        compiler_params=params,
        name="s5_fwd",
    )(u_c, a16, m_f)
    back = lambda j: (0, _scan_block(1, j, n_ctx, nb), 0)
    return pl.pallas_call(
        functools.partial(_s5_bwd_kernel, bsz=bsz),
        grid=(nb,),
        in_specs=[
            pl.BlockSpec(blk3(S5_CW), back),
            pl.BlockSpec(blk3(2 * S5_NS), back),
            pl.BlockSpec((None, None, 2, S5_NS), lambda j: (layer, 1, 0, 0)),
            pl.BlockSpec((None, S5_GROUPS // 2, 512, 256), lambda j: (layer, 0, 0, 0)),
            pl.BlockSpec((None, S5_GROUPS, 256, 256), lambda j: (layer, 0, 0, 0)),
            pl.BlockSpec((None, S5_GROUPS // 2, 512, 512), lambda j: (layer, 0, 0, 0)),
        ],
        out_specs=pl.BlockSpec(blk3(S5_CW), back),
        out_shape=jax.ShapeDtypeStruct((bsz, n_rows, S5_CW), F32),
        scratch_shapes=scratch,
        compiler_params=params,
        name="s5_bwd",
    )(ush, xin_f, a16, m_b, toe, r_t)


def _out_kernel(xc_ref, x_ref, mod_ref, gof_ref, gob_ref, g_ref, att_ref, y_ref, uc_ref, gnw_ref, hm_ref, d_ref, gluw_ref,
                glub_ref, wo_ref, n2_ref, w1_ref, w2_ref, fn_ref, o_ref, ys_ref, *, first_block, final, split):
    j = pl.program_id(1) + first_block
    m = mod_ref[...]
    mrow = jnp.where(j > 0, m[1:2, :], m[0:1, :])
    g1 = mrow[:, 2 * D_MODEL:3 * D_MODEL]
    sh2 = mrow[:, 3 * D_MODEL:4 * D_MODEL]
    sc2 = mrow[:, 4 * D_MODEL:5 * D_MODEL]
    g2 = mrow[:, 5 * D_MODEL:6 * D_MODEL]

    o = gof_ref[...] + gob_ref[...]
    ms = _dot((o * o).astype(BF16), hm_ref[...]) * (1.0 / GLA_DV)
    a = o * lax.rsqrt(ms + EPS) * gnw_ref[...] * _silu(g_ref[...])

    yc = y_ref[...] + d_ref[...] * uc_ref[...]
    for st in range(S5_LC):
        for half in range(S5_WIDTH // 128):
            lane0 = st * S5_WIDTH + half * 128
            ys_ref[half, pl.ds(st, CH_BLOCK, stride=S5_LC), :] = yc[:, lane0:lane0 + 128]
    yy = jnp.concatenate([ys_ref[half] for half in range(S5_WIDTH // 128)], axis=1)
    ge = 0.5 * yy * (1.0 + jnp.tanh(math.sqrt(2.0 / math.pi) * (yy + 0.044715 * (yy * yy * yy))))
    z = _dot(ge.astype(BF16), gluw_ref[...]) + glub_ref[...]
    s = z[:, 0:S5_WIDTH] * _sigmoid(z[:, S5_WIDTH:2 * S5_WIDTH])

    proj = (_dot(a.astype(BF16), wo_ref[0:GLA_WIDTH, :])
            + _dot(att_ref[...], wo_ref[GLA_WIDTH:GLA_WIDTH + ATT_WIDTH, :])
            + _dot(s.astype(BF16), wo_ref[GLA_WIDTH + ATT_WIDTH:D_MODEL, :]))
    x_in = jnp.where(j == 0, xc_ref[...], x_ref[...]) if split else x_ref[...]
    x1 = x_in + g1 * proj
    y2 = x1 * lax.rsqrt(jnp.mean(x1 * x1, axis=-1, keepdims=True) + EPS) * n2_ref[...]
    h2 = y2 * (1.0 + sc2) + sh2
    hid = jnp.maximum(_dot(h2.astype(BF16), w1_ref[...]), 0.0)
    mlp = _dot((hid * hid).astype(BF16), w2_ref[...])
    x2 = x1 + g2 * mlp
    if final:
        x2 = x2 * lax.rsqrt(jnp.mean(x2 * x2, axis=-1, keepdims=True) + EPS) * fn_ref[...]
    o_ref[...] = x2


def _outproj(x_ctx, x_rest, lat_skip, modsel, gla_f, gla_b, gg, att_o, y_c, u_c, gnw, hm, s5d, gluw, glub, wo, n2w,
             w1, w2, fnw, layer, final):
    bsz, tt, _ = gg.shape
    nb = tt // TOK_BLOCK
    first = CTX_LEN // TOK_BLOCK if final else 0
    nsteps = nb - first
    const2 = lambda b, j: (0, 0)
    lyr3 = lambda b, j: (layer, 0, 0)
    single = pl.Buffered(1)
    return pl.pallas_call(
        functools.partial(_out_kernel, first_block=first, final=final, split=lat_skip > 0),
        grid=(bsz, nsteps),
        in_specs=_stream_specs(lat_skip, first) + [
            pl.BlockSpec((None, 2, 6 * D_MODEL), lambda b, j: (b, 0, 0)),
            pl.BlockSpec((None, TOK_BLOCK, GLA_WIDTH), lambda b, j: (b, j + first, 0)),
            pl.BlockSpec((None, TOK_BLOCK, GLA_WIDTH), lambda b, j: (b, j + first, 0)),
            pl.BlockSpec((None, TOK_BLOCK, GLA_WIDTH), lambda b, j: (b, j + first, 0)),
            pl.BlockSpec((None, TOK_BLOCK, ATT_WIDTH), lambda b, j: (b, j + first, 0)),
            pl.BlockSpec((None, CH_BLOCK, S5_CW), lambda b, j: (b, j + first, 0)),
            pl.BlockSpec((None, CH_BLOCK, S5_CW), lambda b, j: (b, j + first, 0)),
            pl.BlockSpec((1, GLA_WIDTH), const2),
            pl.BlockSpec((GLA_WIDTH, GLA_WIDTH), const2),
            pl.BlockSpec((1, S5_CW), const2),
            pl.BlockSpec((None, S5_WIDTH, 2 * S5_WIDTH), lyr3, pipeline_mode=single),
            pl.BlockSpec((1, 2 * S5_WIDTH), const2),
            pl.BlockSpec((None, D_MODEL, D_MODEL), lyr3, pipeline_mode=single),
            pl.BlockSpec((1, D_MODEL), const2),
            pl.BlockSpec((None, D_MODEL, D_FF), lyr3, pipeline_mode=single),
            pl.BlockSpec((None, D_FF, D_MODEL), lyr3, pipeline_mode=single),
            pl.BlockSpec((1, D_MODEL), const2),
        ],
        out_specs=pl.BlockSpec((None, TOK_BLOCK, D_MODEL), lambda b, j: (b, j, 0)),
        out_shape=jax.ShapeDtypeStruct((bsz, nsteps * TOK_BLOCK, D_MODEL), F32),
        scratch_shapes=[pltpu.VMEM((S5_WIDTH // 128, TOK_BLOCK, 128), F32)],
        compiler_params=pltpu.CompilerParams(
            dimension_semantics=("arbitrary", "arbitrary"), vmem_limit_bytes=VMEM_LIMIT),
        name="outproj_mlp",
    )(x_ctx, x_rest, modsel, gla_f, gla_b, gg, att_o, y_c, u_c, gnw, hm, s5d, gluw, glub, wo, n2w, w1, w2, fnw)


def _rope_tables(tt):
    n_lat = tt - CTX_LEN
    rows = n_lat // GRID_W
    row = jnp.repeat(jnp.arange(rows, dtype=F32), GRID_W)
    col = jnp.tile(jnp.arange(GRID_W, dtype=F32), rows)
    n_freq = ATT_HD // 4
    inv_freq = ROPE_BASE ** (-jnp.arange(n_freq, dtype=F32) / n_freq)
    ang_r = row[:, None] * inv_freq
    ang_c = col[:, None] * inv_freq
    ang = jnp.concatenate([ang_r, ang_r, ang_c, ang_c], axis=-1)
    cos = jnp.concatenate([jnp.ones((CTX_LEN, ATT_HD), F32), jnp.cos(ang)], axis=0)
    sin = jnp.concatenate([jnp.zeros((CTX_LEN, ATT_HD), F32), jnp.sin(ang)], axis=0)
    up_quarter = (np.arange(ATT_HD) // 16) % 2 == 0
    sa = jnp.where(up_quarter, -sin, 0.0)
    sb = jnp.where(up_quarter, 0.0, sin)
    two = lambda t: jnp.concatenate([t, t], axis=-1)
    return two(cos), two(sa), two(sb)


def _pad_cols(w, width):
    return jnp.pad(w, ((0, 0), (0, 0), (0, width - w.shape[-1])))


def _layout_w_in(w_in):
    offs = np.cumsum([0, GLA_KW, GLA_KW, GLA_WIDTH, GLA_WIDTH, GLA_RANK, GLA_RANK, ATT_WIDTH, ATT_KVW, ATT_KVW,
                      S5_WIDTH])
    q, k, v, g, zf, zb, aq, ak, av, u = [w_in[:, :, offs[i]:offs[i + 1]] for i in range(10)]
    z = jnp.concatenate([zf, zb], axis=-1)
    cols = [v, g, _pad_cols(q, GLA_KPAD), _pad_cols(k, GLA_KPAD), _pad_cols(z, 128), aq, ak, av, u]
    return jnp.concatenate(cols, axis=-1).astype(BF16)


def _block_diag(t):
    s, g, a, b = t.shape
    eye = jnp.eye(g, dtype=t.dtype)
    return jnp.einsum('sgab,gk->sgakb', t, eye).reshape(s, g * a, g * b)


def kernel(x, c, ctx, c_ctx, w_mod, b_mod, norm1_w, norm2_w, w_in, gla_wa_f, gla_ba_f, gla_wa_b, gla_ba_b,
           gla_norm_w, attn_sink, s5_lam_re_f, s5_lam_im_f, s5_log_step_f, s5_b_re_f, s5_b_im_f, s5_c_re_f,
           s5_c_im_f, s5_lam_re_b, s5_lam_im_b, s5_log_step_b, s5_b_re_b, s5_b_im_b, s5_c_re_b, s5_c_im_b,
           s5_d, glu_w, glu_b, w_out, mlp_w1, mlp_w2, final_norm_w):
    bsz, seq, _ = x.shape
    n_layers = w_mod.shape[0]
    tt = CTX_LEN + seq
    assert bsz % 8 == 0 and seq % TOK_BLOCK == 0 and seq >= 2 * TOK_BLOCK

    mod_rows = -(-(bsz + 1) // 8) * 8
    cvec = jnp.zeros((mod_rows, D_MODEL), F32).at[:bsz].set(c).at[bsz].set(c_ctx)
    mod = _modulation(cvec, w_mod, b_mod)
    mod_ctx = jnp.broadcast_to(mod[:, bsz][:, None], (n_layers, bsz, 6 * D_MODEL))
    modsel = jnp.stack([mod_ctx, mod[:, :bsz]], axis=2)

    w_in_p = _layout_w_in(w_in)
    wa_cat = jnp.zeros((n_layers, 128, 2 * GLA_KPAD), F32)
    wa_cat = wa_cat.at[:, 0:GLA_RANK, 0:GLA_KW].set(gla_wa_f)
    wa_cat = wa_cat.at[:, GLA_RANK:2 * GLA_RANK, GLA_KPAD:GLA_KPAD + GLA_KW].set(gla_wa_b).astype(BF16)
    ba_cat = jnp.zeros((n_layers, 1, 2 * GLA_KPAD), F32)
    ba_cat = ba_cat.at[:, 0, 0:GLA_KW].set(gla_ba_f).at[:, 0, GLA_KPAD:GLA_KPAD + GLA_KW].set(gla_ba_b)
    qscale = jnp.ones((1, 2 * GLA_KPAD), F32).at[:, 0:GLA_KW].set(GLA_DK ** -0.5)
    cos_t, sa_t, sb_t = _rope_tables(tt)
    gnw = jnp.tile(gla_norm_w, (1, GLA_HEADS))[:, None, :]
    head = np.arange(GLA_WIDTH) // GLA_DV
    hm = jnp.asarray(head[:, None] == head[None, :], BF16)
    wo_b = w_out.astype(BF16)
    w1_b = mlp_w1.astype(BF16)
    w2_b = mlp_w2.astype(BF16)
    gluw_b = glu_w.astype(BF16)

    s5_tab = _s5_tables(
        (s5_lam_re_f, s5_lam_im_f, s5_log_step_f, s5_b_re_f, s5_b_im_f, s5_c_re_f, s5_c_im_f),
        (s5_lam_re_b, s5_lam_im_b, s5_log_step_b, s5_b_re_b, s5_b_im_b, s5_c_re_b, s5_c_im_b))
    s5_dt = jnp.tile(s5_d, (1, S5_LC))[:, None, :]

    stream = (ctx, x, CTX_LEN // TOK_BLOCK)
    for l in range(n_layers):
        final = l == n_layers - 1
        gv, gg, gqk, gla, att, u_c = _inproj(*stream, modsel[l], norm1_w[l][None], w_in_p, wa_cat, ba_cat[l],
                                             qscale, cos_t, sa_t, sb_t, l)
        gla_f, gla_b = _gla(gqk, gv, gla)
        att_o = _swa(att, attn_sink[l])
        y_c = _s5(u_c, s5_tab, l)
        xs = _outproj(*stream, modsel[l], gla_f, gla_b, gg, att_o, y_c, u_c,
                      gnw[l], hm, s5_dt[l], gluw_b, glu_b[l][None], wo_b, norm2_w[l][None], w1_b, w2_b,
                      final_norm_w[None], l, final)
        stream = (xs, xs, 0)
    return xs
```

```python
import functools
import math

import jax
import jax.numpy as jnp
import numpy as np
from jax import lax
from jax.experimental import pallas as pl
from jax.experimental.pallas import tpu as pltpu

F32 = jnp.float32
BF16 = jnp.bfloat16

D_MODEL = 1024
D_FF = 4 * D_MODEL
CTX_LEN = 256
GRID_W = 64
EPS = 1e-6
NEG_INF = -1e30
LOG2E = math.log2(math.e)

GLA_HEADS = 4
GLA_DV = 96
GLA_DK = 48
GLA_WIDTH = GLA_HEADS * GLA_DV
GLA_KW = GLA_HEADS * GLA_DK
GLA_KPAD = 256
GLA_RANK = 16
GLA_TAU = 16.0
GLA_CHUNK = 64

ATT_HD = 64
ATT_HEADS = 6
ATT_KV_HEADS = 2
ATT_WIDTH = ATT_HEADS * ATT_HD
ATT_KVW = ATT_KV_HEADS * ATT_HD
WINDOW = 128
ATT_BLOCK = 128
ROPE_BASE = 10000.0

S5_WIDTH = 256
S5_GROUP = 16
S5_GROUPS = 16
S5_STATE = 64
S5_NS = S5_GROUPS * S5_STATE
S5_LC = 16
S5_CW = S5_LC * S5_WIDTH

TOK_BLOCK = 256
CH_BLOCK = TOK_BLOCK // S5_LC
GLA_ROWS = 2
IN_ROWS = 4

C_V, C_G, C_Q, C_K, C_Z, C_AQ, C_AK, C_AV, C_U = 0, 384, 768, 1024, 1280, 1408, 1792, 1920, 2048
IN_PAD = 2304
C_QK_END = C_K + GLA_KPAD

VMEM_LIMIT = 56 * 1024 * 1024


def _sigmoid(x):
    return 1.0 / (1.0 + jnp.exp(-x))


def _silu(x):
    return x * _sigmoid(x)


def _dot(a, b):
    return jnp.dot(a, b, preferred_element_type=F32)


def _dot_nt(a, b):
    return lax.dot_general(a, b, (((1,), (1,)), ((), ())), preferred_element_type=F32)


def _dot_tn(a, b):
    return lax.dot_general(a, b, (((0,), (0,)), ((), ())), preferred_element_type=F32)


def _split3(x):
    hi = x.astype(BF16)
    r1 = x - hi.astype(F32)
    mid = r1.astype(BF16)
    lo = (r1 - mid.astype(F32)).astype(BF16)
    return hi, mid, lo


def _mod_kernel(c_ref, w_ref, b_ref, o_ref):
    a = _silu(c_ref[...]).astype(BF16)
    o_ref[...] = _dot(a, w_ref[...].astype(BF16)) + b_ref[...]


def _modulation(cvec, w_mod, b_mod):
    n_layers = w_mod.shape[0]
    rows = cvec.shape[0]
    tn = 1536
    return pl.pallas_call(
        _mod_kernel,
        grid=(n_layers, 6 * D_MODEL // tn),
        in_specs=[
            pl.BlockSpec((rows, D_MODEL), lambda l, n: (0, 0)),
            pl.BlockSpec((None, D_MODEL, tn), lambda l, n: (l, 0, n)),
            pl.BlockSpec((None, 1, tn), lambda l, n: (l, 0, n)),
        ],
        out_specs=pl.BlockSpec((None, rows, tn), lambda l, n: (l, 0, n)),
        out_shape=jax.ShapeDtypeStruct((n_layers, rows, 6 * D_MODEL), F32),
        compiler_params=pltpu.CompilerParams(
            dimension_semantics=("arbitrary", "arbitrary"), vmem_limit_bytes=VMEM_LIMIT),
        name="modulation",
    )(cvec, w_mod, b_mod.reshape(n_layers, 1, 6 * D_MODEL))


def _inproj_kernel(xc_ref, x_ref, mod_ref, n1_ref, w_ref, wa_ref, ba_ref, qs_ref, cos_ref, sa_ref, sb_ref,
                   gv_ref, gg_ref, gqk_ref, gla_ref, att_ref, uc_ref, us_ref, *, split):
    j = pl.program_id(1)
    cos = cos_ref[...]
    sa = sa_ref[...]
    sb = sb_ref[...]

    def normed(r):
        xf = jnp.where(j == 0, xc_ref[r], x_ref[r]) if split else x_ref[r]
        y = xf * lax.rsqrt(jnp.mean(xf * xf, axis=-1, keepdims=True) + EPS) * n1_ref[...]
        m = mod_ref[r]
        mrow = jnp.where(j > 0, m[1:2, :], m[0:1, :])
        sh1 = mrow[:, 0:D_MODEL]
        sc1 = mrow[:, D_MODEL:2 * D_MODEL]
        return (y * (1.0 + sc1) + sh1).astype(BF16)

    def rope(t, reps):
        w = t.shape[-1]
        c3 = jnp.concatenate([cos] * reps, axis=-1) if reps > 1 else cos
        a3 = jnp.concatenate([sa] * reps, axis=-1) if reps > 1 else sa
        b3 = jnp.concatenate([sb] * reps, axis=-1) if reps > 1 else sb
        up = pltpu.roll(t, w - 16, 1)
        dn = pltpu.roll(t, 16, 1)
        return t * c3 + up * a3 + dn * b3

    def tail(r, p):
        gv_ref[r] = p[:, C_V:C_V + GLA_WIDTH].astype(BF16)
        gg_ref[r] = p[:, C_G:C_G + GLA_WIDTH]
        gqk_ref[r] = p[:, C_Q:C_QK_END] * qs_ref[...]
        z = p[:, C_Z:C_Z + 128].astype(BF16)
        zg = _dot(z, wa_ref[...]) + ba_ref[...]
        gla_ref[r] = (jnp.minimum(zg, 0.0) - jnp.log1p(jnp.exp(-jnp.abs(zg)))) * (1.0 / GLA_TAU)

        aq = rope(p[:, C_AQ:C_AQ + ATT_WIDTH], 3) * (ATT_HD ** -0.5 * LOG2E)
        ak = rope(p[:, C_AK:C_AK + ATT_KVW], 1)
        av = p[:, C_AV:C_AV + ATT_KVW]
        left = lax.broadcasted_iota(jnp.int32, ak.shape, 1) < ATT_HD

        def expand(t):
            sw = pltpu.roll(t, ATT_HD, 1)
            return [jnp.where(left, t, sw), t, jnp.where(left, sw, t)]

        att_ref[r] = jnp.concatenate([aq] + expand(ak) + expand(av), axis=-1).astype(BF16)
        for half in range(S5_WIDTH // 128):
            us_ref[r, half] = p[:, C_U + half * 128:C_U + (half + 1) * 128]
        for st in range(S5_LC):
            for half in range(S5_WIDTH // 128):
                lane0 = st * S5_WIDTH + half * 128
                uc_ref[r, :, lane0:lane0 + 128] = us_ref[r, half, pl.ds(st, CH_BLOCK, stride=S5_LC), :]

    h = normed(0)
    prev = None
    for r in range(IN_ROWS):
        proj = _dot(h, w_ref[...])
        if r + 1 < IN_ROWS:
            h = normed(r + 1)
        if prev is not None:
            tail(r - 1, prev)
        prev = proj
    tail(IN_ROWS - 1, prev)


def _stream_specs(lat_skip, first=0, rows=None):
    return [pl.BlockSpec((rows, TOK_BLOCK, D_MODEL), lambda b, j: (b, 0, 0)),
            pl.BlockSpec((rows, TOK_BLOCK, D_MODEL), lambda b, j: (b, jnp.maximum(j + first - lat_skip, 0), 0))]


def _inproj(x_ctx, x_rest, lat_skip, modsel, n1w, w_in_p, wa_cat, ba_cat, qscale, cos_t, sa_t, sb_t, layer):
    bsz = x_ctx.shape[0]
    tt = cos_t.shape[0]
    nb = tt // TOK_BLOCK
    const = lambda b, j: (0, 0)
    lyr3 = lambda b, j: (layer, 0, 0)
    tok = lambda w: pl.BlockSpec((IN_ROWS, TOK_BLOCK, w), lambda b, j: (b, j, 0))
    return pl.pallas_call(
        functools.partial(_inproj_kernel, split=lat_skip > 0),
        grid=(bsz // IN_ROWS, nb),
        in_specs=_stream_specs(lat_skip, rows=IN_ROWS) + [
            pl.BlockSpec((IN_ROWS, 2, 6 * D_MODEL), lambda b, j: (b, 0, 0)),
            pl.BlockSpec((1, D_MODEL), const),
            pl.BlockSpec((None, D_MODEL, IN_PAD), lyr3),
            pl.BlockSpec((None, 128, 2 * GLA_KPAD), lyr3),
            pl.BlockSpec((1, 2 * GLA_KPAD), const),
            pl.BlockSpec((1, 2 * GLA_KPAD), const),
            pl.BlockSpec((TOK_BLOCK, 128), lambda b, j: (j, 0)),
            pl.BlockSpec((TOK_BLOCK, 128), lambda b, j: (j, 0)),
            pl.BlockSpec((TOK_BLOCK, 128), lambda b, j: (j, 0)),
        ],
        out_specs=[tok(GLA_WIDTH), tok(GLA_WIDTH), tok(2 * GLA_KPAD), tok(2 * GLA_KPAD), tok(3 * ATT_WIDTH),
                   pl.BlockSpec((IN_ROWS, CH_BLOCK, S5_CW), lambda b, j: (b, j, 0))],
        out_shape=[
            jax.ShapeDtypeStruct((bsz, tt, GLA_WIDTH), BF16),
            jax.ShapeDtypeStruct((bsz, tt, GLA_WIDTH), F32),
            jax.ShapeDtypeStruct((bsz, tt, 2 * GLA_KPAD), F32),
            jax.ShapeDtypeStruct((bsz, tt, 2 * GLA_KPAD), F32),
            jax.ShapeDtypeStruct((bsz, tt, 3 * ATT_WIDTH), BF16),
            jax.ShapeDtypeStruct((bsz, tt // S5_LC, S5_CW), F32),
        ],
        scratch_shapes=[pltpu.VMEM((IN_ROWS, S5_WIDTH // 128, TOK_BLOCK, 128), F32)],
        compiler_params=pltpu.CompilerParams(
            dimension_semantics=("arbitrary", "arbitrary"), vmem_limit_bytes=VMEM_LIMIT),
        name="inproj",
    )(x_ctx, x_rest, modsel, n1w, w_in_p, wa_cat, ba_cat, qscale, cos_t, sa_t, sb_t)


def _gla_kernel(qkf_ref, vf_ref, laf_ref, qkb_ref, vb_ref, lab_ref, tri_ref, tril_ref, hm_ref, vm_ref, bd_ref, of_ref, ob_ref, stf_ref, stb_ref):
    j = pl.program_id(1)

    @pl.when(j == 0)
    def _():
        stf_ref[...] = jnp.zeros_like(stf_ref)
        stb_ref[...] = jnp.zeros_like(stb_ref)

    n_chunks = TOK_BLOCK // GLA_CHUNK
    chunk = lambda t, c: t[c * GLA_CHUNK:(c + 1) * GLA_CHUNK]
    dirs = []
    for r in range(GLA_ROWS):
        dirs.append(((qkf_ref.at[r], vf_ref.at[r], laf_ref.at[r]), None, 0, GLA_CHUNK - 1, of_ref.at[r],
                     stf_ref.at[r], list(range(n_chunks))))
        dirs.append(((qkb_ref.at[r], vb_ref.at[r], lab_ref.at[r]), None, 1, 0, ob_ref.at[r],
                     stb_ref.at[r], list(reversed(range(n_chunks)))))

    cum = []
    for (_, _, la_ref), _, d, _, _, _, _ in dirs:
        la = la_ref[...]
        hi = la.astype(BF16)
        lo = (la - hi.astype(F32)).astype(BF16)
        cum.append(_dot(tri_ref[d], hi) + _dot(tri_ref[d], lo))

    ops = []
    for ((qk_ref, v_ref, _), _, d, last_row, _, _, _), b in zip(dirs, cum):
        q = qk_ref[:, 0:GLA_KPAD]
        k = qk_ref[:, GLA_KPAD:2 * GLA_KPAD]
        vb = v_ref[...]
        bl = [chunk(b, c)[last_row:last_row + 1, :] for c in range(n_chunks)]
        blx = jnp.concatenate([jnp.broadcast_to(t, (GLA_CHUNK, GLA_KPAD)) for t in bl], axis=0)
        qb = (q * jnp.exp(b)).astype(BF16)
        kb = (k * jnp.exp(-b)).astype(BF16)
        kd = (k * jnp.exp(blx - b)).astype(BF16)
        ops.append((qb, kb, kd, vb, bl))

    scores = []
    for (qb, kb, _, _, _), (_, _, d, _, _, _, _) in zip(ops, dirs):
        keep = tril_ref[d] > 0
        per = []
        for c in range(n_chunks):
            kst = jnp.concatenate([chunk(kb, c) * hm_ref[h] for h in range(GLA_HEADS)], axis=0)
            per.append(jnp.where(keep, _dot_nt(chunk(qb, c), kst), 0.0).astype(BF16))
        scores.append(per)

    bdmask = bd_ref[...]
    intra, inc = [], []
    for (_, _, kd, vb, _), per in zip(ops, scores):
        oi, ds = [], []
        for c in range(n_chunks):
            vc = chunk(vb, c)
            vbd = jnp.concatenate([vc * vm_ref[h] for h in range(GLA_HEADS)], axis=0)
            oi.append(_dot(per[c], vbd))
            ds.append(_dot_tn(vc, chunk(kd, c)) * bdmask)
        intra.append(oi)
        inc.append(ds)

    states = [d[5][...] for d in dirs]
    for i in range(n_chunks):
        for si, ((_, _, _, _, o_ref, _, order), (qb, _, _, _, bl)) in enumerate(zip(dirs, ops)):
            c = order[i]
            o_ref[c * GLA_CHUNK:(c + 1) * GLA_CHUNK, :] = (
                intra[si][c] + _dot_nt(chunk(qb, c), states[si].astype(BF16)))
            states[si] = states[si] * jnp.exp(bl[c]) + inc[si][c]
    for d, st in zip(dirs, states):
        d[5][...] = st


def _gla_masks():
    r = np.arange(GLA_CHUNK)
    lower = (r[None, :] <= r[:, None]).astype(np.float32)
    tri1 = np.stack([lower, lower.T])
    n_chunks = TOK_BLOCK // GLA_CHUNK
    tri = np.stack([np.kron(np.eye(n_chunks, dtype=np.float32), t) for t in tri1])
    tril = np.tile(tri1, (1, 1, GLA_HEADS))
    klane = np.arange(GLA_KPAD)
    vlane = np.arange(GLA_WIDTH)
    hm = np.stack([np.broadcast_to((klane // GLA_DK) == h, (GLA_CHUNK, GLA_KPAD)) for h in range(GLA_HEADS)])
    vm = np.stack([np.broadcast_to((vlane // GLA_DV) == h, (GLA_CHUNK, GLA_WIDTH)) for h in range(GLA_HEADS)])
    bd = ((vlane[:, None] // GLA_DV) == (klane[None, :] // GLA_DK)).astype(np.float32)
    return (jnp.asarray(tri, BF16), jnp.asarray(tril, F32), jnp.asarray(hm, BF16), jnp.asarray(vm, BF16),
            jnp.asarray(bd))


def _scan_block(d, j, n_ctx, n_all):
    bwd = jnp.where(j < n_ctx, n_ctx - 1 - j, n_all + n_ctx - 1 - j)
    return jnp.where(d == 0, j, bwd)


def _gla(gqk, gv, gla):
    bsz, tt, _ = gv.shape
    nb = tt // TOK_BLOCK
    n_ctx = CTX_LEN // TOK_BLOCK
    tri, tril, hm, vm, bd = _gla_masks()
    const2 = lambda b, j: (0, 0)
    const3 = lambda b, j: (0, 0, 0)
    fwd = lambda b, j: (b, j, 0)
    bwd = lambda b, j: (b, _scan_block(1, j, n_ctx, nb), 0)
    bwd_gate = lambda b, j: (b, _scan_block(1, j, n_ctx, nb), 1)
    tok = lambda w, imap: pl.BlockSpec((GLA_ROWS, TOK_BLOCK, w), imap)
    return pl.pallas_call(
        _gla_kernel,
        grid=(bsz // GLA_ROWS, nb),
        in_specs=[
            tok(2 * GLA_KPAD, fwd),
            tok(GLA_WIDTH, fwd),
            tok(GLA_KPAD, fwd),
            tok(2 * GLA_KPAD, bwd),
            tok(GLA_WIDTH, bwd),
            tok(GLA_KPAD, bwd_gate),
            pl.BlockSpec((2, TOK_BLOCK, TOK_BLOCK), const3),
            pl.BlockSpec((2, GLA_CHUNK, GLA_HEADS * GLA_CHUNK), const3),
            pl.BlockSpec((GLA_HEADS, GLA_CHUNK, GLA_KPAD), const3),
            pl.BlockSpec((GLA_HEADS, GLA_CHUNK, GLA_WIDTH), const3),
            pl.BlockSpec((GLA_WIDTH, GLA_KPAD), const2),
        ],
        out_specs=[tok(GLA_WIDTH, fwd), tok(GLA_WIDTH, bwd)],
        out_shape=[jax.ShapeDtypeStruct((bsz, tt, GLA_WIDTH), F32)] * 2,
        scratch_shapes=[pltpu.VMEM((GLA_ROWS, GLA_WIDTH, GLA_KPAD), F32)] * 2,
        compiler_params=pltpu.CompilerParams(
            dimension_semantics=("arbitrary", "arbitrary"), vmem_limit_bytes=VMEM_LIMIT),
        name="gla",
    )(gqk, gv, gla, gqk, gv, gla, tri, tril, hm, vm, bd)


def _swa_kernel(sink_ref, mask_ref, att_ref, o_ref, *, tt):
    j = pl.program_id(1)
    n_sub = TOK_BLOCK // ATT_BLOCK
    n_ctx_blocks = CTX_LEN // TOK_BLOCK
    last_q = tt // ATT_BLOCK - 1
    lane = lax.broadcasted_iota(jnp.int32, (ATT_BLOCK, 2 * ATT_HD), 1)
    left = lane < ATT_HD
    row2 = lax.broadcasted_iota(jnp.int32, (2 * ATT_BLOCK, 1), 0)
    units = [(sub, p) for sub in range(n_sub) for p in range(ATT_HEADS // 2)]

    def run(windowed):
        def score(sub, p):
            q0 = pl.multiple_of(j * TOK_BLOCK + sub * ATT_BLOCK, ATT_BLOCK)
            qp = att_ref[pl.ds(q0, ATT_BLOCK), p * 128:(p + 1) * 128]
            zero = jnp.zeros_like(qp)
            q2 = jnp.concatenate([jnp.where(left, qp, zero), jnp.where(left, zero, qp)], axis=0)
            kcol = ATT_WIDTH + p * 128
            s_c = _dot_nt(q2, att_ref[0:CTX_LEN, kcol:kcol + 128])
            if not windowed:
                return s_c, None, None
            start = pl.multiple_of(jnp.minimum(q0 - ATT_BLOCK, tt - 3 * ATT_BLOCK), ATT_BLOCK)
            s_w = jnp.where(keeps[sub], _dot_nt(q2, att_ref[pl.ds(start, 3 * ATT_BLOCK), kcol:kcol + 128]), NEG_INF)
            return s_c, s_w, start

        def softmax(p, s_c, s_w, start):
            sink_col = jnp.where(row2 < ATT_BLOCK, sink_ref[2 * p], sink_ref[2 * p + 1]) * LOG2E
            m = jnp.maximum(jnp.max(s_c, axis=-1, keepdims=True), sink_col)
            if windowed:
                m = jnp.maximum(m, jnp.max(s_w, axis=-1, keepdims=True))
            e_c = jnp.exp2(s_c - m)
            den = jnp.sum(e_c, axis=-1, keepdims=True) + jnp.exp2(sink_col - m)
            e_w = None
            if windowed:
                e_w = jnp.exp2(s_w - m)
                den = den + jnp.sum(e_w, axis=-1, keepdims=True)
                e_w = e_w.astype(BF16)
            return e_c.astype(BF16), e_w, den, start

        def values(sub, p, e_c, e_w, den, start):
            vcol = 2 * ATT_WIDTH + p * 128
            pv = _dot(e_c, att_ref[0:CTX_LEN, vcol:vcol + 128])
            if windowed:
                pv = pv + _dot(e_w, att_ref[pl.ds(start, 3 * ATT_BLOCK), vcol:vcol + 128])
            o2 = pv / den
            o = jnp.where(left, o2[0:ATT_BLOCK], o2[ATT_BLOCK:2 * ATT_BLOCK])
            o_ref[sub * ATT_BLOCK:(sub + 1) * ATT_BLOCK, p * 128:(p + 1) * 128] = o.astype(BF16)

        keeps = []
        if windowed:
            for sub in range(n_sub):
                qi = j * n_sub + sub
                kind = jnp.where(qi == CTX_LEN // ATT_BLOCK, 0, jnp.where(qi == last_q, 2, 1))
                keeps.append(mask_ref[kind] > 0)

        n = len(units)
        sc, pr = {}, {}
        for t in range(n + 2):
            if t < n:
                sc[t] = score(*units[t])
            if 0 <= t - 1 < n:
                pr[t - 1] = softmax(units[t - 1][1], *sc.pop(t - 1))
            if 0 <= t - 2 < n:
                values(*units[t - 2], *pr.pop(t - 2))

    @pl.when(j < n_ctx_blocks)
    def _():
        run(False)

    @pl.when(j >= n_ctx_blocks)
    def _():
        run(True)


def _swa_masks():
    r = np.arange(2 * ATT_BLOCK)[:, None] % ATT_BLOCK
    c = np.arange(3 * ATT_BLOCK)[None, :]
    near = lambda delta: np.abs(r - c + delta) <= WINDOW
    first = near(ATT_BLOCK) & (c >= ATT_BLOCK)
    return jnp.asarray(np.stack([first, near(ATT_BLOCK), near(2 * ATT_BLOCK)]), F32)


def _swa(att, sink):
    bsz, tt, _ = att.shape
    nb = tt // TOK_BLOCK
    return pl.pallas_call(
        functools.partial(_swa_kernel, tt=tt),
        grid=(bsz, nb),
        in_specs=[
            pl.BlockSpec(memory_space=pltpu.SMEM),
            pl.BlockSpec((3, 2 * ATT_BLOCK, 3 * ATT_BLOCK), lambda b, j: (0, 0, 0)),
            pl.BlockSpec((None, tt, 3 * ATT_WIDTH), lambda b, j: (b, 0, 0)),
        ],
        out_specs=pl.BlockSpec((None, TOK_BLOCK, ATT_WIDTH), lambda b, j: (b, j, 0)),
        out_shape=jax.ShapeDtypeStruct((bsz, tt, ATT_WIDTH), BF16),
        compiler_params=pltpu.CompilerParams(
            dimension_semantics=("arbitrary", "arbitrary"), vmem_limit_bytes=VMEM_LIMIT),
        name="swa",
    )(sink, _swa_masks(), att)


def _s5_prep_kernel(lr_ref, li_ref, ls_ref, br_ref, bi_ref, cr_ref, ci_ref,
                    a16_ref, mre_ref, mim_ref, rre_ref, rim_ref, k_ref):
    rows = S5_LC * S5_GROUP
    rowg = lax.broadcasted_iota(jnp.int32, (rows, S5_NS), 0) >> 4
    colg = lax.broadcasted_iota(jnp.int32, (rows, S5_NS), 1) >> 6
    same_group = rowg == colg
    taps_t = []
    for d in range(2):
        lr = jnp.minimum(lr_ref[d], -1e-4)
        li = li_ref[d]
        dt = jnp.exp(ls_ref[d])
        mag = jnp.exp(lr * dt)
        ar = mag * jnp.cos(li * dt)
        ai = mag * jnp.sin(li * dt)
        den = lr * lr + li * li
        fr = ((ar - 1.0) * lr + ai * li) / den
        fi = (ai * lr - (ar - 1.0) * li) / den
        br = br_ref[d]
        bi = bi_ref[d]
        bbr = fr * br - fi * bi
        bbi = fr * bi + fi * br
        cr = cr_ref[d]
        ci = ci_ref[d]
        pr = jnp.ones_like(ar)
        pi = jnp.zeros_like(ar)
        cp_re, cp_im = [], []
        for tau in range(S5_LC + 1):
            cpr = cr * pr - ci * pi
            cpi = cr * pi + ci * pr
            if tau < S5_LC:
                mre_ref[d, tau] = pr * bbr - pi * bbi
                mim_ref[d, tau] = pr * bbi + pi * bbr
                cp_re.append(cpr)
                cp_im.append(cpi)
            if tau >= 1:
                rre_ref[d, tau - 1] = cpr
                rim_ref[d, tau - 1] = -cpi
            if tau == S5_LC:
                a16_ref[d, 0:1, :] = pr
                a16_ref[d, 1:2, :] = pi
            pr, pi = pr * ar - pi * ai, pr * ai + pi * ar
        bd_re = jnp.where(same_group, jnp.concatenate([bbr] * S5_GROUPS, axis=0), 0.0)
        bd_im = jnp.where(same_group, jnp.concatenate([bbi] * S5_GROUPS, axis=0), 0.0)
        if d == 1:
            cp_re, cp_im = cp_re[::-1], cp_im[::-1]
        hp = lax.Precision.HIGHEST
        dims = (((1,), (1,)), ((), ()))
        taps_t.append(
            lax.dot_general(bd_re, jnp.concatenate(cp_re, axis=0), dims, precision=hp, preferred_element_type=F32)
            - lax.dot_general(bd_im, jnp.concatenate(cp_im, axis=0), dims, precision=hp,
                              preferred_element_type=F32))
    kf, kb_rev = taps_t
    tap0 = lax.broadcasted_iota(jnp.int32, kf.shape, 1) < S5_GROUP
    k_ref[0] = jnp.where(tap0, kf + pltpu.roll(kb_rev, S5_GROUP, 1), kf)
    k_ref[1] = kb_rev


def _s5_tables(fwd, bwd):
    n_layers = fwd[0].shape[0]
    both = lambda i: jnp.stack([fwd[i], bwd[i]], axis=1)
    lam_re = both(0).reshape(n_layers, 2, 1, S5_NS)
    lam_im = both(1).reshape(n_layers, 2, 1, S5_NS)
    log_step = jnp.repeat(both(2), S5_STATE, axis=-1).reshape(n_layers, 2, 1, S5_NS)
    b_hn = lambda t: t.transpose(0, 1, 4, 2, 3).reshape(n_layers, 2, S5_GROUP, S5_NS)
    c_hn = lambda t: t.transpose(0, 1, 3, 2, 4).reshape(n_layers, 2, S5_GROUP, S5_NS)
    vec = pl.BlockSpec((None, 2, 1, S5_NS), lambda l: (l, 0, 0, 0))
    mat = pl.BlockSpec((None, 2, S5_GROUP, S5_NS), lambda l: (l, 0, 0, 0))
    tab = pl.BlockSpec((None, 2, S5_LC, S5_GROUP, S5_NS), lambda l: (l, 0, 0, 0, 0))
    taps = S5_LC * S5_GROUP
    a16, mre, mim, rre, rim, k = pl.pallas_call(
        _s5_prep_kernel,
        grid=(n_layers,),
        in_specs=[vec, vec, vec, mat, mat, mat, mat],
        out_specs=[
            pl.BlockSpec((None, 2, 2, S5_NS), lambda l: (l, 0, 0, 0)),
            tab, tab, tab, tab,
            pl.BlockSpec((None, 2, S5_WIDTH, taps), lambda l: (l, 0, 0, 0)),
        ],
        out_shape=[
            jax.ShapeDtypeStruct((n_layers, 2, 2, S5_NS), F32),
            jax.ShapeDtypeStruct((n_layers, 2, S5_LC, S5_GROUP, S5_NS), F32),
            jax.ShapeDtypeStruct((n_layers, 2, S5_LC, S5_GROUP, S5_NS), F32),
            jax.ShapeDtypeStruct((n_layers, 2, S5_LC, S5_GROUP, S5_NS), F32),
            jax.ShapeDtypeStruct((n_layers, 2, S5_LC, S5_GROUP, S5_NS), F32),
            jax.ShapeDtypeStruct((n_layers, 2, S5_WIDTH, taps), F32),
        ],
        compiler_params=pltpu.CompilerParams(dimension_semantics=("arbitrary",), vmem_limit_bytes=VMEM_LIMIT),
        name="s5_prep",
    )(lam_re, lam_im, log_step, b_hn(both(3)), b_hn(both(4)), c_hn(both(5)), c_hn(both(6)))

    n_pairs = S5_GROUPS // 2
    lane_group = np.arange(128) // S5_STATE

    def pair_rows(t, flip):
        t = t[:, ::-1] if flip else t
        t = t.reshape(n_layers, S5_LC, S5_GROUP, n_pairs, 128).transpose(0, 3, 1, 2, 4)
        own = jnp.asarray(lane_group[None, :] == np.arange(2)[:, None])
        t = jnp.where(own[None, None, :, None, None, :], t[:, :, None], 0.0)
        return t.reshape(n_layers, n_pairs, 2 * taps, 128)

    m_f = jnp.concatenate([pair_rows(mre[:, 0], True), pair_rows(mim[:, 0], True)], axis=-1).astype(BF16)
    m_b = jnp.concatenate([pair_rows(mre[:, 1], False), pair_rows(mim[:, 1], False)], axis=-1).astype(BF16)

    r_t = jnp.concatenate([pair_rows(rre[:, 0], False), pair_rows(rim[:, 0], False),
                           pair_rows(rre[:, 1], True), pair_rows(rim[:, 1], True)], axis=-1).astype(BF16)

    zeros = jnp.zeros((n_layers, S5_WIDTH, taps), F32)
    f2 = jnp.concatenate([zeros, k[:, 0]], axis=-1)
    b2 = jnp.concatenate([k[:, 1], zeros], axis=-1)
    t_f = jnp.stack([f2[..., taps - S5_GROUP * st:2 * taps - S5_GROUP * st] for st in range(S5_LC)], axis=2)
    t_b = jnp.stack([b2[..., S5_GROUP * (S5_LC - 1 - st):S5_GROUP * (S5_LC - 1 - st) + taps]
                     for st in range(S5_LC)], axis=2)
    later = jnp.asarray(np.arange(taps)[None, :] // S5_GROUP >= np.arange(S5_LC)[:, None])
    toe = jnp.where(later[None, None], t_f, t_b)
    toe = toe.reshape(n_layers, S5_GROUPS, S5_GROUP, S5_LC, taps).transpose(0, 1, 3, 2, 4)
    toe = toe.reshape(n_layers, S5_GROUPS, taps, taps).astype(BF16)
    return a16, m_f, m_b, r_t, toe


def _lane_block_transpose(cols):
    lane = lax.broadcasted_iota(jnp.int32, cols[0].shape, 1)
    out = [None] * 32
    for ah in range(2):
        for bh in range(2):
            v = [cols[(ah * 8 + al) * 2 + bh] for al in range(8)]
            for kbit in range(3):
                width = 16 << kbit
                low = ((lane >> (4 + kbit)) & 1) == 0
                nxt = list(v)
                for i in range(8):
                    if i & (1 << kbit):
                        continue
                    lo_v, hi_v = v[i], v[i | (1 << kbit)]
                    nxt[i] = jnp.where(low, lo_v, pltpu.roll(hi_v, width, 1))
                    nxt[i | (1 << kbit)] = jnp.where(low, pltpu.roll(lo_v, 128 - width, 1), hi_v)
                v = nxt
            for bl in range(8):
                out[(bh * 8 + bl) * 2 + ah] = v[bl]
    return out


def _s5_increments(ush, m_ref, d_re, d_im):
    for p in range(S5_GROUPS // 2):
        dp = _dot(ush[:, p * 512:(p + 1) * 512], m_ref[p])
        d_re[p] = dp[:, 0:128]
        d_im[p] = dp[:, 128:256]


def _s5_recurrence(a16_ref, d_re, d_im, x_re, x_im, sr_ref, si_ref, order, bsz):
    ar = a16_ref[0:1, :]
    ai = a16_ref[1:2, :]
    sr = sr_ref[...]
    si = si_ref[...]
    n_slabs = S5_NS // 128
    gather = lambda ref, rows: jnp.concatenate([ref[p, rows, :] for p in range(n_slabs)], axis=1)
    for c in order:
        rows = pl.ds(c, bsz, stride=CH_BLOCK)
        for p in range(n_slabs):
            x_re[p, rows, :] = sr[:, p * 128:(p + 1) * 128]
            x_im[p, rows, :] = si[:, p * 128:(p + 1) * 128]
        sr, si = ar * sr - ai * si + gather(d_re, rows), ar * si + ai * sr + gather(d_im, rows)
    sr_ref[...] = sr
    si_ref[...] = si


def _s5_fwd_kernel(uc_ref, a16_ref, m_ref, ush_ref, xin_ref, d_re, d_im, x_re, x_im, sr_ref, si_ref, *, bsz):
    @pl.when(pl.program_id(0) == 0)
    def _():
        sr_ref[...] = jnp.zeros_like(sr_ref)
        si_ref[...] = jnp.zeros_like(si_ref)

    rows = bsz * CH_BLOCK
    ub = uc_ref[...].reshape(rows, S5_CW).astype(BF16)
    packed = pltpu.bitcast(ub, jnp.uint32)
    cols = _lane_block_transpose([packed[:, v * 128:(v + 1) * 128] for v in range(32)])
    ush = pltpu.bitcast(jnp.concatenate(cols, axis=1), BF16)
    ush_ref[...] = ush.reshape(bsz, CH_BLOCK, S5_CW)
    _s5_increments(ush, m_ref, d_re, d_im)
    _s5_recurrence(a16_ref, d_re, d_im, x_re, x_im, sr_ref, si_ref, range(CH_BLOCK), bsz)
    for p in range(S5_NS // 128):
        xin_ref[:, :, p * 128:(p + 1) * 128] = x_re[p].astype(BF16).reshape(bsz, CH_BLOCK, 128)
        xin_ref[:, :, S5_NS + p * 128:S5_NS + (p + 1) * 128] = x_im[p].astype(BF16).reshape(bsz, CH_BLOCK, 128)


def _s5_bwd_kernel(ush_ref, xf_ref, a16_ref, m_ref, toe_ref, r_ref, y_ref, d_re, d_im, x_re, x_im, sr_ref, si_ref,
                   *, bsz):
    @pl.when(pl.program_id(0) == 0)
    def _():
        sr_ref[...] = jnp.zeros_like(sr_ref)
        si_ref[...] = jnp.zeros_like(si_ref)

    rows = bsz * CH_BLOCK
    ush = ush_ref[...].reshape(rows, S5_CW)
    _s5_increments(ush, m_ref, d_re, d_im)
    _s5_recurrence(a16_ref, d_re, d_im, x_re, x_im, sr_ref, si_ref, reversed(range(CH_BLOCK)), bsz)
    xf = xf_ref[...].reshape(rows, 2 * S5_NS)
    ycols = []
    for p in range(S5_GROUPS // 2):
        lanes = slice(p * 128, (p + 1) * 128)
        xcat = jnp.concatenate([xf[:, lanes], xf[:, S5_NS + p * 128:S5_NS + (p + 1) * 128],
                                x_re[p].astype(BF16), x_im[p].astype(BF16)], axis=1)
        carry = _dot_nt(xcat, r_ref[p])
        for g2 in range(2):
            g = 2 * p + g2
            yg = carry[:, g2 * 256:(g2 + 1) * 256] + _dot(ush[:, g * 256:(g + 1) * 256], toe_ref[g])
            ycols += [yg[:, 0:128], yg[:, 128:256]]
    ycols = _lane_block_transpose(ycols)
    y_ref[...] = jnp.concatenate(ycols, axis=1).reshape(bsz, CH_BLOCK, S5_CW)


def _s5(u_c, tables, layer):
    a16, m_f, m_b, r_t, toe = tables
    bsz, n_rows, _ = u_c.shape
    nb = n_rows // CH_BLOCK
    n_ctx = CTX_LEN // TOK_BLOCK
    rows = bsz * CH_BLOCK
    blk3 = lambda w: (bsz, CH_BLOCK, w)
    scratch = [pltpu.VMEM((S5_NS // 128, rows, 128), F32)] * 4 + [pltpu.VMEM((bsz, S5_NS), F32)] * 2
    params = pltpu.CompilerParams(dimension_semantics=("arbitrary",), vmem_limit_bytes=VMEM_LIMIT)
    ush, xin_f = pl.pallas_call(
        functools.partial(_s5_fwd_kernel, bsz=bsz),
        grid=(nb,),
        in_specs=[
            pl.BlockSpec(blk3(S5_CW), lambda j: (0, j, 0)),
            pl.BlockSpec((None, None, 2, S5_NS), lambda j: (layer, 0, 0, 0)),
            pl.BlockSpec((None, S5_GROUPS // 2, 512, 256), lambda j: (layer, 0, 0, 0)),
        ],
        out_specs=[
            pl.BlockSpec(blk3(S5_CW), lambda j: (0, j, 0)),
            pl.BlockSpec(blk3(2 * S5_NS), lambda j: (0, j, 0)),
        ],
        out_shape=[
            jax.ShapeDtypeStruct((bsz, n_rows, S5_CW), BF16),
            jax.ShapeDtypeStruct((bsz, n_rows, 2 * S5_NS), BF16),
        ],
        scratch_shapes=scratch,
        compiler_params=params,
        name="s5_fwd",
    )(u_c, a16, m_f)
    back = lambda j: (0, _scan_block(1, j, n_ctx, nb), 0)
    return pl.pallas_call(
        functools.partial(_s5_bwd_kernel, bsz=bsz),
        grid=(nb,),
        in_specs=[
            pl.BlockSpec(blk3(S5_CW), back),
            pl.BlockSpec(blk3(2 * S5_NS), back),
            pl.BlockSpec((None, None, 2, S5_NS), lambda j: (layer, 1, 0, 0)),
            pl.BlockSpec((None, S5_GROUPS // 2, 512, 256), lambda j: (layer, 0, 0, 0)),
            pl.BlockSpec((None, S5_GROUPS, 256, 256), lambda j: (layer, 0, 0, 0)),
            pl.BlockSpec((None, S5_GROUPS // 2, 512, 512), lambda j: (layer, 0, 0, 0)),
        ],
        out_specs=pl.BlockSpec(blk3(S5_CW), back),
        out_shape=jax.ShapeDtypeStruct((bsz, n_rows, S5_CW), F32),
        scratch_shapes=scratch,
        compiler_params=params,
        name="s5_bwd",
    )(ush, xin_f, a16, m_b, toe, r_t)


def _out_kernel(xc_ref, x_ref, mod_ref, gof_ref, gob_ref, g_ref, att_ref, y_ref, uc_ref, gnw_ref, hm_ref, d_ref, gluw_ref,
                glub_ref, wo_ref, n2_ref, w1_ref, w2_ref, fn_ref, o_ref, ys_ref, *, first_block, final, split):
    j = pl.program_id(1) + first_block
    m = mod_ref[...]
    mrow = jnp.where(j > 0, m[1:2, :], m[0:1, :])
    g1 = mrow[:, 2 * D_MODEL:3 * D_MODEL]
    sh2 = mrow[:, 3 * D_MODEL:4 * D_MODEL]
    sc2 = mrow[:, 4 * D_MODEL:5 * D_MODEL]
    g2 = mrow[:, 5 * D_MODEL:6 * D_MODEL]

    o = gof_ref[...] + gob_ref[...]
    ms = _dot((o * o).astype(BF16), hm_ref[...]) * (1.0 / GLA_DV)
    a = o * lax.rsqrt(ms + EPS) * gnw_ref[...] * _silu(g_ref[...])

    yc = y_ref[...] + d_ref[...] * uc_ref[...]
    for st in range(S5_LC):
        for half in range(S5_WIDTH // 128):
            lane0 = st * S5_WIDTH + half * 128
            ys_ref[half, pl.ds(st, CH_BLOCK, stride=S5_LC), :] = yc[:, lane0:lane0 + 128]
    yy = jnp.concatenate([ys_ref[half] for half in range(S5_WIDTH // 128)], axis=1)
    ge = 0.5 * yy * (1.0 + jnp.tanh(math.sqrt(2.0 / math.pi) * (yy + 0.044715 * (yy * yy * yy))))
    z = _dot(ge.astype(BF16), gluw_ref[...]) + glub_ref[...]
    s = z[:, 0:S5_WIDTH] * _sigmoid(z[:, S5_WIDTH:2 * S5_WIDTH])

    proj = (_dot(a.astype(BF16), wo_ref[0:GLA_WIDTH, :])
            + _dot(att_ref[...], wo_ref[GLA_WIDTH:GLA_WIDTH + ATT_WIDTH, :])
            + _dot(s.astype(BF16), wo_ref[GLA_WIDTH + ATT_WIDTH:D_MODEL, :]))
    x_in = jnp.where(j == 0, xc_ref[...], x_ref[...]) if split else x_ref[...]
    x1 = x_in + g1 * proj
    y2 = x1 * lax.rsqrt(jnp.mean(x1 * x1, axis=-1, keepdims=True) + EPS) * n2_ref[...]
    h2 = y2 * (1.0 + sc2) + sh2
    hid = jnp.maximum(_dot(h2.astype(BF16), w1_ref[...]), 0.0)
    mlp = _dot((hid * hid).astype(BF16), w2_ref[...])
    x2 = x1 + g2 * mlp
    if final:
        x2 = x2 * lax.rsqrt(jnp.mean(x2 * x2, axis=-1, keepdims=True) + EPS) * fn_ref[...]
    o_ref[...] = x2


def _outproj(x_ctx, x_rest, lat_skip, modsel, gla_f, gla_b, gg, att_o, y_c, u_c, gnw, hm, s5d, gluw, glub, wo, n2w,
             w1, w2, fnw, layer, final):
    bsz, tt, _ = gg.shape
    nb = tt // TOK_BLOCK
    first = CTX_LEN // TOK_BLOCK if final else 0
    nsteps = nb - first
    const2 = lambda b, j: (0, 0)
    lyr3 = lambda b, j: (layer, 0, 0)
    single = pl.Buffered(1)
    return pl.pallas_call(
        functools.partial(_out_kernel, first_block=first, final=final, split=lat_skip > 0),
        grid=(bsz, nsteps),
        in_specs=_stream_specs(lat_skip, first) + [
            pl.BlockSpec((None, 2, 6 * D_MODEL), lambda b, j: (b, 0, 0)),
            pl.BlockSpec((None, TOK_BLOCK, GLA_WIDTH), lambda b, j: (b, j + first, 0)),
            pl.BlockSpec((None, TOK_BLOCK, GLA_WIDTH), lambda b, j: (b, j + first, 0)),
            pl.BlockSpec((None, TOK_BLOCK, GLA_WIDTH), lambda b, j: (b, j + first, 0)),
            pl.BlockSpec((None, TOK_BLOCK, ATT_WIDTH), lambda b, j: (b, j + first, 0)),
            pl.BlockSpec((None, CH_BLOCK, S5_CW), lambda b, j: (b, j + first, 0)),
            pl.BlockSpec((None, CH_BLOCK, S5_CW), lambda b, j: (b, j + first, 0)),
            pl.BlockSpec((1, GLA_WIDTH), const2),
            pl.BlockSpec((GLA_WIDTH, GLA_WIDTH), const2),
            pl.BlockSpec((1, S5_CW), const2),
            pl.BlockSpec((None, S5_WIDTH, 2 * S5_WIDTH), lyr3, pipeline_mode=single),
            pl.BlockSpec((1, 2 * S5_WIDTH), const2),
            pl.BlockSpec((None, D_MODEL, D_MODEL), lyr3, pipeline_mode=single),
            pl.BlockSpec((1, D_MODEL), const2),
            pl.BlockSpec((None, D_MODEL, D_FF), lyr3, pipeline_mode=single),
            pl.BlockSpec((None, D_FF, D_MODEL), lyr3, pipeline_mode=single),
            pl.BlockSpec((1, D_MODEL), const2),
        ],
        out_specs=pl.BlockSpec((None, TOK_BLOCK, D_MODEL), lambda b, j: (b, j, 0)),
        out_shape=jax.ShapeDtypeStruct((bsz, nsteps * TOK_BLOCK, D_MODEL), F32),
        scratch_shapes=[pltpu.VMEM((S5_WIDTH // 128, TOK_BLOCK, 128), F32)],
        compiler_params=pltpu.CompilerParams(
            dimension_semantics=("arbitrary", "arbitrary"), vmem_limit_bytes=VMEM_LIMIT),
        name="outproj_mlp",
    )(x_ctx, x_rest, modsel, gla_f, gla_b, gg, att_o, y_c, u_c, gnw, hm, s5d, gluw, glub, wo, n2w, w1, w2, fnw)


def _rope_tables(tt):
    n_lat = tt - CTX_LEN
    rows = n_lat // GRID_W
    row = jnp.repeat(jnp.arange(rows, dtype=F32), GRID_W)
    col = jnp.tile(jnp.arange(GRID_W, dtype=F32), rows)
    n_freq = ATT_HD // 4
    inv_freq = ROPE_BASE ** (-jnp.arange(n_freq, dtype=F32) / n_freq)
    ang_r = row[:, None] * inv_freq
    ang_c = col[:, None] * inv_freq
    ang = jnp.concatenate([ang_r, ang_r, ang_c, ang_c], axis=-1)
    cos = jnp.concatenate([jnp.ones((CTX_LEN, ATT_HD), F32), jnp.cos(ang)], axis=0)
    sin = jnp.concatenate([jnp.zeros((CTX_LEN, ATT_HD), F32), jnp.sin(ang)], axis=0)
    up_quarter = (np.arange(ATT_HD) // 16) % 2 == 0
    sa = jnp.where(up_quarter, -sin, 0.0)
    sb = jnp.where(up_quarter, 0.0, sin)
    two = lambda t: jnp.concatenate([t, t], axis=-1)
    return two(cos), two(sa), two(sb)


def _pad_cols(w, width):
    return jnp.pad(w, ((0, 0), (0, 0), (0, width - w.shape[-1])))


def _layout_w_in(w_in):
    offs = np.cumsum([0, GLA_KW, GLA_KW, GLA_WIDTH, GLA_WIDTH, GLA_RANK, GLA_RANK, ATT_WIDTH, ATT_KVW, ATT_KVW,
                      S5_WIDTH])
    q, k, v, g, zf, zb, aq, ak, av, u = [w_in[:, :, offs[i]:offs[i + 1]] for i in range(10)]
    z = jnp.concatenate([zf, zb], axis=-1)
    cols = [v, g, _pad_cols(q, GLA_KPAD), _pad_cols(k, GLA_KPAD), _pad_cols(z, 128), aq, ak, av, u]
    return jnp.concatenate(cols, axis=-1).astype(BF16)


def _block_diag(t):
    s, g, a, b = t.shape
    eye = jnp.eye(g, dtype=t.dtype)
    return jnp.einsum('sgab,gk->sgakb', t, eye).reshape(s, g * a, g * b)


def kernel(x, c, ctx, c_ctx, w_mod, b_mod, norm1_w, norm2_w, w_in, gla_wa_f, gla_ba_f, gla_wa_b, gla_ba_b,
           gla_norm_w, attn_sink, s5_lam_re_f, s5_lam_im_f, s5_log_step_f, s5_b_re_f, s5_b_im_f, s5_c_re_f,
           s5_c_im_f, s5_lam_re_b, s5_lam_im_b, s5_log_step_b, s5_b_re_b, s5_b_im_b, s5_c_re_b, s5_c_im_b,
           s5_d, glu_w, glu_b, w_out, mlp_w1, mlp_w2, final_norm_w):
    bsz, seq, _ = x.shape
    n_layers = w_mod.shape[0]
    tt = CTX_LEN + seq
    assert bsz % 8 == 0 and bsz % IN_ROWS == 0 and bsz % GLA_ROWS == 0 and seq % TOK_BLOCK == 0 and seq >= 2 * TOK_BLOCK

    mod_rows = -(-(bsz + 1) // 8) * 8
    cvec = jnp.zeros((mod_rows, D_MODEL), F32).at[:bsz].set(c).at[bsz].set(c_ctx)
    mod = _modulation(cvec, w_mod, b_mod)
    mod_ctx = jnp.broadcast_to(mod[:, bsz][:, None], (n_layers, bsz, 6 * D_MODEL))
    modsel = jnp.stack([mod_ctx, mod[:, :bsz]], axis=2)

    w_in_p = _layout_w_in(w_in)
    wa_cat = jnp.zeros((n_layers, 128, 2 * GLA_KPAD), F32)
    wa_cat = wa_cat.at[:, 0:GLA_RANK, 0:GLA_KW].set(gla_wa_f)
    wa_cat = wa_cat.at[:, GLA_RANK:2 * GLA_RANK, GLA_KPAD:GLA_KPAD + GLA_KW].set(gla_wa_b).astype(BF16)
    ba_cat = jnp.zeros((n_layers, 1, 2 * GLA_KPAD), F32)
    ba_cat = ba_cat.at[:, 0, 0:GLA_KW].set(gla_ba_f).at[:, 0, GLA_KPAD:GLA_KPAD + GLA_KW].set(gla_ba_b)
    qscale = jnp.ones((1, 2 * GLA_KPAD), F32).at[:, 0:GLA_KW].set(GLA_DK ** -0.5)
    cos_t, sa_t, sb_t = _rope_tables(tt)
    gnw = jnp.tile(gla_norm_w, (1, GLA_HEADS))[:, None, :]
    head = np.arange(GLA_WIDTH) // GLA_DV
    hm = jnp.asarray(head[:, None] == head[None, :], BF16)
    wo_b = w_out.astype(BF16)
    w1_b = mlp_w1.astype(BF16)
    w2_b = mlp_w2.astype(BF16)
    gluw_b = glu_w.astype(BF16)

    s5_tab = _s5_tables(
        (s5_lam_re_f, s5_lam_im_f, s5_log_step_f, s5_b_re_f, s5_b_im_f, s5_c_re_f, s5_c_im_f),
        (s5_lam_re_b, s5_lam_im_b, s5_log_step_b, s5_b_re_b, s5_b_im_b, s5_c_re_b, s5_c_im_b))
    s5_dt = jnp.tile(s5_d, (1, S5_LC))[:, None, :]

    stream = (ctx, x, CTX_LEN // TOK_BLOCK)
    for l in range(n_layers):
        final = l == n_layers - 1
        gv, gg, gqk, gla, att, u_c = _inproj(*stream, modsel[l], norm1_w[l][None], w_in_p, wa_cat, ba_cat[l],
                                             qscale, cos_t, sa_t, sb_t, l)
        gla_f, gla_b = _gla(gqk, gv, gla)
        att_o = _swa(att, attn_sink[l])
        y_c = _s5(u_c, s5_tab, l)
        xs = _outproj(*stream, modsel[l], gla_f, gla_b, gg, att_o, y_c, u_c,
                      gnw[l], hm, s5_dt[l], gluw_b, glu_b[l][None], wo_b, norm2_w[l][None], w1_b, w2_b,
                      final_norm_w[None], l, final)
        stream = (xs, xs, 0)
    return xs
```

```python
import functools
import math

import jax
import jax.numpy as jnp
import numpy as np
from jax import lax
from jax.experimental import pallas as pl
from jax.experimental.pallas import tpu as pltpu

F32 = jnp.float32
BF16 = jnp.bfloat16

D_MODEL = 1024
D_FF = 4 * D_MODEL
CTX_LEN = 256
GRID_W = 64
EPS = 1e-6
NEG_INF = -1e30
LOG2E = math.log2(math.e)

GLA_HEADS = 4
GLA_DV = 96
GLA_DK = 48
GLA_WIDTH = GLA_HEADS * GLA_DV
GLA_KW = GLA_HEADS * GLA_DK
GLA_KPAD = 256
GLA_RANK = 16
GLA_TAU = 16.0
GLA_CHUNK = 64

ATT_HD = 64
ATT_HEADS = 6
ATT_KV_HEADS = 2
ATT_WIDTH = ATT_HEADS * ATT_HD
ATT_KVW = ATT_KV_HEADS * ATT_HD
WINDOW = 128
ATT_BLOCK = 128
ROPE_BASE = 10000.0

S5_WIDTH = 256
S5_GROUP = 16
S5_GROUPS = 16
S5_STATE = 64
S5_NS = S5_GROUPS * S5_STATE
S5_LC = 16
S5_CW = S5_LC * S5_WIDTH

TOK_BLOCK = 256
CH_BLOCK = TOK_BLOCK // S5_LC
GLA_ROWS = 2
IN_ROWS = 4
OUT_ROWS = 2

C_V, C_G, C_Q, C_K, C_Z, C_AQ, C_AK, C_AV, C_U = 0, 384, 768, 1024, 1280, 1408, 1792, 1920, 2048
IN_PAD = 2304
C_QK_END = C_K + GLA_KPAD

VMEM_LIMIT = 56 * 1024 * 1024


def _sigmoid(x):
    return 1.0 / (1.0 + jnp.exp(-x))


def _silu(x):
    return x * _sigmoid(x)


def _dot(a, b):
    return jnp.dot(a, b, preferred_element_type=F32)


def _dot_nt(a, b):
    return lax.dot_general(a, b, (((1,), (1,)), ((), ())), preferred_element_type=F32)


def _dot_tn(a, b):
    return lax.dot_general(a, b, (((0,), (0,)), ((), ())), preferred_element_type=F32)


def _split3(x):
    hi = x.astype(BF16)
    r1 = x - hi.astype(F32)
    mid = r1.astype(BF16)
    lo = (r1 - mid.astype(F32)).astype(BF16)
    return hi, mid, lo


def _mod_kernel(c_ref, w_ref, b_ref, o_ref):
    a = _silu(c_ref[...]).astype(BF16)
    o_ref[...] = _dot(a, w_ref[...].astype(BF16)) + b_ref[...]


def _modulation(cvec, w_mod, b_mod):
    n_layers = w_mod.shape[0]
    rows = cvec.shape[0]
    tn = 1536
    return pl.pallas_call(
        _mod_kernel,
        grid=(n_layers, 6 * D_MODEL // tn),
        in_specs=[
            pl.BlockSpec((rows, D_MODEL), lambda l, n: (0, 0)),
            pl.BlockSpec((None, D_MODEL, tn), lambda l, n: (l, 0, n)),
            pl.BlockSpec((None, 1, tn), lambda l, n: (l, 0, n)),
        ],
        out_specs=pl.BlockSpec((None, rows, tn), lambda l, n: (l, 0, n)),
        out_shape=jax.ShapeDtypeStruct((n_layers, rows, 6 * D_MODEL), F32),
        compiler_params=pltpu.CompilerParams(
            dimension_semantics=("arbitrary", "arbitrary"), vmem_limit_bytes=VMEM_LIMIT),
        name="modulation",
    )(cvec, w_mod, b_mod.reshape(n_layers, 1, 6 * D_MODEL))


def _inproj_kernel(xc_ref, x_ref, mod_ref, n1_ref, w_ref, wa_ref, ba_ref, qs_ref, cos_ref, sa_ref, sb_ref,
                   gv_ref, gg_ref, gqk_ref, gla_ref, att_ref, uc_ref, us_ref, *, split):
    j = pl.program_id(1)
    cos = cos_ref[...]
    sa = sa_ref[...]
    sb = sb_ref[...]

    def normed(r):
        xf = jnp.where(j == 0, xc_ref[r], x_ref[r]) if split else x_ref[r]
        y = xf * lax.rsqrt(jnp.mean(xf * xf, axis=-1, keepdims=True) + EPS) * n1_ref[...]
        m = mod_ref[r]
        mrow = jnp.where(j > 0, m[1:2, :], m[0:1, :])
        sh1 = mrow[:, 0:D_MODEL]
        sc1 = mrow[:, D_MODEL:2 * D_MODEL]
        return (y * (1.0 + sc1) + sh1).astype(BF16)

    def rope(t, reps):
        w = t.shape[-1]
        c3 = jnp.concatenate([cos] * reps, axis=-1) if reps > 1 else cos
        a3 = jnp.concatenate([sa] * reps, axis=-1) if reps > 1 else sa
        b3 = jnp.concatenate([sb] * reps, axis=-1) if reps > 1 else sb
        up = pltpu.roll(t, w - 16, 1)
        dn = pltpu.roll(t, 16, 1)
        return t * c3 + up * a3 + dn * b3

    def tail(r, p):
        gv_ref[r] = p[:, C_V:C_V + GLA_WIDTH].astype(BF16)
        gg_ref[r] = p[:, C_G:C_G + GLA_WIDTH]
        gqk_ref[r] = p[:, C_Q:C_QK_END] * qs_ref[...]
        z = p[:, C_Z:C_Z + 128].astype(BF16)
        zg = _dot(z, wa_ref[...]) + ba_ref[...]
        gla_ref[r] = (jnp.minimum(zg, 0.0) - jnp.log1p(jnp.exp(-jnp.abs(zg)))) * (1.0 / GLA_TAU)

        aq = rope(p[:, C_AQ:C_AQ + ATT_WIDTH], 3) * (ATT_HD ** -0.5 * LOG2E)
        ak = rope(p[:, C_AK:C_AK + ATT_KVW], 1)
        av = p[:, C_AV:C_AV + ATT_KVW]
        left = lax.broadcasted_iota(jnp.int32, ak.shape, 1) < ATT_HD

        def expand(t):
            sw = pltpu.roll(t, ATT_HD, 1)
            return [jnp.where(left, t, sw), t, jnp.where(left, sw, t)]

        att_ref[r] = jnp.concatenate([aq] + expand(ak) + expand(av), axis=-1).astype(BF16)
        for half in range(S5_WIDTH // 128):
            us_ref[r, half] = p[:, C_U + half * 128:C_U + (half + 1) * 128]
        for st in range(S5_LC):
            for half in range(S5_WIDTH // 128):
                lane0 = st * S5_WIDTH + half * 128
                uc_ref[r, :, lane0:lane0 + 128] = us_ref[r, half, pl.ds(st, CH_BLOCK, stride=S5_LC), :]

    h = normed(0)
    prev = None
    for r in range(IN_ROWS):
        proj = _dot(h, w_ref[...])
        if r + 1 < IN_ROWS:
            h = normed(r + 1)
        if prev is not None:
            tail(r - 1, prev)
        prev = proj
    tail(IN_ROWS - 1, prev)


def _stream_specs(lat_skip, first=0, rows=None):
    return [pl.BlockSpec((rows, TOK_BLOCK, D_MODEL), lambda b, j: (b, 0, 0)),
            pl.BlockSpec((rows, TOK_BLOCK, D_MODEL), lambda b, j: (b, jnp.maximum(j + first - lat_skip, 0), 0))]


def _inproj(x_ctx, x_rest, lat_skip, modsel, n1w, w_in_p, wa_cat, ba_cat, qscale, cos_t, sa_t, sb_t, layer):
    bsz = x_ctx.shape[0]
    tt = cos_t.shape[0]
    nb = tt // TOK_BLOCK
    const = lambda b, j: (0, 0)
    lyr3 = lambda b, j: (layer, 0, 0)
    tok = lambda w: pl.BlockSpec((IN_ROWS, TOK_BLOCK, w), lambda b, j: (b, j, 0))
    return pl.pallas_call(
        functools.partial(_inproj_kernel, split=lat_skip > 0),
        grid=(bsz // IN_ROWS, nb),
        in_specs=_stream_specs(lat_skip, rows=IN_ROWS) + [
            pl.BlockSpec((IN_ROWS, 2, 6 * D_MODEL), lambda b, j: (b, 0, 0)),
            pl.BlockSpec((1, D_MODEL), const),
            pl.BlockSpec((None, D_MODEL, IN_PAD), lyr3),
            pl.BlockSpec((None, 128, 2 * GLA_KPAD), lyr3),
            pl.BlockSpec((1, 2 * GLA_KPAD), const),
            pl.BlockSpec((1, 2 * GLA_KPAD), const),
            pl.BlockSpec((TOK_BLOCK, 128), lambda b, j: (j, 0)),
            pl.BlockSpec((TOK_BLOCK, 128), lambda b, j: (j, 0)),
            pl.BlockSpec((TOK_BLOCK, 128), lambda b, j: (j, 0)),
        ],
        out_specs=[tok(GLA_WIDTH), tok(GLA_WIDTH), tok(2 * GLA_KPAD), tok(2 * GLA_KPAD), tok(3 * ATT_WIDTH),
                   pl.BlockSpec((IN_ROWS, CH_BLOCK, S5_CW), lambda b, j: (b, j, 0))],
        out_shape=[
            jax.ShapeDtypeStruct((bsz, tt, GLA_WIDTH), BF16),
            jax.ShapeDtypeStruct((bsz, tt, GLA_WIDTH), F32),
            jax.ShapeDtypeStruct((bsz, tt, 2 * GLA_KPAD), F32),
            jax.ShapeDtypeStruct((bsz, tt, 2 * GLA_KPAD), F32),
            jax.ShapeDtypeStruct((bsz, tt, 3 * ATT_WIDTH), BF16),
            jax.ShapeDtypeStruct((bsz, tt // S5_LC, S5_CW), F32),
        ],
        scratch_shapes=[pltpu.VMEM((IN_ROWS, S5_WIDTH // 128, TOK_BLOCK, 128), F32)],
        compiler_params=pltpu.CompilerParams(
            dimension_semantics=("arbitrary", "arbitrary"), vmem_limit_bytes=VMEM_LIMIT),
        name="inproj",
    )(x_ctx, x_rest, modsel, n1w, w_in_p, wa_cat, ba_cat, qscale, cos_t, sa_t, sb_t)


def _gla_kernel(qkf_ref, vf_ref, laf_ref, qkb_ref, vb_ref, lab_ref, tri_ref, tril_ref, hm_ref, vm_ref, bd_ref, of_ref, ob_ref, stf_ref, stb_ref):
    j = pl.program_id(1)

    @pl.when(j == 0)
    def _():
        stf_ref[...] = jnp.zeros_like(stf_ref)
        stb_ref[...] = jnp.zeros_like(stb_ref)

    n_chunks = TOK_BLOCK // GLA_CHUNK
    chunk = lambda t, c: t[c * GLA_CHUNK:(c + 1) * GLA_CHUNK]
    dirs = []
    for r in range(GLA_ROWS):
        dirs.append(((qkf_ref.at[r], vf_ref.at[r], laf_ref.at[r]), None, 0, GLA_CHUNK - 1, of_ref.at[r],
                     stf_ref.at[r], list(range(n_chunks))))
        dirs.append(((qkb_ref.at[r], vb_ref.at[r], lab_ref.at[r]), None, 1, 0, ob_ref.at[r],
                     stb_ref.at[r], list(reversed(range(n_chunks)))))

    cum = []
    for (_, _, la_ref), _, d, _, _, _, _ in dirs:
        la = la_ref[...]
        hi = la.astype(BF16)
        lo = (la - hi.astype(F32)).astype(BF16)
        cum.append(_dot(tri_ref[d], hi) + _dot(tri_ref[d], lo))

    ops = []
    for ((qk_ref, v_ref, _), _, d, last_row, _, _, _), b in zip(dirs, cum):
        q = qk_ref[:, 0:GLA_KPAD]
        k = qk_ref[:, GLA_KPAD:2 * GLA_KPAD]
        vb = v_ref[...]
        bl = [chunk(b, c)[last_row:last_row + 1, :] for c in range(n_chunks)]
        blx = jnp.concatenate([jnp.broadcast_to(t, (GLA_CHUNK, GLA_KPAD)) for t in bl], axis=0)
        qb = (q * jnp.exp(b)).astype(BF16)
        kb = (k * jnp.exp(-b)).astype(BF16)
        kd = (k * jnp.exp(blx - b)).astype(BF16)
        ops.append((qb, kb, kd, vb, bl))

    scores = []
    for (qb, kb, _, _, _), (_, _, d, _, _, _, _) in zip(ops, dirs):
        keep = tril_ref[d] > 0
        per = []
        for c in range(n_chunks):
            kst = jnp.concatenate([chunk(kb, c) * hm_ref[h] for h in range(GLA_HEADS)], axis=0)
            per.append(jnp.where(keep, _dot_nt(chunk(qb, c), kst), 0.0).astype(BF16))
        scores.append(per)

    bdmask = bd_ref[...]
    intra, inc = [], []
    for (_, _, kd, vb, _), per in zip(ops, scores):
        oi, ds = [], []
        for c in range(n_chunks):
            vc = chunk(vb, c)
            vbd = jnp.concatenate([vc * vm_ref[h] for h in range(GLA_HEADS)], axis=0)
            oi.append(_dot(per[c], vbd))
            ds.append(_dot_tn(vc, chunk(kd, c)) * bdmask)
        intra.append(oi)
        inc.append(ds)

    states = [d[5][...] for d in dirs]
    for i in range(n_chunks):
        for si, ((_, _, _, _, o_ref, _, order), (qb, _, _, _, bl)) in enumerate(zip(dirs, ops)):
            c = order[i]
            o_ref[c * GLA_CHUNK:(c + 1) * GLA_CHUNK, :] = (
                intra[si][c] + _dot_nt(chunk(qb, c), states[si].astype(BF16)))
            states[si] = states[si] * jnp.exp(bl[c]) + inc[si][c]
    for d, st in zip(dirs, states):
        d[5][...] = st


def _gla_masks():
    r = np.arange(GLA_CHUNK)
    lower = (r[None, :] <= r[:, None]).astype(np.float32)
    tri1 = np.stack([lower, lower.T])
    n_chunks = TOK_BLOCK // GLA_CHUNK
    tri = np.stack([np.kron(np.eye(n_chunks, dtype=np.float32), t) for t in tri1])
    tril = np.tile(tri1, (1, 1, GLA_HEADS))
    klane = np.arange(GLA_KPAD)
    vlane = np.arange(GLA_WIDTH)
    hm = np.stack([np.broadcast_to((klane // GLA_DK) == h, (GLA_CHUNK, GLA_KPAD)) for h in range(GLA_HEADS)])
    vm = np.stack([np.broadcast_to((vlane // GLA_DV) == h, (GLA_CHUNK, GLA_WIDTH)) for h in range(GLA_HEADS)])
    bd = ((vlane[:, None] // GLA_DV) == (klane[None, :] // GLA_DK)).astype(np.float32)
    return (jnp.asarray(tri, BF16), jnp.asarray(tril, F32), jnp.asarray(hm, BF16), jnp.asarray(vm, BF16),
            jnp.asarray(bd))


def _scan_block(d, j, n_ctx, n_all):
    bwd = jnp.where(j < n_ctx, n_ctx - 1 - j, n_all + n_ctx - 1 - j)
    return jnp.where(d == 0, j, bwd)


def _gla(gqk, gv, gla):
    bsz, tt, _ = gv.shape
    nb = tt // TOK_BLOCK
    n_ctx = CTX_LEN // TOK_BLOCK
    tri, tril, hm, vm, bd = _gla_masks()
    const2 = lambda b, j: (0, 0)
    const3 = lambda b, j: (0, 0, 0)
    fwd = lambda b, j: (b, j, 0)
    bwd = lambda b, j: (b, _scan_block(1, j, n_ctx, nb), 0)
    bwd_gate = lambda b, j: (b, _scan_block(1, j, n_ctx, nb), 1)
    tok = lambda w, imap: pl.BlockSpec((GLA_ROWS, TOK_BLOCK, w), imap)
    return pl.pallas_call(
        _gla_kernel,
        grid=(bsz // GLA_ROWS, nb),
        in_specs=[
            tok(2 * GLA_KPAD, fwd),
            tok(GLA_WIDTH, fwd),
            tok(GLA_KPAD, fwd),
            tok(2 * GLA_KPAD, bwd),
            tok(GLA_WIDTH, bwd),
            tok(GLA_KPAD, bwd_gate),
            pl.BlockSpec((2, TOK_BLOCK, TOK_BLOCK), const3),
            pl.BlockSpec((2, GLA_CHUNK, GLA_HEADS * GLA_CHUNK), const3),
            pl.BlockSpec((GLA_HEADS, GLA_CHUNK, GLA_KPAD), const3),
            pl.BlockSpec((GLA_HEADS, GLA_CHUNK, GLA_WIDTH), const3),
            pl.BlockSpec((GLA_WIDTH, GLA_KPAD), const2),
        ],
        out_specs=[tok(GLA_WIDTH, fwd), tok(GLA_WIDTH, bwd)],
        out_shape=[jax.ShapeDtypeStruct((bsz, tt, GLA_WIDTH), F32)] * 2,
        scratch_shapes=[pltpu.VMEM((GLA_ROWS, GLA_WIDTH, GLA_KPAD), F32)] * 2,
        compiler_params=pltpu.CompilerParams(
            dimension_semantics=("arbitrary", "arbitrary"), vmem_limit_bytes=VMEM_LIMIT),
        name="gla",
    )(gqk, gv, gla, gqk, gv, gla, tri, tril, hm, vm, bd)


def _swa_kernel(sink_ref, mask_ref, att_ref, o_ref, *, tt):
    j = pl.program_id(1)
    n_sub = TOK_BLOCK // ATT_BLOCK
    n_ctx_blocks = CTX_LEN // TOK_BLOCK
    last_q = tt // ATT_BLOCK - 1
    lane = lax.broadcasted_iota(jnp.int32, (ATT_BLOCK, 2 * ATT_HD), 1)
    left = lane < ATT_HD
    row2 = lax.broadcasted_iota(jnp.int32, (2 * ATT_BLOCK, 1), 0)
    units = [(sub, p) for sub in range(n_sub) for p in range(ATT_HEADS // 2)]

    def run(windowed):
        def score(sub, p):
            q0 = pl.multiple_of(j * TOK_BLOCK + sub * ATT_BLOCK, ATT_BLOCK)
            qp = att_ref[pl.ds(q0, ATT_BLOCK), p * 128:(p + 1) * 128]
            zero = jnp.zeros_like(qp)
            q2 = jnp.concatenate([jnp.where(left, qp, zero), jnp.where(left, zero, qp)], axis=0)
            kcol = ATT_WIDTH + p * 128
            s_c = _dot_nt(q2, att_ref[0:CTX_LEN, kcol:kcol + 128])
            if not windowed:
                return s_c, None, None
            start = pl.multiple_of(jnp.minimum(q0 - ATT_BLOCK, tt - 3 * ATT_BLOCK), ATT_BLOCK)
            s_w = jnp.where(keeps[sub], _dot_nt(q2, att_ref[pl.ds(start, 3 * ATT_BLOCK), kcol:kcol + 128]), NEG_INF)
            return s_c, s_w, start

        def softmax(p, s_c, s_w, start):
            sink_col = jnp.where(row2 < ATT_BLOCK, sink_ref[2 * p], sink_ref[2 * p + 1]) * LOG2E
            m = jnp.maximum(jnp.max(s_c, axis=-1, keepdims=True), sink_col)
            if windowed:
                m = jnp.maximum(m, jnp.max(s_w, axis=-1, keepdims=True))
            e_c = jnp.exp2(s_c - m)
            den = jnp.sum(e_c, axis=-1, keepdims=True) + jnp.exp2(sink_col - m)
            e_w = None
            if windowed:
                e_w = jnp.exp2(s_w - m)
                den = den + jnp.sum(e_w, axis=-1, keepdims=True)
                e_w = e_w.astype(BF16)
            return e_c.astype(BF16), e_w, den, start

        def values(sub, p, e_c, e_w, den, start):
            vcol = 2 * ATT_WIDTH + p * 128
            pv = _dot(e_c, att_ref[0:CTX_LEN, vcol:vcol + 128])
            if windowed:
                pv = pv + _dot(e_w, att_ref[pl.ds(start, 3 * ATT_BLOCK), vcol:vcol + 128])
            o2 = pv / den
            o = jnp.where(left, o2[0:ATT_BLOCK], o2[ATT_BLOCK:2 * ATT_BLOCK])
            o_ref[sub * ATT_BLOCK:(sub + 1) * ATT_BLOCK, p * 128:(p + 1) * 128] = o.astype(BF16)

        keeps = []
        if windowed:
            for sub in range(n_sub):
                qi = j * n_sub + sub
                kind = jnp.where(qi == CTX_LEN // ATT_BLOCK, 0, jnp.where(qi == last_q, 2, 1))
                keeps.append(mask_ref[kind] > 0)

        n = len(units)
        sc, pr = {}, {}
        for t in range(n + 2):
            if t < n:
                sc[t] = score(*units[t])
            if 0 <= t - 1 < n:
                pr[t - 1] = softmax(units[t - 1][1], *sc.pop(t - 1))
            if 0 <= t - 2 < n:
                values(*units[t - 2], *pr.pop(t - 2))

    @pl.when(j < n_ctx_blocks)
    def _():
        run(False)

    @pl.when(j >= n_ctx_blocks)
    def _():
        run(True)


def _swa_masks():
    r = np.arange(2 * ATT_BLOCK)[:, None] % ATT_BLOCK
    c = np.arange(3 * ATT_BLOCK)[None, :]
    near = lambda delta: np.abs(r - c + delta) <= WINDOW
    first = near(ATT_BLOCK) & (c >= ATT_BLOCK)
    return jnp.asarray(np.stack([first, near(ATT_BLOCK), near(2 * ATT_BLOCK)]), F32)


def _swa(att, sink):
    bsz, tt, _ = att.shape
    nb = tt // TOK_BLOCK
    return pl.pallas_call(
        functools.partial(_swa_kernel, tt=tt),
        grid=(bsz, nb),
        in_specs=[
            pl.BlockSpec(memory_space=pltpu.SMEM),
            pl.BlockSpec((3, 2 * ATT_BLOCK, 3 * ATT_BLOCK), lambda b, j: (0, 0, 0)),
            pl.BlockSpec((None, tt, 3 * ATT_WIDTH), lambda b, j: (b, 0, 0)),
        ],
        out_specs=pl.BlockSpec((None, TOK_BLOCK, ATT_WIDTH), lambda b, j: (b, j, 0)),
        out_shape=jax.ShapeDtypeStruct((bsz, tt, ATT_WIDTH), BF16),
        compiler_params=pltpu.CompilerParams(
            dimension_semantics=("arbitrary", "arbitrary"), vmem_limit_bytes=VMEM_LIMIT),
        name="swa",
    )(sink, _swa_masks(), att)


def _s5_prep_kernel(lr_ref, li_ref, ls_ref, br_ref, bi_ref, cr_ref, ci_ref,
                    a16_ref, mre_ref, mim_ref, rre_ref, rim_ref, k_ref):
    rows = S5_LC * S5_GROUP
    rowg = lax.broadcasted_iota(jnp.int32, (rows, S5_NS), 0) >> 4
    colg = lax.broadcasted_iota(jnp.int32, (rows, S5_NS), 1) >> 6
    same_group = rowg == colg
    taps_t = []
    for d in range(2):
        lr = jnp.minimum(lr_ref[d], -1e-4)
        li = li_ref[d]
        dt = jnp.exp(ls_ref[d])
        mag = jnp.exp(lr * dt)
        ar = mag * jnp.cos(li * dt)
        ai = mag * jnp.sin(li * dt)
        den = lr * lr + li * li
        fr = ((ar - 1.0) * lr + ai * li) / den
        fi = (ai * lr - (ar - 1.0) * li) / den
        br = br_ref[d]
        bi = bi_ref[d]
        bbr = fr * br - fi * bi
        bbi = fr * bi + fi * br
        cr = cr_ref[d]
        ci = ci_ref[d]
        pr = jnp.ones_like(ar)
        pi = jnp.zeros_like(ar)
        cp_re, cp_im = [], []
        for tau in range(S5_LC + 1):
            cpr = cr * pr - ci * pi
            cpi = cr * pi + ci * pr
            if tau < S5_LC:
                mre_ref[d, tau] = pr * bbr - pi * bbi
                mim_ref[d, tau] = pr * bbi + pi * bbr
                cp_re.append(cpr)
                cp_im.append(cpi)
            if tau >= 1:
                rre_ref[d, tau - 1] = cpr
                rim_ref[d, tau - 1] = -cpi
            if tau == S5_LC:
                a16_ref[d, 0:1, :] = pr
                a16_ref[d, 1:2, :] = pi
            pr, pi = pr * ar - pi * ai, pr * ai + pi * ar
        bd_re = jnp.where(same_group, jnp.concatenate([bbr] * S5_GROUPS, axis=0), 0.0)
        bd_im = jnp.where(same_group, jnp.concatenate([bbi] * S5_GROUPS, axis=0), 0.0)
        if d == 1:
            cp_re, cp_im = cp_re[::-1], cp_im[::-1]
        hp = lax.Precision.HIGHEST
        dims = (((1,), (1,)), ((), ()))
        taps_t.append(
            lax.dot_general(bd_re, jnp.concatenate(cp_re, axis=0), dims, precision=hp, preferred_element_type=F32)
            - lax.dot_general(bd_im, jnp.concatenate(cp_im, axis=0), dims, precision=hp,
                              preferred_element_type=F32))
    kf, kb_rev = taps_t
    tap0 = lax.broadcasted_iota(jnp.int32, kf.shape, 1) < S5_GROUP
    k_ref[0] = jnp.where(tap0, kf + pltpu.roll(kb_rev, S5_GROUP, 1), kf)
    k_ref[1] = kb_rev


def _s5_tables(fwd, bwd):
    n_layers = fwd[0].shape[0]
    both = lambda i: jnp.stack([fwd[i], bwd[i]], axis=1)
    lam_re = both(0).reshape(n_layers, 2, 1, S5_NS)
    lam_im = both(1).reshape(n_layers, 2, 1, S5_NS)
    log_step = jnp.repeat(both(2), S5_STATE, axis=-1).reshape(n_layers, 2, 1, S5_NS)
    b_hn = lambda t: t.transpose(0, 1, 4, 2, 3).reshape(n_layers, 2, S5_GROUP, S5_NS)
    c_hn = lambda t: t.transpose(0, 1, 3, 2, 4).reshape(n_layers, 2, S5_GROUP, S5_NS)
    vec = pl.BlockSpec((None, 2, 1, S5_NS), lambda l: (l, 0, 0, 0))
    mat = pl.BlockSpec((None, 2, S5_GROUP, S5_NS), lambda l: (l, 0, 0, 0))
    tab = pl.BlockSpec((None, 2, S5_LC, S5_GROUP, S5_NS), lambda l: (l, 0, 0, 0, 0))
    taps = S5_LC * S5_GROUP
    a16, mre, mim, rre, rim, k = pl.pallas_call(
        _s5_prep_kernel,
        grid=(n_layers,),
        in_specs=[vec, vec, vec, mat, mat, mat, mat],
        out_specs=[
            pl.BlockSpec((None, 2, 2, S5_NS), lambda l: (l, 0, 0, 0)),
            tab, tab, tab, tab,
            pl.BlockSpec((None, 2, S5_WIDTH, taps), lambda l: (l, 0, 0, 0)),
        ],
        out_shape=[
            jax.ShapeDtypeStruct((n_layers, 2, 2, S5_NS), F32),
            jax.ShapeDtypeStruct((n_layers, 2, S5_LC, S5_GROUP, S5_NS), F32),
            jax.ShapeDtypeStruct((n_layers, 2, S5_LC, S5_GROUP, S5_NS), F32),
            jax.ShapeDtypeStruct((n_layers, 2, S5_LC, S5_GROUP, S5_NS), F32),
            jax.ShapeDtypeStruct((n_layers, 2, S5_LC, S5_GROUP, S5_NS), F32),
            jax.ShapeDtypeStruct((n_layers, 2, S5_WIDTH, taps), F32),
        ],
        compiler_params=pltpu.CompilerParams(dimension_semantics=("arbitrary",), vmem_limit_bytes=VMEM_LIMIT),
        name="s5_prep",
    )(lam_re, lam_im, log_step, b_hn(both(3)), b_hn(both(4)), c_hn(both(5)), c_hn(both(6)))

    n_pairs = S5_GROUPS // 2
    lane_group = np.arange(128) // S5_STATE

    def pair_rows(t, flip):
        t = t[:, ::-1] if flip else t
        t = t.reshape(n_layers, S5_LC, S5_GROUP, n_pairs, 128).transpose(0, 3, 1, 2, 4)
        own = jnp.asarray(lane_group[None, :] == np.arange(2)[:, None])
        t = jnp.where(own[None, None, :, None, None, :], t[:, :, None], 0.0)
        return t.reshape(n_layers, n_pairs, 2 * taps, 128)

    m_f = jnp.concatenate([pair_rows(mre[:, 0], True), pair_rows(mim[:, 0], True)], axis=-1).astype(BF16)
    m_b = jnp.concatenate([pair_rows(mre[:, 1], False), pair_rows(mim[:, 1], False)], axis=-1).astype(BF16)

    r_t = jnp.concatenate([pair_rows(rre[:, 0], False), pair_rows(rim[:, 0], False),
                           pair_rows(rre[:, 1], True), pair_rows(rim[:, 1], True)], axis=-1).astype(BF16)

    zeros = jnp.zeros((n_layers, S5_WIDTH, taps), F32)
    f2 = jnp.concatenate([zeros, k[:, 0]], axis=-1)
    b2 = jnp.concatenate([k[:, 1], zeros], axis=-1)
    t_f = jnp.stack([f2[..., taps - S5_GROUP * st:2 * taps - S5_GROUP * st] for st in range(S5_LC)], axis=2)
    t_b = jnp.stack([b2[..., S5_GROUP * (S5_LC - 1 - st):S5_GROUP * (S5_LC - 1 - st) + taps]
                     for st in range(S5_LC)], axis=2)
    later = jnp.asarray(np.arange(taps)[None, :] // S5_GROUP >= np.arange(S5_LC)[:, None])
    toe = jnp.where(later[None, None], t_f, t_b)
    toe = toe.reshape(n_layers, S5_GROUPS, S5_GROUP, S5_LC, taps).transpose(0, 1, 3, 2, 4)
    toe = toe.reshape(n_layers, S5_GROUPS, taps, taps).astype(BF16)
    return a16, m_f, m_b, r_t, toe


def _lane_block_transpose(cols):
    lane = lax.broadcasted_iota(jnp.int32, cols[0].shape, 1)
    out = [None] * 32
    for ah in range(2):
        for bh in range(2):
            v = [cols[(ah * 8 + al) * 2 + bh] for al in range(8)]
            for kbit in range(3):
                width = 16 << kbit
                low = ((lane >> (4 + kbit)) & 1) == 0
                nxt = list(v)
                for i in range(8):
                    if i & (1 << kbit):
                        continue
                    lo_v, hi_v = v[i], v[i | (1 << kbit)]
                    nxt[i] = jnp.where(low, lo_v, pltpu.roll(hi_v, width, 1))
                    nxt[i | (1 << kbit)] = jnp.where(low, pltpu.roll(lo_v, 128 - width, 1), hi_v)
                v = nxt
            for bl in range(8):
                out[(bh * 8 + bl) * 2 + ah] = v[bl]
    return out


def _s5_increments(ush, m_ref, d_re, d_im):
    for p in range(S5_GROUPS // 2):
        dp = _dot(ush[:, p * 512:(p + 1) * 512], m_ref[p])
        d_re[p] = dp[:, 0:128]
        d_im[p] = dp[:, 128:256]


def _s5_recurrence(a16_ref, d_re, d_im, x_re, x_im, sr_ref, si_ref, order, bsz):
    ar = a16_ref[0:1, :]
    ai = a16_ref[1:2, :]
    sr = sr_ref[...]
    si = si_ref[...]
    n_slabs = S5_NS // 128
    gather = lambda ref, rows: jnp.concatenate([ref[p, rows, :] for p in range(n_slabs)], axis=1)
    for c in order:
        rows = pl.ds(c, bsz, stride=CH_BLOCK)
        for p in range(n_slabs):
            x_re[p, rows, :] = sr[:, p * 128:(p + 1) * 128]
            x_im[p, rows, :] = si[:, p * 128:(p + 1) * 128]
        sr, si = ar * sr - ai * si + gather(d_re, rows), ar * si + ai * sr + gather(d_im, rows)
    sr_ref[...] = sr
    si_ref[...] = si


def _s5_fwd_kernel(uc_ref, a16_ref, m_ref, ush_ref, xin_ref, d_re, d_im, x_re, x_im, sr_ref, si_ref, *, bsz):
    @pl.when(pl.program_id(0) == 0)
    def _():
        sr_ref[...] = jnp.zeros_like(sr_ref)
        si_ref[...] = jnp.zeros_like(si_ref)

    rows = bsz * CH_BLOCK
    ub = uc_ref[...].reshape(rows, S5_CW).astype(BF16)
    packed = pltpu.bitcast(ub, jnp.uint32)
    cols = _lane_block_transpose([packed[:, v * 128:(v + 1) * 128] for v in range(32)])
    ush = pltpu.bitcast(jnp.concatenate(cols, axis=1), BF16)
    ush_ref[...] = ush.reshape(bsz, CH_BLOCK, S5_CW)
    _s5_increments(ush, m_ref, d_re, d_im)
    _s5_recurrence(a16_ref, d_re, d_im, x_re, x_im, sr_ref, si_ref, range(CH_BLOCK), bsz)
    for p in range(S5_NS // 128):
        xin_ref[:, :, p * 128:(p + 1) * 128] = x_re[p].astype(BF16).reshape(bsz, CH_BLOCK, 128)
        xin_ref[:, :, S5_NS + p * 128:S5_NS + (p + 1) * 128] = x_im[p].astype(BF16).reshape(bsz, CH_BLOCK, 128)


def _s5_bwd_kernel(ush_ref, xf_ref, a16_ref, m_ref, toe_ref, r_ref, y_ref, d_re, d_im, x_re, x_im, sr_ref, si_ref,
                   *, bsz):
    @pl.when(pl.program_id(0) == 0)
    def _():
        sr_ref[...] = jnp.zeros_like(sr_ref)
        si_ref[...] = jnp.zeros_like(si_ref)

    rows = bsz * CH_BLOCK
    ush = ush_ref[...].reshape(rows, S5_CW)
    _s5_increments(ush, m_ref, d_re, d_im)
    _s5_recurrence(a16_ref, d_re, d_im, x_re, x_im, sr_ref, si_ref, reversed(range(CH_BLOCK)), bsz)
    xf = xf_ref[...].reshape(rows, 2 * S5_NS)
    ycols = []
    for p in range(S5_GROUPS // 2):
        lanes = slice(p * 128, (p + 1) * 128)
        xcat = jnp.concatenate([xf[:, lanes], xf[:, S5_NS + p * 128:S5_NS + (p + 1) * 128],
                                x_re[p].astype(BF16), x_im[p].astype(BF16)], axis=1)
        carry = _dot_nt(xcat, r_ref[p])
        for g2 in range(2):
            g = 2 * p + g2
            yg = carry[:, g2 * 256:(g2 + 1) * 256] + _dot(ush[:, g * 256:(g + 1) * 256], toe_ref[g])
            ycols += [yg[:, 0:128], yg[:, 128:256]]
    ycols = _lane_block_transpose(ycols)
    y_ref[...] = jnp.concatenate(ycols, axis=1).reshape(bsz, CH_BLOCK, S5_CW)


def _s5(u_c, tables, layer):
    a16, m_f, m_b, r_t, toe = tables
    bsz, n_rows, _ = u_c.shape
    nb = n_rows // CH_BLOCK
    n_ctx = CTX_LEN // TOK_BLOCK
    rows = bsz * CH_BLOCK
    blk3 = lambda w: (bsz, CH_BLOCK, w)
    scratch = [pltpu.VMEM((S5_NS // 128, rows, 128), F32)] * 4 + [pltpu.VMEM((bsz, S5_NS), F32)] * 2
    params = pltpu.CompilerParams(dimension_semantics=("arbitrary",), vmem_limit_bytes=VMEM_LIMIT)
    ush, xin_f = pl.pallas_call(
        functools.partial(_s5_fwd_kernel, bsz=bsz),
        grid=(nb,),
        in_specs=[
            pl.BlockSpec(blk3(S5_CW), lambda j: (0, j, 0)),
            pl.BlockSpec((None, None, 2, S5_NS), lambda j: (layer, 0, 0, 0)),
            pl.BlockSpec((None, S5_GROUPS // 2, 512, 256), lambda j: (layer, 0, 0, 0)),
        ],
        out_specs=[
            pl.BlockSpec(blk3(S5_CW), lambda j: (0, j, 0)),
            pl.BlockSpec(blk3(2 * S5_NS), lambda j: (0, j, 0)),
        ],
        out_shape=[
            jax.ShapeDtypeStruct((bsz, n_rows, S5_CW), BF16),
            jax.ShapeDtypeStruct((bsz, n_rows, 2 * S5_NS), BF16),
        ],
        scratch_shapes=scratch,
        compiler_params=params,
        name="s5_fwd",
    )(u_c, a16, m_f)
    back = lambda j: (0, _scan_block(1, j, n_ctx, nb), 0)
    return pl.pallas_call(
        functools.partial(_s5_bwd_kernel, bsz=bsz),
        grid=(nb,),
        in_specs=[
            pl.BlockSpec(blk3(S5_CW), back),
            pl.BlockSpec(blk3(2 * S5_NS), back),
            pl.BlockSpec((None, None, 2, S5_NS), lambda j: (layer, 1, 0, 0)),
            pl.BlockSpec((None, S5_GROUPS // 2, 512, 256), lambda j: (layer, 0, 0, 0)),
            pl.BlockSpec((None, S5_GROUPS, 256, 256), lambda j: (layer, 0, 0, 0)),
            pl.BlockSpec((None, S5_GROUPS // 2, 512, 512), lambda j: (layer, 0, 0, 0)),
        ],
        out_specs=pl.BlockSpec(blk3(S5_CW), back),
        out_shape=jax.ShapeDtypeStruct((bsz, n_rows, S5_CW), F32),
        scratch_shapes=scratch,
        compiler_params=params,
        name="s5_bwd",
    )(ush, xin_f, a16, m_b, toe, r_t)


def _out_kernel(xc_ref, x_ref, mod_ref, gof_ref, gob_ref, g_ref, att_ref, y_ref, uc_ref, gnw_ref, hm_ref, d_ref, gluw_ref,
                glub_ref, wo_ref, n2_ref, w1_ref, w2_ref, fn_ref, o_ref, ys_ref, *, first_block, final, split):
    j = pl.program_id(1) + first_block

    def mixed(r):
        m = mod_ref[r]
        mrow = jnp.where(j > 0, m[1:2, :], m[0:1, :])
        g1 = mrow[:, 2 * D_MODEL:3 * D_MODEL]
        sh2 = mrow[:, 3 * D_MODEL:4 * D_MODEL]
        sc2 = mrow[:, 4 * D_MODEL:5 * D_MODEL]
        g2 = mrow[:, 5 * D_MODEL:6 * D_MODEL]

        o = gof_ref[r] + gob_ref[r]
        ms = _dot((o * o).astype(BF16), hm_ref[...]) * (1.0 / GLA_DV)
        a = o * lax.rsqrt(ms + EPS) * gnw_ref[...] * _silu(g_ref[r])

        yc = y_ref[r] + d_ref[...] * uc_ref[r]
        for st in range(S5_LC):
            for half in range(S5_WIDTH // 128):
                lane0 = st * S5_WIDTH + half * 128
                ys_ref[r, half, pl.ds(st, CH_BLOCK, stride=S5_LC), :] = yc[:, lane0:lane0 + 128]
        yy = jnp.concatenate([ys_ref[r, half] for half in range(S5_WIDTH // 128)], axis=1)
        ge = 0.5 * yy * (1.0 + jnp.tanh(math.sqrt(2.0 / math.pi) * (yy + 0.044715 * (yy * yy * yy))))
        z = _dot(ge.astype(BF16), gluw_ref[...]) + glub_ref[...]
        s = z[:, 0:S5_WIDTH] * _sigmoid(z[:, S5_WIDTH:2 * S5_WIDTH])

        proj = (_dot(a.astype(BF16), wo_ref[0:GLA_WIDTH, :])
                + _dot(att_ref[r], wo_ref[GLA_WIDTH:GLA_WIDTH + ATT_WIDTH, :])
                + _dot(s.astype(BF16), wo_ref[GLA_WIDTH + ATT_WIDTH:D_MODEL, :]))
        x_in = jnp.where(j == 0, xc_ref[r], x_ref[r]) if split else x_ref[r]
        x1 = x_in + g1 * proj
        y2 = x1 * lax.rsqrt(jnp.mean(x1 * x1, axis=-1, keepdims=True) + EPS) * n2_ref[...]
        return x1, (y2 * (1.0 + sc2) + sh2).astype(BF16), g2

    def finish(r, x1, g2, mlp):
        x2 = x1 + g2 * mlp
        if final:
            x2 = x2 * lax.rsqrt(jnp.mean(x2 * x2, axis=-1, keepdims=True) + EPS) * fn_ref[...]
        o_ref[r] = x2

    cur = mixed(0)
    for r in range(OUT_ROWS):
        x1, h2, g2 = cur
        hid = jnp.maximum(_dot(h2, w1_ref[...]), 0.0)
        if r + 1 < OUT_ROWS:
            cur = mixed(r + 1)
        mlp = _dot((hid * hid).astype(BF16), w2_ref[...])
        finish(r, x1, g2, mlp)


def _outproj(x_ctx, x_rest, lat_skip, modsel, gla_f, gla_b, gg, att_o, y_c, u_c, gnw, hm, s5d, gluw, glub, wo, n2w,
             w1, w2, fnw, layer, final):
    bsz, tt, _ = gg.shape
    nb = tt // TOK_BLOCK
    first = CTX_LEN // TOK_BLOCK if final else 0
    nsteps = nb - first
    const2 = lambda b, j: (0, 0)
    lyr3 = lambda b, j: (layer, 0, 0)
    single = pl.Buffered(1)
    return pl.pallas_call(
        functools.partial(_out_kernel, first_block=first, final=final, split=lat_skip > 0),
        grid=(bsz // OUT_ROWS, nsteps),
        in_specs=_stream_specs(lat_skip, first, rows=OUT_ROWS) + [
            pl.BlockSpec((OUT_ROWS, 2, 6 * D_MODEL), lambda b, j: (b, 0, 0)),
            pl.BlockSpec((OUT_ROWS, TOK_BLOCK, GLA_WIDTH), lambda b, j: (b, j + first, 0)),
            pl.BlockSpec((OUT_ROWS, TOK_BLOCK, GLA_WIDTH), lambda b, j: (b, j + first, 0)),
            pl.BlockSpec((OUT_ROWS, TOK_BLOCK, GLA_WIDTH), lambda b, j: (b, j + first, 0)),
            pl.BlockSpec((OUT_ROWS, TOK_BLOCK, ATT_WIDTH), lambda b, j: (b, j + first, 0)),
            pl.BlockSpec((OUT_ROWS, CH_BLOCK, S5_CW), lambda b, j: (b, j + first, 0)),
            pl.BlockSpec((OUT_ROWS, CH_BLOCK, S5_CW), lambda b, j: (b, j + first, 0)),
            pl.BlockSpec((1, GLA_WIDTH), const2),
            pl.BlockSpec((GLA_WIDTH, GLA_WIDTH), const2),
            pl.BlockSpec((1, S5_CW), const2),
            pl.BlockSpec((None, S5_WIDTH, 2 * S5_WIDTH), lyr3, pipeline_mode=single),
            pl.BlockSpec((1, 2 * S5_WIDTH), const2),
            pl.BlockSpec((None, D_MODEL, D_MODEL), lyr3, pipeline_mode=single),
            pl.BlockSpec((1, D_MODEL), const2),
            pl.BlockSpec((None, D_MODEL, D_FF), lyr3, pipeline_mode=single),
            pl.BlockSpec((None, D_FF, D_MODEL), lyr3, pipeline_mode=single),
            pl.BlockSpec((1, D_MODEL), const2),
        ],
        out_specs=pl.BlockSpec((OUT_ROWS, TOK_BLOCK, D_MODEL), lambda b, j: (b, j, 0)),
        out_shape=jax.ShapeDtypeStruct((bsz, nsteps * TOK_BLOCK, D_MODEL), F32),
        scratch_shapes=[pltpu.VMEM((OUT_ROWS, S5_WIDTH // 128, TOK_BLOCK, 128), F32)],
        compiler_params=pltpu.CompilerParams(
            dimension_semantics=("arbitrary", "arbitrary"), vmem_limit_bytes=VMEM_LIMIT),
        name="outproj_mlp",
    )(x_ctx, x_rest, modsel, gla_f, gla_b, gg, att_o, y_c, u_c, gnw, hm, s5d, gluw, glub, wo, n2w, w1, w2, fnw)


def _rope_tables(tt):
    n_lat = tt - CTX_LEN
    rows = n_lat // GRID_W
    row = jnp.repeat(jnp.arange(rows, dtype=F32), GRID_W)
    col = jnp.tile(jnp.arange(GRID_W, dtype=F32), rows)
    n_freq = ATT_HD // 4
    inv_freq = ROPE_BASE ** (-jnp.arange(n_freq, dtype=F32) / n_freq)
    ang_r = row[:, None] * inv_freq
    ang_c = col[:, None] * inv_freq
    ang = jnp.concatenate([ang_r, ang_r, ang_c, ang_c], axis=-1)
    cos = jnp.concatenate([jnp.ones((CTX_LEN, ATT_HD), F32), jnp.cos(ang)], axis=0)
    sin = jnp.concatenate([jnp.zeros((CTX_LEN, ATT_HD), F32), jnp.sin(ang)], axis=0)
    up_quarter = (np.arange(ATT_HD) // 16) % 2 == 0
    sa = jnp.where(up_quarter, -sin, 0.0)
    sb = jnp.where(up_quarter, 0.0, sin)
    two = lambda t: jnp.concatenate([t, t], axis=-1)
    return two(cos), two(sa), two(sb)


def _pad_cols(w, width):
    return jnp.pad(w, ((0, 0), (0, 0), (0, width - w.shape[-1])))


def _layout_w_in(w_in):
    offs = np.cumsum([0, GLA_KW, GLA_KW, GLA_WIDTH, GLA_WIDTH, GLA_RANK, GLA_RANK, ATT_WIDTH, ATT_KVW, ATT_KVW,
                      S5_WIDTH])
    q, k, v, g, zf, zb, aq, ak, av, u = [w_in[:, :, offs[i]:offs[i + 1]] for i in range(10)]
    z = jnp.concatenate([zf, zb], axis=-1)
    cols = [v, g, _pad_cols(q, GLA_KPAD), _pad_cols(k, GLA_KPAD), _pad_cols(z, 128), aq, ak, av, u]
    return jnp.concatenate(cols, axis=-1).astype(BF16)


def _block_diag(t):
    s, g, a, b = t.shape
    eye = jnp.eye(g, dtype=t.dtype)
    return jnp.einsum('sgab,gk->sgakb', t, eye).reshape(s, g * a, g * b)


def kernel(x, c, ctx, c_ctx, w_mod, b_mod, norm1_w, norm2_w, w_in, gla_wa_f, gla_ba_f, gla_wa_b, gla_ba_b,
           gla_norm_w, attn_sink, s5_lam_re_f, s5_lam_im_f, s5_log_step_f, s5_b_re_f, s5_b_im_f, s5_c_re_f,
           s5_c_im_f, s5_lam_re_b, s5_lam_im_b, s5_log_step_b, s5_b_re_b, s5_b_im_b, s5_c_re_b, s5_c_im_b,
           s5_d, glu_w, glu_b, w_out, mlp_w1, mlp_w2, final_norm_w):
    bsz, seq, _ = x.shape
    n_layers = w_mod.shape[0]
    tt = CTX_LEN + seq
    assert bsz % 8 == 0 and bsz % IN_ROWS == 0 and bsz % GLA_ROWS == 0 and bsz % OUT_ROWS == 0 and seq % TOK_BLOCK == 0 and seq >= 2 * TOK_BLOCK

    mod_rows = -(-(bsz + 1) // 8) * 8
    cvec = jnp.zeros((mod_rows, D_MODEL), F32).at[:bsz].set(c).at[bsz].set(c_ctx)
    mod = _modulation(cvec, w_mod, b_mod)
    mod_ctx = jnp.broadcast_to(mod[:, bsz][:, None], (n_layers, bsz, 6 * D_MODEL))
    modsel = jnp.stack([mod_ctx, mod[:, :bsz]], axis=2)

    w_in_p = _layout_w_in(w_in)
    wa_cat = jnp.zeros((n_layers, 128, 2 * GLA_KPAD), F32)
    wa_cat = wa_cat.at[:, 0:GLA_RANK, 0:GLA_KW].set(gla_wa_f)
    wa_cat = wa_cat.at[:, GLA_RANK:2 * GLA_RANK, GLA_KPAD:GLA_KPAD + GLA_KW].set(gla_wa_b).astype(BF16)
    ba_cat = jnp.zeros((n_layers, 1, 2 * GLA_KPAD), F32)
    ba_cat = ba_cat.at[:, 0, 0:GLA_KW].set(gla_ba_f).at[:, 0, GLA_KPAD:GLA_KPAD + GLA_KW].set(gla_ba_b)
    qscale = jnp.ones((1, 2 * GLA_KPAD), F32).at[:, 0:GLA_KW].set(GLA_DK ** -0.5)
    cos_t, sa_t, sb_t = _rope_tables(tt)
    gnw = jnp.tile(gla_norm_w, (1, GLA_HEADS))[:, None, :]
    head = np.arange(GLA_WIDTH) // GLA_DV
    hm = jnp.asarray(head[:, None] == head[None, :], BF16)
    wo_b = w_out.astype(BF16)
    w1_b = mlp_w1.astype(BF16)
    w2_b = mlp_w2.astype(BF16)
    gluw_b = glu_w.astype(BF16)

    s5_tab = _s5_tables(
        (s5_lam_re_f, s5_lam_im_f, s5_log_step_f, s5_b_re_f, s5_b_im_f, s5_c_re_f, s5_c_im_f),
        (s5_lam_re_b, s5_lam_im_b, s5_log_step_b, s5_b_re_b, s5_b_im_b, s5_c_re_b, s5_c_im_b))
    s5_dt = jnp.tile(s5_d, (1, S5_LC))[:, None, :]

    stream = (ctx, x, CTX_LEN // TOK_BLOCK)
    for l in range(n_layers):
        final = l == n_layers - 1
        gv, gg, gqk, gla, att, u_c = _inproj(*stream, modsel[l], norm1_w[l][None], w_in_p, wa_cat, ba_cat[l],
                                             qscale, cos_t, sa_t, sb_t, l)
        gla_f, gla_b = _gla(gqk, gv, gla)
        att_o = _swa(att, attn_sink[l])
        y_c = _s5(u_c, s5_tab, l)
        xs = _outproj(*stream, modsel[l], gla_f, gla_b, gg, att_o, y_c, u_c,
                      gnw[l], hm, s5_dt[l], gluw_b, glu_b[l][None], wo_b, norm2_w[l][None], w1_b, w2_b,
                      final_norm_w[None], l, final)
        stream = (xs, xs, 0)
    return xs
```

```python
import functools
import math

import jax
import jax.numpy as jnp
import numpy as np
from jax import lax
from jax.experimental import pallas as pl
from jax.experimental.pallas import tpu as pltpu

F32 = jnp.float32
BF16 = jnp.bfloat16

D_MODEL = 1024
D_FF = 4 * D_MODEL
CTX_LEN = 256
GRID_W = 64
EPS = 1e-6
NEG_INF = -1e30
LOG2E = math.log2(math.e)

GLA_HEADS = 4
GLA_DV = 96
GLA_DK = 48
GLA_WIDTH = GLA_HEADS * GLA_DV
GLA_KW = GLA_HEADS * GLA_DK
GLA_KPAD = 256
GLA_RANK = 16
GLA_TAU = 16.0
GLA_CHUNK = 64

ATT_HD = 64
ATT_HEADS = 6
ATT_KV_HEADS = 2
ATT_WIDTH = ATT_HEADS * ATT_HD
ATT_KVW = ATT_KV_HEADS * ATT_HD
WINDOW = 128
ATT_BLOCK = 128
ROPE_BASE = 10000.0

S5_WIDTH = 256
S5_GROUP = 16
S5_GROUPS = 16
S5_STATE = 64
S5_NS = S5_GROUPS * S5_STATE
S5_LC = 16
S5_CW = S5_LC * S5_WIDTH

TOK_BLOCK = 256
CH_BLOCK = TOK_BLOCK // S5_LC
GLA_ROWS = 4
SWA_ROWS = 2
IN_ROWS = 4
OUT_ROWS = 2

C_V, C_G, C_Q, C_K, C_Z, C_AQ, C_AK, C_AV, C_U = 0, 384, 768, 1024, 1280, 1408, 1792, 1920, 2048
IN_PAD = 2304
C_QK_END = C_K + GLA_KPAD

VMEM_LIMIT = 56 * 1024 * 1024


def _sigmoid(x):
    return 1.0 / (1.0 + jnp.exp(-x))


def _silu(x):
    return x * _sigmoid(x)


def _dot(a, b):
    return jnp.dot(a, b, preferred_element_type=F32)


def _dot_nt(a, b):
    return lax.dot_general(a, b, (((1,), (1,)), ((), ())), preferred_element_type=F32)


def _dot_tn(a, b):
    return lax.dot_general(a, b, (((0,), (0,)), ((), ())), preferred_element_type=F32)


def _split3(x):
    hi = x.astype(BF16)
    r1 = x - hi.astype(F32)
    mid = r1.astype(BF16)
    lo = (r1 - mid.astype(F32)).astype(BF16)
    return hi, mid, lo


def _mod_kernel(c_ref, w_ref, b_ref, o_ref):
    a = _silu(c_ref[...]).astype(BF16)
    o_ref[...] = _dot(a, w_ref[...].astype(BF16)) + b_ref[...]


def _modulation(cvec, w_mod, b_mod):
    n_layers = w_mod.shape[0]
    rows = cvec.shape[0]
    tn = 1536
    return pl.pallas_call(
        _mod_kernel,
        grid=(n_layers, 6 * D_MODEL // tn),
        in_specs=[
            pl.BlockSpec((rows, D_MODEL), lambda l, n: (0, 0)),
            pl.BlockSpec((None, D_MODEL, tn), lambda l, n: (l, 0, n)),
            pl.BlockSpec((None, 1, tn), lambda l, n: (l, 0, n)),
        ],
        out_specs=pl.BlockSpec((None, rows, tn), lambda l, n: (l, 0, n)),
        out_shape=jax.ShapeDtypeStruct((n_layers, rows, 6 * D_MODEL), F32),
        compiler_params=pltpu.CompilerParams(
            dimension_semantics=("arbitrary", "arbitrary"), vmem_limit_bytes=VMEM_LIMIT),
        name="modulation",
    )(cvec, w_mod, b_mod.reshape(n_layers, 1, 6 * D_MODEL))


def _inproj_kernel(xc_ref, x_ref, mod_ref, n1_ref, w_ref, wa_ref, ba_ref, qs_ref, cos_ref, sa_ref, sb_ref,
                   gv_ref, gg_ref, gqk_ref, gla_ref, att_ref, uc_ref, us_ref, *, split):
    j = pl.program_id(1)
    cos = cos_ref[...]
    sa = sa_ref[...]
    sb = sb_ref[...]

    def normed(r):
        xf = jnp.where(j == 0, xc_ref[r], x_ref[r]) if split else x_ref[r]
        y = xf * lax.rsqrt(jnp.mean(xf * xf, axis=-1, keepdims=True) + EPS) * n1_ref[...]
        m = mod_ref[r]
        mrow = jnp.where(j > 0, m[1:2, :], m[0:1, :])
        sh1 = mrow[:, 0:D_MODEL]
        sc1 = mrow[:, D_MODEL:2 * D_MODEL]
        return (y * (1.0 + sc1) + sh1).astype(BF16)

    def rope(t, reps):
        w = t.shape[-1]
        c3 = jnp.concatenate([cos] * reps, axis=-1) if reps > 1 else cos
        a3 = jnp.concatenate([sa] * reps, axis=-1) if reps > 1 else sa
        b3 = jnp.concatenate([sb] * reps, axis=-1) if reps > 1 else sb
        up = pltpu.roll(t, w - 16, 1)
        dn = pltpu.roll(t, 16, 1)
        return t * c3 + up * a3 + dn * b3

    def tail(r, p):
        gv_ref[r] = p[:, C_V:C_V + GLA_WIDTH].astype(BF16)
        gg_ref[r] = p[:, C_G:C_G + GLA_WIDTH]
        gqk_ref[r] = p[:, C_Q:C_QK_END] * qs_ref[...]
        z = p[:, C_Z:C_Z + 128].astype(BF16)
        zg = _dot(z, wa_ref[...]) + ba_ref[...]
        gla_ref[r] = (jnp.minimum(zg, 0.0) - jnp.log1p(jnp.exp(-jnp.abs(zg)))) * (1.0 / GLA_TAU)

        aq = rope(p[:, C_AQ:C_AQ + ATT_WIDTH], 3) * (ATT_HD ** -0.5 * LOG2E)
        ak = rope(p[:, C_AK:C_AK + ATT_KVW], 1)
        av = p[:, C_AV:C_AV + ATT_KVW]
        left = lax.broadcasted_iota(jnp.int32, ak.shape, 1) < ATT_HD

        def expand(t):
            sw = pltpu.roll(t, ATT_HD, 1)
            return [jnp.where(left, t, sw), t, jnp.where(left, sw, t)]

        att_ref[r] = jnp.concatenate([aq] + expand(ak) + expand(av), axis=-1).astype(BF16)
        for half in range(S5_WIDTH // 128):
            us_ref[r, half] = p[:, C_U + half * 128:C_U + (half + 1) * 128]
        for st in range(S5_LC):
            for half in range(S5_WIDTH // 128):
                lane0 = st * S5_WIDTH + half * 128
                uc_ref[r, :, lane0:lane0 + 128] = us_ref[r, half, pl.ds(st, CH_BLOCK, stride=S5_LC), :]

    h = normed(0)
    prev = None
    for r in range(IN_ROWS):
        proj = _dot(h, w_ref[...])
        if r + 1 < IN_ROWS:
            h = normed(r + 1)
        if prev is not None:
            tail(r - 1, prev)
        prev = proj
    tail(IN_ROWS - 1, prev)


def _stream_specs(lat_skip, first=0, rows=None):
    return [pl.BlockSpec((rows, TOK_BLOCK, D_MODEL), lambda b, j: (b, 0, 0)),
            pl.BlockSpec((rows, TOK_BLOCK, D_MODEL), lambda b, j: (b, jnp.maximum(j + first - lat_skip, 0), 0))]


def _inproj(x_ctx, x_rest, lat_skip, modsel, n1w, w_in_p, wa_cat, ba_cat, qscale, cos_t, sa_t, sb_t, layer):
    bsz = x_ctx.shape[0]
    tt = cos_t.shape[0]
    nb = tt // TOK_BLOCK
    const = lambda b, j: (0, 0)
    lyr3 = lambda b, j: (layer, 0, 0)
    tok = lambda w: pl.BlockSpec((IN_ROWS, TOK_BLOCK, w), lambda b, j: (b, j, 0))
    return pl.pallas_call(
        functools.partial(_inproj_kernel, split=lat_skip > 0),
        grid=(bsz // IN_ROWS, nb),
        in_specs=_stream_specs(lat_skip, rows=IN_ROWS) + [
            pl.BlockSpec((IN_ROWS, 2, 6 * D_MODEL), lambda b, j: (b, 0, 0)),
            pl.BlockSpec((1, D_MODEL), const),
            pl.BlockSpec((None, D_MODEL, IN_PAD), lyr3),
            pl.BlockSpec((None, 128, 2 * GLA_KPAD), lyr3),
            pl.BlockSpec((1, 2 * GLA_KPAD), const),
            pl.BlockSpec((1, 2 * GLA_KPAD), const),
            pl.BlockSpec((TOK_BLOCK, 128), lambda b, j: (j, 0)),
            pl.BlockSpec((TOK_BLOCK, 128), lambda b, j: (j, 0)),
            pl.BlockSpec((TOK_BLOCK, 128), lambda b, j: (j, 0)),
        ],
        out_specs=[tok(GLA_WIDTH), tok(GLA_WIDTH), tok(2 * GLA_KPAD), tok(2 * GLA_KPAD), tok(3 * ATT_WIDTH),
                   pl.BlockSpec((IN_ROWS, CH_BLOCK, S5_CW), lambda b, j: (b, j, 0))],
        out_shape=[
            jax.ShapeDtypeStruct((bsz, tt, GLA_WIDTH), BF16),
            jax.ShapeDtypeStruct((bsz, tt, GLA_WIDTH), F32),
            jax.ShapeDtypeStruct((bsz, tt, 2 * GLA_KPAD), F32),
            jax.ShapeDtypeStruct((bsz, tt, 2 * GLA_KPAD), F32),
            jax.ShapeDtypeStruct((bsz, tt, 3 * ATT_WIDTH), BF16),
            jax.ShapeDtypeStruct((bsz, tt // S5_LC, S5_CW), F32),
        ],
        scratch_shapes=[pltpu.VMEM((IN_ROWS, S5_WIDTH // 128, TOK_BLOCK, 128), F32)],
        compiler_params=pltpu.CompilerParams(
            dimension_semantics=("arbitrary", "arbitrary"), vmem_limit_bytes=VMEM_LIMIT),
        name="inproj",
    )(x_ctx, x_rest, modsel, n1w, w_in_p, wa_cat, ba_cat, qscale, cos_t, sa_t, sb_t)


def _gla_kernel(qkf_ref, vf_ref, laf_ref, qkb_ref, vb_ref, lab_ref, tri_ref, tril_ref, hm_ref, vm_ref, bd_ref, of_ref, ob_ref, stf_ref, stb_ref):
    j = pl.program_id(1)

    @pl.when(j == 0)
    def _():
        stf_ref[...] = jnp.zeros_like(stf_ref)
        stb_ref[...] = jnp.zeros_like(stb_ref)

    n_chunks = TOK_BLOCK // GLA_CHUNK
    chunk = lambda t, c: t[c * GLA_CHUNK:(c + 1) * GLA_CHUNK]
    dirs = []
    for r in range(GLA_ROWS):
        dirs.append(((qkf_ref.at[r], vf_ref.at[r], laf_ref.at[r]), None, 0, GLA_CHUNK - 1, of_ref.at[r],
                     stf_ref.at[r], list(range(n_chunks))))
        dirs.append(((qkb_ref.at[r], vb_ref.at[r], lab_ref.at[r]), None, 1, 0, ob_ref.at[r],
                     stb_ref.at[r], list(reversed(range(n_chunks)))))

    cum = []
    for (_, _, la_ref), _, d, _, _, _, _ in dirs:
        la = la_ref[...]
        hi = la.astype(BF16)
        lo = (la - hi.astype(F32)).astype(BF16)
        cum.append(_dot(tri_ref[d], hi) + _dot(tri_ref[d], lo))

    ops = []
    for ((qk_ref, v_ref, _), _, d, last_row, _, _, _), b in zip(dirs, cum):
        q = qk_ref[:, 0:GLA_KPAD]
        k = qk_ref[:, GLA_KPAD:2 * GLA_KPAD]
        vb = v_ref[...]
        bl = [chunk(b, c)[last_row:last_row + 1, :] for c in range(n_chunks)]
        blx = jnp.concatenate([jnp.broadcast_to(t, (GLA_CHUNK, GLA_KPAD)) for t in bl], axis=0)
        qb = (q * jnp.exp(b)).astype(BF16)
        kb = (k * jnp.exp(-b)).astype(BF16)
        kd = (k * jnp.exp(blx - b)).astype(BF16)
        ops.append((qb, kb, kd, vb, bl))

    scores = []
    for (qb, kb, _, _, _), (_, _, d, _, _, _, _) in zip(ops, dirs):
        keep = tril_ref[d] > 0
        per = []
        for c in range(n_chunks):
            kst = jnp.concatenate([chunk(kb, c) * hm_ref[h] for h in range(GLA_HEADS)], axis=0)
            per.append(jnp.where(keep, _dot_nt(chunk(qb, c), kst), 0.0).astype(BF16))
        scores.append(per)

    bdmask = bd_ref[...]
    intra, inc = [], []
    for (_, _, kd, vb, _), per in zip(ops, scores):
        oi, ds = [], []
        for c in range(n_chunks):
            vc = chunk(vb, c)
            vbd = jnp.concatenate([vc * vm_ref[h] for h in range(GLA_HEADS)], axis=0)
            oi.append(_dot(per[c], vbd))
            ds.append(_dot_tn(vc, chunk(kd, c)) * bdmask)
        intra.append(oi)
        inc.append(ds)

    states = [d[5][...] for d in dirs]
    for i in range(n_chunks):
        for si, ((_, _, _, _, o_ref, _, order), (qb, _, _, _, bl)) in enumerate(zip(dirs, ops)):
            c = order[i]
            o_ref[c * GLA_CHUNK:(c + 1) * GLA_CHUNK, :] = (
                intra[si][c] + _dot_nt(chunk(qb, c), states[si].astype(BF16)))
            states[si] = states[si] * jnp.exp(bl[c]) + inc[si][c]
    for d, st in zip(dirs, states):
        d[5][...] = st


def _gla_masks():
    r = np.arange(GLA_CHUNK)
    lower = (r[None, :] <= r[:, None]).astype(np.float32)
    tri1 = np.stack([lower, lower.T])
    n_chunks = TOK_BLOCK // GLA_CHUNK
    tri = np.stack([np.kron(np.eye(n_chunks, dtype=np.float32), t) for t in tri1])
    tril = np.tile(tri1, (1, 1, GLA_HEADS))
    klane = np.arange(GLA_KPAD)
    vlane = np.arange(GLA_WIDTH)
    hm = np.stack([np.broadcast_to((klane // GLA_DK) == h, (GLA_CHUNK, GLA_KPAD)) for h in range(GLA_HEADS)])
    vm = np.stack([np.broadcast_to((vlane // GLA_DV) == h, (GLA_CHUNK, GLA_WIDTH)) for h in range(GLA_HEADS)])
    bd = ((vlane[:, None] // GLA_DV) == (klane[None, :] // GLA_DK)).astype(np.float32)
    return (jnp.asarray(tri, BF16), jnp.asarray(tril, F32), jnp.asarray(hm, BF16), jnp.asarray(vm, BF16),
            jnp.asarray(bd))


def _scan_block(d, j, n_ctx, n_all):
    bwd = jnp.where(j < n_ctx, n_ctx - 1 - j, n_all + n_ctx - 1 - j)
    return jnp.where(d == 0, j, bwd)


def _gla(gqk, gv, gla):
    bsz, tt, _ = gv.shape
    nb = tt // TOK_BLOCK
    n_ctx = CTX_LEN // TOK_BLOCK
    tri, tril, hm, vm, bd = _gla_masks()
    const2 = lambda b, j: (0, 0)
    const3 = lambda b, j: (0, 0, 0)
    fwd = lambda b, j: (b, j, 0)
    bwd = lambda b, j: (b, _scan_block(1, j, n_ctx, nb), 0)
    bwd_gate = lambda b, j: (b, _scan_block(1, j, n_ctx, nb), 1)
    tok = lambda w, imap: pl.BlockSpec((GLA_ROWS, TOK_BLOCK, w), imap)
    return pl.pallas_call(
        _gla_kernel,
        grid=(bsz // GLA_ROWS, nb),
        in_specs=[
            tok(2 * GLA_KPAD, fwd),
            tok(GLA_WIDTH, fwd),
            tok(GLA_KPAD, fwd),
            tok(2 * GLA_KPAD, bwd),
            tok(GLA_WIDTH, bwd),
            tok(GLA_KPAD, bwd_gate),
            pl.BlockSpec((2, TOK_BLOCK, TOK_BLOCK), const3),
            pl.BlockSpec((2, GLA_CHUNK, GLA_HEADS * GLA_CHUNK), const3),
            pl.BlockSpec((GLA_HEADS, GLA_CHUNK, GLA_KPAD), const3),
            pl.BlockSpec((GLA_HEADS, GLA_CHUNK, GLA_WIDTH), const3),
            pl.BlockSpec((GLA_WIDTH, GLA_KPAD), const2),
        ],
        out_specs=[tok(GLA_WIDTH, fwd), tok(GLA_WIDTH, bwd)],
        out_shape=[jax.ShapeDtypeStruct((bsz, tt, GLA_WIDTH), F32)] * 2,
        scratch_shapes=[pltpu.VMEM((GLA_ROWS, GLA_WIDTH, GLA_KPAD), F32)] * 2,
        compiler_params=pltpu.CompilerParams(
            dimension_semantics=("arbitrary", "arbitrary"), vmem_limit_bytes=VMEM_LIMIT),
        name="gla",
    )(gqk, gv, gla, gqk, gv, gla, tri, tril, hm, vm, bd)


def _swa_kernel(sink_ref, mask_ref, att_ref, o_ref, *, tt):
    j = pl.program_id(1)
    n_sub = TOK_BLOCK // ATT_BLOCK
    n_ctx_blocks = CTX_LEN // TOK_BLOCK
    last_q = tt // ATT_BLOCK - 1
    lane = lax.broadcasted_iota(jnp.int32, (ATT_BLOCK, 2 * ATT_HD), 1)
    left = lane < ATT_HD
    row2 = lax.broadcasted_iota(jnp.int32, (2 * ATT_BLOCK, 1), 0)
    units = [(r, sub, p) for r in range(SWA_ROWS) for sub in range(n_sub) for p in range(ATT_HEADS // 2)]

    def run(windowed):
        def score(r, sub, p):
            q0 = pl.multiple_of(j * TOK_BLOCK + sub * ATT_BLOCK, ATT_BLOCK)
            qp = att_ref[r, pl.ds(q0, ATT_BLOCK), p * 128:(p + 1) * 128]
            zero = jnp.zeros_like(qp)
            q2 = jnp.concatenate([jnp.where(left, qp, zero), jnp.where(left, zero, qp)], axis=0)
            kcol = ATT_WIDTH + p * 128
            s_c = _dot_nt(q2, att_ref[r, 0:CTX_LEN, kcol:kcol + 128])
            if not windowed:
                return s_c, None, None
            start = pl.multiple_of(jnp.minimum(q0 - ATT_BLOCK, tt - 3 * ATT_BLOCK), ATT_BLOCK)
            s_w = jnp.where(keeps[sub], _dot_nt(q2, att_ref[r, pl.ds(start, 3 * ATT_BLOCK), kcol:kcol + 128]),
                            NEG_INF)
            return s_c, s_w, start

        def softmax(p, s_c, s_w, start):
            sink_col = jnp.where(row2 < ATT_BLOCK, sink_ref[2 * p], sink_ref[2 * p + 1]) * LOG2E
            m = jnp.maximum(jnp.max(s_c, axis=-1, keepdims=True), sink_col)
            if windowed:
                m = jnp.maximum(m, jnp.max(s_w, axis=-1, keepdims=True))
            e_c = jnp.exp2(s_c - m)
            den = jnp.sum(e_c, axis=-1, keepdims=True) + jnp.exp2(sink_col - m)
            e_w = None
            if windowed:
                e_w = jnp.exp2(s_w - m)
                den = den + jnp.sum(e_w, axis=-1, keepdims=True)
                e_w = e_w.astype(BF16)
            return e_c.astype(BF16), e_w, den, start

        def values(r, sub, p, e_c, e_w, den, start):
            vcol = 2 * ATT_WIDTH + p * 128
            pv = _dot(e_c, att_ref[r, 0:CTX_LEN, vcol:vcol + 128])
            if windowed:
                pv = pv + _dot(e_w, att_ref[r, pl.ds(start, 3 * ATT_BLOCK), vcol:vcol + 128])
            o2 = pv / den
            o = jnp.where(left, o2[0:ATT_BLOCK], o2[ATT_BLOCK:2 * ATT_BLOCK])
            o_ref[r, sub * ATT_BLOCK:(sub + 1) * ATT_BLOCK, p * 128:(p + 1) * 128] = o.astype(BF16)

        keeps = []
        if windowed:
            for sub in range(n_sub):
                qi = j * n_sub + sub
                kind = jnp.where(qi == CTX_LEN // ATT_BLOCK, 0, jnp.where(qi == last_q, 2, 1))
                keeps.append(mask_ref[kind] > 0)

        n = len(units)
        sc, pr = {}, {}
        for t in range(n + 2):
            if t < n:
                sc[t] = score(*units[t])
            if 0 <= t - 1 < n:
                pr[t - 1] = softmax(units[t - 1][2], *sc.pop(t - 1))
            if 0 <= t - 2 < n:
                values(*units[t - 2], *pr.pop(t - 2))

    @pl.when(j < n_ctx_blocks)
    def _():
        run(False)

    @pl.when(j >= n_ctx_blocks)
    def _():
        run(True)


def _swa_masks():
    r = np.arange(2 * ATT_BLOCK)[:, None] % ATT_BLOCK
    c = np.arange(3 * ATT_BLOCK)[None, :]
    near = lambda delta: np.abs(r - c + delta) <= WINDOW
    first = near(ATT_BLOCK) & (c >= ATT_BLOCK)
    return jnp.asarray(np.stack([first, near(ATT_BLOCK), near(2 * ATT_BLOCK)]), F32)


def _swa(att, sink):
    bsz, tt, _ = att.shape
    nb = tt // TOK_BLOCK
    return pl.pallas_call(
        functools.partial(_swa_kernel, tt=tt),
        grid=(bsz // SWA_ROWS, nb),
        in_specs=[
            pl.BlockSpec(memory_space=pltpu.SMEM),
            pl.BlockSpec((3, 2 * ATT_BLOCK, 3 * ATT_BLOCK), lambda b, j: (0, 0, 0)),
            pl.BlockSpec((SWA_ROWS, tt, 3 * ATT_WIDTH), lambda b, j: (b, 0, 0)),
        ],
        out_specs=pl.BlockSpec((SWA_ROWS, TOK_BLOCK, ATT_WIDTH), lambda b, j: (b, j, 0)),
        out_shape=jax.ShapeDtypeStruct((bsz, tt, ATT_WIDTH), BF16),
        compiler_params=pltpu.CompilerParams(
            dimension_semantics=("arbitrary", "arbitrary"), vmem_limit_bytes=VMEM_LIMIT),
        name="swa",
    )(sink, _swa_masks(), att)


def _s5_prep_kernel(lr_ref, li_ref, ls_ref, br_ref, bi_ref, cr_ref, ci_ref,
                    a16_ref, mre_ref, mim_ref, rre_ref, rim_ref, k_ref):
    rows = S5_LC * S5_GROUP
    rowg = lax.broadcasted_iota(jnp.int32, (rows, S5_NS), 0) >> 4
    colg = lax.broadcasted_iota(jnp.int32, (rows, S5_NS), 1) >> 6
    same_group = rowg == colg
    taps_t = []
    for d in range(2):
        lr = jnp.minimum(lr_ref[d], -1e-4)
        li = li_ref[d]
        dt = jnp.exp(ls_ref[d])
        mag = jnp.exp(lr * dt)
        ar = mag * jnp.cos(li * dt)
        ai = mag * jnp.sin(li * dt)
        den = lr * lr + li * li
        fr = ((ar - 1.0) * lr + ai * li) / den
        fi = (ai * lr - (ar - 1.0) * li) / den
        br = br_ref[d]
        bi = bi_ref[d]
        bbr = fr * br - fi * bi
        bbi = fr * bi + fi * br
        cr = cr_ref[d]
        ci = ci_ref[d]
        pr = jnp.ones_like(ar)
        pi = jnp.zeros_like(ar)
        cp_re, cp_im = [], []
        for tau in range(S5_LC + 1):
            cpr = cr * pr - ci * pi
            cpi = cr * pi + ci * pr
            if tau < S5_LC:
                mre_ref[d, tau] = pr * bbr - pi * bbi
                mim_ref[d, tau] = pr * bbi + pi * bbr
                cp_re.append(cpr)
                cp_im.append(cpi)
            if tau >= 1:
                rre_ref[d, tau - 1] = cpr
                rim_ref[d, tau - 1] = -cpi
            if tau == S5_LC:
                a16_ref[d, 0:1, :] = pr
                a16_ref[d, 1:2, :] = pi
            pr, pi = pr * ar - pi * ai, pr * ai + pi * ar
        bd_re = jnp.where(same_group, jnp.concatenate([bbr] * S5_GROUPS, axis=0), 0.0)
        bd_im = jnp.where(same_group, jnp.concatenate([bbi] * S5_GROUPS, axis=0), 0.0)
        if d == 1:
            cp_re, cp_im = cp_re[::-1], cp_im[::-1]
        hp = lax.Precision.HIGHEST
        dims = (((1,), (1,)), ((), ()))
        taps_t.append(
            lax.dot_general(bd_re, jnp.concatenate(cp_re, axis=0), dims, precision=hp, preferred_element_type=F32)
            - lax.dot_general(bd_im, jnp.concatenate(cp_im, axis=0), dims, precision=hp,
                              preferred_element_type=F32))
    kf, kb_rev = taps_t
    tap0 = lax.broadcasted_iota(jnp.int32, kf.shape, 1) < S5_GROUP
    k_ref[0] = jnp.where(tap0, kf + pltpu.roll(kb_rev, S5_GROUP, 1), kf)
    k_ref[1] = kb_rev


def _s5_tables(fwd, bwd):
    n_layers = fwd[0].shape[0]
    both = lambda i: jnp.stack([fwd[i], bwd[i]], axis=1)
    lam_re = both(0).reshape(n_layers, 2, 1, S5_NS)
    lam_im = both(1).reshape(n_layers, 2, 1, S5_NS)
    log_step = jnp.repeat(both(2), S5_STATE, axis=-1).reshape(n_layers, 2, 1, S5_NS)
    b_hn = lambda t: t.transpose(0, 1, 4, 2, 3).reshape(n_layers, 2, S5_GROUP, S5_NS)
    c_hn = lambda t: t.transpose(0, 1, 3, 2, 4).reshape(n_layers, 2, S5_GROUP, S5_NS)
    vec = pl.BlockSpec((None, 2, 1, S5_NS), lambda l: (l, 0, 0, 0))
    mat = pl.BlockSpec((None, 2, S5_GROUP, S5_NS), lambda l: (l, 0, 0, 0))
    tab = pl.BlockSpec((None, 2, S5_LC, S5_GROUP, S5_NS), lambda l: (l, 0, 0, 0, 0))
    taps = S5_LC * S5_GROUP
    a16, mre, mim, rre, rim, k = pl.pallas_call(
        _s5_prep_kernel,
        grid=(n_layers,),
        in_specs=[vec, vec, vec, mat, mat, mat, mat],
        out_specs=[
            pl.BlockSpec((None, 2, 2, S5_NS), lambda l: (l, 0, 0, 0)),
            tab, tab, tab, tab,
            pl.BlockSpec((None, 2, S5_WIDTH, taps), lambda l: (l, 0, 0, 0)),
        ],
        out_shape=[
            jax.ShapeDtypeStruct((n_layers, 2, 2, S5_NS), F32),
            jax.ShapeDtypeStruct((n_layers, 2, S5_LC, S5_GROUP, S5_NS), F32),
            jax.ShapeDtypeStruct((n_layers, 2, S5_LC, S5_GROUP, S5_NS), F32),
            jax.ShapeDtypeStruct((n_layers, 2, S5_LC, S5_GROUP, S5_NS), F32),
            jax.ShapeDtypeStruct((n_layers, 2, S5_LC, S5_GROUP, S5_NS), F32),
            jax.ShapeDtypeStruct((n_layers, 2, S5_WIDTH, taps), F32),
        ],
        compiler_params=pltpu.CompilerParams(dimension_semantics=("arbitrary",), vmem_limit_bytes=VMEM_LIMIT),
        name="s5_prep",
    )(lam_re, lam_im, log_step, b_hn(both(3)), b_hn(both(4)), c_hn(both(5)), c_hn(both(6)))

    n_pairs = S5_GROUPS // 2
    lane_group = np.arange(128) // S5_STATE

    def pair_rows(t, flip):
        t = t[:, ::-1] if flip else t
        t = t.reshape(n_layers, S5_LC, S5_GROUP, n_pairs, 128).transpose(0, 3, 1, 2, 4)
        own = jnp.asarray(lane_group[None, :] == np.arange(2)[:, None])
        t = jnp.where(own[None, None, :, None, None, :], t[:, :, None], 0.0)
        return t.reshape(n_layers, n_pairs, 2 * taps, 128)

    m_f = jnp.concatenate([pair_rows(mre[:, 0], True), pair_rows(mim[:, 0], True)], axis=-1).astype(BF16)
    m_b = jnp.concatenate([pair_rows(mre[:, 1], False), pair_rows(mim[:, 1], False)], axis=-1).astype(BF16)

    r_t = jnp.concatenate([pair_rows(rre[:, 0], False), pair_rows(rim[:, 0], False),
                           pair_rows(rre[:, 1], True), pair_rows(rim[:, 1], True)], axis=-1).astype(BF16)

    zeros = jnp.zeros((n_layers, S5_WIDTH, taps), F32)
    f2 = jnp.concatenate([zeros, k[:, 0]], axis=-1)
    b2 = jnp.concatenate([k[:, 1], zeros], axis=-1)
    t_f = jnp.stack([f2[..., taps - S5_GROUP * st:2 * taps - S5_GROUP * st] for st in range(S5_LC)], axis=2)
    t_b = jnp.stack([b2[..., S5_GROUP * (S5_LC - 1 - st):S5_GROUP * (S5_LC - 1 - st) + taps]
                     for st in range(S5_LC)], axis=2)
    later = jnp.asarray(np.arange(taps)[None, :] // S5_GROUP >= np.arange(S5_LC)[:, None])
    toe = jnp.where(later[None, None], t_f, t_b)
    toe = toe.reshape(n_layers, S5_GROUPS, S5_GROUP, S5_LC, taps).transpose(0, 1, 3, 2, 4)
    toe = toe.reshape(n_layers, S5_GROUPS, taps, taps).astype(BF16)
    return a16, m_f, m_b, r_t, toe


def _lane_block_transpose(cols):
    lane = lax.broadcasted_iota(jnp.int32, cols[0].shape, 1)
    out = [None] * 32
    for ah in range(2):
        for bh in range(2):
            v = [cols[(ah * 8 + al) * 2 + bh] for al in range(8)]
            for kbit in range(3):
                width = 16 << kbit
                low = ((lane >> (4 + kbit)) & 1) == 0
                nxt = list(v)
                for i in range(8):
                    if i & (1 << kbit):
                        continue
                    lo_v, hi_v = v[i], v[i | (1 << kbit)]
                    nxt[i] = jnp.where(low, lo_v, pltpu.roll(hi_v, width, 1))
                    nxt[i | (1 << kbit)] = jnp.where(low, pltpu.roll(lo_v, 128 - width, 1), hi_v)
                v = nxt
            for bl in range(8):
                out[(bh * 8 + bl) * 2 + ah] = v[bl]
    return out


def _s5_increments(ush, m_ref, d_re, d_im):
    for p in range(S5_GROUPS // 2):
        dp = _dot(ush[:, p * 512:(p + 1) * 512], m_ref[p])
        d_re[p] = dp[:, 0:128]
        d_im[p] = dp[:, 128:256]


def _s5_recurrence(a16_ref, d_re, d_im, x_re, x_im, sr_ref, si_ref, order, bsz):
    ar = a16_ref[0:1, :]
    ai = a16_ref[1:2, :]
    sr = sr_ref[...]
    si = si_ref[...]
    n_slabs = S5_NS // 128
    gather = lambda ref, rows: jnp.concatenate([ref[p, rows, :] for p in range(n_slabs)], axis=1)
    for c in order:
        rows = pl.ds(c, bsz, stride=CH_BLOCK)
        for p in range(n_slabs):
            x_re[p, rows, :] = sr[:, p * 128:(p + 1) * 128]
            x_im[p, rows, :] = si[:, p * 128:(p + 1) * 128]
        sr, si = ar * sr - ai * si + gather(d_re, rows), ar * si + ai * sr + gather(d_im, rows)
    sr_ref[...] = sr
    si_ref[...] = si


def _s5_fwd_kernel(uc_ref, a16_ref, m_ref, ush_ref, xin_ref, d_re, d_im, x_re, x_im, sr_ref, si_ref, *, bsz):
    @pl.when(pl.program_id(0) == 0)
    def _():
        sr_ref[...] = jnp.zeros_like(sr_ref)
        si_ref[...] = jnp.zeros_like(si_ref)

    rows = bsz * CH_BLOCK
    ub = uc_ref[...].reshape(rows, S5_CW).astype(BF16)
    packed = pltpu.bitcast(ub, jnp.uint32)
    cols = _lane_block_transpose([packed[:, v * 128:(v + 1) * 128] for v in range(32)])
    ush = pltpu.bitcast(jnp.concatenate(cols, axis=1), BF16)
    ush_ref[...] = ush.reshape(bsz, CH_BLOCK, S5_CW)
    _s5_increments(ush, m_ref, d_re, d_im)
    _s5_recurrence(a16_ref, d_re, d_im, x_re, x_im, sr_ref, si_ref, range(CH_BLOCK), bsz)
    for p in range(S5_NS // 128):
        xin_ref[:, :, p * 128:(p + 1) * 128] = x_re[p].astype(BF16).reshape(bsz, CH_BLOCK, 128)
        xin_ref[:, :, S5_NS + p * 128:S5_NS + (p + 1) * 128] = x_im[p].astype(BF16).reshape(bsz, CH_BLOCK, 128)


def _s5_bwd_kernel(ush_ref, xf_ref, a16_ref, m_ref, toe_ref, r_ref, y_ref, d_re, d_im, x_re, x_im, sr_ref, si_ref,
                   *, bsz):
    @pl.when(pl.program_id(0) == 0)
    def _():
        sr_ref[...] = jnp.zeros_like(sr_ref)
        si_ref[...] = jnp.zeros_like(si_ref)

    rows = bsz * CH_BLOCK
    ush = ush_ref[...].reshape(rows, S5_CW)
    _s5_increments(ush, m_ref, d_re, d_im)
    _s5_recurrence(a16_ref, d_re, d_im, x_re, x_im, sr_ref, si_ref, reversed(range(CH_BLOCK)), bsz)
    xf = xf_ref[...].reshape(rows, 2 * S5_NS)
    ycols = []
    for p in range(S5_GROUPS // 2):
        lanes = slice(p * 128, (p + 1) * 128)
        xcat = jnp.concatenate([xf[:, lanes], xf[:, S5_NS + p * 128:S5_NS + (p + 1) * 128],
                                x_re[p].astype(BF16), x_im[p].astype(BF16)], axis=1)
        carry = _dot_nt(xcat, r_ref[p])
        for g2 in range(2):
            g = 2 * p + g2
            yg = carry[:, g2 * 256:(g2 + 1) * 256] + _dot(ush[:, g * 256:(g + 1) * 256], toe_ref[g])
            ycols += [yg[:, 0:128], yg[:, 128:256]]
    ycols = _lane_block_transpose(ycols)
    y_ref[...] = jnp.concatenate(ycols, axis=1).reshape(bsz, CH_BLOCK, S5_CW)


def _s5(u_c, tables, layer):
    a16, m_f, m_b, r_t, toe = tables
    bsz, n_rows, _ = u_c.shape
    nb = n_rows // CH_BLOCK
    n_ctx = CTX_LEN // TOK_BLOCK
    rows = bsz * CH_BLOCK
    blk3 = lambda w: (bsz, CH_BLOCK, w)
    scratch = [pltpu.VMEM((S5_NS // 128, rows, 128), F32)] * 4 + [pltpu.VMEM((bsz, S5_NS), F32)] * 2
    params = pltpu.CompilerParams(dimension_semantics=("arbitrary",), vmem_limit_bytes=VMEM_LIMIT)
    ush, xin_f = pl.pallas_call(
        functools.partial(_s5_fwd_kernel, bsz=bsz),
        grid=(nb,),
        in_specs=[
            pl.BlockSpec(blk3(S5_CW), lambda j: (0, j, 0)),
            pl.BlockSpec((None, None, 2, S5_NS), lambda j: (layer, 0, 0, 0)),
            pl.BlockSpec((None, S5_GROUPS // 2, 512, 256), lambda j: (layer, 0, 0, 0)),
        ],
        out_specs=[
            pl.BlockSpec(blk3(S5_CW), lambda j: (0, j, 0)),
            pl.BlockSpec(blk3(2 * S5_NS), lambda j: (0, j, 0)),
        ],
        out_shape=[
            jax.ShapeDtypeStruct((bsz, n_rows, S5_CW), BF16),
            jax.ShapeDtypeStruct((bsz, n_rows, 2 * S5_NS), BF16),
        ],
        scratch_shapes=scratch,
        compiler_params=params,
        name="s5_fwd",
    )(u_c, a16, m_f)
    back = lambda j: (0, _scan_block(1, j, n_ctx, nb), 0)
    return pl.pallas_call(
        functools.partial(_s5_bwd_kernel, bsz=bsz),
        grid=(nb,),
        in_specs=[
            pl.BlockSpec(blk3(S5_CW), back),
            pl.BlockSpec(blk3(2 * S5_NS), back),
            pl.BlockSpec((None, None, 2, S5_NS), lambda j: (layer, 1, 0, 0)),
            pl.BlockSpec((None, S5_GROUPS // 2, 512, 256), lambda j: (layer, 0, 0, 0)),
            pl.BlockSpec((None, S5_GROUPS, 256, 256), lambda j: (layer, 0, 0, 0)),
            pl.BlockSpec((None, S5_GROUPS // 2, 512, 512), lambda j: (layer, 0, 0, 0)),
        ],
        out_specs=pl.BlockSpec(blk3(S5_CW), back),
        out_shape=jax.ShapeDtypeStruct((bsz, n_rows, S5_CW), F32),
        scratch_shapes=scratch,
        compiler_params=params,
        name="s5_bwd",
    )(ush, xin_f, a16, m_b, toe, r_t)


def _out_kernel(xc_ref, x_ref, mod_ref, gof_ref, gob_ref, g_ref, att_ref, y_ref, uc_ref, gnw_ref, hm_ref, d_ref, gluw_ref,
                glub_ref, wo_ref, n2_ref, w1_ref, w2_ref, fn_ref, o_ref, ys_ref, *, first_block, final, split):
    j = pl.program_id(1) + first_block

    def mixed(r):
        m = mod_ref[r]
        mrow = jnp.where(j > 0, m[1:2, :], m[0:1, :])
        g1 = mrow[:, 2 * D_MODEL:3 * D_MODEL]
        sh2 = mrow[:, 3 * D_MODEL:4 * D_MODEL]
        sc2 = mrow[:, 4 * D_MODEL:5 * D_MODEL]
        g2 = mrow[:, 5 * D_MODEL:6 * D_MODEL]

        o = gof_ref[r] + gob_ref[r]
        ms = _dot((o * o).astype(BF16), hm_ref[...]) * (1.0 / GLA_DV)
        a = o * lax.rsqrt(ms + EPS) * gnw_ref[...] * _silu(g_ref[r])

        yc = y_ref[r] + d_ref[...] * uc_ref[r]
        for st in range(S5_LC):
            for half in range(S5_WIDTH // 128):
                lane0 = st * S5_WIDTH + half * 128
                ys_ref[r, half, pl.ds(st, CH_BLOCK, stride=S5_LC), :] = yc[:, lane0:lane0 + 128]
        yy = jnp.concatenate([ys_ref[r, half] for half in range(S5_WIDTH // 128)], axis=1)
        ge = 0.5 * yy * (1.0 + jnp.tanh(math.sqrt(2.0 / math.pi) * (yy + 0.044715 * (yy * yy * yy))))
        z = _dot(ge.astype(BF16), gluw_ref[...]) + glub_ref[...]
        s = z[:, 0:S5_WIDTH] * _sigmoid(z[:, S5_WIDTH:2 * S5_WIDTH])

        proj = (_dot(a.astype(BF16), wo_ref[0:GLA_WIDTH, :])
                + _dot(att_ref[r], wo_ref[GLA_WIDTH:GLA_WIDTH + ATT_WIDTH, :])
                + _dot(s.astype(BF16), wo_ref[GLA_WIDTH + ATT_WIDTH:D_MODEL, :]))
        x_in = jnp.where(j == 0, xc_ref[r], x_ref[r]) if split else x_ref[r]
        x1 = x_in + g1 * proj
        y2 = x1 * lax.rsqrt(jnp.mean(x1 * x1, axis=-1, keepdims=True) + EPS) * n2_ref[...]
        return x1, (y2 * (1.0 + sc2) + sh2).astype(BF16), g2

    def finish(r, x1, g2, mlp):
        x2 = x1 + g2 * mlp
        if final:
            x2 = x2 * lax.rsqrt(jnp.mean(x2 * x2, axis=-1, keepdims=True) + EPS) * fn_ref[...]
        o_ref[r] = x2

    cur = mixed(0)
    for r in range(OUT_ROWS):
        x1, h2, g2 = cur
        hid = jnp.maximum(_dot(h2, w1_ref[...]), 0.0)
        if r + 1 < OUT_ROWS:
            cur = mixed(r + 1)
        mlp = _dot((hid * hid).astype(BF16), w2_ref[...])
        finish(r, x1, g2, mlp)


def _outproj(x_ctx, x_rest, lat_skip, modsel, gla_f, gla_b, gg, att_o, y_c, u_c, gnw, hm, s5d, gluw, glub, wo, n2w,
             w1, w2, fnw, layer, final):
    bsz, tt, _ = gg.shape
    nb = tt // TOK_BLOCK
    first = CTX_LEN // TOK_BLOCK if final else 0
    nsteps = nb - first
    const2 = lambda b, j: (0, 0)
    lyr3 = lambda b, j: (layer, 0, 0)
    single = pl.Buffered(1)
    return pl.pallas_call(
        functools.partial(_out_kernel, first_block=first, final=final, split=lat_skip > 0),
        grid=(bsz // OUT_ROWS, nsteps),
        in_specs=_stream_specs(lat_skip, first, rows=OUT_ROWS) + [
            pl.BlockSpec((OUT_ROWS, 2, 6 * D_MODEL), lambda b, j: (b, 0, 0)),
            pl.BlockSpec((OUT_ROWS, TOK_BLOCK, GLA_WIDTH), lambda b, j: (b, j + first, 0)),
            pl.BlockSpec((OUT_ROWS, TOK_BLOCK, GLA_WIDTH), lambda b, j: (b, j + first, 0)),
            pl.BlockSpec((OUT_ROWS, TOK_BLOCK, GLA_WIDTH), lambda b, j: (b, j + first, 0)),
            pl.BlockSpec((OUT_ROWS, TOK_BLOCK, ATT_WIDTH), lambda b, j: (b, j + first, 0)),
            pl.BlockSpec((OUT_ROWS, CH_BLOCK, S5_CW), lambda b, j: (b, j + first, 0)),
            pl.BlockSpec((OUT_ROWS, CH_BLOCK, S5_CW), lambda b, j: (b, j + first, 0)),
            pl.BlockSpec((1, GLA_WIDTH), const2),
            pl.BlockSpec((GLA_WIDTH, GLA_WIDTH), const2),
            pl.BlockSpec((1, S5_CW), const2),
            pl.BlockSpec((None, S5_WIDTH, 2 * S5_WIDTH), lyr3, pipeline_mode=single),
            pl.BlockSpec((1, 2 * S5_WIDTH), const2),
            pl.BlockSpec((None, D_MODEL, D_MODEL), lyr3, pipeline_mode=single),
            pl.BlockSpec((1, D_MODEL), const2),
            pl.BlockSpec((None, D_MODEL, D_FF), lyr3, pipeline_mode=single),
            pl.BlockSpec((None, D_FF, D_MODEL), lyr3, pipeline_mode=single),
            pl.BlockSpec((1, D_MODEL), const2),
        ],
        out_specs=pl.BlockSpec((OUT_ROWS, TOK_BLOCK, D_MODEL), lambda b, j: (b, j, 0)),
        out_shape=jax.ShapeDtypeStruct((bsz, nsteps * TOK_BLOCK, D_MODEL), F32),
        scratch_shapes=[pltpu.VMEM((OUT_ROWS, S5_WIDTH // 128, TOK_BLOCK, 128), F32)],
        compiler_params=pltpu.CompilerParams(
            dimension_semantics=("arbitrary", "arbitrary"), vmem_limit_bytes=VMEM_LIMIT),
        name="outproj_mlp",
    )(x_ctx, x_rest, modsel, gla_f, gla_b, gg, att_o, y_c, u_c, gnw, hm, s5d, gluw, glub, wo, n2w, w1, w2, fnw)


def _rope_tables(tt):
    n_lat = tt - CTX_LEN
    rows = n_lat // GRID_W
    row = jnp.repeat(jnp.arange(rows, dtype=F32), GRID_W)
    col = jnp.tile(jnp.arange(GRID_W, dtype=F32), rows)
    n_freq = ATT_HD // 4
    inv_freq = ROPE_BASE ** (-jnp.arange(n_freq, dtype=F32) / n_freq)
    ang_r = row[:, None] * inv_freq
    ang_c = col[:, None] * inv_freq
    ang = jnp.concatenate([ang_r, ang_r, ang_c, ang_c], axis=-1)
    cos = jnp.concatenate([jnp.ones((CTX_LEN, ATT_HD), F32), jnp.cos(ang)], axis=0)
    sin = jnp.concatenate([jnp.zeros((CTX_LEN, ATT_HD), F32), jnp.sin(ang)], axis=0)
    up_quarter = (np.arange(ATT_HD) // 16) % 2 == 0
    sa = jnp.where(up_quarter, -sin, 0.0)
    sb = jnp.where(up_quarter, 0.0, sin)
    two = lambda t: jnp.concatenate([t, t], axis=-1)
    return two(cos), two(sa), two(sb)


def _pad_cols(w, width):
    return jnp.pad(w, ((0, 0), (0, 0), (0, width - w.shape[-1])))


def _layout_w_in(w_in):
    offs = np.cumsum([0, GLA_KW, GLA_KW, GLA_WIDTH, GLA_WIDTH, GLA_RANK, GLA_RANK, ATT_WIDTH, ATT_KVW, ATT_KVW,
                      S5_WIDTH])
    q, k, v, g, zf, zb, aq, ak, av, u = [w_in[:, :, offs[i]:offs[i + 1]] for i in range(10)]
    z = jnp.concatenate([zf, zb], axis=-1)
    cols = [v, g, _pad_cols(q, GLA_KPAD), _pad_cols(k, GLA_KPAD), _pad_cols(z, 128), aq, ak, av, u]
    return jnp.concatenate(cols, axis=-1).astype(BF16)


def _block_diag(t):
    s, g, a, b = t.shape
    eye = jnp.eye(g, dtype=t.dtype)
    return jnp.einsum('sgab,gk->sgakb', t, eye).reshape(s, g * a, g * b)


def kernel(x, c, ctx, c_ctx, w_mod, b_mod, norm1_w, norm2_w, w_in, gla_wa_f, gla_ba_f, gla_wa_b, gla_ba_b,
           gla_norm_w, attn_sink, s5_lam_re_f, s5_lam_im_f, s5_log_step_f, s5_b_re_f, s5_b_im_f, s5_c_re_f,
           s5_c_im_f, s5_lam_re_b, s5_lam_im_b, s5_log_step_b, s5_b_re_b, s5_b_im_b, s5_c_re_b, s5_c_im_b,
           s5_d, glu_w, glu_b, w_out, mlp_w1, mlp_w2, final_norm_w):
    bsz, seq, _ = x.shape
    n_layers = w_mod.shape[0]
    tt = CTX_LEN + seq
    assert bsz % 8 == 0 and bsz % IN_ROWS == 0 and bsz % GLA_ROWS == 0 and bsz % SWA_ROWS == 0 and bsz % OUT_ROWS == 0 and seq % TOK_BLOCK == 0 and seq >= 2 * TOK_BLOCK

    mod_rows = -(-(bsz + 1) // 8) * 8
    cvec = jnp.zeros((mod_rows, D_MODEL), F32).at[:bsz].set(c).at[bsz].set(c_ctx)
    mod = _modulation(cvec, w_mod, b_mod)
    mod_ctx = jnp.broadcast_to(mod[:, bsz][:, None], (n_layers, bsz, 6 * D_MODEL))
    modsel = jnp.stack([mod_ctx, mod[:, :bsz]], axis=2)

    w_in_p = _layout_w_in(w_in)
    wa_cat = jnp.zeros((n_layers, 128, 2 * GLA_KPAD), F32)
    wa_cat = wa_cat.at[:, 0:GLA_RANK, 0:GLA_KW].set(gla_wa_f)
    wa_cat = wa_cat.at[:, GLA_RANK:2 * GLA_RANK, GLA_KPAD:GLA_KPAD + GLA_KW].set(gla_wa_b).astype(BF16)
    ba_cat = jnp.zeros((n_layers, 1, 2 * GLA_KPAD), F32)
    ba_cat = ba_cat.at[:, 0, 0:GLA_KW].set(gla_ba_f).at[:, 0, GLA_KPAD:GLA_KPAD + GLA_KW].set(gla_ba_b)
    qscale = jnp.ones((1, 2 * GLA_KPAD), F32).at[:, 0:GLA_KW].set(GLA_DK ** -0.5)
    cos_t, sa_t, sb_t = _rope_tables(tt)
    gnw = jnp.tile(gla_norm_w, (1, GLA_HEADS))[:, None, :]
    head = np.arange(GLA_WIDTH) // GLA_DV
    hm = jnp.asarray(head[:, None] == head[None, :], BF16)
    wo_b = w_out.astype(BF16)
    w1_b = mlp_w1.astype(BF16)
    w2_b = mlp_w2.astype(BF16)
    gluw_b = glu_w.astype(BF16)

    s5_tab = _s5_tables(
        (s5_lam_re_f, s5_lam_im_f, s5_log_step_f, s5_b_re_f, s5_b_im_f, s5_c_re_f, s5_c_im_f),
        (s5_lam_re_b, s5_lam_im_b, s5_log_step_b, s5_b_re_b, s5_b_im_b, s5_c_re_b, s5_c_im_b))
    s5_dt = jnp.tile(s5_d, (1, S5_LC))[:, None, :]

    stream = (ctx, x, CTX_LEN // TOK_BLOCK)
    for l in range(n_layers):
        final = l == n_layers - 1
        gv, gg, gqk, gla, att, u_c = _inproj(*stream, modsel[l], norm1_w[l][None], w_in_p, wa_cat, ba_cat[l],
                                             qscale, cos_t, sa_t, sb_t, l)
        gla_f, gla_b = _gla(gqk, gv, gla)
        att_o = _swa(att, attn_sink[l])
        y_c = _s5(u_c, s5_tab, l)
        xs = _outproj(*stream, modsel[l], gla_f, gla_b, gg, att_o, y_c, u_c,
                      gnw[l], hm, s5_dt[l], gluw_b, glu_b[l][None], wo_b, norm2_w[l][None], w1_b, w2_b,
                      final_norm_w[None], l, final)
        stream = (xs, xs, 0)
    return xs
```

```python
import functools
import math

import jax
import jax.numpy as jnp
import numpy as np
from jax import lax
from jax.experimental import pallas as pl
from jax.experimental.pallas import tpu as pltpu

F32 = jnp.float32
BF16 = jnp.bfloat16

D_MODEL = 1024
D_FF = 4 * D_MODEL
CTX_LEN = 256
GRID_W = 64
EPS = 1e-6
NEG_INF = -1e30
LOG2E = math.log2(math.e)

GLA_HEADS = 4
GLA_DV = 96
GLA_DK = 48
GLA_WIDTH = GLA_HEADS * GLA_DV
GLA_KW = GLA_HEADS * GLA_DK
GLA_KPAD = 256
GLA_RANK = 16
GLA_TAU = 16.0
GLA_CHUNK = 64

ATT_HD = 64
ATT_HEADS = 6
ATT_KV_HEADS = 2
ATT_WIDTH = ATT_HEADS * ATT_HD
ATT_KVW = ATT_KV_HEADS * ATT_HD
WINDOW = 128
ATT_BLOCK = 128
ROPE_BASE = 10000.0

S5_WIDTH = 256
S5_GROUP = 16
S5_GROUPS = 16
S5_STATE = 64
S5_NS = S5_GROUPS * S5_STATE
S5_LC = 16
S5_CW = S5_LC * S5_WIDTH

TOK_BLOCK = 256
CH_BLOCK = TOK_BLOCK // S5_LC
GLA_ROWS = 4
SWA_ROWS = 2
IN_ROWS = 4
OUT_ROWS = 2

C_V, C_G, C_Q, C_K, C_Z, C_AQ, C_AK, C_AV, C_U = 0, 384, 768, 1024, 1280, 1408, 1792, 1920, 2048
IN_PAD = 2304
C_QK_END = C_K + GLA_KPAD

V7X_VMEM_BYTES = 64 * 1024 * 1024
VMEM_LIMIT = V7X_VMEM_BYTES - 8 * 1024 * 1024


def _sigmoid(x):
    return 1.0 / (1.0 + jnp.exp(-x))


def _silu(x):
    return x * _sigmoid(x)


def _dot(a, b):
    return jnp.dot(a, b, preferred_element_type=F32)


def _dot_nt(a, b):
    return lax.dot_general(a, b, (((1,), (1,)), ((), ())), preferred_element_type=F32)


def _dot_tn(a, b):
    return lax.dot_general(a, b, (((0,), (0,)), ((), ())), preferred_element_type=F32)


def _mod_kernel(c_ref, w_ref, b_ref, o_ref):
    a = _silu(c_ref[...]).astype(BF16)
    o_ref[...] = _dot(a, w_ref[...].astype(BF16)) + b_ref[...]


def _modulation(cvec, w_mod, b_mod):
    n_layers = w_mod.shape[0]
    rows = cvec.shape[0]
    tn = 1536
    return pl.pallas_call(
        _mod_kernel,
        grid=(n_layers, 6 * D_MODEL // tn),
        in_specs=[
            pl.BlockSpec((rows, D_MODEL), lambda l, n: (0, 0)),
            pl.BlockSpec((None, D_MODEL, tn), lambda l, n: (l, 0, n)),
            pl.BlockSpec((None, 1, tn), lambda l, n: (l, 0, n)),
        ],
        out_specs=pl.BlockSpec((None, rows, tn), lambda l, n: (l, 0, n)),
        out_shape=jax.ShapeDtypeStruct((n_layers, rows, 6 * D_MODEL), F32),
        compiler_params=pltpu.CompilerParams(
            dimension_semantics=("arbitrary", "arbitrary"), vmem_limit_bytes=VMEM_LIMIT),
        name="modulation",
    )(cvec, w_mod, b_mod.reshape(n_layers, 1, 6 * D_MODEL))


def _inproj_kernel(xc_ref, x_ref, mod_ref, n1_ref, w_ref, wa_ref, ba_ref, qs_ref, cos_ref, sa_ref, sb_ref,
                   gv_ref, gg_ref, gqk_ref, gla_ref, att_ref, uc_ref, us_ref, *, split):
    j = pl.program_id(1)
    cos = cos_ref[...]
    sa = sa_ref[...]
    sb = sb_ref[...]

    def normed(r):
        xf = jnp.where(j == 0, xc_ref[r], x_ref[r]) if split else x_ref[r]
        y = xf * lax.rsqrt(jnp.mean(xf * xf, axis=-1, keepdims=True) + EPS) * n1_ref[...]
        m = mod_ref[r]
        mrow = jnp.where(j > 0, m[1:2, :], m[0:1, :])
        sh1 = mrow[:, 0:D_MODEL]
        sc1 = mrow[:, D_MODEL:2 * D_MODEL]
        return (y * (1.0 + sc1) + sh1).astype(BF16)

    def rope(t, reps):
        w = t.shape[-1]
        c3 = jnp.concatenate([cos] * reps, axis=-1) if reps > 1 else cos
        a3 = jnp.concatenate([sa] * reps, axis=-1) if reps > 1 else sa
        b3 = jnp.concatenate([sb] * reps, axis=-1) if reps > 1 else sb
        up = pltpu.roll(t, w - 16, 1)
        dn = pltpu.roll(t, 16, 1)
        return t * c3 + up * a3 + dn * b3

    def tail_gla(r, p):
        gv_ref[r] = p[:, C_V:C_V + GLA_WIDTH].astype(BF16)
        gg_ref[r] = p[:, C_G:C_G + GLA_WIDTH]
        gqk_ref[r] = p[:, C_Q:C_QK_END] * qs_ref[...]

    def tail_rest(r, p):
        off = C_QK_END
        z = p[:, C_Z - off:C_Z - off + 128].astype(BF16)
        zg = _dot(z, wa_ref[...]) + ba_ref[...]
        gla_ref[r] = (jnp.minimum(zg, 0.0) - jnp.log1p(jnp.exp(-jnp.abs(zg)))) * (1.0 / GLA_TAU)

        aq = rope(p[:, C_AQ - off:C_AQ - off + ATT_WIDTH], 3) * (ATT_HD ** -0.5 * LOG2E)
        ak = rope(p[:, C_AK - off:C_AK - off + ATT_KVW], 1)
        av = p[:, C_AV - off:C_AV - off + ATT_KVW]
        left = lax.broadcasted_iota(jnp.int32, ak.shape, 1) < ATT_HD

        def expand(t):
            sw = pltpu.roll(t, ATT_HD, 1)
            return [jnp.where(left, t, sw), t, jnp.where(left, sw, t)]

        att_ref[r] = jnp.concatenate([aq] + expand(ak) + expand(av), axis=-1).astype(BF16)
        for half in range(S5_WIDTH // 128):
            us_ref[r, half] = p[:, C_U - off + half * 128:C_U - off + (half + 1) * 128]
        for st in range(S5_LC):
            for half in range(S5_WIDTH // 128):
                lane0 = st * S5_WIDTH + half * 128
                uc_ref[r, :, lane0:lane0 + 128] = us_ref[r, half, pl.ds(st, CH_BLOCK, stride=S5_LC), :]

    h = normed(0)
    prev_rest = None
    for r in range(IN_ROWS):
        p_gla = _dot(h, w_ref[:, 0:C_QK_END])
        if prev_rest is not None:
            tail_rest(r - 1, prev_rest)
        p_rest = _dot(h, w_ref[:, C_QK_END:IN_PAD])
        if r + 1 < IN_ROWS:
            h = normed(r + 1)
        tail_gla(r, p_gla)
        prev_rest = p_rest
    tail_rest(IN_ROWS - 1, prev_rest)


def _stream_specs(lat_skip, first=0, rows=None):
    return [pl.BlockSpec((rows, TOK_BLOCK, D_MODEL), lambda b, j: (b, 0, 0)),
            pl.BlockSpec((rows, TOK_BLOCK, D_MODEL), lambda b, j: (b, jnp.maximum(j + first - lat_skip, 0), 0))]


def _inproj(x_ctx, x_rest, lat_skip, modsel, n1w, w_in_p, wa_cat, ba_cat, qscale, cos_t, sa_t, sb_t, layer):
    bsz = x_ctx.shape[0]
    tt = cos_t.shape[0]
    nb = tt // TOK_BLOCK
    const = lambda b, j: (0, 0)
    lyr3 = lambda b, j: (layer, 0, 0)
    tok = lambda w: pl.BlockSpec((IN_ROWS, TOK_BLOCK, w), lambda b, j: (b, j, 0))
    return pl.pallas_call(
        functools.partial(_inproj_kernel, split=lat_skip > 0),
        grid=(bsz // IN_ROWS, nb),
        in_specs=_stream_specs(lat_skip, rows=IN_ROWS) + [
            pl.BlockSpec((IN_ROWS, 2, 6 * D_MODEL), lambda b, j: (b, 0, 0)),
            pl.BlockSpec((1, D_MODEL), const),
            pl.BlockSpec((None, D_MODEL, IN_PAD), lyr3),
            pl.BlockSpec((None, 128, 2 * GLA_KPAD), lyr3),
            pl.BlockSpec((1, 2 * GLA_KPAD), const),
            pl.BlockSpec((1, 2 * GLA_KPAD), const),
            pl.BlockSpec((TOK_BLOCK, 128), lambda b, j: (j, 0)),
            pl.BlockSpec((TOK_BLOCK, 128), lambda b, j: (j, 0)),
            pl.BlockSpec((TOK_BLOCK, 128), lambda b, j: (j, 0)),
        ],
        out_specs=[tok(GLA_WIDTH), tok(GLA_WIDTH), tok(2 * GLA_KPAD), tok(2 * GLA_KPAD), tok(3 * ATT_WIDTH),
                   pl.BlockSpec((IN_ROWS, CH_BLOCK, S5_CW), lambda b, j: (b, j, 0))],
        out_shape=[
            jax.ShapeDtypeStruct((bsz, tt, GLA_WIDTH), BF16),
            jax.ShapeDtypeStruct((bsz, tt, GLA_WIDTH), F32),
            jax.ShapeDtypeStruct((bsz, tt, 2 * GLA_KPAD), F32),
            jax.ShapeDtypeStruct((bsz, tt, 2 * GLA_KPAD), F32),
            jax.ShapeDtypeStruct((bsz, tt, 3 * ATT_WIDTH), BF16),
            jax.ShapeDtypeStruct((bsz, tt // S5_LC, S5_CW), F32),
        ],
        scratch_shapes=[pltpu.VMEM((IN_ROWS, S5_WIDTH // 128, TOK_BLOCK, 128), F32)],
        compiler_params=pltpu.CompilerParams(
            dimension_semantics=("arbitrary", "arbitrary"), vmem_limit_bytes=VMEM_LIMIT),
        name="inproj",
    )(x_ctx, x_rest, modsel, n1w, w_in_p, wa_cat, ba_cat, qscale, cos_t, sa_t, sb_t)


def _gla_kernel(qkf_ref, vf_ref, laf_ref, qkb_ref, vb_ref, lab_ref, tri_ref, tril_ref, hm_ref, vm_ref, bd_ref,
                of_ref, ob_ref, stf_ref, stb_ref):
    j = pl.program_id(1)

    @pl.when(j == 0)
    def _():
        stf_ref[...] = jnp.zeros_like(stf_ref)
        stb_ref[...] = jnp.zeros_like(stb_ref)

    n_chunks = TOK_BLOCK // GLA_CHUNK
    chunk = lambda t, c: t[c * GLA_CHUNK:(c + 1) * GLA_CHUNK]
    dirs = []
    for r in range(GLA_ROWS):
        dirs.append(((qkf_ref.at[r], vf_ref.at[r], laf_ref.at[r]), 0, GLA_CHUNK - 1, of_ref.at[r],
                     stf_ref.at[r], list(range(n_chunks))))
        dirs.append(((qkb_ref.at[r], vb_ref.at[r], lab_ref.at[r]), 1, 0, ob_ref.at[r],
                     stb_ref.at[r], list(reversed(range(n_chunks)))))

    cum = []
    for (_, _, la_ref), d, _, _, _, _ in dirs:
        la = la_ref[...]
        hi = la.astype(BF16)
        lo = (la - hi.astype(F32)).astype(BF16)
        cum.append(_dot(tri_ref[d], hi) + _dot(tri_ref[d], lo))

    ops = []
    for ((qk_ref, v_ref, _), d, last_row, _, _, _), b in zip(dirs, cum):
        q = qk_ref[:, 0:GLA_KPAD]
        k = qk_ref[:, GLA_KPAD:2 * GLA_KPAD]
        vb = v_ref[...]
        bl = [chunk(b, c)[last_row:last_row + 1, :] for c in range(n_chunks)]
        blx = jnp.concatenate([jnp.broadcast_to(t, (GLA_CHUNK, GLA_KPAD)) for t in bl], axis=0)
        qb = (q * jnp.exp(b)).astype(BF16)
        kb = (k * jnp.exp(-b)).astype(BF16)
        kd = (k * jnp.exp(blx - b)).astype(BF16)
        ops.append((qb, kb, kd, vb, bl))

    scores = []
    for (qb, kb, _, _, _), (_, d, _, _, _, _) in zip(ops, dirs):
        keep = tril_ref[d] > 0
        per = []
        for c in range(n_chunks):
            kst = jnp.concatenate([chunk(kb, c) * hm_ref[h] for h in range(GLA_HEADS)], axis=0)
            per.append(jnp.where(keep, _dot_nt(chunk(qb, c), kst), 0.0).astype(BF16))
        scores.append(per)

    bdmask = bd_ref[...]
    intra, inc = [], []
    for (_, _, kd, vb, _), per in zip(ops, scores):
        oi, ds = [], []
        for c in range(n_chunks):
            vc = chunk(vb, c)
            vbd = jnp.concatenate([vc * vm_ref[h] for h in range(GLA_HEADS)], axis=0)
            oi.append(_dot(per[c], vbd))
            ds.append(_dot_tn(vc, chunk(kd, c)) * bdmask)
        intra.append(oi)
        inc.append(ds)

    states = [d[4][...] for d in dirs]
    for i in range(n_chunks):
        for si, ((_, _, _, o_ref, _, order), (qb, _, _, _, bl)) in enumerate(zip(dirs, ops)):
            c = order[i]
            o_ref[c * GLA_CHUNK:(c + 1) * GLA_CHUNK, :] = (
                intra[si][c] + _dot_nt(chunk(qb, c), states[si].astype(BF16)))
            states[si] = states[si] * jnp.exp(bl[c]) + inc[si][c]
    for d, st in zip(dirs, states):
        d[4][...] = st


def _gla_masks():
    r = np.arange(GLA_CHUNK)
    lower = (r[None, :] <= r[:, None]).astype(np.float32)
    tri1 = np.stack([lower, lower.T])
    n_chunks = TOK_BLOCK // GLA_CHUNK
    tri = np.stack([np.kron(np.eye(n_chunks, dtype=np.float32), t) for t in tri1])
    tril = np.tile(tri1, (1, 1, GLA_HEADS))
    klane = np.arange(GLA_KPAD)
    vlane = np.arange(GLA_WIDTH)
    hm = np.stack([np.broadcast_to((klane // GLA_DK) == h, (GLA_CHUNK, GLA_KPAD)) for h in range(GLA_HEADS)])
    vm = np.stack([np.broadcast_to((vlane // GLA_DV) == h, (GLA_CHUNK, GLA_WIDTH)) for h in range(GLA_HEADS)])
    bd = ((vlane[:, None] // GLA_DV) == (klane[None, :] // GLA_DK)).astype(np.float32)
    return (jnp.asarray(tri, BF16), jnp.asarray(tril, F32), jnp.asarray(hm, BF16), jnp.asarray(vm, BF16),
            jnp.asarray(bd))


def _scan_block(d, j, n_ctx, n_all):
    bwd = jnp.where(j < n_ctx, n_ctx - 1 - j, n_all + n_ctx - 1 - j)
    return jnp.where(d == 0, j, bwd)


def _gla(gqk, gv, gla):
    bsz, tt, _ = gv.shape
    nb = tt // TOK_BLOCK
    n_ctx = CTX_LEN // TOK_BLOCK
    tri, tril, hm, vm, bd = _gla_masks()
    const2 = lambda b, j: (0, 0)
    const3 = lambda b, j: (0, 0, 0)
    fwd = lambda b, j: (b, j, 0)
    bwd = lambda b, j: (b, _scan_block(1, j, n_ctx, nb), 0)
    bwd_gate = lambda b, j: (b, _scan_block(1, j, n_ctx, nb), 1)
    tok = lambda w, imap: pl.BlockSpec((GLA_ROWS, TOK_BLOCK, w), imap)
    return pl.pallas_call(
        _gla_kernel,
        grid=(bsz // GLA_ROWS, nb),
        in_specs=[
            tok(2 * GLA_KPAD, fwd),
            tok(GLA_WIDTH, fwd),
            tok(GLA_KPAD, fwd),
            tok(2 * GLA_KPAD, bwd),
            tok(GLA_WIDTH, bwd),
            tok(GLA_KPAD, bwd_gate),
            pl.BlockSpec((2, TOK_BLOCK, TOK_BLOCK), const3),
            pl.BlockSpec((2, GLA_CHUNK, GLA_HEADS * GLA_CHUNK), const3),
            pl.BlockSpec((GLA_HEADS, GLA_CHUNK, GLA_KPAD), const3),
            pl.BlockSpec((GLA_HEADS, GLA_CHUNK, GLA_WIDTH), const3),
            pl.BlockSpec((GLA_WIDTH, GLA_KPAD), const2),
        ],
        out_specs=[tok(GLA_WIDTH, fwd), tok(GLA_WIDTH, bwd)],
        out_shape=[jax.ShapeDtypeStruct((bsz, tt, GLA_WIDTH), F32)] * 2,
        scratch_shapes=[pltpu.VMEM((GLA_ROWS, GLA_WIDTH, GLA_KPAD), F32)] * 2,
        compiler_params=pltpu.CompilerParams(
            dimension_semantics=("arbitrary", "arbitrary"), vmem_limit_bytes=VMEM_LIMIT),
        name="gla",
    )(gqk, gv, gla, gqk, gv, gla, tri, tril, hm, vm, bd)


def _swa_kernel(sink_ref, mask_ref, att_ref, o_ref, *, tt):
    j = pl.program_id(1)
    n_sub = TOK_BLOCK // ATT_BLOCK
    n_ctx_blocks = CTX_LEN // TOK_BLOCK
    last_q = tt // ATT_BLOCK - 1
    lane = lax.broadcasted_iota(jnp.int32, (ATT_BLOCK, 2 * ATT_HD), 1)
    left = lane < ATT_HD
    row2 = lax.broadcasted_iota(jnp.int32, (2 * ATT_BLOCK, 1), 0)
    units = [(r, sub, p) for r in range(SWA_ROWS) for sub in range(n_sub) for p in range(ATT_HEADS // 2)]

    def run(windowed):
        def score(r, sub, p):
            q0 = pl.multiple_of(j * TOK_BLOCK + sub * ATT_BLOCK, ATT_BLOCK)
            qp = att_ref[r, pl.ds(q0, ATT_BLOCK), p * 128:(p + 1) * 128]
            zero = jnp.zeros_like(qp)
            q2 = jnp.concatenate([jnp.where(left, qp, zero), jnp.where(left, zero, qp)], axis=0)
            kcol = ATT_WIDTH + p * 128
            s_c = _dot_nt(q2, att_ref[r, 0:CTX_LEN, kcol:kcol + 128])
            if not windowed:
                return s_c, None, None
            start = pl.multiple_of(jnp.minimum(q0 - ATT_BLOCK, tt - 3 * ATT_BLOCK), ATT_BLOCK)
            s_w = jnp.where(keeps[sub], _dot_nt(q2, att_ref[r, pl.ds(start, 3 * ATT_BLOCK), kcol:kcol + 128]),
                            NEG_INF)
            return s_c, s_w, start

        def softmax(p, s_c, s_w, start):
            sink_col = jnp.where(row2 < ATT_BLOCK, sink_ref[2 * p], sink_ref[2 * p + 1]) * LOG2E
            m = jnp.maximum(jnp.max(s_c, axis=-1, keepdims=True), sink_col)
            if windowed:
                m = jnp.maximum(m, jnp.max(s_w, axis=-1, keepdims=True))
            e_c = jnp.exp2(s_c - m)
            den = jnp.sum(e_c, axis=-1, keepdims=True) + jnp.exp2(sink_col - m)
            e_w = None
            if windowed:
                e_w = jnp.exp2(s_w - m)
                den = den + jnp.sum(e_w, axis=-1, keepdims=True)
                e_w = e_w.astype(BF16)
            return e_c.astype(BF16), e_w, den, start

        def values(r, sub, p, e_c, e_w, den, start):
            vcol = 2 * ATT_WIDTH + p * 128
            pv = _dot(e_c, att_ref[r, 0:CTX_LEN, vcol:vcol + 128])
            if windowed:
                pv = pv + _dot(e_w, att_ref[r, pl.ds(start, 3 * ATT_BLOCK), vcol:vcol + 128])
            o2 = pv / den
            o = jnp.where(left, o2[0:ATT_BLOCK], o2[ATT_BLOCK:2 * ATT_BLOCK])
            o_ref[r, sub * ATT_BLOCK:(sub + 1) * ATT_BLOCK, p * 128:(p + 1) * 128] = o.astype(BF16)

        keeps = []
        if windowed:
            for sub in range(n_sub):
                qi = j * n_sub + sub
                kind = jnp.where(qi == CTX_LEN // ATT_BLOCK, 0, jnp.where(qi == last_q, 2, 1))
                keeps.append(mask_ref[kind] > 0)

        n = len(units)
        sc, pr = {}, {}
        for t in range(n + 2):
            if t < n:
                sc[t] = score(*units[t])
            if 0 <= t - 1 < n:
                pr[t - 1] = softmax(units[t - 1][2], *sc.pop(t - 1))
            if 0 <= t - 2 < n:
                values(*units[t - 2], *pr.pop(t - 2))

    @pl.when(j < n_ctx_blocks)
    def _():
        run(False)

    @pl.when(j >= n_ctx_blocks)
    def _():
        run(True)


def _swa_masks():
    r = np.arange(2 * ATT_BLOCK)[:, None] % ATT_BLOCK
    c = np.arange(3 * ATT_BLOCK)[None, :]
    near = lambda delta: np.abs(r - c + delta) <= WINDOW
    first = near(ATT_BLOCK) & (c >= ATT_BLOCK)
    return jnp.asarray(np.stack([first, near(ATT_BLOCK), near(2 * ATT_BLOCK)]), F32)


def _swa(att, sink):
    bsz, tt, _ = att.shape
    nb = tt // TOK_BLOCK
    return pl.pallas_call(
        functools.partial(_swa_kernel, tt=tt),
        grid=(bsz // SWA_ROWS, nb),
        in_specs=[
            pl.BlockSpec(memory_space=pltpu.SMEM),
            pl.BlockSpec((3, 2 * ATT_BLOCK, 3 * ATT_BLOCK), lambda b, j: (0, 0, 0)),
            pl.BlockSpec((SWA_ROWS, tt, 3 * ATT_WIDTH), lambda b, j: (b, 0, 0)),
        ],
        out_specs=pl.BlockSpec((SWA_ROWS, TOK_BLOCK, ATT_WIDTH), lambda b, j: (b, j, 0)),
        out_shape=jax.ShapeDtypeStruct((bsz, tt, ATT_WIDTH), BF16),
        compiler_params=pltpu.CompilerParams(
            dimension_semantics=("arbitrary", "arbitrary"), vmem_limit_bytes=VMEM_LIMIT),
        name="swa",
    )(sink, _swa_masks(), att)


def _s5_prep_kernel(lr_ref, li_ref, ls_ref, br_ref, bi_ref, cr_ref, ci_ref,
                    a16_ref, mre_ref, mim_ref, rre_ref, rim_ref, k_ref):
    rows = S5_LC * S5_GROUP
    rowg = lax.broadcasted_iota(jnp.int32, (rows, S5_NS), 0) >> 4
    colg = lax.broadcasted_iota(jnp.int32, (rows, S5_NS), 1) >> 6
    same_group = rowg == colg
    taps_t = []
    for d in range(2):
        lr = jnp.minimum(lr_ref[d], -1e-4)
        li = li_ref[d]
        dt = jnp.exp(ls_ref[d])
        mag = jnp.exp(lr * dt)
        ar = mag * jnp.cos(li * dt)
        ai = mag * jnp.sin(li * dt)
        den = lr * lr + li * li
        fr = ((ar - 1.0) * lr + ai * li) / den
        fi = (ai * lr - (ar - 1.0) * li) / den
        br = br_ref[d]
        bi = bi_ref[d]
        bbr = fr * br - fi * bi
        bbi = fr * bi + fi * br
        cr = cr_ref[d]
        ci = ci_ref[d]
        pr = jnp.ones_like(ar)
        pi = jnp.zeros_like(ar)
        cp_re, cp_im = [], []
        for tau in range(S5_LC + 1):
            cpr = cr * pr - ci * pi
            cpi = cr * pi + ci * pr
            if tau < S5_LC:
                mre_ref[d, tau] = pr * bbr - pi * bbi
                mim_ref[d, tau] = pr * bbi + pi * bbr
                cp_re.append(cpr)
                cp_im.append(cpi)
            if tau >= 1:
                rre_ref[d, tau - 1] = cpr
                rim_ref[d, tau - 1] = -cpi
            if tau == S5_LC:
                a16_ref[d, 0:1, :] = pr
                a16_ref[d, 1:2, :] = pi
            pr, pi = pr * ar - pi * ai, pr * ai + pi * ar
        bd_re = jnp.where(same_group, jnp.concatenate([bbr] * S5_GROUPS, axis=0), 0.0)
        bd_im = jnp.where(same_group, jnp.concatenate([bbi] * S5_GROUPS, axis=0), 0.0)
        if d == 1:
            cp_re, cp_im = cp_re[::-1], cp_im[::-1]
        hp = lax.Precision.HIGHEST
        dims = (((1,), (1,)), ((), ()))
        taps_t.append(
            lax.dot_general(bd_re, jnp.concatenate(cp_re, axis=0), dims, precision=hp, preferred_element_type=F32)
            - lax.dot_general(bd_im, jnp.concatenate(cp_im, axis=0), dims, precision=hp,
                              preferred_element_type=F32))
    kf, kb_rev = taps_t
    tap0 = lax.broadcasted_iota(jnp.int32, kf.shape, 1) < S5_GROUP
    k_ref[0] = jnp.where(tap0, kf + pltpu.roll(kb_rev, S5_GROUP, 1), kf)
    k_ref[1] = kb_rev


def _s5_tables(fwd, bwd):
    n_layers = fwd[0].shape[0]
    both = lambda i: jnp.stack([fwd[i], bwd[i]], axis=1)
    lam_re = both(0).reshape(n_layers, 2, 1, S5_NS)
    lam_im = both(1).reshape(n_layers, 2, 1, S5_NS)
    log_step = jnp.repeat(both(2), S5_STATE, axis=-1).reshape(n_layers, 2, 1, S5_NS)
    b_hn = lambda t: t.transpose(0, 1, 4, 2, 3).reshape(n_layers, 2, S5_GROUP, S5_NS)
    c_hn = lambda t: t.transpose(0, 1, 3, 2, 4).reshape(n_layers, 2, S5_GROUP, S5_NS)
    vec = pl.BlockSpec((None, 2, 1, S5_NS), lambda l: (l, 0, 0, 0))
    mat = pl.BlockSpec((None, 2, S5_GROUP, S5_NS), lambda l: (l, 0, 0, 0))
    tab = pl.BlockSpec((None, 2, S5_LC, S5_GROUP, S5_NS), lambda l: (l, 0, 0, 0, 0))
    taps = S5_LC * S5_GROUP
    a16, mre, mim, rre, rim, k = pl.pallas_call(
        _s5_prep_kernel,
        grid=(n_layers,),
        in_specs=[vec, vec, vec, mat, mat, mat, mat],
        out_specs=[
            pl.BlockSpec((None, 2, 2, S5_NS), lambda l: (l, 0, 0, 0)),
            tab, tab, tab, tab,
            pl.BlockSpec((None, 2, S5_WIDTH, taps), lambda l: (l, 0, 0, 0)),
        ],
        out_shape=[
            jax.ShapeDtypeStruct((n_layers, 2, 2, S5_NS), F32),
            jax.ShapeDtypeStruct((n_layers, 2, S5_LC, S5_GROUP, S5_NS), F32),
            jax.ShapeDtypeStruct((n_layers, 2, S5_LC, S5_GROUP, S5_NS), F32),
            jax.ShapeDtypeStruct((n_layers, 2, S5_LC, S5_GROUP, S5_NS), F32),
            jax.ShapeDtypeStruct((n_layers, 2, S5_LC, S5_GROUP, S5_NS), F32),
            jax.ShapeDtypeStruct((n_layers, 2, S5_WIDTH, taps), F32),
        ],
        compiler_params=pltpu.CompilerParams(dimension_semantics=("arbitrary",), vmem_limit_bytes=VMEM_LIMIT),
        name="s5_prep",
    )(lam_re, lam_im, log_step, b_hn(both(3)), b_hn(both(4)), c_hn(both(5)), c_hn(both(6)))

    n_pairs = S5_GROUPS // 2
    lane_group = np.arange(128) // S5_STATE

    def pair_rows(t, flip):
        t = t[:, ::-1] if flip else t
        t = t.reshape(n_layers, S5_LC, S5_GROUP, n_pairs, 128).transpose(0, 3, 1, 2, 4)
        own = jnp.asarray(lane_group[None, :] == np.arange(2)[:, None])
        t = jnp.where(own[None, None, :, None, None, :], t[:, :, None], 0.0)
        return t.reshape(n_layers, n_pairs, 2 * taps, 128)

    m_f = jnp.concatenate([pair_rows(mre[:, 0], True), pair_rows(mim[:, 0], True)], axis=-1).astype(BF16)
    m_b = jnp.concatenate([pair_rows(mre[:, 1], False), pair_rows(mim[:, 1], False)], axis=-1).astype(BF16)

    r_t = jnp.concatenate([pair_rows(rre[:, 0], False), pair_rows(rim[:, 0], False),
                           pair_rows(rre[:, 1], True), pair_rows(rim[:, 1], True)], axis=-1).astype(BF16)

    zeros = jnp.zeros((n_layers, S5_WIDTH, taps), F32)
    f2 = jnp.concatenate([zeros, k[:, 0]], axis=-1)
    b2 = jnp.concatenate([k[:, 1], zeros], axis=-1)
    t_f = jnp.stack([f2[..., taps - S5_GROUP * st:2 * taps - S5_GROUP * st] for st in range(S5_LC)], axis=2)
    t_b = jnp.stack([b2[..., S5_GROUP * (S5_LC - 1 - st):S5_GROUP * (S5_LC - 1 - st) + taps]
                     for st in range(S5_LC)], axis=2)
    later = jnp.asarray(np.arange(taps)[None, :] // S5_GROUP >= np.arange(S5_LC)[:, None])
    toe = jnp.where(later[None, None], t_f, t_b)
    toe = toe.reshape(n_layers, S5_GROUPS, S5_GROUP, S5_LC, taps).transpose(0, 1, 3, 2, 4)
    toe = toe.reshape(n_layers, S5_GROUPS, taps, taps).astype(BF16)
    return a16, m_f, m_b, r_t, toe


def _lane_block_transpose(cols):
    lane = lax.broadcasted_iota(jnp.int32, cols[0].shape, 1)
    out = [None] * 32
    for ah in range(2):
        for bh in range(2):
            v = [cols[(ah * 8 + al) * 2 + bh] for al in range(8)]
            for kbit in range(3):
                width = 16 << kbit
                low = ((lane >> (4 + kbit)) & 1) == 0
                nxt = list(v)
                for i in range(8):
                    if i & (1 << kbit):
                        continue
                    lo_v, hi_v = v[i], v[i | (1 << kbit)]
                    nxt[i] = jnp.where(low, lo_v, pltpu.roll(hi_v, width, 1))
                    nxt[i | (1 << kbit)] = jnp.where(low, pltpu.roll(lo_v, 128 - width, 1), hi_v)
                v = nxt
            for bl in range(8):
                out[(bh * 8 + bl) * 2 + ah] = v[bl]
    return out


def _s5_increments(ush, m_ref, d_re, d_im):
    for p in range(S5_GROUPS // 2):
        dp = _dot(ush[:, p * 512:(p + 1) * 512], m_ref[p])
        d_re[p] = dp[:, 0:128]
        d_im[p] = dp[:, 128:256]


def _s5_recurrence(a16_ref, d_re, d_im, x_re, x_im, sr_ref, si_ref, order, bsz):
    ar = a16_ref[0:1, :]
    ai = a16_ref[1:2, :]
    sr = sr_ref[...]
    si = si_ref[...]
    n_slabs = S5_NS // 128
    gather = lambda ref, rows: jnp.concatenate([ref[p, rows, :] for p in range(n_slabs)], axis=1)
    for c in order:
        rows = pl.ds(c, bsz, stride=CH_BLOCK)
        for p in range(n_slabs):
            x_re[p, rows, :] = sr[:, p * 128:(p + 1) * 128]
            x_im[p, rows, :] = si[:, p * 128:(p + 1) * 128]
        sr, si = ar * sr - ai * si + gather(d_re, rows), ar * si + ai * sr + gather(d_im, rows)
    sr_ref[...] = sr
    si_ref[...] = si


def _s5_fwd_kernel(uc_ref, a16_ref, m_ref, ush_ref, xin_ref, d_re, d_im, x_re, x_im, sr_ref, si_ref, *, bsz):
    @pl.when(pl.program_id(0) == 0)
    def _():
        sr_ref[...] = jnp.zeros_like(sr_ref)
        si_ref[...] = jnp.zeros_like(si_ref)

    rows = bsz * CH_BLOCK
    ub = uc_ref[...].reshape(rows, S5_CW).astype(BF16)
    packed = pltpu.bitcast(ub, jnp.uint32)
    cols = _lane_block_transpose([packed[:, v * 128:(v + 1) * 128] for v in range(32)])
    ush = pltpu.bitcast(jnp.concatenate(cols, axis=1), BF16)
    ush_ref[...] = ush.reshape(bsz, CH_BLOCK, S5_CW)
    _s5_increments(ush, m_ref, d_re, d_im)
    _s5_recurrence(a16_ref, d_re, d_im, x_re, x_im, sr_ref, si_ref, range(CH_BLOCK), bsz)
    for p in range(S5_NS // 128):
        xin_ref[:, :, p * 128:(p + 1) * 128] = x_re[p].astype(BF16).reshape(bsz, CH_BLOCK, 128)
        xin_ref[:, :, S5_NS + p * 128:S5_NS + (p + 1) * 128] = x_im[p].astype(BF16).reshape(bsz, CH_BLOCK, 128)


def _s5_bwd_kernel(ush_ref, xf_ref, a16_ref, m_ref, toe_ref, r_ref, y_ref, d_re, d_im, x_re, x_im, sr_ref, si_ref,
                   *, bsz):
    @pl.when(pl.program_id(0) == 0)
    def _():
        sr_ref[...] = jnp.zeros_like(sr_ref)
        si_ref[...] = jnp.zeros_like(si_ref)

    rows = bsz * CH_BLOCK
    ush = ush_ref[...].reshape(rows, S5_CW)
    _s5_increments(ush, m_ref, d_re, d_im)
    _s5_recurrence(a16_ref, d_re, d_im, x_re, x_im, sr_ref, si_ref, reversed(range(CH_BLOCK)), bsz)
    xf = xf_ref[...].reshape(rows, 2 * S5_NS)
    ycols = []
    for p in range(S5_GROUPS // 2):
        lanes = slice(p * 128, (p + 1) * 128)
        xcat = jnp.concatenate([xf[:, lanes], xf[:, S5_NS + p * 128:S5_NS + (p + 1) * 128],
                                x_re[p].astype(BF16), x_im[p].astype(BF16)], axis=1)
        carry = _dot_nt(xcat, r_ref[p])
        for g2 in range(2):
            g = 2 * p + g2
            yg = carry[:, g2 * 256:(g2 + 1) * 256] + _dot(ush[:, g * 256:(g + 1) * 256], toe_ref[g])
            ycols += [yg[:, 0:128], yg[:, 128:256]]
    ycols = _lane_block_transpose(ycols)
    y_ref[...] = jnp.concatenate(ycols, axis=1).reshape(bsz, CH_BLOCK, S5_CW)


def _s5(u_c, tables, layer):
    a16, m_f, m_b, r_t, toe = tables
    bsz, n_rows, _ = u_c.shape
    nb = n_rows // CH_BLOCK
    n_ctx = CTX_LEN // TOK_BLOCK
    rows = bsz * CH_BLOCK
    blk3 = lambda w: (bsz, CH_BLOCK, w)
    scratch = [pltpu.VMEM((S5_NS // 128, rows, 128), F32)] * 4 + [pltpu.VMEM((bsz, S5_NS), F32)] * 2
    params = pltpu.CompilerParams(dimension_semantics=("arbitrary",), vmem_limit_bytes=VMEM_LIMIT)
    ush, xin_f = pl.pallas_call(
        functools.partial(_s5_fwd_kernel, bsz=bsz),
        grid=(nb,),
        in_specs=[
            pl.BlockSpec(blk3(S5_CW), lambda j: (0, j, 0)),
            pl.BlockSpec((None, None, 2, S5_NS), lambda j: (layer, 0, 0, 0)),
            pl.BlockSpec((None, S5_GROUPS // 2, 512, 256), lambda j: (layer, 0, 0, 0)),
        ],
        out_specs=[
            pl.BlockSpec(blk3(S5_CW), lambda j: (0, j, 0)),
            pl.BlockSpec(blk3(2 * S5_NS), lambda j: (0, j, 0)),
        ],
        out_shape=[
            jax.ShapeDtypeStruct((bsz, n_rows, S5_CW), BF16),
            jax.ShapeDtypeStruct((bsz, n_rows, 2 * S5_NS), BF16),
        ],
        scratch_shapes=scratch,
        compiler_params=params,
        name="s5_fwd",
    )(u_c, a16, m_f)
    back = lambda j: (0, _scan_block(1, j, n_ctx, nb), 0)
    return pl.pallas_call(
        functools.partial(_s5_bwd_kernel, bsz=bsz),
        grid=(nb,),
        in_specs=[
            pl.BlockSpec(blk3(S5_CW), back),
            pl.BlockSpec(blk3(2 * S5_NS), back),
            pl.BlockSpec((None, None, 2, S5_NS), lambda j: (layer, 1, 0, 0)),
            pl.BlockSpec((None, S5_GROUPS // 2, 512, 256), lambda j: (layer, 0, 0, 0)),
            pl.BlockSpec((None, S5_GROUPS, 256, 256), lambda j: (layer, 0, 0, 0)),
            pl.BlockSpec((None, S5_GROUPS // 2, 512, 512), lambda j: (layer, 0, 0, 0)),
        ],
        out_specs=pl.BlockSpec(blk3(S5_CW), back),
        out_shape=jax.ShapeDtypeStruct((bsz, n_rows, S5_CW), F32),
        scratch_shapes=scratch,
        compiler_params=params,
        name="s5_bwd",
    )(ush, xin_f, a16, m_b, toe, r_t)


def _out_kernel(xc_ref, x_ref, mod_ref, gof_ref, gob_ref, g_ref, att_ref, y_ref, uc_ref, gnw_ref, hm_ref, d_ref,
                gluw_ref, glub_ref, wo_ref, n2_ref, w1_ref, w2_ref, fn_ref, o_ref, ys_ref,
                *, first_block, final, split):
    j = pl.program_id(1) + first_block

    def mixed(r):
        m = mod_ref[r]
        mrow = jnp.where(j > 0, m[1:2, :], m[0:1, :])
        g1 = mrow[:, 2 * D_MODEL:3 * D_MODEL]
        sh2 = mrow[:, 3 * D_MODEL:4 * D_MODEL]
        sc2 = mrow[:, 4 * D_MODEL:5 * D_MODEL]
        g2 = mrow[:, 5 * D_MODEL:6 * D_MODEL]

        o = gof_ref[r] + gob_ref[r]
        ms = _dot((o * o).astype(BF16), hm_ref[...]) * (1.0 / GLA_DV)
        a = o * lax.rsqrt(ms + EPS) * gnw_ref[...] * _silu(g_ref[r])

        yc = y_ref[r] + d_ref[...] * uc_ref[r]
        for st in range(S5_LC):
            for half in range(S5_WIDTH // 128):
                lane0 = st * S5_WIDTH + half * 128
                ys_ref[r, half, pl.ds(st, CH_BLOCK, stride=S5_LC), :] = yc[:, lane0:lane0 + 128]
        yy = jnp.concatenate([ys_ref[r, half] for half in range(S5_WIDTH // 128)], axis=1)
        ge = 0.5 * yy * (1.0 + jnp.tanh(math.sqrt(2.0 / math.pi) * (yy + 0.044715 * (yy * yy * yy))))
        z = _dot(ge.astype(BF16), gluw_ref[...]) + glub_ref[...]
        s = z[:, 0:S5_WIDTH] * _sigmoid(z[:, S5_WIDTH:2 * S5_WIDTH])

        proj = (_dot(a.astype(BF16), wo_ref[0:GLA_WIDTH, :])
                + _dot(att_ref[r], wo_ref[GLA_WIDTH:GLA_WIDTH + ATT_WIDTH, :])
                + _dot(s.astype(BF16), wo_ref[GLA_WIDTH + ATT_WIDTH:D_MODEL, :]))
        x_in = jnp.where(j == 0, xc_ref[r], x_ref[r]) if split else x_ref[r]
        x1 = x_in + g1 * proj
        y2 = x1 * lax.rsqrt(jnp.mean(x1 * x1, axis=-1, keepdims=True) + EPS) * n2_ref[...]
        return x1, (y2 * (1.0 + sc2) + sh2).astype(BF16), g2

    def finish(r, x1, g2, mlp):
        x2 = x1 + g2 * mlp
        if final:
            x2 = x2 * lax.rsqrt(jnp.mean(x2 * x2, axis=-1, keepdims=True) + EPS) * fn_ref[...]
        o_ref[r] = x2

    cur = mixed(0)
    for r in range(OUT_ROWS):
        x1, h2, g2 = cur
        hid = jnp.maximum(_dot(h2, w1_ref[...]), 0.0)
        if r + 1 < OUT_ROWS:
            cur = mixed(r + 1)
        mlp = _dot((hid * hid).astype(BF16), w2_ref[...])
        finish(r, x1, g2, mlp)


def _outproj(x_ctx, x_rest, lat_skip, modsel, gla_f, gla_b, gg, att_o, y_c, u_c, gnw, hm, s5d, gluw, glub, wo, n2w,
             w1, w2, fnw, layer, final):
    bsz, tt, _ = gg.shape
    nb = tt // TOK_BLOCK
    first = CTX_LEN // TOK_BLOCK if final else 0
    nsteps = nb - first
    const2 = lambda b, j: (0, 0)
    lyr3 = lambda b, j: (layer, 0, 0)
    single = pl.Buffered(1)
    return pl.pallas_call(
        functools.partial(_out_kernel, first_block=first, final=final, split=lat_skip > 0),
        grid=(bsz // OUT_ROWS, nsteps),
        in_specs=_stream_specs(lat_skip, first, rows=OUT_ROWS) + [
            pl.BlockSpec((OUT_ROWS, 2, 6 * D_MODEL), lambda b, j: (b, 0, 0)),
            pl.BlockSpec((OUT_ROWS, TOK_BLOCK, GLA_WIDTH), lambda b, j: (b, j + first, 0)),
            pl.BlockSpec((OUT_ROWS, TOK_BLOCK, GLA_WIDTH), lambda b, j: (b, j + first, 0)),
            pl.BlockSpec((OUT_ROWS, TOK_BLOCK, GLA_WIDTH), lambda b, j: (b, j + first, 0)),
            pl.BlockSpec((OUT_ROWS, TOK_BLOCK, ATT_WIDTH), lambda b, j: (b, j + first, 0)),
            pl.BlockSpec((OUT_ROWS, CH_BLOCK, S5_CW), lambda b, j: (b, j + first, 0)),
            pl.BlockSpec((OUT_ROWS, CH_BLOCK, S5_CW), lambda b, j: (b, j + first, 0)),
            pl.BlockSpec((1, GLA_WIDTH), const2),
            pl.BlockSpec((GLA_WIDTH, GLA_WIDTH), const2),
            pl.BlockSpec((1, S5_CW), const2),
            pl.BlockSpec((None, S5_WIDTH, 2 * S5_WIDTH), lyr3, pipeline_mode=single),
            pl.BlockSpec((1, 2 * S5_WIDTH), const2),
            pl.BlockSpec((None, D_MODEL, D_MODEL), lyr3, pipeline_mode=single),
            pl.BlockSpec((1, D_MODEL), const2),
            pl.BlockSpec((None, D_MODEL, D_FF), lyr3, pipeline_mode=single),
            pl.BlockSpec((None, D_FF, D_MODEL), lyr3, pipeline_mode=single),
            pl.BlockSpec((1, D_MODEL), const2),
        ],
        out_specs=pl.BlockSpec((OUT_ROWS, TOK_BLOCK, D_MODEL), lambda b, j: (b, j, 0)),
        out_shape=jax.ShapeDtypeStruct((bsz, nsteps * TOK_BLOCK, D_MODEL), F32),
        scratch_shapes=[pltpu.VMEM((OUT_ROWS, S5_WIDTH // 128, TOK_BLOCK, 128), F32)],
        compiler_params=pltpu.CompilerParams(
            dimension_semantics=("arbitrary", "arbitrary"), vmem_limit_bytes=VMEM_LIMIT),
        name="outproj_mlp",
    )(x_ctx, x_rest, modsel, gla_f, gla_b, gg, att_o, y_c, u_c, gnw, hm, s5d, gluw, glub, wo, n2w, w1, w2, fnw)


def _rope_tables(tt):
    n_lat = tt - CTX_LEN
    rows = n_lat // GRID_W
    row = jnp.repeat(jnp.arange(rows, dtype=F32), GRID_W)
    col = jnp.tile(jnp.arange(GRID_W, dtype=F32), rows)
    n_freq = ATT_HD // 4
    inv_freq = ROPE_BASE ** (-jnp.arange(n_freq, dtype=F32) / n_freq)
    ang_r = row[:, None] * inv_freq
    ang_c = col[:, None] * inv_freq
    ang = jnp.concatenate([ang_r, ang_r, ang_c, ang_c], axis=-1)
    cos = jnp.concatenate([jnp.ones((CTX_LEN, ATT_HD), F32), jnp.cos(ang)], axis=0)
    sin = jnp.concatenate([jnp.zeros((CTX_LEN, ATT_HD), F32), jnp.sin(ang)], axis=0)
    up_quarter = (np.arange(ATT_HD) // 16) % 2 == 0
    sa = jnp.where(up_quarter, -sin, 0.0)
    sb = jnp.where(up_quarter, 0.0, sin)
    two = lambda t: jnp.concatenate([t, t], axis=-1)
    return two(cos), two(sa), two(sb)


def _pad_cols(w, width):
    return jnp.pad(w, ((0, 0), (0, 0), (0, width - w.shape[-1])))


def _layout_w_in(w_in):
    offs = np.cumsum([0, GLA_KW, GLA_KW, GLA_WIDTH, GLA_WIDTH, GLA_RANK, GLA_RANK, ATT_WIDTH, ATT_KVW, ATT_KVW,
                      S5_WIDTH])
    q, k, v, g, zf, zb, aq, ak, av, u = [w_in[:, :, offs[i]:offs[i + 1]] for i in range(10)]
    z = jnp.concatenate([zf, zb], axis=-1)
    cols = [v, g, _pad_cols(q, GLA_KPAD), _pad_cols(k, GLA_KPAD), _pad_cols(z, 128), aq, ak, av, u]
    return jnp.concatenate(cols, axis=-1).astype(BF16)


def _block_diag(t):
    s, g, a, b = t.shape
    eye = jnp.eye(g, dtype=t.dtype)
    return jnp.einsum('sgab,gk->sgakb', t, eye).reshape(s, g * a, g * b)


def kernel(x, c, ctx, c_ctx, w_mod, b_mod, norm1_w, norm2_w, w_in, gla_wa_f, gla_ba_f, gla_wa_b, gla_ba_b,
           gla_norm_w, attn_sink, s5_lam_re_f, s5_lam_im_f, s5_log_step_f, s5_b_re_f, s5_b_im_f, s5_c_re_f,
           s5_c_im_f, s5_lam_re_b, s5_lam_im_b, s5_log_step_b, s5_b_re_b, s5_b_im_b, s5_c_re_b, s5_c_im_b,
           s5_d, glu_w, glu_b, w_out, mlp_w1, mlp_w2, final_norm_w):
    bsz, seq, _ = x.shape
    n_layers = w_mod.shape[0]
    tt = CTX_LEN + seq
    assert all(bsz % rows == 0 for rows in (8, IN_ROWS, GLA_ROWS, SWA_ROWS, OUT_ROWS))
    assert seq % TOK_BLOCK == 0 and seq >= 2 * TOK_BLOCK and ctx.shape[1] == CTX_LEN

    mod_rows = -(-(bsz + 1) // 8) * 8
    cvec = jnp.zeros((mod_rows, D_MODEL), F32).at[:bsz].set(c).at[bsz].set(c_ctx)
    mod = _modulation(cvec, w_mod, b_mod)
    mod_ctx = jnp.broadcast_to(mod[:, bsz][:, None], (n_layers, bsz, 6 * D_MODEL))
    modsel = jnp.stack([mod_ctx, mod[:, :bsz]], axis=2)

    w_in_p = _layout_w_in(w_in)
    wa_cat = jnp.zeros((n_layers, 128, 2 * GLA_KPAD), F32)
    wa_cat = wa_cat.at[:, 0:GLA_RANK, 0:GLA_KW].set(gla_wa_f)
    wa_cat = wa_cat.at[:, GLA_RANK:2 * GLA_RANK, GLA_KPAD:GLA_KPAD + GLA_KW].set(gla_wa_b).astype(BF16)
    ba_cat = jnp.zeros((n_layers, 1, 2 * GLA_KPAD), F32)
    ba_cat = ba_cat.at[:, 0, 0:GLA_KW].set(gla_ba_f).at[:, 0, GLA_KPAD:GLA_KPAD + GLA_KW].set(gla_ba_b)
    qscale = jnp.ones((1, 2 * GLA_KPAD), F32).at[:, 0:GLA_KW].set(GLA_DK ** -0.5)
    cos_t, sa_t, sb_t = _rope_tables(tt)
    gnw = jnp.tile(gla_norm_w, (1, GLA_HEADS))[:, None, :]
    head = np.arange(GLA_WIDTH) // GLA_DV
    hm = jnp.asarray(head[:, None] == head[None, :], BF16)
    wo_b = w_out.astype(BF16)
    w1_b = mlp_w1.astype(BF16)
    w2_b = mlp_w2.astype(BF16)
    gluw_b = glu_w.astype(BF16)

    s5_tab = _s5_tables(
        (s5_lam_re_f, s5_lam_im_f, s5_log_step_f, s5_b_re_f, s5_b_im_f, s5_c_re_f, s5_c_im_f),
        (s5_lam_re_b, s5_lam_im_b, s5_log_step_b, s5_b_re_b, s5_b_im_b, s5_c_re_b, s5_c_im_b))
    s5_dt = jnp.tile(s5_d, (1, S5_LC))[:, None, :]

    stream = (ctx, x, CTX_LEN // TOK_BLOCK)
    for l in range(n_layers):
        final = l == n_layers - 1
        gv, gg, gqk, gla, att, u_c = _inproj(*stream, modsel[l], norm1_w[l][None], w_in_p, wa_cat, ba_cat[l],
                                             qscale, cos_t, sa_t, sb_t, l)
        gla_f, gla_b = _gla(gqk, gv, gla)
        att_o = _swa(att, attn_sink[l])
        y_c = _s5(u_c, s5_tab, l)
        xs = _outproj(*stream, modsel[l], gla_f, gla_b, gg, att_o, y_c, u_c,
                      gnw[l], hm, s5_dt[l], gluw_b, glu_b[l][None], wo_b, norm2_w[l][None], w1_b, w2_b,
                      final_norm_w[None], l, final)
        stream = (xs, xs, 0)
    return xs
```

```python
import functools
import math

import jax
import jax.numpy as jnp
import numpy as np
from jax import lax
from jax.experimental import pallas as pl
from jax.experimental.pallas import tpu as pltpu

F32 = jnp.float32
BF16 = jnp.bfloat16

D_MODEL = 1024
D_FF = 4 * D_MODEL
CTX_LEN = 256
GRID_W = 64
EPS = 1e-6
NEG_INF = -1e30
LOG2E = math.log2(math.e)

GLA_HEADS = 4
GLA_DV = 96
GLA_DK = 48
GLA_WIDTH = GLA_HEADS * GLA_DV
GLA_KW = GLA_HEADS * GLA_DK
GLA_KPAD = 256
GLA_RANK = 16
GLA_TAU = 16.0
GLA_CHUNK = 64

ATT_HD = 64
ATT_HEADS = 6
ATT_KV_HEADS = 2
ATT_WIDTH = ATT_HEADS * ATT_HD
ATT_KVW = ATT_KV_HEADS * ATT_HD
WINDOW = 128
ATT_BLOCK = 128
ROPE_BASE = 10000.0

S5_WIDTH = 256
S5_GROUP = 16
S5_GROUPS = 16
S5_STATE = 64
S5_NS = S5_GROUPS * S5_STATE
S5_LC = 16
S5_CW = S5_LC * S5_WIDTH

TOK_BLOCK = 256
CH_BLOCK = TOK_BLOCK // S5_LC
GLA_ROWS = 4
SWA_ROWS = 2
IN_ROWS = 4
OUT_ROWS = 2

C_V, C_G, C_Q, C_K, C_Z, C_AQ, C_AK, C_AV, C_U = 0, 384, 768, 1024, 1280, 1408, 1792, 1920, 2048
IN_PAD = 2304
C_QK_END = C_K + GLA_KPAD

V7X_VMEM_BYTES = 64 * 1024 * 1024
VMEM_LIMIT = V7X_VMEM_BYTES - 8 * 1024 * 1024


def _sigmoid(x):
    return 1.0 / (1.0 + jnp.exp(-x))


def _silu(x):
    return x * _sigmoid(x)


def _dot(a, b):
    return jnp.dot(a, b, preferred_element_type=F32)


def _dot_nt(a, b):
    return lax.dot_general(a, b, (((1,), (1,)), ((), ())), preferred_element_type=F32)


def _dot_tn(a, b):
    return lax.dot_general(a, b, (((0,), (0,)), ((), ())), preferred_element_type=F32)


def _mod_kernel(c_ref, w_ref, b_ref, o_ref):
    a = _silu(c_ref[...]).astype(BF16)
    o_ref[...] = _dot(a, w_ref[...].astype(BF16)) + b_ref[...]


def _modulation(cvec, w_mod, b_mod):
    n_layers = w_mod.shape[0]
    rows = cvec.shape[0]
    tn = 1536
    return pl.pallas_call(
        _mod_kernel,
        grid=(n_layers, 6 * D_MODEL // tn),
        in_specs=[
            pl.BlockSpec((rows, D_MODEL), lambda l, n: (0, 0)),
            pl.BlockSpec((None, D_MODEL, tn), lambda l, n: (l, 0, n)),
            pl.BlockSpec((None, 1, tn), lambda l, n: (l, 0, n)),
        ],
        out_specs=pl.BlockSpec((None, rows, tn), lambda l, n: (l, 0, n)),
        out_shape=jax.ShapeDtypeStruct((n_layers, rows, 6 * D_MODEL), F32),
        compiler_params=pltpu.CompilerParams(
            dimension_semantics=("arbitrary", "arbitrary"), vmem_limit_bytes=VMEM_LIMIT),
        name="modulation",
    )(cvec, w_mod, b_mod.reshape(n_layers, 1, 6 * D_MODEL))


def _inproj_kernel(xc_ref, x_ref, mod_ref, n1_ref, w_ref, wa_ref, ba_ref, qs_ref, cos_ref, sa_ref, sb_ref,
                   gv_ref, gg_ref, gqk_ref, gla_ref, att_ref, uc_ref, us_ref, *, split):
    j = pl.program_id(1)
    cos = cos_ref[...]
    sa = sa_ref[...]
    sb = sb_ref[...]

    def normed(r):
        xf = jnp.where(j == 0, xc_ref[r], x_ref[r]) if split else x_ref[r]
        y = xf * lax.rsqrt(jnp.mean(xf * xf, axis=-1, keepdims=True) + EPS) * n1_ref[...]
        m = mod_ref[r]
        mrow = jnp.where(j > 0, m[1:2, :], m[0:1, :])
        sh1 = mrow[:, 0:D_MODEL]
        sc1 = mrow[:, D_MODEL:2 * D_MODEL]
        return (y * (1.0 + sc1) + sh1).astype(BF16)

    def rope(t, reps):
        w = t.shape[-1]
        c3 = jnp.concatenate([cos] * reps, axis=-1) if reps > 1 else cos
        a3 = jnp.concatenate([sa] * reps, axis=-1) if reps > 1 else sa
        b3 = jnp.concatenate([sb] * reps, axis=-1) if reps > 1 else sb
        up = pltpu.roll(t, w - 16, 1)
        dn = pltpu.roll(t, 16, 1)
        return t * c3 + up * a3 + dn * b3

    def tail_gla(r, p):
        gv_ref[r] = p[:, C_V:C_V + GLA_WIDTH].astype(BF16)
        gg_ref[r] = p[:, C_G:C_G + GLA_WIDTH]
        gqk_ref[r] = p[:, C_Q:C_QK_END] * qs_ref[...]

    def tail_rest(r, p):
        off = C_QK_END
        z = p[:, C_Z - off:C_Z - off + 128].astype(BF16)
        zg = _dot(z, wa_ref[...]) + ba_ref[...]
        gla_ref[r] = (jnp.minimum(zg, 0.0) - jnp.log1p(jnp.exp(-jnp.abs(zg)))) * (1.0 / GLA_TAU)

        aq = rope(p[:, C_AQ - off:C_AQ - off + ATT_WIDTH], 3) * (ATT_HD ** -0.5 * LOG2E)
        ak = rope(p[:, C_AK - off:C_AK - off + ATT_KVW], 1)
        av = p[:, C_AV - off:C_AV - off + ATT_KVW]
        left = lax.broadcasted_iota(jnp.int32, ak.shape, 1) < ATT_HD

        def expand(t):
            sw = pltpu.roll(t, ATT_HD, 1)
            return [jnp.where(left, t, sw), t, jnp.where(left, sw, t)]

        att_ref[r] = jnp.concatenate([aq] + expand(ak) + expand(av), axis=-1).astype(BF16)
        for half in range(S5_WIDTH // 128):
            us_ref[r, half] = p[:, C_U - off + half * 128:C_U - off + (half + 1) * 128]
        for st in range(S5_LC):
            for half in range(S5_WIDTH // 128):
                lane0 = st * S5_WIDTH + half * 128
                uc_ref[r, :, lane0:lane0 + 128] = us_ref[r, half, pl.ds(st, CH_BLOCK, stride=S5_LC), :]

    h = normed(0)
    prev_rest = None
    for r in range(IN_ROWS):
        p_gla = _dot(h, w_ref[:, 0:C_QK_END])
        if prev_rest is not None:
            tail_rest(r - 1, prev_rest)
        p_rest = _dot(h, w_ref[:, C_QK_END:IN_PAD])
        if r + 1 < IN_ROWS:
            h = normed(r + 1)
        tail_gla(r, p_gla)
        prev_rest = p_rest
    tail_rest(IN_ROWS - 1, prev_rest)


def _stream_specs(lat_skip, first=0, rows=None):
    return [pl.BlockSpec((rows, TOK_BLOCK, D_MODEL), lambda b, j: (b, 0, 0)),
            pl.BlockSpec((rows, TOK_BLOCK, D_MODEL), lambda b, j: (b, jnp.maximum(j + first - lat_skip, 0), 0))]


def _inproj(x_ctx, x_rest, lat_skip, modsel, n1w, w_in_p, wa_cat, ba_cat, qscale, cos_t, sa_t, sb_t, layer):
    bsz = x_ctx.shape[0]
    tt = cos_t.shape[0]
    nb = tt // TOK_BLOCK
    const = lambda b, j: (0, 0)
    lyr3 = lambda b, j: (layer, 0, 0)
    tok = lambda w: pl.BlockSpec((IN_ROWS, TOK_BLOCK, w), lambda b, j: (b, j, 0))
    return pl.pallas_call(
        functools.partial(_inproj_kernel, split=lat_skip > 0),
        grid=(bsz // IN_ROWS, nb),
        in_specs=_stream_specs(lat_skip, rows=IN_ROWS) + [
            pl.BlockSpec((IN_ROWS, 2, 6 * D_MODEL), lambda b, j: (b, 0, 0)),
            pl.BlockSpec((1, D_MODEL), const),
            pl.BlockSpec((None, D_MODEL, IN_PAD), lyr3),
            pl.BlockSpec((None, 128, 2 * GLA_KPAD), lyr3),
            pl.BlockSpec((1, 2 * GLA_KPAD), const),
            pl.BlockSpec((1, 2 * GLA_KPAD), const),
            pl.BlockSpec((TOK_BLOCK, 128), lambda b, j: (j, 0)),
            pl.BlockSpec((TOK_BLOCK, 128), lambda b, j: (j, 0)),
            pl.BlockSpec((TOK_BLOCK, 128), lambda b, j: (j, 0)),
        ],
        out_specs=[tok(GLA_WIDTH), tok(GLA_WIDTH), tok(2 * GLA_KPAD), tok(2 * GLA_KPAD), tok(3 * ATT_WIDTH),
                   pl.BlockSpec((IN_ROWS, CH_BLOCK, S5_CW), lambda b, j: (b, j, 0))],
        out_shape=[
            jax.ShapeDtypeStruct((bsz, tt, GLA_WIDTH), BF16),
            jax.ShapeDtypeStruct((bsz, tt, GLA_WIDTH), F32),
            jax.ShapeDtypeStruct((bsz, tt, 2 * GLA_KPAD), F32),
            jax.ShapeDtypeStruct((bsz, tt, 2 * GLA_KPAD), F32),
            jax.ShapeDtypeStruct((bsz, tt, 3 * ATT_WIDTH), BF16),
            jax.ShapeDtypeStruct((bsz, tt // S5_LC, S5_CW), F32),
        ],
        scratch_shapes=[pltpu.VMEM((IN_ROWS, S5_WIDTH // 128, TOK_BLOCK, 128), F32)],
        compiler_params=pltpu.CompilerParams(
            dimension_semantics=("arbitrary", "arbitrary"), vmem_limit_bytes=VMEM_LIMIT),
        name="inproj",
    )(x_ctx, x_rest, modsel, n1w, w_in_p, wa_cat, ba_cat, qscale, cos_t, sa_t, sb_t)


def _gla_kernel(qkf_ref, vf_ref, laf_ref, qkb_ref, vb_ref, lab_ref, tri_ref, tril_ref, hm_ref, vm_ref, bd_ref,
                of_ref, ob_ref, stf_ref, stb_ref):
    j = pl.program_id(1)

    @pl.when(j == 0)
    def _():
        stf_ref[...] = jnp.zeros_like(stf_ref)
        stb_ref[...] = jnp.zeros_like(stb_ref)

    n_chunks = TOK_BLOCK // GLA_CHUNK
    chunk = lambda t, c: t[c * GLA_CHUNK:(c + 1) * GLA_CHUNK]
    dirs = []
    for r in range(GLA_ROWS):
        dirs.append(((qkf_ref.at[r], vf_ref.at[r], laf_ref.at[r]), 0, GLA_CHUNK - 1, of_ref.at[r],
                     stf_ref.at[r], list(range(n_chunks))))
        dirs.append(((qkb_ref.at[r], vb_ref.at[r], lab_ref.at[r]), 1, 0, ob_ref.at[r],
                     stb_ref.at[r], list(reversed(range(n_chunks)))))

    cum = []
    for (_, _, la_ref), d, _, _, _, _ in dirs:
        la = la_ref[...]
        hi = la.astype(BF16)
        lo = (la - hi.astype(F32)).astype(BF16)
        cum.append(_dot(tri_ref[d], hi) + _dot(tri_ref[d], lo))

    ops = []
    for ((qk_ref, v_ref, _), d, last_row, _, _, _), b in zip(dirs, cum):
        q = qk_ref[:, 0:GLA_KPAD]
        k = qk_ref[:, GLA_KPAD:2 * GLA_KPAD]
        vb = v_ref[...]
        bl = [chunk(b, c)[last_row:last_row + 1, :] for c in range(n_chunks)]
        blx = jnp.concatenate([jnp.broadcast_to(t, (GLA_CHUNK, GLA_KPAD)) for t in bl], axis=0)
        qb = (q * jnp.exp(b)).astype(BF16)
        kb = (k * jnp.exp(-b)).astype(BF16)
        kd = (k * jnp.exp(blx - b)).astype(BF16)
        ops.append((qb, kb, kd, vb, bl))

    scores = []
    for (qb, kb, _, _, _), (_, d, _, _, _, _) in zip(ops, dirs):
        keep = tril_ref[d] > 0
        per = []
        for c in range(n_chunks):
            kst = jnp.concatenate([chunk(kb, c) * hm_ref[h] for h in range(GLA_HEADS)], axis=0)
            per.append(jnp.where(keep, _dot_nt(chunk(qb, c), kst), 0.0).astype(BF16))
        scores.append(per)

    bdmask = bd_ref[...]
    intra, inc = [], []
    for (_, _, kd, vb, _), per in zip(ops, scores):
        oi, ds = [], []
        for c in range(n_chunks):
            vc = chunk(vb, c)
            vbd = jnp.concatenate([vc * vm_ref[h] for h in range(GLA_HEADS)], axis=0)
            oi.append(_dot(per[c], vbd))
            ds.append(_dot_tn(vc, chunk(kd, c)) * bdmask)
        intra.append(oi)
        inc.append(ds)

    states = [d[4][...] for d in dirs]
    for i in range(n_chunks):
        for si, ((_, _, _, o_ref, _, order), (qb, _, _, _, bl)) in enumerate(zip(dirs, ops)):
            c = order[i]
            o_ref[c * GLA_CHUNK:(c + 1) * GLA_CHUNK, :] = (
                intra[si][c] + _dot_nt(chunk(qb, c), states[si].astype(BF16)))
            states[si] = states[si] * jnp.exp(bl[c]) + inc[si][c]
    for d, st in zip(dirs, states):
        d[4][...] = st


def _gla_masks():
    r = np.arange(GLA_CHUNK)
    lower = (r[None, :] <= r[:, None]).astype(np.float32)
    tri1 = np.stack([lower, lower.T])
    n_chunks = TOK_BLOCK // GLA_CHUNK
    tri = np.stack([np.kron(np.eye(n_chunks, dtype=np.float32), t) for t in tri1])
    tril = np.tile(tri1, (1, 1, GLA_HEADS))
    klane = np.arange(GLA_KPAD)
    vlane = np.arange(GLA_WIDTH)
    hm = np.stack([np.broadcast_to((klane // GLA_DK) == h, (GLA_CHUNK, GLA_KPAD)) for h in range(GLA_HEADS)])
    vm = np.stack([np.broadcast_to((vlane // GLA_DV) == h, (GLA_CHUNK, GLA_WIDTH)) for h in range(GLA_HEADS)])
    bd = ((vlane[:, None] // GLA_DV) == (klane[None, :] // GLA_DK)).astype(np.float32)
    return (jnp.asarray(tri, BF16), jnp.asarray(tril, F32), jnp.asarray(hm, BF16), jnp.asarray(vm, BF16),
            jnp.asarray(bd))


def _scan_block(d, j, n_ctx, n_all):
    bwd = jnp.where(j < n_ctx, n_ctx - 1 - j, n_all + n_ctx - 1 - j)
    return jnp.where(d == 0, j, bwd)


def _gla(gqk, gv, gla):
    bsz, tt, _ = gv.shape
    nb = tt // TOK_BLOCK
    n_ctx = CTX_LEN // TOK_BLOCK
    tri, tril, hm, vm, bd = _gla_masks()
    const2 = lambda b, j: (0, 0)
    const3 = lambda b, j: (0, 0, 0)
    fwd = lambda b, j: (b, j, 0)
    bwd = lambda b, j: (b, _scan_block(1, j, n_ctx, nb), 0)
    bwd_gate = lambda b, j: (b, _scan_block(1, j, n_ctx, nb), 1)
    tok = lambda w, imap: pl.BlockSpec((GLA_ROWS, TOK_BLOCK, w), imap)
    return pl.pallas_call(
        _gla_kernel,
        grid=(bsz // GLA_ROWS, nb),
        in_specs=[
            tok(2 * GLA_KPAD, fwd),
            tok(GLA_WIDTH, fwd),
            tok(GLA_KPAD, fwd),
            tok(2 * GLA_KPAD, bwd),
            tok(GLA_WIDTH, bwd),
            tok(GLA_KPAD, bwd_gate),
            pl.BlockSpec((2, TOK_BLOCK, TOK_BLOCK), const3),
            pl.BlockSpec((2, GLA_CHUNK, GLA_HEADS * GLA_CHUNK), const3),
            pl.BlockSpec((GLA_HEADS, GLA_CHUNK, GLA_KPAD), const3),
            pl.BlockSpec((GLA_HEADS, GLA_CHUNK, GLA_WIDTH), const3),
            pl.BlockSpec((GLA_WIDTH, GLA_KPAD), const2),
        ],
        out_specs=[tok(GLA_WIDTH, fwd), tok(GLA_WIDTH, bwd)],
        out_shape=[jax.ShapeDtypeStruct((bsz, tt, GLA_WIDTH), F32)] * 2,
        scratch_shapes=[pltpu.VMEM((GLA_ROWS, GLA_WIDTH, GLA_KPAD), F32)] * 2,
        compiler_params=pltpu.CompilerParams(
            dimension_semantics=("arbitrary", "arbitrary"), vmem_limit_bytes=VMEM_LIMIT),
        name="gla",
    )(gqk, gv, gla, gqk, gv, gla, tri, tril, hm, vm, bd)


def _swa_kernel(sink_ref, mask_ref, att_ref, o_ref, *, tt):
    j = pl.program_id(1)
    n_sub = TOK_BLOCK // ATT_BLOCK
    n_ctx_blocks = CTX_LEN // TOK_BLOCK
    last_q = tt // ATT_BLOCK - 1
    lane = lax.broadcasted_iota(jnp.int32, (ATT_BLOCK, 2 * ATT_HD), 1)
    left = lane < ATT_HD
    row2 = lax.broadcasted_iota(jnp.int32, (2 * ATT_BLOCK, 1), 0)
    units = [(r, sub, p) for r in range(SWA_ROWS) for sub in range(n_sub) for p in range(ATT_HEADS // 2)]

    def run(windowed):
        def score(r, sub, p):
            q0 = pl.multiple_of(j * TOK_BLOCK + sub * ATT_BLOCK, ATT_BLOCK)
            qp = att_ref[r, pl.ds(q0, ATT_BLOCK), p * 128:(p + 1) * 128]
            zero = jnp.zeros_like(qp)
            q2 = jnp.concatenate([jnp.where(left, qp, zero), jnp.where(left, zero, qp)], axis=0)
            kcol = ATT_WIDTH + p * 128
            s_c = _dot_nt(q2, att_ref[r, 0:CTX_LEN, kcol:kcol + 128])
            if not windowed:
                return s_c, None, None
            start = pl.multiple_of(jnp.minimum(q0 - ATT_BLOCK, tt - 3 * ATT_BLOCK), ATT_BLOCK)
            s_w = jnp.where(keeps[sub], _dot_nt(q2, att_ref[r, pl.ds(start, 3 * ATT_BLOCK), kcol:kcol + 128]),
                            NEG_INF)
            return s_c, s_w, start

        def softmax(p, s_c, s_w, start):
            sink_col = jnp.where(row2 < ATT_BLOCK, sink_ref[2 * p], sink_ref[2 * p + 1]) * LOG2E
            m = jnp.maximum(jnp.max(s_c, axis=-1, keepdims=True), sink_col)
            if windowed:
                m = jnp.maximum(m, jnp.max(s_w, axis=-1, keepdims=True))
            e_c = jnp.exp2(s_c - m)
            den = jnp.sum(e_c, axis=-1, keepdims=True) + jnp.exp2(sink_col - m)
            e_w = None
            if windowed:
                e_w = jnp.exp2(s_w - m)
                den = den + jnp.sum(e_w, axis=-1, keepdims=True)
                e_w = e_w.astype(BF16)
            return e_c.astype(BF16), e_w, den, start

        def values(r, sub, p, e_c, e_w, den, start):
            vcol = 2 * ATT_WIDTH + p * 128
            pv = _dot(e_c, att_ref[r, 0:CTX_LEN, vcol:vcol + 128])
            if windowed:
                pv = pv + _dot(e_w, att_ref[r, pl.ds(start, 3 * ATT_BLOCK), vcol:vcol + 128])
            o2 = pv / den
            o = jnp.where(left, o2[0:ATT_BLOCK], o2[ATT_BLOCK:2 * ATT_BLOCK])
            o_ref[r, sub * ATT_BLOCK:(sub + 1) * ATT_BLOCK, p * 128:(p + 1) * 128] = o.astype(BF16)

        keeps = []
        if windowed:
            for sub in range(n_sub):
                qi = j * n_sub + sub
                kind = jnp.where(qi == CTX_LEN // ATT_BLOCK, 0, jnp.where(qi == last_q, 2, 1))
                keeps.append(mask_ref[kind] > 0)

        n = len(units)
        sc, pr = {}, {}
        for t in range(n + 2):
            if t < n:
                sc[t] = score(*units[t])
            if 0 <= t - 1 < n:
                pr[t - 1] = softmax(units[t - 1][2], *sc.pop(t - 1))
            if 0 <= t - 2 < n:
                values(*units[t - 2], *pr.pop(t - 2))

    @pl.when(j < n_ctx_blocks)
    def _():
        run(False)

    @pl.when(j >= n_ctx_blocks)
    def _():
        run(True)


def _swa_masks():
    r = np.arange(2 * ATT_BLOCK)[:, None] % ATT_BLOCK
    c = np.arange(3 * ATT_BLOCK)[None, :]
    near = lambda delta: np.abs(r - c + delta) <= WINDOW
    first = near(ATT_BLOCK) & (c >= ATT_BLOCK)
    return jnp.asarray(np.stack([first, near(ATT_BLOCK), near(2 * ATT_BLOCK)]), F32)


def _swa(att, sink):
    bsz, tt, _ = att.shape
    nb = tt // TOK_BLOCK
    return pl.pallas_call(
        functools.partial(_swa_kernel, tt=tt),
        grid=(bsz // SWA_ROWS, nb),
        in_specs=[
            pl.BlockSpec(memory_space=pltpu.SMEM),
            pl.BlockSpec((3, 2 * ATT_BLOCK, 3 * ATT_BLOCK), lambda b, j: (0, 0, 0)),
            pl.BlockSpec((SWA_ROWS, tt, 3 * ATT_WIDTH), lambda b, j: (b, 0, 0)),
        ],
        out_specs=pl.BlockSpec((SWA_ROWS, TOK_BLOCK, ATT_WIDTH), lambda b, j: (b, j, 0)),
        out_shape=jax.ShapeDtypeStruct((bsz, tt, ATT_WIDTH), BF16),
        compiler_params=pltpu.CompilerParams(
            dimension_semantics=("arbitrary", "arbitrary"), vmem_limit_bytes=VMEM_LIMIT),
        name="swa",
    )(sink, _swa_masks(), att)


def _s5_prep_kernel(lr_ref, li_ref, ls_ref, br_ref, bi_ref, cr_ref, ci_ref,
                    a16_ref, mre_ref, mim_ref, rre_ref, rim_ref, k_ref):
    rows = S5_LC * S5_GROUP
    rowg = lax.broadcasted_iota(jnp.int32, (rows, S5_NS), 0) >> 4
    colg = lax.broadcasted_iota(jnp.int32, (rows, S5_NS), 1) >> 6
    same_group = rowg == colg
    taps_t = []
    for d in range(2):
        lr = jnp.minimum(lr_ref[d], -1e-4)
        li = li_ref[d]
        dt = jnp.exp(ls_ref[d])
        mag = jnp.exp(lr * dt)
        ar = mag * jnp.cos(li * dt)
        ai = mag * jnp.sin(li * dt)
        den = lr * lr + li * li
        fr = ((ar - 1.0) * lr + ai * li) / den
        fi = (ai * lr - (ar - 1.0) * li) / den
        br = br_ref[d]
        bi = bi_ref[d]
        bbr = fr * br - fi * bi
        bbi = fr * bi + fi * br
        cr = cr_ref[d]
        ci = ci_ref[d]
        pr = jnp.ones_like(ar)
        pi = jnp.zeros_like(ar)
        cp_re, cp_im = [], []
        for tau in range(S5_LC + 1):
            cpr = cr * pr - ci * pi
            cpi = cr * pi + ci * pr
            if tau < S5_LC:
                mre_ref[d, tau] = pr * bbr - pi * bbi
                mim_ref[d, tau] = pr * bbi + pi * bbr
                cp_re.append(cpr)
                cp_im.append(cpi)
            if tau >= 1:
                rre_ref[d, tau - 1] = cpr
                rim_ref[d, tau - 1] = -cpi
            if tau == S5_LC:
                a16_ref[d, 0:1, :] = pr
                a16_ref[d, 1:2, :] = pi
            pr, pi = pr * ar - pi * ai, pr * ai + pi * ar
        bd_re = jnp.where(same_group, jnp.concatenate([bbr] * S5_GROUPS, axis=0), 0.0)
        bd_im = jnp.where(same_group, jnp.concatenate([bbi] * S5_GROUPS, axis=0), 0.0)
        if d == 1:
            cp_re, cp_im = cp_re[::-1], cp_im[::-1]
        hp = lax.Precision.HIGHEST
        dims = (((1,), (1,)), ((), ()))
        taps_t.append(
            lax.dot_general(bd_re, jnp.concatenate(cp_re, axis=0), dims, precision=hp, preferred_element_type=F32)
            - lax.dot_general(bd_im, jnp.concatenate(cp_im, axis=0), dims, precision=hp,
                              preferred_element_type=F32))
    kf, kb_rev = taps_t
    tap0 = lax.broadcasted_iota(jnp.int32, kf.shape, 1) < S5_GROUP
    k_ref[0] = jnp.where(tap0, kf + pltpu.roll(kb_rev, S5_GROUP, 1), kf)
    k_ref[1] = kb_rev


def _s5_tables(fwd, bwd):
    n_layers = fwd[0].shape[0]
    both = lambda i: jnp.stack([fwd[i], bwd[i]], axis=1)
    lam_re = both(0).reshape(n_layers, 2, 1, S5_NS)
    lam_im = both(1).reshape(n_layers, 2, 1, S5_NS)
    log_step = jnp.repeat(both(2), S5_STATE, axis=-1).reshape(n_layers, 2, 1, S5_NS)
    b_hn = lambda t: t.transpose(0, 1, 4, 2, 3).reshape(n_layers, 2, S5_GROUP, S5_NS)
    c_hn = lambda t: t.transpose(0, 1, 3, 2, 4).reshape(n_layers, 2, S5_GROUP, S5_NS)
    vec = pl.BlockSpec((None, 2, 1, S5_NS), lambda l: (l, 0, 0, 0))
    mat = pl.BlockSpec((None, 2, S5_GROUP, S5_NS), lambda l: (l, 0, 0, 0))
    tab = pl.BlockSpec((None, 2, S5_LC, S5_GROUP, S5_NS), lambda l: (l, 0, 0, 0, 0))
    taps = S5_LC * S5_GROUP
    a16, mre, mim, rre, rim, k = pl.pallas_call(
        _s5_prep_kernel,
        grid=(n_layers,),
        in_specs=[vec, vec, vec, mat, mat, mat, mat],
        out_specs=[
            pl.BlockSpec((None, 2, 2, S5_NS), lambda l: (l, 0, 0, 0)),
            tab, tab, tab, tab,
            pl.BlockSpec((None, 2, S5_WIDTH, taps), lambda l: (l, 0, 0, 0)),
        ],
        out_shape=[
            jax.ShapeDtypeStruct((n_layers, 2, 2, S5_NS), F32),
            jax.ShapeDtypeStruct((n_layers, 2, S5_LC, S5_GROUP, S5_NS), F32),
            jax.ShapeDtypeStruct((n_layers, 2, S5_LC, S5_GROUP, S5_NS), F32),
            jax.ShapeDtypeStruct((n_layers, 2, S5_LC, S5_GROUP, S5_NS), F32),
            jax.ShapeDtypeStruct((n_layers, 2, S5_LC, S5_GROUP, S5_NS), F32),
            jax.ShapeDtypeStruct((n_layers, 2, S5_WIDTH, taps), F32),
        ],
        compiler_params=pltpu.CompilerParams(dimension_semantics=("arbitrary",), vmem_limit_bytes=VMEM_LIMIT),
        name="s5_prep",
    )(lam_re, lam_im, log_step, b_hn(both(3)), b_hn(both(4)), c_hn(both(5)), c_hn(both(6)))

    n_pairs = S5_GROUPS // 2
    lane_group = np.arange(128) // S5_STATE

    def pair_rows(t, flip):
        t = t[:, ::-1] if flip else t
        t = t.reshape(n_layers, S5_LC, S5_GROUP, n_pairs, 128).transpose(0, 3, 1, 2, 4)
        own = jnp.asarray(lane_group[None, :] == np.arange(2)[:, None])
        t = jnp.where(own[None, None, :, None, None, :], t[:, :, None], 0.0)
        return t.reshape(n_layers, n_pairs, 2 * taps, 128)

    m_f = jnp.concatenate([pair_rows(mre[:, 0], True), pair_rows(mim[:, 0], True)], axis=-1).astype(BF16)
    m_b = jnp.concatenate([pair_rows(mre[:, 1], False), pair_rows(mim[:, 1], False)], axis=-1).astype(BF16)

    r_t = jnp.concatenate([pair_rows(rre[:, 0], False), pair_rows(rim[:, 0], False),
                           pair_rows(rre[:, 1], True), pair_rows(rim[:, 1], True)], axis=-1).astype(BF16)

    zeros = jnp.zeros((n_layers, S5_WIDTH, taps), F32)
    f2 = jnp.concatenate([zeros, k[:, 0]], axis=-1)
    b2 = jnp.concatenate([k[:, 1], zeros], axis=-1)
    t_f = jnp.stack([f2[..., taps - S5_GROUP * st:2 * taps - S5_GROUP * st] for st in range(S5_LC)], axis=2)
    t_b = jnp.stack([b2[..., S5_GROUP * (S5_LC - 1 - st):S5_GROUP * (S5_LC - 1 - st) + taps]
                     for st in range(S5_LC)], axis=2)
    later = jnp.asarray(np.arange(taps)[None, :] // S5_GROUP >= np.arange(S5_LC)[:, None])
    toe = jnp.where(later[None, None], t_f, t_b)
    toe = toe.reshape(n_layers, S5_GROUPS, S5_GROUP, S5_LC, taps).transpose(0, 1, 3, 2, 4)
    toe = toe.reshape(n_layers, S5_GROUPS, taps, taps).astype(BF16)
    return a16, m_f, m_b, r_t, toe


def _lane_block_transpose(cols):
    lane = lax.broadcasted_iota(jnp.int32, cols[0].shape, 1)
    out = [None] * 32
    for ah in range(2):
        for bh in range(2):
            v = [cols[(ah * 8 + al) * 2 + bh] for al in range(8)]
            for kbit in range(3):
                width = 16 << kbit
                low = ((lane >> (4 + kbit)) & 1) == 0
                nxt = list(v)
                for i in range(8):
                    if i & (1 << kbit):
                        continue
                    lo_v, hi_v = v[i], v[i | (1 << kbit)]
                    nxt[i] = jnp.where(low, lo_v, pltpu.roll(hi_v, width, 1))
                    nxt[i | (1 << kbit)] = jnp.where(low, pltpu.roll(lo_v, 128 - width, 1), hi_v)
                v = nxt
            for bl in range(8):
                out[(bh * 8 + bl) * 2 + ah] = v[bl]
    return out


def _s5_increments(ush, m_ref, d_re, d_im):
    for p in range(S5_GROUPS // 2):
        dp = _dot(ush[:, p * 512:(p + 1) * 512], m_ref[p])
        d_re[p] = dp[:, 0:128]
        d_im[p] = dp[:, 128:256]


def _s5_recurrence(a16_ref, d_re, d_im, x_re, x_im, sr_ref, si_ref, order, bsz):
    ar = a16_ref[0:1, :]
    ai = a16_ref[1:2, :]
    sr = sr_ref[...]
    si = si_ref[...]
    n_slabs = S5_NS // 128
    gather = lambda ref, rows: jnp.concatenate([ref[p, rows, :] for p in range(n_slabs)], axis=1)
    for c in order:
        rows = pl.ds(c, bsz, stride=CH_BLOCK)
        for p in range(n_slabs):
            x_re[p, rows, :] = sr[:, p * 128:(p + 1) * 128]
            x_im[p, rows, :] = si[:, p * 128:(p + 1) * 128]
        sr, si = ar * sr - ai * si + gather(d_re, rows), ar * si + ai * sr + gather(d_im, rows)
    sr_ref[...] = sr
    si_ref[...] = si


def _s5_fwd_kernel(uc_ref, a16_ref, m_ref, ush_ref, xin_ref, d_re, d_im, x_re, x_im, sr_ref, si_ref, *, bsz):
    @pl.when(pl.program_id(0) == 0)
    def _():
        sr_ref[...] = jnp.zeros_like(sr_ref)
        si_ref[...] = jnp.zeros_like(si_ref)

    rows = bsz * CH_BLOCK
    ub = uc_ref[...].reshape(rows, S5_CW).astype(BF16)
    packed = pltpu.bitcast(ub, jnp.uint32)
    cols = _lane_block_transpose([packed[:, v * 128:(v + 1) * 128] for v in range(32)])
    ush = pltpu.bitcast(jnp.concatenate(cols, axis=1), BF16)
    ush_ref[...] = ush.reshape(bsz, CH_BLOCK, S5_CW)
    _s5_increments(ush, m_ref, d_re, d_im)
    _s5_recurrence(a16_ref, d_re, d_im, x_re, x_im, sr_ref, si_ref, range(CH_BLOCK), bsz)
    for p in range(S5_NS // 128):
        xin_ref[:, :, p * 128:(p + 1) * 128] = x_re[p].astype(BF16).reshape(bsz, CH_BLOCK, 128)
        xin_ref[:, :, S5_NS + p * 128:S5_NS + (p + 1) * 128] = x_im[p].astype(BF16).reshape(bsz, CH_BLOCK, 128)


def _s5_bwd_kernel(ush_ref, xf_ref, a16_ref, m_ref, toe_ref, r_ref, y_ref, d_re, d_im, x_re, x_im, sr_ref, si_ref,
                   *, bsz):
    @pl.when(pl.program_id(0) == 0)
    def _():
        sr_ref[...] = jnp.zeros_like(sr_ref)
        si_ref[...] = jnp.zeros_like(si_ref)

    rows = bsz * CH_BLOCK
    ush = ush_ref[...].reshape(rows, S5_CW)
    _s5_increments(ush, m_ref, d_re, d_im)
    _s5_recurrence(a16_ref, d_re, d_im, x_re, x_im, sr_ref, si_ref, reversed(range(CH_BLOCK)), bsz)
    xf = xf_ref[...].reshape(rows, 2 * S5_NS)
    ycols = []
    for p in range(S5_GROUPS // 2):
        lanes = slice(p * 128, (p + 1) * 128)
        xcat = jnp.concatenate([xf[:, lanes], xf[:, S5_NS + p * 128:S5_NS + (p + 1) * 128],
                                x_re[p].astype(BF16), x_im[p].astype(BF16)], axis=1)
        carry = _dot_nt(xcat, r_ref[p])
        for g2 in range(2):
            g = 2 * p + g2
            yg = carry[:, g2 * 256:(g2 + 1) * 256] + _dot(ush[:, g * 256:(g + 1) * 256], toe_ref[g])
            ycols += [yg[:, 0:128], yg[:, 128:256]]
    ycols = _lane_block_transpose(ycols)
    y_ref[...] = jnp.concatenate(ycols, axis=1).reshape(bsz, CH_BLOCK, S5_CW)


def _s5(u_c, tables, layer):
    a16, m_f, m_b, r_t, toe = tables
    bsz, n_rows, _ = u_c.shape
    nb = n_rows // CH_BLOCK
    n_ctx = CTX_LEN // TOK_BLOCK
    rows = bsz * CH_BLOCK
    blk3 = lambda w: (bsz, CH_BLOCK, w)
    scratch = [pltpu.VMEM((S5_NS // 128, rows, 128), F32)] * 4 + [pltpu.VMEM((bsz, S5_NS), F32)] * 2
    params = pltpu.CompilerParams(dimension_semantics=("arbitrary",), vmem_limit_bytes=VMEM_LIMIT)
    ush, xin_f = pl.pallas_call(
        functools.partial(_s5_fwd_kernel, bsz=bsz),
        grid=(nb,),
        in_specs=[
            pl.BlockSpec(blk3(S5_CW), lambda j: (0, j, 0)),
            pl.BlockSpec((None, None, 2, S5_NS), lambda j: (layer, 0, 0, 0)),
            pl.BlockSpec((None, S5_GROUPS // 2, 512, 256), lambda j: (layer, 0, 0, 0)),
        ],
        out_specs=[
            pl.BlockSpec(blk3(S5_CW), lambda j: (0, j, 0)),
            pl.BlockSpec(blk3(2 * S5_NS), lambda j: (0, j, 0)),
        ],
        out_shape=[
            jax.ShapeDtypeStruct((bsz, n_rows, S5_CW), BF16),
            jax.ShapeDtypeStruct((bsz, n_rows, 2 * S5_NS), BF16),
        ],
        scratch_shapes=scratch,
        compiler_params=params,
        name="s5_fwd",
    )(u_c, a16, m_f)
    back = lambda j: (0, _scan_block(1, j, n_ctx, nb), 0)
    return pl.pallas_call(
        functools.partial(_s5_bwd_kernel, bsz=bsz),
        grid=(nb,),
        in_specs=[
            pl.BlockSpec(blk3(S5_CW), back),
            pl.BlockSpec(blk3(2 * S5_NS), back),
            pl.BlockSpec((None, None, 2, S5_NS), lambda j: (layer, 1, 0, 0)),
            pl.BlockSpec((None, S5_GROUPS // 2, 512, 256), lambda j: (layer, 0, 0, 0)),
            pl.BlockSpec((None, S5_GROUPS, 256, 256), lambda j: (layer, 0, 0, 0)),
            pl.BlockSpec((None, S5_GROUPS // 2, 512, 512), lambda j: (layer, 0, 0, 0)),
        ],
        out_specs=pl.BlockSpec(blk3(S5_CW), back),
        out_shape=jax.ShapeDtypeStruct((bsz, n_rows, S5_CW), F32),
        scratch_shapes=scratch,
        compiler_params=params,
        name="s5_bwd",
    )(ush, xin_f, a16, m_b, toe, r_t)


def _out_kernel(xc_ref, x_ref, mod_ref, gof_ref, gob_ref, g_ref, att_ref, y_ref, uc_ref, gnw_ref, hm_ref, d_ref,
                gluw_ref, glub_ref, wo_ref, n2_ref, w1_ref, w2_ref, fn_ref, o_ref, ys_ref,
                *, first_block, final, split):
    j = pl.program_id(1) + first_block

    def mixed(r):
        m = mod_ref[r]
        mrow = jnp.where(j > 0, m[1:2, :], m[0:1, :])
        g1 = mrow[:, 2 * D_MODEL:3 * D_MODEL]
        sh2 = mrow[:, 3 * D_MODEL:4 * D_MODEL]
        sc2 = mrow[:, 4 * D_MODEL:5 * D_MODEL]
        g2 = mrow[:, 5 * D_MODEL:6 * D_MODEL]

        o = gof_ref[r] + gob_ref[r]
        ms = _dot((o * o).astype(BF16), hm_ref[...]) * (1.0 / GLA_DV)
        a = o * lax.rsqrt(ms + EPS) * gnw_ref[...] * _silu(g_ref[r])

        yc = y_ref[r] + d_ref[...] * uc_ref[r]
        for st in range(S5_LC):
            for half in range(S5_WIDTH // 128):
                lane0 = st * S5_WIDTH + half * 128
                ys_ref[r, half, pl.ds(st, CH_BLOCK, stride=S5_LC), :] = yc[:, lane0:lane0 + 128]
        yy = jnp.concatenate([ys_ref[r, half] for half in range(S5_WIDTH // 128)], axis=1)
        ge = 0.5 * yy * (1.0 + jnp.tanh(math.sqrt(2.0 / math.pi) * (yy + 0.044715 * (yy * yy * yy))))
        z = _dot(ge.astype(BF16), gluw_ref[...]) + glub_ref[...]
        s = z[:, 0:S5_WIDTH] * _sigmoid(z[:, S5_WIDTH:2 * S5_WIDTH])

        proj = _dot(jnp.concatenate([a.astype(BF16), att_ref[r], s.astype(BF16)], axis=1), wo_ref[...])
        x_in = jnp.where(j == 0, xc_ref[r], x_ref[r]) if split else x_ref[r]
        x1 = x_in + g1 * proj
        y2 = x1 * lax.rsqrt(jnp.mean(x1 * x1, axis=-1, keepdims=True) + EPS) * n2_ref[...]
        return x1, (y2 * (1.0 + sc2) + sh2).astype(BF16), g2

    def finish(r, x1, g2, mlp):
        x2 = x1 + g2 * mlp
        if final:
            x2 = x2 * lax.rsqrt(jnp.mean(x2 * x2, axis=-1, keepdims=True) + EPS) * fn_ref[...]
        o_ref[r] = x2

    cur = mixed(0)
    for r in range(OUT_ROWS):
        x1, h2, g2 = cur
        hid = jnp.maximum(_dot(h2, w1_ref[...]), 0.0)
        if r + 1 < OUT_ROWS:
            cur = mixed(r + 1)
        mlp = _dot((hid * hid).astype(BF16), w2_ref[...])
        finish(r, x1, g2, mlp)


def _outproj(x_ctx, x_rest, lat_skip, modsel, gla_f, gla_b, gg, att_o, y_c, u_c, gnw, hm, s5d, gluw, glub, wo, n2w,
             w1, w2, fnw, layer, final):
    bsz, tt, _ = gg.shape
    nb = tt // TOK_BLOCK
    first = CTX_LEN // TOK_BLOCK if final else 0
    nsteps = nb - first
    const2 = lambda b, j: (0, 0)
    lyr3 = lambda b, j: (layer, 0, 0)
    single = pl.Buffered(1)
    return pl.pallas_call(
        functools.partial(_out_kernel, first_block=first, final=final, split=lat_skip > 0),
        grid=(bsz // OUT_ROWS, nsteps),
        in_specs=_stream_specs(lat_skip, first, rows=OUT_ROWS) + [
            pl.BlockSpec((OUT_ROWS, 2, 6 * D_MODEL), lambda b, j: (b, 0, 0)),
            pl.BlockSpec((OUT_ROWS, TOK_BLOCK, GLA_WIDTH), lambda b, j: (b, j + first, 0)),
            pl.BlockSpec((OUT_ROWS, TOK_BLOCK, GLA_WIDTH), lambda b, j: (b, j + first, 0)),
            pl.BlockSpec((OUT_ROWS, TOK_BLOCK, GLA_WIDTH), lambda b, j: (b, j + first, 0)),
            pl.BlockSpec((OUT_ROWS, TOK_BLOCK, ATT_WIDTH), lambda b, j: (b, j + first, 0)),
            pl.BlockSpec((OUT_ROWS, CH_BLOCK, S5_CW), lambda b, j: (b, j + first, 0)),
            pl.BlockSpec((OUT_ROWS, CH_BLOCK, S5_CW), lambda b, j: (b, j + first, 0)),
            pl.BlockSpec((1, GLA_WIDTH), const2),
            pl.BlockSpec((GLA_WIDTH, GLA_WIDTH), const2),
            pl.BlockSpec((1, S5_CW), const2),
            pl.BlockSpec((None, S5_WIDTH, 2 * S5_WIDTH), lyr3, pipeline_mode=single),
            pl.BlockSpec((1, 2 * S5_WIDTH), const2),
            pl.BlockSpec((None, D_MODEL, D_MODEL), lyr3, pipeline_mode=single),
            pl.BlockSpec((1, D_MODEL), const2),
            pl.BlockSpec((None, D_MODEL, D_FF), lyr3, pipeline_mode=single),
            pl.BlockSpec((None, D_FF, D_MODEL), lyr3, pipeline_mode=single),
            pl.BlockSpec((1, D_MODEL), const2),
        ],
        out_specs=pl.BlockSpec((OUT_ROWS, TOK_BLOCK, D_MODEL), lambda b, j: (b, j, 0)),
        out_shape=jax.ShapeDtypeStruct((bsz, nsteps * TOK_BLOCK, D_MODEL), F32),
        scratch_shapes=[pltpu.VMEM((OUT_ROWS, S5_WIDTH // 128, TOK_BLOCK, 128), F32)],
        compiler_params=pltpu.CompilerParams(
            dimension_semantics=("arbitrary", "arbitrary"), vmem_limit_bytes=VMEM_LIMIT),
        name="outproj_mlp",
    )(x_ctx, x_rest, modsel, gla_f, gla_b, gg, att_o, y_c, u_c, gnw, hm, s5d, gluw, glub, wo, n2w, w1, w2, fnw)


def _rope_tables(tt):
    n_lat = tt - CTX_LEN
    rows = n_lat // GRID_W
    row = jnp.repeat(jnp.arange(rows, dtype=F32), GRID_W)
    col = jnp.tile(jnp.arange(GRID_W, dtype=F32), rows)
    n_freq = ATT_HD // 4
    inv_freq = ROPE_BASE ** (-jnp.arange(n_freq, dtype=F32) / n_freq)
    ang_r = row[:, None] * inv_freq
    ang_c = col[:, None] * inv_freq
    ang = jnp.concatenate([ang_r, ang_r, ang_c, ang_c], axis=-1)
    cos = jnp.concatenate([jnp.ones((CTX_LEN, ATT_HD), F32), jnp.cos(ang)], axis=0)
    sin = jnp.concatenate([jnp.zeros((CTX_LEN, ATT_HD), F32), jnp.sin(ang)], axis=0)
    up_quarter = (np.arange(ATT_HD) // 16) % 2 == 0
    sa = jnp.where(up_quarter, -sin, 0.0)
    sb = jnp.where(up_quarter, 0.0, sin)
    two = lambda t: jnp.concatenate([t, t], axis=-1)
    return two(cos), two(sa), two(sb)


def _pad_cols(w, width):
    return jnp.pad(w, ((0, 0), (0, 0), (0, width - w.shape[-1])))


def _layout_w_in(w_in):
    offs = np.cumsum([0, GLA_KW, GLA_KW, GLA_WIDTH, GLA_WIDTH, GLA_RANK, GLA_RANK, ATT_WIDTH, ATT_KVW, ATT_KVW,
                      S5_WIDTH])
    q, k, v, g, zf, zb, aq, ak, av, u = [w_in[:, :, offs[i]:offs[i + 1]] for i in range(10)]
    z = jnp.concatenate([zf, zb], axis=-1)
    cols = [v, g, _pad_cols(q, GLA_KPAD), _pad_cols(k, GLA_KPAD), _pad_cols(z, 128), aq, ak, av, u]
    return jnp.concatenate(cols, axis=-1).astype(BF16)


def _block_diag(t):
    s, g, a, b = t.shape
    eye = jnp.eye(g, dtype=t.dtype)
    return jnp.einsum('sgab,gk->sgakb', t, eye).reshape(s, g * a, g * b)


def kernel(x, c, ctx, c_ctx, w_mod, b_mod, norm1_w, norm2_w, w_in, gla_wa_f, gla_ba_f, gla_wa_b, gla_ba_b,
           gla_norm_w, attn_sink, s5_lam_re_f, s5_lam_im_f, s5_log_step_f, s5_b_re_f, s5_b_im_f, s5_c_re_f,
           s5_c_im_f, s5_lam_re_b, s5_lam_im_b, s5_log_step_b, s5_b_re_b, s5_b_im_b, s5_c_re_b, s5_c_im_b,
           s5_d, glu_w, glu_b, w_out, mlp_w1, mlp_w2, final_norm_w):
    bsz, seq, _ = x.shape
    n_layers = w_mod.shape[0]
    tt = CTX_LEN + seq
    assert all(bsz % rows == 0 for rows in (8, IN_ROWS, GLA_ROWS, SWA_ROWS, OUT_ROWS))
    assert seq % TOK_BLOCK == 0 and seq >= 2 * TOK_BLOCK and ctx.shape[1] == CTX_LEN

    mod_rows = -(-(bsz + 1) // 8) * 8
    cvec = jnp.zeros((mod_rows, D_MODEL), F32).at[:bsz].set(c).at[bsz].set(c_ctx)
    mod = _modulation(cvec, w_mod, b_mod)
    mod_ctx = jnp.broadcast_to(mod[:, bsz][:, None], (n_layers, bsz, 6 * D_MODEL))
    modsel = jnp.stack([mod_ctx, mod[:, :bsz]], axis=2)

    w_in_p = _layout_w_in(w_in)
    wa_cat = jnp.zeros((n_layers, 128, 2 * GLA_KPAD), F32)
    wa_cat = wa_cat.at[:, 0:GLA_RANK, 0:GLA_KW].set(gla_wa_f)
    wa_cat = wa_cat.at[:, GLA_RANK:2 * GLA_RANK, GLA_KPAD:GLA_KPAD + GLA_KW].set(gla_wa_b).astype(BF16)
    ba_cat = jnp.zeros((n_layers, 1, 2 * GLA_KPAD), F32)
    ba_cat = ba_cat.at[:, 0, 0:GLA_KW].set(gla_ba_f).at[:, 0, GLA_KPAD:GLA_KPAD + GLA_KW].set(gla_ba_b)
    qscale = jnp.ones((1, 2 * GLA_KPAD), F32).at[:, 0:GLA_KW].set(GLA_DK ** -0.5)
    cos_t, sa_t, sb_t = _rope_tables(tt)
    gnw = jnp.tile(gla_norm_w, (1, GLA_HEADS))[:, None, :]
    head = np.arange(GLA_WIDTH) // GLA_DV
    hm = jnp.asarray(head[:, None] == head[None, :], BF16)
    wo_b = w_out.astype(BF16)
    w1_b = mlp_w1.astype(BF16)
    w2_b = mlp_w2.astype(BF16)
    gluw_b = glu_w.astype(BF16)

    s5_tab = _s5_tables(
        (s5_lam_re_f, s5_lam_im_f, s5_log_step_f, s5_b_re_f, s5_b_im_f, s5_c_re_f, s5_c_im_f),
        (s5_lam_re_b, s5_lam_im_b, s5_log_step_b, s5_b_re_b, s5_b_im_b, s5_c_re_b, s5_c_im_b))
    s5_dt = jnp.tile(s5_d, (1, S5_LC))[:, None, :]

    stream = (ctx, x, CTX_LEN // TOK_BLOCK)
    for l in range(n_layers):
        final = l == n_layers - 1
        gv, gg, gqk, gla, att, u_c = _inproj(*stream, modsel[l], norm1_w[l][None], w_in_p, wa_cat, ba_cat[l],
                                             qscale, cos_t, sa_t, sb_t, l)
        gla_f, gla_b = _gla(gqk, gv, gla)
        att_o = _swa(att, attn_sink[l])
        y_c = _s5(u_c, s5_tab, l)
        xs = _outproj(*stream, modsel[l], gla_f, gla_b, gg, att_o, y_c, u_c,
                      gnw[l], hm, s5_dt[l], gluw_b, glu_b[l][None], wo_b, norm2_w[l][None], w1_b, w2_b,
                      final_norm_w[None], l, final)
        stream = (xs, xs, 0)
    return xs
```

```python
import functools
import math

import jax
import jax.numpy as jnp
import numpy as np
from jax import lax
from jax.experimental import pallas as pl
from jax.experimental.pallas import tpu as pltpu

F32 = jnp.float32
BF16 = jnp.bfloat16

D_MODEL = 1024
D_FF = 4 * D_MODEL
CTX_LEN = 256
GRID_W = 64
EPS = 1e-6
NEG_INF = -1e30
LOG2E = math.log2(math.e)

GLA_HEADS = 4
GLA_DV = 96
GLA_DK = 48
GLA_WIDTH = GLA_HEADS * GLA_DV
GLA_KW = GLA_HEADS * GLA_DK
GLA_KPAD = 256
GLA_RANK = 16
GLA_TAU = 16.0
GLA_CHUNK = 64

ATT_HD = 64
ATT_HEADS = 6
ATT_KV_HEADS = 2
ATT_WIDTH = ATT_HEADS * ATT_HD
ATT_KVW = ATT_KV_HEADS * ATT_HD
WINDOW = 128
ATT_BLOCK = 128
ROPE_BASE = 10000.0

S5_WIDTH = 256
S5_GROUP = 16
S5_GROUPS = 16
S5_STATE = 64
S5_NS = S5_GROUPS * S5_STATE
S5_LC = 16
S5_CW = S5_LC * S5_WIDTH

TOK_BLOCK = 256
CH_BLOCK = TOK_BLOCK // S5_LC
GLA_ROWS = 4
SWA_ROWS = 2
IN_ROWS = 4
OUT_ROWS = 2

C_V, C_G, C_Q, C_K, C_Z, C_AQ, C_AK, C_AV, C_U = 0, 384, 768, 1024, 1280, 1408, 1792, 1920, 2048
IN_PAD = 2304
C_QK_END = C_K + GLA_KPAD

V7X_VMEM_BYTES = 64 * 1024 * 1024
VMEM_LIMIT = V7X_VMEM_BYTES - 8 * 1024 * 1024


def _sigmoid(x):
    return 1.0 / (1.0 + jnp.exp(-x))


def _silu(x):
    return x * _sigmoid(x)


def _dot(a, b):
    return jnp.dot(a, b, preferred_element_type=F32)


def _dot_nt(a, b):
    return lax.dot_general(a, b, (((1,), (1,)), ((), ())), preferred_element_type=F32)


def _dot_tn(a, b):
    return lax.dot_general(a, b, (((0,), (0,)), ((), ())), preferred_element_type=F32)


def _mod_kernel(c_ref, w_ref, b_ref, o_ref):
    a = _silu(c_ref[...]).astype(BF16)
    o_ref[...] = _dot(a, w_ref[...].astype(BF16)) + b_ref[...]


def _modulation(cvec, w_mod, b_mod):
    n_layers = w_mod.shape[0]
    rows = cvec.shape[0]
    tn = 1536
    return pl.pallas_call(
        _mod_kernel,
        grid=(n_layers, 6 * D_MODEL // tn),
        in_specs=[
            pl.BlockSpec((rows, D_MODEL), lambda l, n: (0, 0)),
            pl.BlockSpec((None, D_MODEL, tn), lambda l, n: (l, 0, n)),
            pl.BlockSpec((None, 1, tn), lambda l, n: (l, 0, n)),
        ],
        out_specs=pl.BlockSpec((None, rows, tn), lambda l, n: (l, 0, n)),
        out_shape=jax.ShapeDtypeStruct((n_layers, rows, 6 * D_MODEL), F32),
        compiler_params=pltpu.CompilerParams(
            dimension_semantics=("arbitrary", "arbitrary"), vmem_limit_bytes=VMEM_LIMIT),
        name="modulation",
    )(cvec, w_mod, b_mod.reshape(n_layers, 1, 6 * D_MODEL))


def _inproj_kernel(xc_ref, x_ref, mod_ref, n1_ref, w_ref, wa_ref, ba_ref, qs_ref, cos_ref, sa_ref, sb_ref,
                   gv_ref, gg_ref, gqk_ref, gla_ref, att_ref, uc_ref, us_ref, *, split):
    j = pl.program_id(1)
    cos = cos_ref[...]
    sa = sa_ref[...]
    sb = sb_ref[...]

    def normed(r):
        xf = jnp.where(j == 0, xc_ref[r], x_ref[r]) if split else x_ref[r]
        y = xf * lax.rsqrt(jnp.mean(xf * xf, axis=-1, keepdims=True) + EPS) * n1_ref[...]
        m = mod_ref[r]
        mrow = jnp.where(j > 0, m[1:2, :], m[0:1, :])
        sh1 = mrow[:, 0:D_MODEL]
        sc1 = mrow[:, D_MODEL:2 * D_MODEL]
        return (y * (1.0 + sc1) + sh1).astype(BF16)

    def rope(t, reps):
        w = t.shape[-1]
        c3 = jnp.concatenate([cos] * reps, axis=-1) if reps > 1 else cos
        a3 = jnp.concatenate([sa] * reps, axis=-1) if reps > 1 else sa
        b3 = jnp.concatenate([sb] * reps, axis=-1) if reps > 1 else sb
        up = pltpu.roll(t, w - 16, 1)
        dn = pltpu.roll(t, 16, 1)
        return t * c3 + up * a3 + dn * b3

    def tail_gla(r, p):
        gv_ref[r] = p[:, C_V:C_V + GLA_WIDTH].astype(BF16)
        gg_ref[r] = p[:, C_G:C_G + GLA_WIDTH]
        gqk_ref[r] = p[:, C_Q:C_QK_END] * qs_ref[...]

    def tail_rest(r, p):
        off = C_QK_END
        z = p[:, C_Z - off:C_Z - off + 128].astype(BF16)
        zg = _dot(z, wa_ref[...]) + ba_ref[...]
        gla_ref[r] = (jnp.minimum(zg, 0.0) - jnp.log1p(jnp.exp(-jnp.abs(zg)))) * (1.0 / GLA_TAU)

        aq = rope(p[:, C_AQ - off:C_AQ - off + ATT_WIDTH], 3) * (ATT_HD ** -0.5 * LOG2E)
        ak = rope(p[:, C_AK - off:C_AK - off + ATT_KVW], 1)
        av = p[:, C_AV - off:C_AV - off + ATT_KVW]
        left = lax.broadcasted_iota(jnp.int32, ak.shape, 1) < ATT_HD

        def expand(t):
            sw = pltpu.roll(t, ATT_HD, 1)
            return [jnp.where(left, t, sw), t, jnp.where(left, sw, t)]

        att_ref[r] = jnp.concatenate([aq] + expand(ak) + expand(av), axis=-1).astype(BF16)
        for half in range(S5_WIDTH // 128):
            us_ref[r, half] = p[:, C_U - off + half * 128:C_U - off + (half + 1) * 128]
        for st in range(S5_LC):
            for half in range(S5_WIDTH // 128):
                lane0 = st * S5_WIDTH + half * 128
                uc_ref[r, :, lane0:lane0 + 128] = us_ref[r, half, pl.ds(st, CH_BLOCK, stride=S5_LC), :]

    h = normed(0)
    prev_rest = None
    for r in range(IN_ROWS):
        p_gla = _dot(h, w_ref[:, 0:C_QK_END])
        if prev_rest is not None:
            tail_rest(r - 1, prev_rest)
        p_rest = _dot(h, w_ref[:, C_QK_END:IN_PAD])
        if r + 1 < IN_ROWS:
            h = normed(r + 1)
        tail_gla(r, p_gla)
        prev_rest = p_rest
    tail_rest(IN_ROWS - 1, prev_rest)


def _stream_specs(lat_skip, first=0, rows=None):
    return [pl.BlockSpec((rows, TOK_BLOCK, D_MODEL), lambda b, j: (b, 0, 0)),
            pl.BlockSpec((rows, TOK_BLOCK, D_MODEL), lambda b, j: (b, jnp.maximum(j + first - lat_skip, 0), 0))]


def _inproj(x_ctx, x_rest, lat_skip, modsel, n1w, w_in_p, wa_cat, ba_cat, qscale, cos_t, sa_t, sb_t, layer):
    bsz = x_ctx.shape[0]
    tt = cos_t.shape[0]
    nb = tt // TOK_BLOCK
    const = lambda b, j: (0, 0)
    lyr3 = lambda b, j: (layer, 0, 0)
    tok = lambda w: pl.BlockSpec((IN_ROWS, TOK_BLOCK, w), lambda b, j: (b, j, 0))
    return pl.pallas_call(
        functools.partial(_inproj_kernel, split=lat_skip > 0),
        grid=(bsz // IN_ROWS, nb),
        in_specs=_stream_specs(lat_skip, rows=IN_ROWS) + [
            pl.BlockSpec((IN_ROWS, 2, 6 * D_MODEL), lambda b, j: (b, 0, 0)),
            pl.BlockSpec((1, D_MODEL), const),
            pl.BlockSpec((None, D_MODEL, IN_PAD), lyr3),
            pl.BlockSpec((None, 128, 2 * GLA_KPAD), lyr3),
            pl.BlockSpec((1, 2 * GLA_KPAD), const),
            pl.BlockSpec((1, 2 * GLA_KPAD), const),
            pl.BlockSpec((TOK_BLOCK, 128), lambda b, j: (j, 0)),
            pl.BlockSpec((TOK_BLOCK, 128), lambda b, j: (j, 0)),
            pl.BlockSpec((TOK_BLOCK, 128), lambda b, j: (j, 0)),
        ],
        out_specs=[tok(GLA_WIDTH), tok(GLA_WIDTH), tok(2 * GLA_KPAD), tok(2 * GLA_KPAD), tok(3 * ATT_WIDTH),
                   pl.BlockSpec((IN_ROWS, CH_BLOCK, S5_CW), lambda b, j: (b, j, 0))],
        out_shape=[
            jax.ShapeDtypeStruct((bsz, tt, GLA_WIDTH), BF16),
            jax.ShapeDtypeStruct((bsz, tt, GLA_WIDTH), F32),
            jax.ShapeDtypeStruct((bsz, tt, 2 * GLA_KPAD), F32),
            jax.ShapeDtypeStruct((bsz, tt, 2 * GLA_KPAD), F32),
            jax.ShapeDtypeStruct((bsz, tt, 3 * ATT_WIDTH), BF16),
            jax.ShapeDtypeStruct((bsz, tt // S5_LC, S5_CW), F32),
        ],
        scratch_shapes=[pltpu.VMEM((IN_ROWS, S5_WIDTH // 128, TOK_BLOCK, 128), F32)],
        compiler_params=pltpu.CompilerParams(
            dimension_semantics=("arbitrary", "arbitrary"), vmem_limit_bytes=VMEM_LIMIT),
        name="inproj",
    )(x_ctx, x_rest, modsel, n1w, w_in_p, wa_cat, ba_cat, qscale, cos_t, sa_t, sb_t)


def _gla_kernel(qkf_ref, vf_ref, laf_ref, qkb_ref, vb_ref, lab_ref, tri_ref, tril_ref, hm_ref, vm_ref, bd_ref,
                of_ref, ob_ref, stf_ref, stb_ref):
    j = pl.program_id(1)

    @pl.when(j == 0)
    def _():
        stf_ref[...] = jnp.zeros_like(stf_ref)
        stb_ref[...] = jnp.zeros_like(stb_ref)

    n_chunks = TOK_BLOCK // GLA_CHUNK
    chunk = lambda t, c: t[c * GLA_CHUNK:(c + 1) * GLA_CHUNK]
    dirs = []
    for r in range(GLA_ROWS):
        dirs.append(((qkf_ref.at[r], vf_ref.at[r], laf_ref.at[r]), 0, GLA_CHUNK - 1, of_ref.at[r],
                     stf_ref.at[r], list(range(n_chunks))))
        dirs.append(((qkb_ref.at[r], vb_ref.at[r], lab_ref.at[r]), 1, 0, ob_ref.at[r],
                     stb_ref.at[r], list(reversed(range(n_chunks)))))

    cum = []
    for (_, _, la_ref), d, _, _, _, _ in dirs:
        la = la_ref[...]
        hi = la.astype(BF16)
        lo = (la - hi.astype(F32)).astype(BF16)
        cum.append(_dot(tri_ref[d], hi) + _dot(tri_ref[d], lo))

    ops = []
    for ((qk_ref, v_ref, _), d, last_row, _, _, _), b in zip(dirs, cum):
        q = qk_ref[:, 0:GLA_KPAD]
        k = qk_ref[:, GLA_KPAD:2 * GLA_KPAD]
        vb = v_ref[...]
        bl = [chunk(b, c)[last_row:last_row + 1, :] for c in range(n_chunks)]
        blx = jnp.concatenate([jnp.broadcast_to(t, (GLA_CHUNK, GLA_KPAD)) for t in bl], axis=0)
        qb = (q * jnp.exp(b)).astype(BF16)
        kb = (k * jnp.exp(-b)).astype(BF16)
        kd = (k * jnp.exp(blx - b)).astype(BF16)
        ops.append((qb, kb, kd, vb, bl))

    def chunk_scores(si, c):
        qb, kb = ops[si][0], ops[si][1]
        keep = tril_ref[dirs[si][1]] > 0
        kst = jnp.concatenate([chunk(kb, c) * hm_ref[h] for h in range(GLA_HEADS)], axis=0)
        return jnp.where(keep, _dot_nt(chunk(qb, c), kst), 0.0).astype(BF16)

    bdmask = bd_ref[...]
    states = [d[4][...] for d in dirs]
    nxt = [chunk_scores(si, d[5][0]) for si, d in enumerate(dirs)]
    for i in range(n_chunks):
        cur, nxt = nxt, [None] * len(dirs)
        for si, ((_, _, _, o_ref, _, order), (qb, _, kd, vb, bl)) in enumerate(zip(dirs, ops)):
            c = order[i]
            vc = chunk(vb, c)
            inc = _dot_tn(vc, chunk(kd, c)) * bdmask
            if i + 1 < n_chunks:
                nxt[si] = chunk_scores(si, order[i + 1])
            vbd = jnp.concatenate([vc * vm_ref[h] for h in range(GLA_HEADS)], axis=0)
            o_ref[c * GLA_CHUNK:(c + 1) * GLA_CHUNK, :] = (
                _dot(cur[si], vbd) + _dot_nt(chunk(qb, c), states[si].astype(BF16)))
            states[si] = states[si] * jnp.exp(bl[c]) + inc
    for d, st in zip(dirs, states):
        d[4][...] = st


def _gla_masks():
    r = np.arange(GLA_CHUNK)
    lower = (r[None, :] <= r[:, None]).astype(np.float32)
    tri1 = np.stack([lower, lower.T])
    n_chunks = TOK_BLOCK // GLA_CHUNK
    tri = np.stack([np.kron(np.eye(n_chunks, dtype=np.float32), t) for t in tri1])
    tril = np.tile(tri1, (1, 1, GLA_HEADS))
    klane = np.arange(GLA_KPAD)
    vlane = np.arange(GLA_WIDTH)
    hm = np.stack([np.broadcast_to((klane // GLA_DK) == h, (GLA_CHUNK, GLA_KPAD)) for h in range(GLA_HEADS)])
    vm = np.stack([np.broadcast_to((vlane // GLA_DV) == h, (GLA_CHUNK, GLA_WIDTH)) for h in range(GLA_HEADS)])
    bd = ((vlane[:, None] // GLA_DV) == (klane[None, :] // GLA_DK)).astype(np.float32)
    return (jnp.asarray(tri, BF16), jnp.asarray(tril, F32), jnp.asarray(hm, BF16), jnp.asarray(vm, BF16),
            jnp.asarray(bd))


def _scan_block(d, j, n_ctx, n_all):
    bwd = jnp.where(j < n_ctx, n_ctx - 1 - j, n_all + n_ctx - 1 - j)
    return jnp.where(d == 0, j, bwd)


def _gla(gqk, gv, gla):
    bsz, tt, _ = gv.shape
    nb = tt // TOK_BLOCK
    n_ctx = CTX_LEN // TOK_BLOCK
    tri, tril, hm, vm, bd = _gla_masks()
    const2 = lambda b, j: (0, 0)
    const3 = lambda b, j: (0, 0, 0)
    fwd = lambda b, j: (b, j, 0)
    bwd = lambda b, j: (b, _scan_block(1, j, n_ctx, nb), 0)
    bwd_gate = lambda b, j: (b, _scan_block(1, j, n_ctx, nb), 1)
    tok = lambda w, imap: pl.BlockSpec((GLA_ROWS, TOK_BLOCK, w), imap)
    return pl.pallas_call(
        _gla_kernel,
        grid=(bsz // GLA_ROWS, nb),
        in_specs=[
            tok(2 * GLA_KPAD, fwd),
            tok(GLA_WIDTH, fwd),
            tok(GLA_KPAD, fwd),
            tok(2 * GLA_KPAD, bwd),
            tok(GLA_WIDTH, bwd),
            tok(GLA_KPAD, bwd_gate),
            pl.BlockSpec((2, TOK_BLOCK, TOK_BLOCK), const3),
            pl.BlockSpec((2, GLA_CHUNK, GLA_HEADS * GLA_CHUNK), const3),
            pl.BlockSpec((GLA_HEADS, GLA_CHUNK, GLA_KPAD), const3),
            pl.BlockSpec((GLA_HEADS, GLA_CHUNK, GLA_WIDTH), const3),
            pl.BlockSpec((GLA_WIDTH, GLA_KPAD), const2),
        ],
        out_specs=[tok(GLA_WIDTH, fwd), tok(GLA_WIDTH, bwd)],
        out_shape=[jax.ShapeDtypeStruct((bsz, tt, GLA_WIDTH), F32)] * 2,
        scratch_shapes=[pltpu.VMEM((GLA_ROWS, GLA_WIDTH, GLA_KPAD), F32)] * 2,
        compiler_params=pltpu.CompilerParams(
            dimension_semantics=("arbitrary", "arbitrary"), vmem_limit_bytes=VMEM_LIMIT),
        name="gla",
    )(gqk, gv, gla, gqk, gv, gla, tri, tril, hm, vm, bd)


def _swa_kernel(sink_ref, mask_ref, att_ref, o_ref, *, tt):
    j = pl.program_id(1)
    n_sub = TOK_BLOCK // ATT_BLOCK
    n_ctx_blocks = CTX_LEN // TOK_BLOCK
    last_q = tt // ATT_BLOCK - 1
    lane = lax.broadcasted_iota(jnp.int32, (ATT_BLOCK, 2 * ATT_HD), 1)
    left = lane < ATT_HD
    row2 = lax.broadcasted_iota(jnp.int32, (2 * ATT_BLOCK, 1), 0)
    units = [(r, sub, p) for r in range(SWA_ROWS) for sub in range(n_sub) for p in range(ATT_HEADS // 2)]

    def run(windowed):
        def score(r, sub, p):
            q0 = pl.multiple_of(j * TOK_BLOCK + sub * ATT_BLOCK, ATT_BLOCK)
            qp = att_ref[r, pl.ds(q0, ATT_BLOCK), p * 128:(p + 1) * 128]
            zero = jnp.zeros_like(qp)
            q2 = jnp.concatenate([jnp.where(left, qp, zero), jnp.where(left, zero, qp)], axis=0)
            kcol = ATT_WIDTH + p * 128
            s_c = _dot_nt(q2, att_ref[r, 0:CTX_LEN, kcol:kcol + 128])
            if not windowed:
                return s_c, None, None
            start = pl.multiple_of(jnp.minimum(q0 - ATT_BLOCK, tt - 3 * ATT_BLOCK), ATT_BLOCK)
            s_w = jnp.where(keeps[sub], _dot_nt(q2, att_ref[r, pl.ds(start, 3 * ATT_BLOCK), kcol:kcol + 128]),
                            NEG_INF)
            return s_c, s_w, start

        def softmax(p, s_c, s_w, start):
            sink_col = jnp.where(row2 < ATT_BLOCK, sink_ref[2 * p], sink_ref[2 * p + 1]) * LOG2E
            m = jnp.maximum(jnp.max(s_c, axis=-1, keepdims=True), sink_col)
            if windowed:
                m = jnp.maximum(m, jnp.max(s_w, axis=-1, keepdims=True))
            e_c = jnp.exp2(s_c - m)
            den = jnp.sum(e_c, axis=-1, keepdims=True) + jnp.exp2(sink_col - m)
            e_w = None
            if windowed:
                e_w = jnp.exp2(s_w - m)
                den = den + jnp.sum(e_w, axis=-1, keepdims=True)
                e_w = e_w.astype(BF16)
            return e_c.astype(BF16), e_w, den, start

        def values(r, sub, p, e_c, e_w, den, start):
            vcol = 2 * ATT_WIDTH + p * 128
            pv = _dot(e_c, att_ref[r, 0:CTX_LEN, vcol:vcol + 128])
            if windowed:
                pv = pv + _dot(e_w, att_ref[r, pl.ds(start, 3 * ATT_BLOCK), vcol:vcol + 128])
            o2 = pv / den
            o = jnp.where(left, o2[0:ATT_BLOCK], o2[ATT_BLOCK:2 * ATT_BLOCK])
            o_ref[r, sub * ATT_BLOCK:(sub + 1) * ATT_BLOCK, p * 128:(p + 1) * 128] = o.astype(BF16)

        keeps = []
        if windowed:
            for sub in range(n_sub):
                qi = j * n_sub + sub
                kind = jnp.where(qi == CTX_LEN // ATT_BLOCK, 0, jnp.where(qi == last_q, 2, 1))
                keeps.append(mask_ref[kind] > 0)

        n = len(units)
        sc, pr = {}, {}
        for t in range(n + 2):
            if t < n:
                sc[t] = score(*units[t])
            if 0 <= t - 1 < n:
                pr[t - 1] = softmax(units[t - 1][2], *sc.pop(t - 1))
            if 0 <= t - 2 < n:
                values(*units[t - 2], *pr.pop(t - 2))

    @pl.when(j < n_ctx_blocks)
    def _():
        run(False)

    @pl.when(j >= n_ctx_blocks)
    def _():
        run(True)


def _swa_masks():
    r = np.arange(2 * ATT_BLOCK)[:, None] % ATT_BLOCK
    c = np.arange(3 * ATT_BLOCK)[None, :]
    near = lambda delta: np.abs(r - c + delta) <= WINDOW
    first = near(ATT_BLOCK) & (c >= ATT_BLOCK)
    return jnp.asarray(np.stack([first, near(ATT_BLOCK), near(2 * ATT_BLOCK)]), F32)


def _swa(att, sink):
    bsz, tt, _ = att.shape
    nb = tt // TOK_BLOCK
    return pl.pallas_call(
        functools.partial(_swa_kernel, tt=tt),
        grid=(bsz // SWA_ROWS, nb),
        in_specs=[
            pl.BlockSpec(memory_space=pltpu.SMEM),
            pl.BlockSpec((3, 2 * ATT_BLOCK, 3 * ATT_BLOCK), lambda b, j: (0, 0, 0)),
            pl.BlockSpec((SWA_ROWS, tt, 3 * ATT_WIDTH), lambda b, j: (b, 0, 0)),
        ],
        out_specs=pl.BlockSpec((SWA_ROWS, TOK_BLOCK, ATT_WIDTH), lambda b, j: (b, j, 0)),
        out_shape=jax.ShapeDtypeStruct((bsz, tt, ATT_WIDTH), BF16),
        compiler_params=pltpu.CompilerParams(
            dimension_semantics=("arbitrary", "arbitrary"), vmem_limit_bytes=VMEM_LIMIT),
        name="swa",
    )(sink, _swa_masks(), att)


def _s5_prep_kernel(lr_ref, li_ref, ls_ref, br_ref, bi_ref, cr_ref, ci_ref,
                    a16_ref, mre_ref, mim_ref, rre_ref, rim_ref, k_ref):
    rows = S5_LC * S5_GROUP
    rowg = lax.broadcasted_iota(jnp.int32, (rows, S5_NS), 0) >> 4
    colg = lax.broadcasted_iota(jnp.int32, (rows, S5_NS), 1) >> 6
    same_group = rowg == colg
    taps_t = []
    for d in range(2):
        lr = jnp.minimum(lr_ref[d], -1e-4)
        li = li_ref[d]
        dt = jnp.exp(ls_ref[d])
        mag = jnp.exp(lr * dt)
        ar = mag * jnp.cos(li * dt)
        ai = mag * jnp.sin(li * dt)
        den = lr * lr + li * li
        fr = ((ar - 1.0) * lr + ai * li) / den
        fi = (ai * lr - (ar - 1.0) * li) / den
        br = br_ref[d]
        bi = bi_ref[d]
        bbr = fr * br - fi * bi
        bbi = fr * bi + fi * br
        cr = cr_ref[d]
        ci = ci_ref[d]
        pr = jnp.ones_like(ar)
        pi = jnp.zeros_like(ar)
        cp_re, cp_im = [], []
        for tau in range(S5_LC + 1):
            cpr = cr * pr - ci * pi
            cpi = cr * pi + ci * pr
            if tau < S5_LC:
                mre_ref[d, tau] = pr * bbr - pi * bbi
                mim_ref[d, tau] = pr * bbi + pi * bbr
                cp_re.append(cpr)
                cp_im.append(cpi)
            if tau >= 1:
                rre_ref[d, tau - 1] = cpr
                rim_ref[d, tau - 1] = -cpi
            if tau == S5_LC:
                a16_ref[d, 0:1, :] = pr
                a16_ref[d, 1:2, :] = pi
            pr, pi = pr * ar - pi * ai, pr * ai + pi * ar
        bd_re = jnp.where(same_group, jnp.concatenate([bbr] * S5_GROUPS, axis=0), 0.0)
        bd_im = jnp.where(same_group, jnp.concatenate([bbi] * S5_GROUPS, axis=0), 0.0)
        if d == 1:
            cp_re, cp_im = cp_re[::-1], cp_im[::-1]
        hp = lax.Precision.HIGHEST
        dims = (((1,), (1,)), ((), ()))
        taps_t.append(
            lax.dot_general(bd_re, jnp.concatenate(cp_re, axis=0), dims, precision=hp, preferred_element_type=F32)
            - lax.dot_general(bd_im, jnp.concatenate(cp_im, axis=0), dims, precision=hp,
                              preferred_element_type=F32))
    kf, kb_rev = taps_t
    tap0 = lax.broadcasted_iota(jnp.int32, kf.shape, 1) < S5_GROUP
    k_ref[0] = jnp.where(tap0, kf + pltpu.roll(kb_rev, S5_GROUP, 1), kf)
    k_ref[1] = kb_rev


def _s5_tables(fwd, bwd):
    n_layers = fwd[0].shape[0]
    both = lambda i: jnp.stack([fwd[i], bwd[i]], axis=1)
    lam_re = both(0).reshape(n_layers, 2, 1, S5_NS)
    lam_im = both(1).reshape(n_layers, 2, 1, S5_NS)
    log_step = jnp.repeat(both(2), S5_STATE, axis=-1).reshape(n_layers, 2, 1, S5_NS)
    b_hn = lambda t: t.transpose(0, 1, 4, 2, 3).reshape(n_layers, 2, S5_GROUP, S5_NS)
    c_hn = lambda t: t.transpose(0, 1, 3, 2, 4).reshape(n_layers, 2, S5_GROUP, S5_NS)
    vec = pl.BlockSpec((None, 2, 1, S5_NS), lambda l: (l, 0, 0, 0))
    mat = pl.BlockSpec((None, 2, S5_GROUP, S5_NS), lambda l: (l, 0, 0, 0))
    tab = pl.BlockSpec((None, 2, S5_LC, S5_GROUP, S5_NS), lambda l: (l, 0, 0, 0, 0))
    taps = S5_LC * S5_GROUP
    a16, mre, mim, rre, rim, k = pl.pallas_call(
        _s5_prep_kernel,
        grid=(n_layers,),
        in_specs=[vec, vec, vec, mat, mat, mat, mat],
        out_specs=[
            pl.BlockSpec((None, 2, 2, S5_NS), lambda l: (l, 0, 0, 0)),
            tab, tab, tab, tab,
            pl.BlockSpec((None, 2, S5_WIDTH, taps), lambda l: (l, 0, 0, 0)),
        ],
        out_shape=[
            jax.ShapeDtypeStruct((n_layers, 2, 2, S5_NS), F32),
            jax.ShapeDtypeStruct((n_layers, 2, S5_LC, S5_GROUP, S5_NS), F32),
            jax.ShapeDtypeStruct((n_layers, 2, S5_LC, S5_GROUP, S5_NS), F32),
            jax.ShapeDtypeStruct((n_layers, 2, S5_LC, S5_GROUP, S5_NS), F32),
            jax.ShapeDtypeStruct((n_layers, 2, S5_LC, S5_GROUP, S5_NS), F32),
            jax.ShapeDtypeStruct((n_layers, 2, S5_WIDTH, taps), F32),
        ],
        compiler_params=pltpu.CompilerParams(dimension_semantics=("arbitrary",), vmem_limit_bytes=VMEM_LIMIT),
        name="s5_prep",
    )(lam_re, lam_im, log_step, b_hn(both(3)), b_hn(both(4)), c_hn(both(5)), c_hn(both(6)))

    n_pairs = S5_GROUPS // 2
    lane_group = np.arange(128) // S5_STATE

    def pair_rows(t, flip):
        t = t[:, ::-1] if flip else t
        t = t.reshape(n_layers, S5_LC, S5_GROUP, n_pairs, 128).transpose(0, 3, 1, 2, 4)
        own = jnp.asarray(lane_group[None, :] == np.arange(2)[:, None])
        t = jnp.where(own[None, None, :, None, None, :], t[:, :, None], 0.0)
        return t.reshape(n_layers, n_pairs, 2 * taps, 128)

    m_f = jnp.concatenate([pair_rows(mre[:, 0], True), pair_rows(mim[:, 0], True)], axis=-1).astype(BF16)
    m_b = jnp.concatenate([pair_rows(mre[:, 1], False), pair_rows(mim[:, 1], False)], axis=-1).astype(BF16)

    r_t = jnp.concatenate([pair_rows(rre[:, 0], False), pair_rows(rim[:, 0], False),
                           pair_rows(rre[:, 1], True), pair_rows(rim[:, 1], True)], axis=-1).astype(BF16)

    zeros = jnp.zeros((n_layers, S5_WIDTH, taps), F32)
    f2 = jnp.concatenate([zeros, k[:, 0]], axis=-1)
    b2 = jnp.concatenate([k[:, 1], zeros], axis=-1)
    t_f = jnp.stack([f2[..., taps - S5_GROUP * st:2 * taps - S5_GROUP * st] for st in range(S5_LC)], axis=2)
    t_b = jnp.stack([b2[..., S5_GROUP * (S5_LC - 1 - st):S5_GROUP * (S5_LC - 1 - st) + taps]
                     for st in range(S5_LC)], axis=2)
    later = jnp.asarray(np.arange(taps)[None, :] // S5_GROUP >= np.arange(S5_LC)[:, None])
    toe = jnp.where(later[None, None], t_f, t_b)
    toe = toe.reshape(n_layers, S5_GROUPS, S5_GROUP, S5_LC, taps).transpose(0, 1, 3, 2, 4)
    toe = toe.reshape(n_layers, S5_GROUPS, taps, taps).astype(BF16)
    return a16, m_f, m_b, r_t, toe


def _lane_block_transpose(cols):
    lane = lax.broadcasted_iota(jnp.int32, cols[0].shape, 1)
    out = [None] * 32
    for ah in range(2):
        for bh in range(2):
            v = [cols[(ah * 8 + al) * 2 + bh] for al in range(8)]
            for kbit in range(3):
                width = 16 << kbit
                low = ((lane >> (4 + kbit)) & 1) == 0
                nxt = list(v)
                for i in range(8):
                    if i & (1 << kbit):
                        continue
                    lo_v, hi_v = v[i], v[i | (1 << kbit)]
                    nxt[i] = jnp.where(low, lo_v, pltpu.roll(hi_v, width, 1))
                    nxt[i | (1 << kbit)] = jnp.where(low, pltpu.roll(lo_v, 128 - width, 1), hi_v)
                v = nxt
            for bl in range(8):
                out[(bh * 8 + bl) * 2 + ah] = v[bl]
    return out


def _s5_increments(ush, m_ref, d_re, d_im):
    for p in range(S5_GROUPS // 2):
        dp = _dot(ush[:, p * 512:(p + 1) * 512], m_ref[p])
        d_re[p] = dp[:, 0:128]
        d_im[p] = dp[:, 128:256]


def _s5_recurrence(a16_ref, d_re, d_im, x_re, x_im, sr_ref, si_ref, order, bsz):
    ar = a16_ref[0:1, :]
    ai = a16_ref[1:2, :]
    sr = sr_ref[...]
    si = si_ref[...]
    n_slabs = S5_NS // 128
    gather = lambda ref, rows: jnp.concatenate([ref[p, rows, :] for p in range(n_slabs)], axis=1)
    for c in order:
        rows = pl.ds(c, bsz, stride=CH_BLOCK)
        for p in range(n_slabs):
            x_re[p, rows, :] = sr[:, p * 128:(p + 1) * 128]
            x_im[p, rows, :] = si[:, p * 128:(p + 1) * 128]
        sr, si = ar * sr - ai * si + gather(d_re, rows), ar * si + ai * sr + gather(d_im, rows)
    sr_ref[...] = sr
    si_ref[...] = si


def _s5_fwd_kernel(uc_ref, a16_ref, m_ref, ush_ref, xin_ref, d_re, d_im, x_re, x_im, sr_ref, si_ref, *, bsz):
    @pl.when(pl.program_id(0) == 0)
    def _():
        sr_ref[...] = jnp.zeros_like(sr_ref)
        si_ref[...] = jnp.zeros_like(si_ref)

    rows = bsz * CH_BLOCK
    ub = uc_ref[...].reshape(rows, S5_CW).astype(BF16)
    packed = pltpu.bitcast(ub, jnp.uint32)
    cols = _lane_block_transpose([packed[:, v * 128:(v + 1) * 128] for v in range(32)])
    ush = pltpu.bitcast(jnp.concatenate(cols, axis=1), BF16)
    ush_ref[...] = ush.reshape(bsz, CH_BLOCK, S5_CW)
    _s5_increments(ush, m_ref, d_re, d_im)
    _s5_recurrence(a16_ref, d_re, d_im, x_re, x_im, sr_ref, si_ref, range(CH_BLOCK), bsz)
    for p in range(S5_NS // 128):
        xin_ref[:, :, p * 128:(p + 1) * 128] = x_re[p].astype(BF16).reshape(bsz, CH_BLOCK, 128)
        xin_ref[:, :, S5_NS + p * 128:S5_NS + (p + 1) * 128] = x_im[p].astype(BF16).reshape(bsz, CH_BLOCK, 128)


def _s5_bwd_kernel(ush_ref, xf_ref, a16_ref, m_ref, toe_ref, r_ref, y_ref, d_re, d_im, x_re, x_im, sr_ref, si_ref,
                   *, bsz):
    @pl.when(pl.program_id(0) == 0)
    def _():
        sr_ref[...] = jnp.zeros_like(sr_ref)
        si_ref[...] = jnp.zeros_like(si_ref)

    rows = bsz * CH_BLOCK
    ush = ush_ref[...].reshape(rows, S5_CW)
    _s5_increments(ush, m_ref, d_re, d_im)
    _s5_recurrence(a16_ref, d_re, d_im, x_re, x_im, sr_ref, si_ref, reversed(range(CH_BLOCK)), bsz)
    xf = xf_ref[...].reshape(rows, 2 * S5_NS)
    ycols = []
    for p in range(S5_GROUPS // 2):
        lanes = slice(p * 128, (p + 1) * 128)
        xcat = jnp.concatenate([xf[:, lanes], xf[:, S5_NS + p * 128:S5_NS + (p + 1) * 128],
                                x_re[p].astype(BF16), x_im[p].astype(BF16)], axis=1)
        carry = _dot_nt(xcat, r_ref[p])
        for g2 in range(2):
            g = 2 * p + g2
            yg = carry[:, g2 * 256:(g2 + 1) * 256] + _dot(ush[:, g * 256:(g + 1) * 256], toe_ref[g])
            ycols += [yg[:, 0:128], yg[:, 128:256]]
    ycols = _lane_block_transpose(ycols)
    y_ref[...] = jnp.concatenate(ycols, axis=1).reshape(bsz, CH_BLOCK, S5_CW)


def _s5(u_c, tables, layer):
    a16, m_f, m_b, r_t, toe = tables
    bsz, n_rows, _ = u_c.shape
    nb = n_rows // CH_BLOCK
    n_ctx = CTX_LEN // TOK_BLOCK
    rows = bsz * CH_BLOCK
    blk3 = lambda w: (bsz, CH_BLOCK, w)
    scratch = [pltpu.VMEM((S5_NS // 128, rows, 128), F32)] * 4 + [pltpu.VMEM((bsz, S5_NS), F32)] * 2
    params = pltpu.CompilerParams(dimension_semantics=("arbitrary",), vmem_limit_bytes=VMEM_LIMIT)
    ush, xin_f = pl.pallas_call(
        functools.partial(_s5_fwd_kernel, bsz=bsz),
        grid=(nb,),
        in_specs=[
            pl.BlockSpec(blk3(S5_CW), lambda j: (0, j, 0)),
            pl.BlockSpec((None, None, 2, S5_NS), lambda j: (layer, 0, 0, 0)),
            pl.BlockSpec((None, S5_GROUPS // 2, 512, 256), lambda j: (layer, 0, 0, 0)),
        ],
        out_specs=[
            pl.BlockSpec(blk3(S5_CW), lambda j: (0, j, 0)),
            pl.BlockSpec(blk3(2 * S5_NS), lambda j: (0, j, 0)),
        ],
        out_shape=[
            jax.ShapeDtypeStruct((bsz, n_rows, S5_CW), BF16),
            jax.ShapeDtypeStruct((bsz, n_rows, 2 * S5_NS), BF16),
        ],
        scratch_shapes=scratch,
        compiler_params=params,
        name="s5_fwd",
    )(u_c, a16, m_f)
    back = lambda j: (0, _scan_block(1, j, n_ctx, nb), 0)
    return pl.pallas_call(
        functools.partial(_s5_bwd_kernel, bsz=bsz),
        grid=(nb,),
        in_specs=[
            pl.BlockSpec(blk3(S5_CW), back),
            pl.BlockSpec(blk3(2 * S5_NS), back),
            pl.BlockSpec((None, None, 2, S5_NS), lambda j: (layer, 1, 0, 0)),
            pl.BlockSpec((None, S5_GROUPS // 2, 512, 256), lambda j: (layer, 0, 0, 0)),
            pl.BlockSpec((None, S5_GROUPS, 256, 256), lambda j: (layer, 0, 0, 0)),
            pl.BlockSpec((None, S5_GROUPS // 2, 512, 512), lambda j: (layer, 0, 0, 0)),
        ],
        out_specs=pl.BlockSpec(blk3(S5_CW), back),
        out_shape=jax.ShapeDtypeStruct((bsz, n_rows, S5_CW), F32),
        scratch_shapes=scratch,
        compiler_params=params,
        name="s5_bwd",
    )(ush, xin_f, a16, m_b, toe, r_t)


def _out_kernel(xc_ref, x_ref, mod_ref, gof_ref, gob_ref, g_ref, att_ref, y_ref, uc_ref, gnw_ref, hm_ref, d_ref,
                gluw_ref, glub_ref, wo_ref, n2_ref, w1_ref, w2_ref, fn_ref, o_ref, ys_ref,
                *, first_block, final, split):
    j = pl.program_id(1) + first_block

    def mixed(r):
        m = mod_ref[r]
        mrow = jnp.where(j > 0, m[1:2, :], m[0:1, :])
        g1 = mrow[:, 2 * D_MODEL:3 * D_MODEL]
        sh2 = mrow[:, 3 * D_MODEL:4 * D_MODEL]
        sc2 = mrow[:, 4 * D_MODEL:5 * D_MODEL]
        g2 = mrow[:, 5 * D_MODEL:6 * D_MODEL]

        o = gof_ref[r] + gob_ref[r]
        ms = _dot((o * o).astype(BF16), hm_ref[...]) * (1.0 / GLA_DV)
        a = o * lax.rsqrt(ms + EPS) * gnw_ref[...] * _silu(g_ref[r])

        yc = y_ref[r] + d_ref[...] * uc_ref[r]
        for st in range(S5_LC):
            for half in range(S5_WIDTH // 128):
                lane0 = st * S5_WIDTH + half * 128
                ys_ref[r, half, pl.ds(st, CH_BLOCK, stride=S5_LC), :] = yc[:, lane0:lane0 + 128]
        yy = jnp.concatenate([ys_ref[r, half] for half in range(S5_WIDTH // 128)], axis=1)
        ge = 0.5 * yy * (1.0 + jnp.tanh(math.sqrt(2.0 / math.pi) * (yy + 0.044715 * (yy * yy * yy))))
        z = _dot(ge.astype(BF16), gluw_ref[...]) + glub_ref[...]
        s = z[:, 0:S5_WIDTH] * _sigmoid(z[:, S5_WIDTH:2 * S5_WIDTH])

        proj = _dot(jnp.concatenate([a.astype(BF16), att_ref[r], s.astype(BF16)], axis=1), wo_ref[...])
        x_in = jnp.where(j == 0, xc_ref[r], x_ref[r]) if split else x_ref[r]
        x1 = x_in + g1 * proj
        y2 = x1 * lax.rsqrt(jnp.mean(x1 * x1, axis=-1, keepdims=True) + EPS) * n2_ref[...]
        return x1, (y2 * (1.0 + sc2) + sh2).astype(BF16), g2

    def finish(r, x1, g2, mlp):
        x2 = x1 + g2 * mlp
        if final:
            x2 = x2 * lax.rsqrt(jnp.mean(x2 * x2, axis=-1, keepdims=True) + EPS) * fn_ref[...]
        o_ref[r] = x2

    cur = mixed(0)
    for r in range(OUT_ROWS):
        x1, h2, g2 = cur
        hid = jnp.maximum(_dot(h2, w1_ref[...]), 0.0)
        if r + 1 < OUT_ROWS:
            cur = mixed(r + 1)
        mlp = _dot((hid * hid).astype(BF16), w2_ref[...])
        finish(r, x1, g2, mlp)


def _outproj(x_ctx, x_rest, lat_skip, modsel, gla_f, gla_b, gg, att_o, y_c, u_c, gnw, hm, s5d, gluw, glub, wo, n2w,
             w1, w2, fnw, layer, final):
    bsz, tt, _ = gg.shape
    nb = tt // TOK_BLOCK
    first = CTX_LEN // TOK_BLOCK if final else 0
    nsteps = nb - first
    const2 = lambda b, j: (0, 0)
    lyr3 = lambda b, j: (layer, 0, 0)
    single = pl.Buffered(1)
    return pl.pallas_call(
        functools.partial(_out_kernel, first_block=first, final=final, split=lat_skip > 0),
        grid=(bsz // OUT_ROWS, nsteps),
        in_specs=_stream_specs(lat_skip, first, rows=OUT_ROWS) + [
            pl.BlockSpec((OUT_ROWS, 2, 6 * D_MODEL), lambda b, j: (b, 0, 0)),
            pl.BlockSpec((OUT_ROWS, TOK_BLOCK, GLA_WIDTH), lambda b, j: (b, j + first, 0)),
            pl.BlockSpec((OUT_ROWS, TOK_BLOCK, GLA_WIDTH), lambda b, j: (b, j + first, 0)),
            pl.BlockSpec((OUT_ROWS, TOK_BLOCK, GLA_WIDTH), lambda b, j: (b, j + first, 0)),
            pl.BlockSpec((OUT_ROWS, TOK_BLOCK, ATT_WIDTH), lambda b, j: (b, j + first, 0)),
            pl.BlockSpec((OUT_ROWS, CH_BLOCK, S5_CW), lambda b, j: (b, j + first, 0)),
            pl.BlockSpec((OUT_ROWS, CH_BLOCK, S5_CW), lambda b, j: (b, j + first, 0)),
            pl.BlockSpec((1, GLA_WIDTH), const2),
            pl.BlockSpec((GLA_WIDTH, GLA_WIDTH), const2),
            pl.BlockSpec((1, S5_CW), const2),
            pl.BlockSpec((None, S5_WIDTH, 2 * S5_WIDTH), lyr3, pipeline_mode=single),
            pl.BlockSpec((1, 2 * S5_WIDTH), const2),
            pl.BlockSpec((None, D_MODEL, D_MODEL), lyr3, pipeline_mode=single),
            pl.BlockSpec((1, D_MODEL), const2),
            pl.BlockSpec((None, D_MODEL, D_FF), lyr3, pipeline_mode=single),
            pl.BlockSpec((None, D_FF, D_MODEL), lyr3, pipeline_mode=single),
            pl.BlockSpec((1, D_MODEL), const2),
        ],
        out_specs=pl.BlockSpec((OUT_ROWS, TOK_BLOCK, D_MODEL), lambda b, j: (b, j, 0)),
        out_shape=jax.ShapeDtypeStruct((bsz, nsteps * TOK_BLOCK, D_MODEL), F32),
        scratch_shapes=[pltpu.VMEM((OUT_ROWS, S5_WIDTH // 128, TOK_BLOCK, 128), F32)],
        compiler_params=pltpu.CompilerParams(
            dimension_semantics=("arbitrary", "arbitrary"), vmem_limit_bytes=VMEM_LIMIT),
        name="outproj_mlp",
    )(x_ctx, x_rest, modsel, gla_f, gla_b, gg, att_o, y_c, u_c, gnw, hm, s5d, gluw, glub, wo, n2w, w1, w2, fnw)


def _rope_tables(tt):
    n_lat = tt - CTX_LEN
    rows = n_lat // GRID_W
    row = jnp.repeat(jnp.arange(rows, dtype=F32), GRID_W)
    col = jnp.tile(jnp.arange(GRID_W, dtype=F32), rows)
    n_freq = ATT_HD // 4
    inv_freq = ROPE_BASE ** (-jnp.arange(n_freq, dtype=F32) / n_freq)
    ang_r = row[:, None] * inv_freq
    ang_c = col[:, None] * inv_freq
    ang = jnp.concatenate([ang_r, ang_r, ang_c, ang_c], axis=-1)
    cos = jnp.concatenate([jnp.ones((CTX_LEN, ATT_HD), F32), jnp.cos(ang)], axis=0)
    sin = jnp.concatenate([jnp.zeros((CTX_LEN, ATT_HD), F32), jnp.sin(ang)], axis=0)
    up_quarter = (np.arange(ATT_HD) // 16) % 2 == 0
    sa = jnp.where(up_quarter, -sin, 0.0)
    sb = jnp.where(up_quarter, 0.0, sin)
    two = lambda t: jnp.concatenate([t, t], axis=-1)
    return two(cos), two(sa), two(sb)


def _pad_cols(w, width):
    return jnp.pad(w, ((0, 0), (0, 0), (0, width - w.shape[-1])))


def _layout_w_in(w_in):
    offs = np.cumsum([0, GLA_KW, GLA_KW, GLA_WIDTH, GLA_WIDTH, GLA_RANK, GLA_RANK, ATT_WIDTH, ATT_KVW, ATT_KVW,
                      S5_WIDTH])
    q, k, v, g, zf, zb, aq, ak, av, u = [w_in[:, :, offs[i]:offs[i + 1]] for i in range(10)]
    z = jnp.concatenate([zf, zb], axis=-1)
    cols = [v, g, _pad_cols(q, GLA_KPAD), _pad_cols(k, GLA_KPAD), _pad_cols(z, 128), aq, ak, av, u]
    return jnp.concatenate(cols, axis=-1).astype(BF16)


def _block_diag(t):
    s, g, a, b = t.shape
    eye = jnp.eye(g, dtype=t.dtype)
    return jnp.einsum('sgab,gk->sgakb', t, eye).reshape(s, g * a, g * b)


def kernel(x, c, ctx, c_ctx, w_mod, b_mod, norm1_w, norm2_w, w_in, gla_wa_f, gla_ba_f, gla_wa_b, gla_ba_b,
           gla_norm_w, attn_sink, s5_lam_re_f, s5_lam_im_f, s5_log_step_f, s5_b_re_f, s5_b_im_f, s5_c_re_f,
           s5_c_im_f, s5_lam_re_b, s5_lam_im_b, s5_log_step_b, s5_b_re_b, s5_b_im_b, s5_c_re_b, s5_c_im_b,
           s5_d, glu_w, glu_b, w_out, mlp_w1, mlp_w2, final_norm_w):
    bsz, seq, _ = x.shape
    n_layers = w_mod.shape[0]
    tt = CTX_LEN + seq
    assert all(bsz % rows == 0 for rows in (8, IN_ROWS, GLA_ROWS, SWA_ROWS, OUT_ROWS))
    assert seq % TOK_BLOCK == 0 and seq >= 2 * TOK_BLOCK and ctx.shape[1] == CTX_LEN

    mod_rows = -(-(bsz + 1) // 8) * 8
    cvec = jnp.zeros((mod_rows, D_MODEL), F32).at[:bsz].set(c).at[bsz].set(c_ctx)
    mod = _modulation(cvec, w_mod, b_mod)
    mod_ctx = jnp.broadcast_to(mod[:, bsz][:, None], (n_layers, bsz, 6 * D_MODEL))
    modsel = jnp.stack([mod_ctx, mod[:, :bsz]], axis=2)

    w_in_p = _layout_w_in(w_in)
    wa_cat = jnp.zeros((n_layers, 128, 2 * GLA_KPAD), F32)
    wa_cat = wa_cat.at[:, 0:GLA_RANK, 0:GLA_KW].set(gla_wa_f)
    wa_cat = wa_cat.at[:, GLA_RANK:2 * GLA_RANK, GLA_KPAD:GLA_KPAD + GLA_KW].set(gla_wa_b).astype(BF16)
    ba_cat = jnp.zeros((n_layers, 1, 2 * GLA_KPAD), F32)
    ba_cat = ba_cat.at[:, 0, 0:GLA_KW].set(gla_ba_f).at[:, 0, GLA_KPAD:GLA_KPAD + GLA_KW].set(gla_ba_b)
    qscale = jnp.ones((1, 2 * GLA_KPAD), F32).at[:, 0:GLA_KW].set(GLA_DK ** -0.5)
    cos_t, sa_t, sb_t = _rope_tables(tt)
    gnw = jnp.tile(gla_norm_w, (1, GLA_HEADS))[:, None, :]
    head = np.arange(GLA_WIDTH) // GLA_DV
    hm = jnp.asarray(head[:, None] == head[None, :], BF16)
    wo_b = w_out.astype(BF16)
    w1_b = mlp_w1.astype(BF16)
    w2_b = mlp_w2.astype(BF16)
    gluw_b = glu_w.astype(BF16)

    s5_tab = _s5_tables(
        (s5_lam_re_f, s5_lam_im_f, s5_log_step_f, s5_b_re_f, s5_b_im_f, s5_c_re_f, s5_c_im_f),
        (s5_lam_re_b, s5_lam_im_b, s5_log_step_b, s5_b_re_b, s5_b_im_b, s5_c_re_b, s5_c_im_b))
    s5_dt = jnp.tile(s5_d, (1, S5_LC))[:, None, :]

    stream = (ctx, x, CTX_LEN // TOK_BLOCK)
    for l in range(n_layers):
        final = l == n_layers - 1
        gv, gg, gqk, gla, att, u_c = _inproj(*stream, modsel[l], norm1_w[l][None], w_in_p, wa_cat, ba_cat[l],
                                             qscale, cos_t, sa_t, sb_t, l)
        gla_f, gla_b = _gla(gqk, gv, gla)
        att_o = _swa(att, attn_sink[l])
        y_c = _s5(u_c, s5_tab, l)
        xs = _outproj(*stream, modsel[l], gla_f, gla_b, gg, att_o, y_c, u_c,
                      gnw[l], hm, s5_dt[l], gluw_b, glu_b[l][None], wo_b, norm2_w[l][None], w1_b, w2_b,
                      final_norm_w[None], l, final)
        stream = (xs, xs, 0)
    return xs
```

```python
import functools
import math

import jax
import jax.numpy as jnp
import numpy as np
from jax import lax
from jax.experimental import pallas as pl
from jax.experimental.pallas import tpu as pltpu

F32 = jnp.float32
BF16 = jnp.bfloat16

D_MODEL = 1024
D_FF = 4 * D_MODEL
CTX_LEN = 256
GRID_W = 64
EPS = 1e-6
NEG_INF = -1e30
LOG2E = math.log2(math.e)

GLA_HEADS = 4
GLA_DV = 96
GLA_DK = 48
GLA_WIDTH = GLA_HEADS * GLA_DV
GLA_KW = GLA_HEADS * GLA_DK
GLA_KPAD = 256
GLA_RANK = 16
GLA_TAU = 16.0
GLA_CHUNK = 64

ATT_HD = 64
ATT_HEADS = 6
ATT_KV_HEADS = 2
ATT_WIDTH = ATT_HEADS * ATT_HD
ATT_KVW = ATT_KV_HEADS * ATT_HD
WINDOW = 128
ATT_BLOCK = 128
ROPE_BASE = 10000.0

S5_WIDTH = 256
S5_GROUP = 16
S5_GROUPS = 16
S5_STATE = 64
S5_NS = S5_GROUPS * S5_STATE
S5_LC = 16
S5_CW = S5_LC * S5_WIDTH

TOK_BLOCK = 256
CH_BLOCK = TOK_BLOCK // S5_LC
GLA_ROWS = 4
SWA_ROWS = 2
IN_ROWS = 4
OUT_ROWS = 2

C_V, C_G, C_Q, C_K, C_Z, C_AQ, C_AK, C_AV, C_U = 0, 384, 768, 1024, 1280, 1408, 1792, 1920, 2048
IN_PAD = 2304
C_QK_END = C_K + GLA_KPAD

V7X_VMEM_BYTES = 64 * 1024 * 1024
VMEM_LIMIT = V7X_VMEM_BYTES - 8 * 1024 * 1024


def _sigmoid(x):
    return 1.0 / (1.0 + jnp.exp(-x))


def _silu(x):
    return x * _sigmoid(x)


def _dot(a, b):
    return jnp.dot(a, b, preferred_element_type=F32)


def _dot_nt(a, b):
    return lax.dot_general(a, b, (((1,), (1,)), ((), ())), preferred_element_type=F32)


def _dot_tn(a, b):
    return lax.dot_general(a, b, (((0,), (0,)), ((), ())), preferred_element_type=F32)


def _mod_kernel(c_ref, w_ref, b_ref, o_ref):
    a = _silu(c_ref[...]).astype(BF16)
    o_ref[...] = _dot(a, w_ref[...].astype(BF16)) + b_ref[...]


def _modulation(cvec, w_mod, b_mod):
    n_layers = w_mod.shape[0]
    rows = cvec.shape[0]
    tn = 1536
    return pl.pallas_call(
        _mod_kernel,
        grid=(n_layers, 6 * D_MODEL // tn),
        in_specs=[
            pl.BlockSpec((rows, D_MODEL), lambda l, n: (0, 0)),
            pl.BlockSpec((None, D_MODEL, tn), lambda l, n: (l, 0, n)),
            pl.BlockSpec((None, 1, tn), lambda l, n: (l, 0, n)),
        ],
        out_specs=pl.BlockSpec((None, rows, tn), lambda l, n: (l, 0, n)),
        out_shape=jax.ShapeDtypeStruct((n_layers, rows, 6 * D_MODEL), F32),
        compiler_params=pltpu.CompilerParams(
            dimension_semantics=("arbitrary", "arbitrary"), vmem_limit_bytes=VMEM_LIMIT),
        name="modulation",
    )(cvec, w_mod, b_mod.reshape(n_layers, 1, 6 * D_MODEL))


def _inproj_kernel(xc_ref, x_ref, mod_ref, n1_ref, w_ref, wa_ref, ba_ref, qs_ref, cos_ref, sa_ref, sb_ref,
                   gv_ref, gg_ref, gqk_ref, gla_ref, att_ref, uc_ref, us_ref, *, split):
    j = pl.program_id(1)
    cos = cos_ref[...]
    sa = sa_ref[...]
    sb = sb_ref[...]

    def normed(r):
        xf = jnp.where(j == 0, xc_ref[r], x_ref[r]) if split else x_ref[r]
        y = xf * lax.rsqrt(jnp.mean(xf * xf, axis=-1, keepdims=True) + EPS) * n1_ref[...]
        m = mod_ref[r]
        mrow = jnp.where(j > 0, m[1:2, :], m[0:1, :])
        sh1 = mrow[:, 0:D_MODEL]
        sc1 = mrow[:, D_MODEL:2 * D_MODEL]
        return (y * (1.0 + sc1) + sh1).astype(BF16)

    def rope(t, reps):
        w = t.shape[-1]
        c3 = jnp.concatenate([cos] * reps, axis=-1) if reps > 1 else cos
        a3 = jnp.concatenate([sa] * reps, axis=-1) if reps > 1 else sa
        b3 = jnp.concatenate([sb] * reps, axis=-1) if reps > 1 else sb
        up = pltpu.roll(t, w - 16, 1)
        dn = pltpu.roll(t, 16, 1)
        return t * c3 + up * a3 + dn * b3

    def tail_gla(r, p):
        gv_ref[r] = p[:, C_V:C_V + GLA_WIDTH].astype(BF16)
        gg_ref[r] = p[:, C_G:C_G + GLA_WIDTH]
        gqk_ref[r] = p[:, C_Q:C_QK_END] * qs_ref[...]

    def tail_rest(r, p):
        off = C_QK_END
        z = p[:, C_Z - off:C_Z - off + 128].astype(BF16)
        zg = _dot(z, wa_ref[...]) + ba_ref[...]
        gla_ref[r] = (jnp.minimum(zg, 0.0) - jnp.log1p(jnp.exp(-jnp.abs(zg)))) * (1.0 / GLA_TAU)

        aq = rope(p[:, C_AQ - off:C_AQ - off + ATT_WIDTH], 3) * (ATT_HD ** -0.5 * LOG2E)
        ak = rope(p[:, C_AK - off:C_AK - off + ATT_KVW], 1)
        av = p[:, C_AV - off:C_AV - off + ATT_KVW]
        left = lax.broadcasted_iota(jnp.int32, ak.shape, 1) < ATT_HD

        def expand(t):
            sw = pltpu.roll(t, ATT_HD, 1)
            return [jnp.where(left, t, sw), t, jnp.where(left, sw, t)]

        att_ref[r] = jnp.concatenate([aq] + expand(ak) + expand(av), axis=-1).astype(BF16)
        for half in range(S5_WIDTH // 128):
            us_ref[r, half] = p[:, C_U - off + half * 128:C_U - off + (half + 1) * 128]
        for st in range(S5_LC):
            for half in range(S5_WIDTH // 128):
                lane0 = st * S5_WIDTH + half * 128
                uc_ref[r, :, lane0:lane0 + 128] = us_ref[r, half, pl.ds(st, CH_BLOCK, stride=S5_LC), :]

    h = normed(0)
    prev_rest = None
    for r in range(IN_ROWS):
        p_gla = _dot(h, w_ref[:, 0:C_QK_END])
        if prev_rest is not None:
            tail_rest(r - 1, prev_rest)
        p_rest = _dot(h, w_ref[:, C_QK_END:IN_PAD])
        if r + 1 < IN_ROWS:
            h = normed(r + 1)
        tail_gla(r, p_gla)
        prev_rest = p_rest
    tail_rest(IN_ROWS - 1, prev_rest)


def _stream_specs(lat_skip, first=0, rows=None):
    return [pl.BlockSpec((rows, TOK_BLOCK, D_MODEL), lambda b, j: (b, 0, 0)),
            pl.BlockSpec((rows, TOK_BLOCK, D_MODEL), lambda b, j: (b, jnp.maximum(j + first - lat_skip, 0), 0))]


def _inproj(x_ctx, x_rest, lat_skip, modsel, n1w, w_in_p, wa_cat, ba_cat, qscale, cos_t, sa_t, sb_t, layer):
    bsz = x_ctx.shape[0]
    tt = cos_t.shape[0]
    nb = tt // TOK_BLOCK
    const = lambda b, j: (0, 0)
    lyr3 = lambda b, j: (layer, 0, 0)
    tok = lambda w: pl.BlockSpec((IN_ROWS, TOK_BLOCK, w), lambda b, j: (b, j, 0))
    return pl.pallas_call(
        functools.partial(_inproj_kernel, split=lat_skip > 0),
        grid=(bsz // IN_ROWS, nb),
        in_specs=_stream_specs(lat_skip, rows=IN_ROWS) + [
            pl.BlockSpec((IN_ROWS, 2, 6 * D_MODEL), lambda b, j: (b, 0, 0)),
            pl.BlockSpec((1, D_MODEL), const),
            pl.BlockSpec((None, D_MODEL, IN_PAD), lyr3),
            pl.BlockSpec((None, 128, 2 * GLA_KPAD), lyr3),
            pl.BlockSpec((1, 2 * GLA_KPAD), const),
            pl.BlockSpec((1, 2 * GLA_KPAD), const),
            pl.BlockSpec((TOK_BLOCK, 128), lambda b, j: (j, 0)),
            pl.BlockSpec((TOK_BLOCK, 128), lambda b, j: (j, 0)),
            pl.BlockSpec((TOK_BLOCK, 128), lambda b, j: (j, 0)),
        ],
        out_specs=[tok(GLA_WIDTH), tok(GLA_WIDTH), tok(2 * GLA_KPAD), tok(2 * GLA_KPAD), tok(3 * ATT_WIDTH),
                   pl.BlockSpec((IN_ROWS, CH_BLOCK, S5_CW), lambda b, j: (b, j, 0))],
        out_shape=[
            jax.ShapeDtypeStruct((bsz, tt, GLA_WIDTH), BF16),
            jax.ShapeDtypeStruct((bsz, tt, GLA_WIDTH), F32),
            jax.ShapeDtypeStruct((bsz, tt, 2 * GLA_KPAD), F32),
            jax.ShapeDtypeStruct((bsz, tt, 2 * GLA_KPAD), F32),
            jax.ShapeDtypeStruct((bsz, tt, 3 * ATT_WIDTH), BF16),
            jax.ShapeDtypeStruct((bsz, tt // S5_LC, S5_CW), F32),
        ],
        scratch_shapes=[pltpu.VMEM((IN_ROWS, S5_WIDTH // 128, TOK_BLOCK, 128), F32)],
        compiler_params=pltpu.CompilerParams(
            dimension_semantics=("arbitrary", "arbitrary"), vmem_limit_bytes=VMEM_LIMIT),
        name="inproj",
    )(x_ctx, x_rest, modsel, n1w, w_in_p, wa_cat, ba_cat, qscale, cos_t, sa_t, sb_t)


def _gla_kernel(qkf_ref, vf_ref, laf_ref, qkb_ref, vb_ref, lab_ref, tri_ref, tril_ref, hm_ref, vm_ref, bd_ref,
                of_ref, ob_ref, stf_ref, stb_ref):
    j = pl.program_id(1)

    @pl.when(j == 0)
    def _():
        stf_ref[...] = jnp.zeros_like(stf_ref)
        stb_ref[...] = jnp.zeros_like(stb_ref)

    n_chunks = TOK_BLOCK // GLA_CHUNK
    chunk = lambda t, c: t[c * GLA_CHUNK:(c + 1) * GLA_CHUNK]
    dirs = []
    for r in range(GLA_ROWS):
        dirs.append(((qkf_ref.at[r], vf_ref.at[r], laf_ref.at[r]), 0, GLA_CHUNK - 1, of_ref.at[r],
                     stf_ref.at[r], list(range(n_chunks))))
        dirs.append(((qkb_ref.at[r], vb_ref.at[r], lab_ref.at[r]), 1, 0, ob_ref.at[r],
                     stb_ref.at[r], list(reversed(range(n_chunks)))))

    cum = []
    for (_, _, la_ref), d, _, _, _, _ in dirs:
        la = la_ref[...]
        hi = la.astype(BF16)
        lo = (la - hi.astype(F32)).astype(BF16)
        cum.append(_dot(tri_ref[d], hi) + _dot(tri_ref[d], lo))

    ops = []
    for ((qk_ref, v_ref, _), d, last_row, _, _, _), b in zip(dirs, cum):
        q = qk_ref[:, 0:GLA_KPAD]
        k = qk_ref[:, GLA_KPAD:2 * GLA_KPAD]
        vb = v_ref[...]
        bl = [chunk(b, c)[last_row:last_row + 1, :] for c in range(n_chunks)]
        blx = jnp.concatenate([jnp.broadcast_to(t, (GLA_CHUNK, GLA_KPAD)) for t in bl], axis=0)
        qb = (q * jnp.exp(b)).astype(BF16)
        kb = (k * jnp.exp(-b)).astype(BF16)
        kd = (k * jnp.exp(blx - b)).astype(BF16)
        ops.append((qb, kb, kd, vb, bl))

    def chunk_scores(si, c):
        qb, kb = ops[si][0], ops[si][1]
        keep = tril_ref[dirs[si][1]] > 0
        kst = jnp.concatenate([chunk(kb, c) * hm_ref[h] for h in range(GLA_HEADS)], axis=0)
        return jnp.where(keep, _dot_nt(chunk(qb, c), kst), 0.0).astype(BF16)

    bdmask = bd_ref[...]
    states = [d[4][...] for d in dirs]
    nxt = [chunk_scores(si, d[5][0]) for si, d in enumerate(dirs)]
    for i in range(n_chunks):
        cur, nxt = nxt, [None] * len(dirs)
        for si, ((_, _, _, o_ref, _, order), (qb, _, kd, vb, bl)) in enumerate(zip(dirs, ops)):
            c = order[i]
            vc = chunk(vb, c)
            inc = _dot_tn(vc, chunk(kd, c)) * bdmask
            if i + 1 < n_chunks:
                nxt[si] = chunk_scores(si, order[i + 1])
            vbd = jnp.concatenate([vc * vm_ref[h] for h in range(GLA_HEADS)], axis=0)
            o_ref[c * GLA_CHUNK:(c + 1) * GLA_CHUNK, :] = (
                _dot(cur[si], vbd) + _dot_nt(chunk(qb, c), states[si].astype(BF16)))
            states[si] = states[si] * jnp.exp(bl[c]) + inc
    for d, st in zip(dirs, states):
        d[4][...] = st


def _gla_masks():
    r = np.arange(GLA_CHUNK)
    lower = (r[None, :] <= r[:, None]).astype(np.float32)
    tri1 = np.stack([lower, lower.T])
    n_chunks = TOK_BLOCK // GLA_CHUNK
    tri = np.stack([np.kron(np.eye(n_chunks, dtype=np.float32), t) for t in tri1])
    tril = np.tile(tri1, (1, 1, GLA_HEADS))
    klane = np.arange(GLA_KPAD)
    vlane = np.arange(GLA_WIDTH)
    hm = np.stack([np.broadcast_to((klane // GLA_DK) == h, (GLA_CHUNK, GLA_KPAD)) for h in range(GLA_HEADS)])
    vm = np.stack([np.broadcast_to((vlane // GLA_DV) == h, (GLA_CHUNK, GLA_WIDTH)) for h in range(GLA_HEADS)])
    bd = ((vlane[:, None] // GLA_DV) == (klane[None, :] // GLA_DK)).astype(np.float32)
    return (jnp.asarray(tri, BF16), jnp.asarray(tril, F32), jnp.asarray(hm, BF16), jnp.asarray(vm, BF16),
            jnp.asarray(bd))


def _scan_block(d, j, n_ctx, n_all):
    bwd = jnp.where(j < n_ctx, n_ctx - 1 - j, n_all + n_ctx - 1 - j)
    return jnp.where(d == 0, j, bwd)


def _gla(gqk, gv, gla):
    bsz, tt, _ = gv.shape
    nb = tt // TOK_BLOCK
    n_ctx = CTX_LEN // TOK_BLOCK
    tri, tril, hm, vm, bd = _gla_masks()
    const2 = lambda b, j: (0, 0)
    const3 = lambda b, j: (0, 0, 0)
    fwd = lambda b, j: (b, j, 0)
    bwd = lambda b, j: (b, _scan_block(1, j, n_ctx, nb), 0)
    bwd_gate = lambda b, j: (b, _scan_block(1, j, n_ctx, nb), 1)
    tok = lambda w, imap: pl.BlockSpec((GLA_ROWS, TOK_BLOCK, w), imap)
    return pl.pallas_call(
        _gla_kernel,
        grid=(bsz // GLA_ROWS, nb),
        in_specs=[
            tok(2 * GLA_KPAD, fwd),
            tok(GLA_WIDTH, fwd),
            tok(GLA_KPAD, fwd),
            tok(2 * GLA_KPAD, bwd),
            tok(GLA_WIDTH, bwd),
            tok(GLA_KPAD, bwd_gate),
            pl.BlockSpec((2, TOK_BLOCK, TOK_BLOCK), const3),
            pl.BlockSpec((2, GLA_CHUNK, GLA_HEADS * GLA_CHUNK), const3),
            pl.BlockSpec((GLA_HEADS, GLA_CHUNK, GLA_KPAD), const3),
            pl.BlockSpec((GLA_HEADS, GLA_CHUNK, GLA_WIDTH), const3),
            pl.BlockSpec((GLA_WIDTH, GLA_KPAD), const2),
        ],
        out_specs=[tok(GLA_WIDTH, fwd), tok(GLA_WIDTH, bwd)],
        out_shape=[jax.ShapeDtypeStruct((bsz, tt, GLA_WIDTH), F32)] * 2,
        scratch_shapes=[pltpu.VMEM((GLA_ROWS, GLA_WIDTH, GLA_KPAD), F32)] * 2,
        compiler_params=pltpu.CompilerParams(
            dimension_semantics=("arbitrary", "arbitrary"), vmem_limit_bytes=VMEM_LIMIT),
        name="gla",
    )(gqk, gv, gla, gqk, gv, gla, tri, tril, hm, vm, bd)


def _swa_kernel(sink_ref, mask_ref, att_ref, o_ref, *, tt):
    j = pl.program_id(1)
    n_sub = TOK_BLOCK // ATT_BLOCK
    n_ctx_blocks = CTX_LEN // TOK_BLOCK
    last_q = tt // ATT_BLOCK - 1
    lane = lax.broadcasted_iota(jnp.int32, (ATT_BLOCK, 2 * ATT_HD), 1)
    left = lane < ATT_HD
    row2 = lax.broadcasted_iota(jnp.int32, (2 * ATT_BLOCK, 1), 0)
    units = [(r, sub, p) for r in range(SWA_ROWS) for sub in range(n_sub) for p in range(ATT_HEADS // 2)]

    def run(windowed):
        def score(r, sub, p):
            q0 = pl.multiple_of(j * TOK_BLOCK + sub * ATT_BLOCK, ATT_BLOCK)
            qp = att_ref[r, pl.ds(q0, ATT_BLOCK), p * 128:(p + 1) * 128]
            zero = jnp.zeros_like(qp)
            q2 = jnp.concatenate([jnp.where(left, qp, zero), jnp.where(left, zero, qp)], axis=0)
            kcol = ATT_WIDTH + p * 128
            s_c = _dot_nt(q2, att_ref[r, 0:CTX_LEN, kcol:kcol + 128])
            if not windowed:
                return s_c, None, None
            start = pl.multiple_of(jnp.minimum(q0 - ATT_BLOCK, tt - 3 * ATT_BLOCK), ATT_BLOCK)
            s_w = jnp.where(keeps[sub], _dot_nt(q2, att_ref[r, pl.ds(start, 3 * ATT_BLOCK), kcol:kcol + 128]),
                            NEG_INF)
            return s_c, s_w, start

        def softmax(p, s_c, s_w, start):
            sink_col = jnp.where(row2 < ATT_BLOCK, sink_ref[2 * p], sink_ref[2 * p + 1]) * LOG2E
            m = jnp.maximum(jnp.max(s_c, axis=-1, keepdims=True), sink_col)
            if windowed:
                m = jnp.maximum(m, jnp.max(s_w, axis=-1, keepdims=True))
            e_c = jnp.exp2(s_c - m)
            den = jnp.sum(e_c, axis=-1, keepdims=True) + jnp.exp2(sink_col - m)
            e_w = None
            if windowed:
                e_w = jnp.exp2(s_w - m)
                den = den + jnp.sum(e_w, axis=-1, keepdims=True)
                e_w = e_w.astype(BF16)
            return e_c.astype(BF16), e_w, den, start

        def values(r, sub, p, e_c, e_w, den, start):
            vcol = 2 * ATT_WIDTH + p * 128
            pv = _dot(e_c, att_ref[r, 0:CTX_LEN, vcol:vcol + 128])
            if windowed:
                pv = pv + _dot(e_w, att_ref[r, pl.ds(start, 3 * ATT_BLOCK), vcol:vcol + 128])
            o2 = pv / den
            o = jnp.where(left, o2[0:ATT_BLOCK], o2[ATT_BLOCK:2 * ATT_BLOCK])
            o_ref[r, sub * ATT_BLOCK:(sub + 1) * ATT_BLOCK, p * 128:(p + 1) * 128] = o.astype(BF16)

        keeps = []
        if windowed:
            for sub in range(n_sub):
                qi = j * n_sub + sub
                kind = jnp.where(qi == CTX_LEN // ATT_BLOCK, 0, jnp.where(qi == last_q, 2, 1))
                keeps.append(mask_ref[kind] > 0)

        n = len(units)
        sc, pr = {}, {}
        for t in range(n + 2):
            if t < n:
                sc[t] = score(*units[t])
            if 0 <= t - 1 < n:
                pr[t - 1] = softmax(units[t - 1][2], *sc.pop(t - 1))
            if 0 <= t - 2 < n:
                values(*units[t - 2], *pr.pop(t - 2))

    @pl.when(j < n_ctx_blocks)
    def _():
        run(False)

    @pl.when(j >= n_ctx_blocks)
    def _():
        run(True)


def _swa_masks():
    r = np.arange(2 * ATT_BLOCK)[:, None] % ATT_BLOCK
    c = np.arange(3 * ATT_BLOCK)[None, :]
    near = lambda delta: np.abs(r - c + delta) <= WINDOW
    first = near(ATT_BLOCK) & (c >= ATT_BLOCK)
    return jnp.asarray(np.stack([first, near(ATT_BLOCK), near(2 * ATT_BLOCK)]), F32)


def _swa(att, sink):
    bsz, tt, _ = att.shape
    nb = tt // TOK_BLOCK
    return pl.pallas_call(
        functools.partial(_swa_kernel, tt=tt),
        grid=(bsz // SWA_ROWS, nb),
        in_specs=[
            pl.BlockSpec(memory_space=pltpu.SMEM),
            pl.BlockSpec((3, 2 * ATT_BLOCK, 3 * ATT_BLOCK), lambda b, j: (0, 0, 0)),
            pl.BlockSpec((SWA_ROWS, tt, 3 * ATT_WIDTH), lambda b, j: (b, 0, 0)),
        ],
        out_specs=pl.BlockSpec((SWA_ROWS, TOK_BLOCK, ATT_WIDTH), lambda b, j: (b, j, 0)),
        out_shape=jax.ShapeDtypeStruct((bsz, tt, ATT_WIDTH), BF16),
        compiler_params=pltpu.CompilerParams(
            dimension_semantics=("arbitrary", "arbitrary"), vmem_limit_bytes=VMEM_LIMIT),
        name="swa",
    )(sink, _swa_masks(), att)


def _s5_prep_kernel(lr_ref, li_ref, ls_ref, br_ref, bi_ref, cr_ref, ci_ref,
                    a16_ref, mre_ref, mim_ref, rre_ref, rim_ref, k_ref):
    rows = S5_LC * S5_GROUP
    rowg = lax.broadcasted_iota(jnp.int32, (rows, S5_NS), 0) >> 4
    colg = lax.broadcasted_iota(jnp.int32, (rows, S5_NS), 1) >> 6
    same_group = rowg == colg
    taps_t = []
    for d in range(2):
        lr = jnp.minimum(lr_ref[d], -1e-4)
        li = li_ref[d]
        dt = jnp.exp(ls_ref[d])
        mag = jnp.exp(lr * dt)
        ar = mag * jnp.cos(li * dt)
        ai = mag * jnp.sin(li * dt)
        den = lr * lr + li * li
        fr = ((ar - 1.0) * lr + ai * li) / den
        fi = (ai * lr - (ar - 1.0) * li) / den
        br = br_ref[d]
        bi = bi_ref[d]
        bbr = fr * br - fi * bi
        bbi = fr * bi + fi * br
        cr = cr_ref[d]
        ci = ci_ref[d]
        pr = jnp.ones_like(ar)
        pi = jnp.zeros_like(ar)
        cp_re, cp_im = [], []
        for tau in range(S5_LC + 1):
            cpr = cr * pr - ci * pi
            cpi = cr * pi + ci * pr
            if tau < S5_LC:
                mre_ref[d, tau] = pr * bbr - pi * bbi
                mim_ref[d, tau] = pr * bbi + pi * bbr
                cp_re.append(cpr)
                cp_im.append(cpi)
            if tau >= 1:
                rre_ref[d, tau - 1] = cpr
                rim_ref[d, tau - 1] = -cpi
            if tau == S5_LC:
                a16_ref[d, 0:1, :] = pr
                a16_ref[d, 1:2, :] = pi
            pr, pi = pr * ar - pi * ai, pr * ai + pi * ar
        bd_re = jnp.where(same_group, jnp.concatenate([bbr] * S5_GROUPS, axis=0), 0.0)
        bd_im = jnp.where(same_group, jnp.concatenate([bbi] * S5_GROUPS, axis=0), 0.0)
        if d == 1:
            cp_re, cp_im = cp_re[::-1], cp_im[::-1]
        hp = lax.Precision.HIGHEST
        dims = (((1,), (1,)), ((), ()))
        taps_t.append(
            lax.dot_general(bd_re, jnp.concatenate(cp_re, axis=0), dims, precision=hp, preferred_element_type=F32)
            - lax.dot_general(bd_im, jnp.concatenate(cp_im, axis=0), dims, precision=hp,
                              preferred_element_type=F32))
    kf, kb_rev = taps_t
    tap0 = lax.broadcasted_iota(jnp.int32, kf.shape, 1) < S5_GROUP
    k_ref[0] = jnp.where(tap0, kf + pltpu.roll(kb_rev, S5_GROUP, 1), kf)
    k_ref[1] = kb_rev


def _s5_tables(fwd, bwd):
    n_layers = fwd[0].shape[0]
    both = lambda i: jnp.stack([fwd[i], bwd[i]], axis=1)
    lam_re = both(0).reshape(n_layers, 2, 1, S5_NS)
    lam_im = both(1).reshape(n_layers, 2, 1, S5_NS)
    log_step = jnp.repeat(both(2), S5_STATE, axis=-1).reshape(n_layers, 2, 1, S5_NS)
    b_hn = lambda t: t.transpose(0, 1, 4, 2, 3).reshape(n_layers, 2, S5_GROUP, S5_NS)
    c_hn = lambda t: t.transpose(0, 1, 3, 2, 4).reshape(n_layers, 2, S5_GROUP, S5_NS)
    vec = pl.BlockSpec((None, 2, 1, S5_NS), lambda l: (l, 0, 0, 0))
    mat = pl.BlockSpec((None, 2, S5_GROUP, S5_NS), lambda l: (l, 0, 0, 0))
    tab = pl.BlockSpec((None, 2, S5_LC, S5_GROUP, S5_NS), lambda l: (l, 0, 0, 0, 0))
    taps = S5_LC * S5_GROUP
    a16, mre, mim, rre, rim, k = pl.pallas_call(
        _s5_prep_kernel,
        grid=(n_layers,),
        in_specs=[vec, vec, vec, mat, mat, mat, mat],
        out_specs=[
            pl.BlockSpec((None, 2, 2, S5_NS), lambda l: (l, 0, 0, 0)),
            tab, tab, tab, tab,
            pl.BlockSpec((None, 2, S5_WIDTH, taps), lambda l: (l, 0, 0, 0)),
        ],
        out_shape=[
            jax.ShapeDtypeStruct((n_layers, 2, 2, S5_NS), F32),
            jax.ShapeDtypeStruct((n_layers, 2, S5_LC, S5_GROUP, S5_NS), F32),
            jax.ShapeDtypeStruct((n_layers, 2, S5_LC, S5_GROUP, S5_NS), F32),
            jax.ShapeDtypeStruct((n_layers, 2, S5_LC, S5_GROUP, S5_NS), F32),
            jax.ShapeDtypeStruct((n_layers, 2, S5_LC, S5_GROUP, S5_NS), F32),
            jax.ShapeDtypeStruct((n_layers, 2, S5_WIDTH, taps), F32),
        ],
        compiler_params=pltpu.CompilerParams(dimension_semantics=("arbitrary",), vmem_limit_bytes=VMEM_LIMIT),
        name="s5_prep",
    )(lam_re, lam_im, log_step, b_hn(both(3)), b_hn(both(4)), c_hn(both(5)), c_hn(both(6)))

    n_pairs = S5_GROUPS // 2
    lane_group = np.arange(128) // S5_STATE

    def pair_rows(t, flip):
        t = t[:, ::-1] if flip else t
        t = t.reshape(n_layers, S5_LC, S5_GROUP, n_pairs, 128).transpose(0, 3, 1, 2, 4)
        own = jnp.asarray(lane_group[None, :] == np.arange(2)[:, None])
        t = jnp.where(own[None, None, :, None, None, :], t[:, :, None], 0.0)
        return t.reshape(n_layers, n_pairs, 2 * taps, 128)

    m_f = jnp.concatenate([pair_rows(mre[:, 0], True), pair_rows(mim[:, 0], True)], axis=-1).astype(BF16)
    m_b = jnp.concatenate([pair_rows(mre[:, 1], False), pair_rows(mim[:, 1], False)], axis=-1).astype(BF16)

    r_t = jnp.concatenate([pair_rows(rre[:, 0], False), pair_rows(rim[:, 0], False),
                           pair_rows(rre[:, 1], True), pair_rows(rim[:, 1], True)], axis=-1).astype(BF16)

    zeros = jnp.zeros((n_layers, S5_WIDTH, taps), F32)
    f2 = jnp.concatenate([zeros, k[:, 0]], axis=-1)
    b2 = jnp.concatenate([k[:, 1], zeros], axis=-1)
    t_f = jnp.stack([f2[..., taps - S5_GROUP * st:2 * taps - S5_GROUP * st] for st in range(S5_LC)], axis=2)
    t_b = jnp.stack([b2[..., S5_GROUP * (S5_LC - 1 - st):S5_GROUP * (S5_LC - 1 - st) + taps]
                     for st in range(S5_LC)], axis=2)
    later = jnp.asarray(np.arange(taps)[None, :] // S5_GROUP >= np.arange(S5_LC)[:, None])
    toe = jnp.where(later[None, None], t_f, t_b)
    toe = toe.reshape(n_layers, S5_GROUPS, S5_GROUP, S5_LC, taps).transpose(0, 1, 3, 2, 4)
    toe = toe.reshape(n_layers, S5_GROUPS, taps, taps).astype(BF16)
    return a16, m_f, m_b, r_t, toe


def _lane_block_transpose(cols):
    lane = lax.broadcasted_iota(jnp.int32, cols[0].shape, 1)
    out = [None] * 32
    for ah in range(2):
        for bh in range(2):
            v = [cols[(ah * 8 + al) * 2 + bh] for al in range(8)]
            for kbit in range(3):
                width = 16 << kbit
                low = ((lane >> (4 + kbit)) & 1) == 0
                nxt = list(v)
                for i in range(8):
                    if i & (1 << kbit):
                        continue
                    lo_v, hi_v = v[i], v[i | (1 << kbit)]
                    nxt[i] = jnp.where(low, lo_v, pltpu.roll(hi_v, width, 1))
                    nxt[i | (1 << kbit)] = jnp.where(low, pltpu.roll(lo_v, 128 - width, 1), hi_v)
                v = nxt
            for bl in range(8):
                out[(bh * 8 + bl) * 2 + ah] = v[bl]
    return out


def _s5_increments(ush, m_ref, d_re, d_im):
    for p in range(S5_GROUPS // 2):
        dp = _dot(ush[:, p * 512:(p + 1) * 512], m_ref[p])
        d_re[p] = dp[:, 0:128]
        d_im[p] = dp[:, 128:256]


def _s5_recurrence(a16_ref, d_re, d_im, x_re, x_im, sr_ref, si_ref, order, bsz):
    ar = a16_ref[0:1, :]
    ai = a16_ref[1:2, :]
    sr = sr_ref[...]
    si = si_ref[...]
    n_slabs = S5_NS // 128
    gather = lambda ref, rows: jnp.concatenate([ref[p, rows, :] for p in range(n_slabs)], axis=1)
    for c in order:
        rows = pl.ds(c, bsz, stride=CH_BLOCK)
        for p in range(n_slabs):
            x_re[p, rows, :] = sr[:, p * 128:(p + 1) * 128]
            x_im[p, rows, :] = si[:, p * 128:(p + 1) * 128]
        sr, si = ar * sr - ai * si + gather(d_re, rows), ar * si + ai * sr + gather(d_im, rows)
    sr_ref[...] = sr
    si_ref[...] = si


def _s5_fwd_kernel(uc_ref, a16_ref, m_ref, ush_ref, xin_ref, d_re, d_im, x_re, x_im, sr_ref, si_ref, *, bsz):
    @pl.when(pl.program_id(0) == 0)
    def _():
        sr_ref[...] = jnp.zeros_like(sr_ref)
        si_ref[...] = jnp.zeros_like(si_ref)

    rows = bsz * CH_BLOCK
    ub = uc_ref[...].reshape(rows, S5_CW).astype(BF16)
    packed = pltpu.bitcast(ub, jnp.uint32)
    cols = _lane_block_transpose([packed[:, v * 128:(v + 1) * 128] for v in range(32)])
    ush = pltpu.bitcast(jnp.concatenate(cols, axis=1), BF16)
    ush_ref[...] = ush.reshape(bsz, CH_BLOCK, S5_CW)
    _s5_increments(ush, m_ref, d_re, d_im)
    _s5_recurrence(a16_ref, d_re, d_im, x_re, x_im, sr_ref, si_ref, range(CH_BLOCK), bsz)
    for p in range(S5_NS // 128):
        xin_ref[:, :, p * 128:(p + 1) * 128] = x_re[p].astype(BF16).reshape(bsz, CH_BLOCK, 128)
        xin_ref[:, :, S5_NS + p * 128:S5_NS + (p + 1) * 128] = x_im[p].astype(BF16).reshape(bsz, CH_BLOCK, 128)


def _s5_bwd_kernel(ush_ref, xf_ref, a16_ref, m_ref, toe_ref, r_ref, y_ref, d_re, d_im, x_re, x_im, sr_ref, si_ref,
                   *, bsz):
    @pl.when(pl.program_id(0) == 0)
    def _():
        sr_ref[...] = jnp.zeros_like(sr_ref)
        si_ref[...] = jnp.zeros_like(si_ref)

    rows = bsz * CH_BLOCK
    ush = ush_ref[...].reshape(rows, S5_CW)
    _s5_increments(ush, m_ref, d_re, d_im)
    _s5_recurrence(a16_ref, d_re, d_im, x_re, x_im, sr_ref, si_ref, reversed(range(CH_BLOCK)), bsz)
    xf = xf_ref[...].reshape(rows, 2 * S5_NS)
    ycols = []
    for p in range(S5_GROUPS // 2):
        lanes = slice(p * 128, (p + 1) * 128)
        xcat = jnp.concatenate([xf[:, lanes], xf[:, S5_NS + p * 128:S5_NS + (p + 1) * 128],
                                x_re[p].astype(BF16), x_im[p].astype(BF16)], axis=1)
        carry = _dot_nt(xcat, r_ref[p])
        for g2 in range(2):
            g = 2 * p + g2
            yg = carry[:, g2 * 256:(g2 + 1) * 256] + _dot(ush[:, g * 256:(g + 1) * 256], toe_ref[g])
            ycols += [yg[:, 0:128], yg[:, 128:256]]
    ycols = _lane_block_transpose(ycols)
    y_ref[...] = jnp.concatenate(ycols, axis=1).reshape(bsz, CH_BLOCK, S5_CW)


def _s5(u_c, tables, layer):
    a16, m_f, m_b, r_t, toe = tables
    bsz, n_rows, _ = u_c.shape
    nb = n_rows // CH_BLOCK
    n_ctx = CTX_LEN // TOK_BLOCK
    rows = bsz * CH_BLOCK
    blk3 = lambda w: (bsz, CH_BLOCK, w)
    scratch = [pltpu.VMEM((S5_NS // 128, rows, 128), F32)] * 4 + [pltpu.VMEM((bsz, S5_NS), F32)] * 2
    params = pltpu.CompilerParams(dimension_semantics=("arbitrary",), vmem_limit_bytes=VMEM_LIMIT)
    ush, xin_f = pl.pallas_call(
        functools.partial(_s5_fwd_kernel, bsz=bsz),
        grid=(nb,),
        in_specs=[
            pl.BlockSpec(blk3(S5_CW), lambda j: (0, j, 0)),
            pl.BlockSpec((None, None, 2, S5_NS), lambda j: (layer, 0, 0, 0)),
            pl.BlockSpec((None, S5_GROUPS // 2, 512, 256), lambda j: (layer, 0, 0, 0)),
        ],
        out_specs=[
            pl.BlockSpec(blk3(S5_CW), lambda j: (0, j, 0)),
            pl.BlockSpec(blk3(2 * S5_NS), lambda j: (0, j, 0)),
        ],
        out_shape=[
            jax.ShapeDtypeStruct((bsz, n_rows, S5_CW), BF16),
            jax.ShapeDtypeStruct((bsz, n_rows, 2 * S5_NS), BF16),
        ],
        scratch_shapes=scratch,
        compiler_params=params,
        name="s5_fwd",
    )(u_c, a16, m_f)
    back = lambda j: (0, _scan_block(1, j, n_ctx, nb), 0)
    return pl.pallas_call(
        functools.partial(_s5_bwd_kernel, bsz=bsz),
        grid=(nb,),
        in_specs=[
            pl.BlockSpec(blk3(S5_CW), back),
            pl.BlockSpec(blk3(2 * S5_NS), back),
            pl.BlockSpec((None, None, 2, S5_NS), lambda j: (layer, 1, 0, 0)),
            pl.BlockSpec((None, S5_GROUPS // 2, 512, 256), lambda j: (layer, 0, 0, 0)),
            pl.BlockSpec((None, S5_GROUPS, 256, 256), lambda j: (layer, 0, 0, 0)),
            pl.BlockSpec((None, S5_GROUPS // 2, 512, 512), lambda j: (layer, 0, 0, 0)),
        ],
        out_specs=pl.BlockSpec(blk3(S5_CW), back),
        out_shape=jax.ShapeDtypeStruct((bsz, n_rows, S5_CW), F32),
        scratch_shapes=scratch,
        compiler_params=params,
        name="s5_bwd",
    )(ush, xin_f, a16, m_b, toe, r_t)


def _out_kernel(xc_ref, x_ref, mod_ref, gof_ref, gob_ref, g_ref, att_ref, y_ref, uc_ref, gnw_ref, hm_ref, d_ref,
                gluw_ref, glub_ref, wo_ref, n2_ref, w1_ref, w2_ref, fn_ref, o_ref, ys_ref,
                *, first_block, final, split):
    j = pl.program_id(1) + first_block

    def mixed(r):
        m = mod_ref[r]
        mrow = jnp.where(j > 0, m[1:2, :], m[0:1, :])
        g1 = mrow[:, 2 * D_MODEL:3 * D_MODEL]
        sh2 = mrow[:, 3 * D_MODEL:4 * D_MODEL]
        sc2 = mrow[:, 4 * D_MODEL:5 * D_MODEL]
        g2 = mrow[:, 5 * D_MODEL:6 * D_MODEL]

        o = gof_ref[r] + gob_ref[r]
        ms = _dot((o * o).astype(BF16), hm_ref[...]) * (1.0 / GLA_DV)
        a = o * lax.rsqrt(ms + EPS) * gnw_ref[...] * _silu(g_ref[r])

        yc = y_ref[r] + d_ref[...] * uc_ref[r]
        for st in range(S5_LC):
            for half in range(S5_WIDTH // 128):
                lane0 = st * S5_WIDTH + half * 128
                ys_ref[r, half, pl.ds(st, CH_BLOCK, stride=S5_LC), :] = yc[:, lane0:lane0 + 128]
        yy = jnp.concatenate([ys_ref[r, half] for half in range(S5_WIDTH // 128)], axis=1)
        ge = 0.5 * yy * (1.0 + jnp.tanh(math.sqrt(2.0 / math.pi) * (yy + 0.044715 * (yy * yy * yy))))
        z = _dot(ge.astype(BF16), gluw_ref[...]) + glub_ref[...]
        s = z[:, 0:S5_WIDTH] * _sigmoid(z[:, S5_WIDTH:2 * S5_WIDTH])

        proj = _dot(jnp.concatenate([a.astype(BF16), att_ref[r], s.astype(BF16)], axis=1), wo_ref[...])
        x_in = jnp.where(j == 0, xc_ref[r], x_ref[r]) if split else x_ref[r]
        x1 = x_in + g1 * proj
        y2 = x1 * lax.rsqrt(jnp.mean(x1 * x1, axis=-1, keepdims=True) + EPS) * n2_ref[...]
        return x1, (y2 * (1.0 + sc2) + sh2).astype(BF16), g2

    def finish(r, x1, g2, mlp):
        x2 = x1 + g2 * mlp
        if final:
            x2 = x2 * lax.rsqrt(jnp.mean(x2 * x2, axis=-1, keepdims=True) + EPS) * fn_ref[...]
        o_ref[r] = x2

    cur = mixed(0)
    for r in range(OUT_ROWS):
        x1, h2, g2 = cur
        hid = jnp.maximum(_dot(h2, w1_ref[...]), 0.0)
        if r + 1 < OUT_ROWS:
            cur = mixed(r + 1)
        mlp = _dot((hid * hid).astype(BF16), w2_ref[...])
        finish(r, x1, g2, mlp)


def _outproj(x_ctx, x_rest, lat_skip, modsel, gla_f, gla_b, gg, att_o, y_c, u_c, gnw, hm, s5d, gluw, glub, wo, n2w,
             w1, w2, fnw, layer, final):
    bsz, tt, _ = gg.shape
    nb = tt // TOK_BLOCK
    first = CTX_LEN // TOK_BLOCK if final else 0
    nsteps = nb - first
    const2 = lambda b, j: (0, 0)
    lyr3 = lambda b, j: (layer, 0, 0)
    single = pl.Buffered(1)
    return pl.pallas_call(
        functools.partial(_out_kernel, first_block=first, final=final, split=lat_skip > 0),
        grid=(bsz // OUT_ROWS, nsteps),
        in_specs=_stream_specs(lat_skip, first, rows=OUT_ROWS) + [
            pl.BlockSpec((OUT_ROWS, 2, 6 * D_MODEL), lambda b, j: (b, 0, 0)),
            pl.BlockSpec((OUT_ROWS, TOK_BLOCK, GLA_WIDTH), lambda b, j: (b, j + first, 0)),
            pl.BlockSpec((OUT_ROWS, TOK_BLOCK, GLA_WIDTH), lambda b, j: (b, j + first, 0)),
            pl.BlockSpec((OUT_ROWS, TOK_BLOCK, GLA_WIDTH), lambda b, j: (b, j + first, 0)),
            pl.BlockSpec((OUT_ROWS, TOK_BLOCK, ATT_WIDTH), lambda b, j: (b, j + first, 0)),
            pl.BlockSpec((OUT_ROWS, CH_BLOCK, S5_CW), lambda b, j: (b, j + first, 0)),
            pl.BlockSpec((OUT_ROWS, CH_BLOCK, S5_CW), lambda b, j: (b, j + first, 0)),
            pl.BlockSpec((1, GLA_WIDTH), const2),
            pl.BlockSpec((GLA_WIDTH, GLA_WIDTH), const2),
            pl.BlockSpec((1, S5_CW), const2),
            pl.BlockSpec((None, S5_WIDTH, 2 * S5_WIDTH), lyr3, pipeline_mode=single),
            pl.BlockSpec((1, 2 * S5_WIDTH), const2),
            pl.BlockSpec((None, D_MODEL, D_MODEL), lyr3, pipeline_mode=single),
            pl.BlockSpec((1, D_MODEL), const2),
            pl.BlockSpec((None, D_MODEL, D_FF), lyr3, pipeline_mode=single),
            pl.BlockSpec((None, D_FF, D_MODEL), lyr3, pipeline_mode=single),
            pl.BlockSpec((1, D_MODEL), const2),
        ],
        out_specs=pl.BlockSpec((OUT_ROWS, TOK_BLOCK, D_MODEL), lambda b, j: (b, j, 0)),
        out_shape=jax.ShapeDtypeStruct((bsz, nsteps * TOK_BLOCK, D_MODEL), F32),
        scratch_shapes=[pltpu.VMEM((OUT_ROWS, S5_WIDTH // 128, TOK_BLOCK, 128), F32)],
        compiler_params=pltpu.CompilerParams(
            dimension_semantics=("arbitrary", "arbitrary"), vmem_limit_bytes=VMEM_LIMIT),
        name="outproj_mlp",
    )(x_ctx, x_rest, modsel, gla_f, gla_b, gg, att_o, y_c, u_c, gnw, hm, s5d, gluw, glub, wo, n2w, w1, w2, fnw)


def _rope_tables(tt):
    n_lat = tt - CTX_LEN
    rows = n_lat // GRID_W
    row = jnp.repeat(jnp.arange(rows, dtype=F32), GRID_W)
    col = jnp.tile(jnp.arange(GRID_W, dtype=F32), rows)
    n_freq = ATT_HD // 4
    inv_freq = ROPE_BASE ** (-jnp.arange(n_freq, dtype=F32) / n_freq)
    ang_r = row[:, None] * inv_freq
    ang_c = col[:, None] * inv_freq
    ang = jnp.concatenate([ang_r, ang_r, ang_c, ang_c], axis=-1)
    cos = jnp.concatenate([jnp.ones((CTX_LEN, ATT_HD), F32), jnp.cos(ang)], axis=0)
    sin = jnp.concatenate([jnp.zeros((CTX_LEN, ATT_HD), F32), jnp.sin(ang)], axis=0)
    up_quarter = (np.arange(ATT_HD) // 16) % 2 == 0
    sa = jnp.where(up_quarter, -sin, 0.0)
    sb = jnp.where(up_quarter, 0.0, sin)
    two = lambda t: jnp.concatenate([t, t], axis=-1)
    return two(cos), two(sa), two(sb)


def _pad_cols(w, width):
    return jnp.pad(w, ((0, 0), (0, 0), (0, width - w.shape[-1])))


def _layout_w_in(w_in):
    offs = np.cumsum([0, GLA_KW, GLA_KW, GLA_WIDTH, GLA_WIDTH, GLA_RANK, GLA_RANK, ATT_WIDTH, ATT_KVW, ATT_KVW,
                      S5_WIDTH])
    q, k, v, g, zf, zb, aq, ak, av, u = [w_in[:, :, offs[i]:offs[i + 1]] for i in range(10)]
    z = jnp.concatenate([zf, zb], axis=-1)
    cols = [v, g, _pad_cols(q, GLA_KPAD), _pad_cols(k, GLA_KPAD), _pad_cols(z, 128), aq, ak, av, u]
    return jnp.concatenate(cols, axis=-1).astype(BF16)


def kernel(x, c, ctx, c_ctx, w_mod, b_mod, norm1_w, norm2_w, w_in, gla_wa_f, gla_ba_f, gla_wa_b, gla_ba_b,
           gla_norm_w, attn_sink, s5_lam_re_f, s5_lam_im_f, s5_log_step_f, s5_b_re_f, s5_b_im_f, s5_c_re_f,
           s5_c_im_f, s5_lam_re_b, s5_lam_im_b, s5_log_step_b, s5_b_re_b, s5_b_im_b, s5_c_re_b, s5_c_im_b,
           s5_d, glu_w, glu_b, w_out, mlp_w1, mlp_w2, final_norm_w):
    bsz, seq, _ = x.shape
    n_layers = w_mod.shape[0]
    tt = CTX_LEN + seq
    assert all(bsz % rows == 0 for rows in (8, IN_ROWS, GLA_ROWS, SWA_ROWS, OUT_ROWS))
    assert seq % TOK_BLOCK == 0 and seq >= 2 * TOK_BLOCK and ctx.shape[1] == CTX_LEN

    mod_rows = -(-(bsz + 1) // 8) * 8
    cvec = jnp.zeros((mod_rows, D_MODEL), F32).at[:bsz].set(c).at[bsz].set(c_ctx)
    mod = _modulation(cvec, w_mod, b_mod)
    mod_ctx = jnp.broadcast_to(mod[:, bsz][:, None], (n_layers, bsz, 6 * D_MODEL))
    modsel = jnp.stack([mod_ctx, mod[:, :bsz]], axis=2)

    w_in_p = _layout_w_in(w_in)
    wa_cat = jnp.zeros((n_layers, 128, 2 * GLA_KPAD), F32)
    wa_cat = wa_cat.at[:, 0:GLA_RANK, 0:GLA_KW].set(gla_wa_f)
    wa_cat = wa_cat.at[:, GLA_RANK:2 * GLA_RANK, GLA_KPAD:GLA_KPAD + GLA_KW].set(gla_wa_b).astype(BF16)
    ba_cat = jnp.zeros((n_layers, 1, 2 * GLA_KPAD), F32)
    ba_cat = ba_cat.at[:, 0, 0:GLA_KW].set(gla_ba_f).at[:, 0, GLA_KPAD:GLA_KPAD + GLA_KW].set(gla_ba_b)
    qscale = jnp.ones((1, 2 * GLA_KPAD), F32).at[:, 0:GLA_KW].set(GLA_DK ** -0.5)
    cos_t, sa_t, sb_t = _rope_tables(tt)
    gnw = jnp.tile(gla_norm_w, (1, GLA_HEADS))[:, None, :]
    head = np.arange(GLA_WIDTH) // GLA_DV
    hm = jnp.asarray(head[:, None] == head[None, :], BF16)
    wo_b = w_out.astype(BF16)
    w1_b = mlp_w1.astype(BF16)
    w2_b = mlp_w2.astype(BF16)
    gluw_b = glu_w.astype(BF16)

    s5_tab = _s5_tables(
        (s5_lam_re_f, s5_lam_im_f, s5_log_step_f, s5_b_re_f, s5_b_im_f, s5_c_re_f, s5_c_im_f),
        (s5_lam_re_b, s5_lam_im_b, s5_log_step_b, s5_b_re_b, s5_b_im_b, s5_c_re_b, s5_c_im_b))
    s5_dt = jnp.tile(s5_d, (1, S5_LC))[:, None, :]

    stream = (ctx, x, CTX_LEN // TOK_BLOCK)
    for l in range(n_layers):
        final = l == n_layers - 1
        gv, gg, gqk, gla, att, u_c = _inproj(*stream, modsel[l], norm1_w[l][None], w_in_p, wa_cat, ba_cat[l],
                                             qscale, cos_t, sa_t, sb_t, l)
        gla_f, gla_b = _gla(gqk, gv, gla)
        att_o = _swa(att, attn_sink[l])
        y_c = _s5(u_c, s5_tab, l)
        xs = _outproj(*stream, modsel[l], gla_f, gla_b, gg, att_o, y_c, u_c,
                      gnw[l], hm, s5_dt[l], gluw_b, glu_b[l][None], wo_b, norm2_w[l][None], w1_b, w2_b,
                      final_norm_w[None], l, final)
        stream = (xs, xs, 0)
    return xs
```

```python
import functools
import math

import jax
import jax.numpy as jnp
import numpy as np
from jax import lax
from jax.experimental import pallas as pl
from jax.experimental.pallas import tpu as pltpu

F32 = jnp.float32
BF16 = jnp.bfloat16

D_MODEL = 1024
D_FF = 4 * D_MODEL
CTX_LEN = 256
GRID_W = 64
EPS = 1e-6
NEG_INF = -1e30
LOG2E = math.log2(math.e)

GLA_HEADS = 4
GLA_DV = 96
GLA_DK = 48
GLA_WIDTH = GLA_HEADS * GLA_DV
GLA_KW = GLA_HEADS * GLA_DK
GLA_KPAD = 256
GLA_RANK = 16
GLA_TAU = 16.0
GLA_CHUNK = 64

ATT_HD = 64
ATT_HEADS = 6
ATT_KV_HEADS = 2
ATT_WIDTH = ATT_HEADS * ATT_HD
ATT_KVW = ATT_KV_HEADS * ATT_HD
WINDOW = 128
ATT_BLOCK = 128
ROPE_BASE = 10000.0

S5_WIDTH = 256
S5_GROUP = 16
S5_GROUPS = 16
S5_STATE = 64
S5_NS = S5_GROUPS * S5_STATE
S5_LC = 16
S5_CW = S5_LC * S5_WIDTH

TOK_BLOCK = 256
CH_BLOCK = TOK_BLOCK // S5_LC
GLA_ROWS = 4
SWA_ROWS = 2
IN_ROWS = 4
IN_RING = 3
OUT_ROWS = 2

C_V, C_G, C_Q, C_K, C_Z, C_AQ, C_AK, C_AV, C_U = 0, 384, 768, 1024, 1280, 1408, 1792, 1920, 2048
IN_PAD = 2304
C_QK_END = C_K + GLA_KPAD

V7X_VMEM_BYTES = 64 * 1024 * 1024
VMEM_LIMIT = V7X_VMEM_BYTES - 8 * 1024 * 1024


def _sigmoid(x):
    return 1.0 / (1.0 + jnp.exp(-x))


def _silu(x):
    return x * _sigmoid(x)


def _dot(a, b):
    return jnp.dot(a, b, preferred_element_type=F32)


def _dot_nt(a, b):
    return lax.dot_general(a, b, (((1,), (1,)), ((), ())), preferred_element_type=F32)


def _dot_tn(a, b):
    return lax.dot_general(a, b, (((0,), (0,)), ((), ())), preferred_element_type=F32)


def _mod_kernel(c_ref, w_ref, b_ref, o_ref):
    a = _silu(c_ref[...]).astype(BF16)
    o_ref[...] = _dot(a, w_ref[...].astype(BF16)) + b_ref[...]


def _modulation(cvec, w_mod, b_mod):
    n_layers = w_mod.shape[0]
    rows = cvec.shape[0]
    tn = 1536
    return pl.pallas_call(
        _mod_kernel,
        grid=(n_layers, 6 * D_MODEL // tn),
        in_specs=[
            pl.BlockSpec((rows, D_MODEL), lambda l, n: (0, 0)),
            pl.BlockSpec((None, D_MODEL, tn), lambda l, n: (l, 0, n)),
            pl.BlockSpec((None, 1, tn), lambda l, n: (l, 0, n)),
        ],
        out_specs=pl.BlockSpec((None, rows, tn), lambda l, n: (l, 0, n)),
        out_shape=jax.ShapeDtypeStruct((n_layers, rows, 6 * D_MODEL), F32),
        compiler_params=pltpu.CompilerParams(
            dimension_semantics=("arbitrary", "arbitrary"), vmem_limit_bytes=VMEM_LIMIT),
        name="modulation",
    )(cvec, w_mod, b_mod.reshape(n_layers, 1, 6 * D_MODEL))


def _inproj_kernel(xc_ref, x_hbm, mod_ref, n1_ref, w_ref, wa_ref, ba_ref, qs_ref, cos_ref, sa_ref, sb_ref,
                   gv_ref, gg_ref, gqk_ref, gla_ref, att_ref, uc_ref, us_ref, xbuf, xsem, *, split, lat_skip):
    bi = pl.program_id(0)
    j = pl.program_id(1)
    nb = pl.num_programs(1)
    step = bi * nb + j
    n_steps = pl.num_programs(0) * nb
    cos = cos_ref[...]
    sa = sa_ref[...]
    sb = sb_ref[...]

    def stream_copy(t):
        row0 = (t // nb) * IN_ROWS
        tok0 = jnp.maximum(t % nb - lat_skip, 0) * TOK_BLOCK
        slot = t % IN_RING
        return pltpu.make_async_copy(x_hbm.at[pl.ds(row0, IN_ROWS), pl.ds(tok0, TOK_BLOCK), :],
                                     xbuf.at[slot], xsem.at[slot])

    @pl.when(step == 0)
    def _():
        for t in range(IN_RING - 1):
            stream_copy(t).start()

    @pl.when(step + IN_RING - 1 < n_steps)
    def _():
        stream_copy(step + IN_RING - 1).start()

    stream_copy(step).wait()
    x_ref = xbuf.at[step % IN_RING]

    def normed(r):
        xf = jnp.where(j == 0, xc_ref[r], x_ref[r]) if split else x_ref[r]
        y = xf * lax.rsqrt(jnp.mean(xf * xf, axis=-1, keepdims=True) + EPS) * n1_ref[...]
        m = mod_ref[r]
        mrow = jnp.where(j > 0, m[1:2, :], m[0:1, :])
        sh1 = mrow[:, 0:D_MODEL]
        sc1 = mrow[:, D_MODEL:2 * D_MODEL]
        return (y * (1.0 + sc1) + sh1).astype(BF16)

    def rope(t, reps):
        w = t.shape[-1]
        c3 = jnp.concatenate([cos] * reps, axis=-1) if reps > 1 else cos
        a3 = jnp.concatenate([sa] * reps, axis=-1) if reps > 1 else sa
        b3 = jnp.concatenate([sb] * reps, axis=-1) if reps > 1 else sb
        up = pltpu.roll(t, w - 16, 1)
        dn = pltpu.roll(t, 16, 1)
        return t * c3 + up * a3 + dn * b3

    def tail_gla(r, p):
        gv_ref[r] = p[:, C_V:C_V + GLA_WIDTH].astype(BF16)
        gg_ref[r] = p[:, C_G:C_G + GLA_WIDTH]
        gqk_ref[r] = p[:, C_Q:C_QK_END] * qs_ref[...]

    def tail_rest(r, p):
        off = C_QK_END
        z = p[:, C_Z - off:C_Z - off + 128].astype(BF16)
        zg = _dot(z, wa_ref[...]) + ba_ref[...]
        gla_ref[r] = (jnp.minimum(zg, 0.0) - jnp.log1p(jnp.exp(-jnp.abs(zg)))) * (1.0 / GLA_TAU)

        aq = rope(p[:, C_AQ - off:C_AQ - off + ATT_WIDTH], 3) * (ATT_HD ** -0.5 * LOG2E)
        ak = rope(p[:, C_AK - off:C_AK - off + ATT_KVW], 1)
        av = p[:, C_AV - off:C_AV - off + ATT_KVW]
        left = lax.broadcasted_iota(jnp.int32, ak.shape, 1) < ATT_HD

        def expand(t):
            sw = pltpu.roll(t, ATT_HD, 1)
            return [jnp.where(left, t, sw), t, jnp.where(left, sw, t)]

        att_ref[r] = jnp.concatenate([aq] + expand(ak) + expand(av), axis=-1).astype(BF16)
        for half in range(S5_WIDTH // 128):
            us_ref[r, half] = p[:, C_U - off + half * 128:C_U - off + (half + 1) * 128]
        for st in range(S5_LC):
            for half in range(S5_WIDTH // 128):
                lane0 = st * S5_WIDTH + half * 128
                uc_ref[r, :, lane0:lane0 + 128] = us_ref[r, half, pl.ds(st, CH_BLOCK, stride=S5_LC), :]

    h = normed(0)
    prev_rest = None
    for r in range(IN_ROWS):
        p_gla = _dot(h, w_ref[:, 0:C_QK_END])
        if prev_rest is not None:
            tail_rest(r - 1, prev_rest)
        p_rest = _dot(h, w_ref[:, C_QK_END:IN_PAD])
        if r + 1 < IN_ROWS:
            h = normed(r + 1)
        tail_gla(r, p_gla)
        prev_rest = p_rest
    tail_rest(IN_ROWS - 1, prev_rest)


def _stream_specs(lat_skip, first=0, rows=None):
    return [pl.BlockSpec((rows, TOK_BLOCK, D_MODEL), lambda b, j: (b, 0, 0)),
            pl.BlockSpec((rows, TOK_BLOCK, D_MODEL), lambda b, j: (b, jnp.maximum(j + first - lat_skip, 0), 0))]


def _inproj(x_ctx, x_rest, lat_skip, modsel, n1w, w_in_p, wa_cat, ba_cat, qscale, cos_t, sa_t, sb_t, layer):
    bsz = x_ctx.shape[0]
    tt = cos_t.shape[0]
    nb = tt // TOK_BLOCK
    const = lambda b, j: (0, 0)
    lyr3 = lambda b, j: (layer, 0, 0)
    tok = lambda w: pl.BlockSpec((IN_ROWS, TOK_BLOCK, w), lambda b, j: (b, j, 0))
    split = lat_skip > 0
    assert (bsz // IN_ROWS) * nb >= IN_RING - 1
    ctx_rows = TOK_BLOCK if split else 8
    return pl.pallas_call(
        functools.partial(_inproj_kernel, split=split, lat_skip=lat_skip),
        grid=(bsz // IN_ROWS, nb),
        in_specs=[
            pl.BlockSpec((IN_ROWS, ctx_rows, D_MODEL), lambda b, j: (b, 0, 0)),
            pl.BlockSpec(memory_space=pl.ANY),
            pl.BlockSpec((IN_ROWS, 2, 6 * D_MODEL), lambda b, j: (b, 0, 0)),
            pl.BlockSpec((1, D_MODEL), const),
            pl.BlockSpec((None, D_MODEL, IN_PAD), lyr3),
            pl.BlockSpec((None, 128, 2 * GLA_KPAD), lyr3),
            pl.BlockSpec((1, 2 * GLA_KPAD), const),
            pl.BlockSpec((1, 2 * GLA_KPAD), const),
            pl.BlockSpec((TOK_BLOCK, 128), lambda b, j: (j, 0)),
            pl.BlockSpec((TOK_BLOCK, 128), lambda b, j: (j, 0)),
            pl.BlockSpec((TOK_BLOCK, 128), lambda b, j: (j, 0)),
        ],
        out_specs=[tok(GLA_WIDTH), tok(GLA_WIDTH), tok(2 * GLA_KPAD), tok(2 * GLA_KPAD), tok(3 * ATT_WIDTH),
                   pl.BlockSpec((IN_ROWS, CH_BLOCK, S5_CW), lambda b, j: (b, j, 0))],
        out_shape=[
            jax.ShapeDtypeStruct((bsz, tt, GLA_WIDTH), BF16),
            jax.ShapeDtypeStruct((bsz, tt, GLA_WIDTH), F32),
            jax.ShapeDtypeStruct((bsz, tt, 2 * GLA_KPAD), F32),
            jax.ShapeDtypeStruct((bsz, tt, 2 * GLA_KPAD), F32),
            jax.ShapeDtypeStruct((bsz, tt, 3 * ATT_WIDTH), BF16),
            jax.ShapeDtypeStruct((bsz, tt // S5_LC, S5_CW), F32),
        ],
        scratch_shapes=[pltpu.VMEM((IN_ROWS, S5_WIDTH // 128, TOK_BLOCK, 128), F32),
                        pltpu.VMEM((IN_RING, IN_ROWS, TOK_BLOCK, D_MODEL), F32),
                        pltpu.SemaphoreType.DMA((IN_RING,))],
        compiler_params=pltpu.CompilerParams(
            dimension_semantics=("arbitrary", "arbitrary"), vmem_limit_bytes=VMEM_LIMIT),
        name="inproj",
    )(x_ctx, x_rest, modsel, n1w, w_in_p, wa_cat, ba_cat, qscale, cos_t, sa_t, sb_t)


def _gla_kernel(qkf_ref, vf_ref, laf_ref, qkb_ref, vb_ref, lab_ref, tri_ref, tril_ref, hm_ref, vm_ref, bd_ref,
                of_ref, ob_ref, stf_ref, stb_ref):
    j = pl.program_id(1)

    @pl.when(j == 0)
    def _():
        stf_ref[...] = jnp.zeros_like(stf_ref)
        stb_ref[...] = jnp.zeros_like(stb_ref)

    n_chunks = TOK_BLOCK // GLA_CHUNK
    chunk = lambda t, c: t[c * GLA_CHUNK:(c + 1) * GLA_CHUNK]
    dirs = []
    for r in range(GLA_ROWS):
        dirs.append(((qkf_ref.at[r], vf_ref.at[r], laf_ref.at[r]), 0, GLA_CHUNK - 1, of_ref.at[r],
                     stf_ref.at[r], list(range(n_chunks))))
        dirs.append(((qkb_ref.at[r], vb_ref.at[r], lab_ref.at[r]), 1, 0, ob_ref.at[r],
                     stb_ref.at[r], list(reversed(range(n_chunks)))))

    cum = []
    for (_, _, la_ref), d, _, _, _, _ in dirs:
        la = la_ref[...]
        hi = la.astype(BF16)
        lo = (la - hi.astype(F32)).astype(BF16)
        cum.append(_dot(tri_ref[d], hi) + _dot(tri_ref[d], lo))

    ops = []
    for ((qk_ref, v_ref, _), d, last_row, _, _, _), b in zip(dirs, cum):
        q = qk_ref[:, 0:GLA_KPAD]
        k = qk_ref[:, GLA_KPAD:2 * GLA_KPAD]
        vb = v_ref[...]
        bl = [chunk(b, c)[last_row:last_row + 1, :] for c in range(n_chunks)]
        blx = jnp.concatenate([jnp.broadcast_to(t, (GLA_CHUNK, GLA_KPAD)) for t in bl], axis=0)
        qb = (q * jnp.exp(b)).astype(BF16)
        kb = (k * jnp.exp(-b)).astype(BF16)
        kd = (k * jnp.exp(blx - b)).astype(BF16)
        ops.append((qb, kb, kd, vb, bl))

    def chunk_scores(si, c):
        qb, kb = ops[si][0], ops[si][1]
        keep = tril_ref[dirs[si][1]] > 0
        kst = jnp.concatenate([chunk(kb, c) * hm_ref[h] for h in range(GLA_HEADS)], axis=0)
        return jnp.where(keep, _dot_nt(chunk(qb, c), kst), 0.0).astype(BF16)

    bdmask = bd_ref[...]
    states = [d[4][...] for d in dirs]
    nxt = [chunk_scores(si, d[5][0]) for si, d in enumerate(dirs)]
    for i in range(n_chunks):
        cur, nxt = nxt, [None] * len(dirs)
        for si, ((_, _, _, o_ref, _, order), (qb, _, kd, vb, bl)) in enumerate(zip(dirs, ops)):
            c = order[i]
            vc = chunk(vb, c)
            inc = _dot_tn(vc, chunk(kd, c)) * bdmask
            if i + 1 < n_chunks:
                nxt[si] = chunk_scores(si, order[i + 1])
            vbd = jnp.concatenate([vc * vm_ref[h] for h in range(GLA_HEADS)], axis=0)
            o_ref[c * GLA_CHUNK:(c + 1) * GLA_CHUNK, :] = (
                _dot(cur[si], vbd) + _dot_nt(chunk(qb, c), states[si].astype(BF16)))
            states[si] = states[si] * jnp.exp(bl[c]) + inc
    for d, st in zip(dirs, states):
        d[4][...] = st


def _gla_masks():
    r = np.arange(GLA_CHUNK)
    lower = (r[None, :] <= r[:, None]).astype(np.float32)
    tri1 = np.stack([lower, lower.T])
    n_chunks = TOK_BLOCK // GLA_CHUNK
    tri = np.stack([np.kron(np.eye(n_chunks, dtype=np.float32), t) for t in tri1])
    tril = np.tile(tri1, (1, 1, GLA_HEADS))
    klane = np.arange(GLA_KPAD)
    vlane = np.arange(GLA_WIDTH)
    hm = np.stack([np.broadcast_to((klane // GLA_DK) == h, (GLA_CHUNK, GLA_KPAD)) for h in range(GLA_HEADS)])
    vm = np.stack([np.broadcast_to((vlane // GLA_DV) == h, (GLA_CHUNK, GLA_WIDTH)) for h in range(GLA_HEADS)])
    bd = ((vlane[:, None] // GLA_DV) == (klane[None, :] // GLA_DK)).astype(np.float32)
    return (jnp.asarray(tri, BF16), jnp.asarray(tril, F32), jnp.asarray(hm, BF16), jnp.asarray(vm, BF16),
            jnp.asarray(bd))


def _scan_block(d, j, n_ctx, n_all):
    bwd = jnp.where(j < n_ctx, n_ctx - 1 - j, n_all + n_ctx - 1 - j)
    return jnp.where(d == 0, j, bwd)


def _gla(gqk, gv, gla):
    bsz, tt, _ = gv.shape
    nb = tt // TOK_BLOCK
    n_ctx = CTX_LEN // TOK_BLOCK
    tri, tril, hm, vm, bd = _gla_masks()
    const2 = lambda b, j: (0, 0)
    const3 = lambda b, j: (0, 0, 0)
    fwd = lambda b, j: (b, j, 0)
    bwd = lambda b, j: (b, _scan_block(1, j, n_ctx, nb), 0)
    bwd_gate = lambda b, j: (b, _scan_block(1, j, n_ctx, nb), 1)
    tok = lambda w, imap: pl.BlockSpec((GLA_ROWS, TOK_BLOCK, w), imap)
    return pl.pallas_call(
        _gla_kernel,
        grid=(bsz // GLA_ROWS, nb),
        in_specs=[
            tok(2 * GLA_KPAD, fwd),
            tok(GLA_WIDTH, fwd),
            tok(GLA_KPAD, fwd),
            tok(2 * GLA_KPAD, bwd),
            tok(GLA_WIDTH, bwd),
            tok(GLA_KPAD, bwd_gate),
            pl.BlockSpec((2, TOK_BLOCK, TOK_BLOCK), const3),
            pl.BlockSpec((2, GLA_CHUNK, GLA_HEADS * GLA_CHUNK), const3),
            pl.BlockSpec((GLA_HEADS, GLA_CHUNK, GLA_KPAD), const3),
            pl.BlockSpec((GLA_HEADS, GLA_CHUNK, GLA_WIDTH), const3),
            pl.BlockSpec((GLA_WIDTH, GLA_KPAD), const2),
        ],
        out_specs=[tok(GLA_WIDTH, fwd), tok(GLA_WIDTH, bwd)],
        out_shape=[jax.ShapeDtypeStruct((bsz, tt, GLA_WIDTH), F32)] * 2,
        scratch_shapes=[pltpu.VMEM((GLA_ROWS, GLA_WIDTH, GLA_KPAD), F32)] * 2,
        compiler_params=pltpu.CompilerParams(
            dimension_semantics=("arbitrary", "arbitrary"), vmem_limit_bytes=VMEM_LIMIT),
        name="gla",
    )(gqk, gv, gla, gqk, gv, gla, tri, tril, hm, vm, bd)


def _swa_kernel(sink_ref, mask_ref, att_ref, o_ref, *, tt):
    j = pl.program_id(1)
    n_sub = TOK_BLOCK // ATT_BLOCK
    n_ctx_blocks = CTX_LEN // TOK_BLOCK
    last_q = tt // ATT_BLOCK - 1
    lane = lax.broadcasted_iota(jnp.int32, (ATT_BLOCK, 2 * ATT_HD), 1)
    left = lane < ATT_HD
    row2 = lax.broadcasted_iota(jnp.int32, (2 * ATT_BLOCK, 1), 0)
    units = [(r, sub, p) for r in range(SWA_ROWS) for sub in range(n_sub) for p in range(ATT_HEADS // 2)]

    def run(windowed):
        def score(r, sub, p):
            q0 = pl.multiple_of(j * TOK_BLOCK + sub * ATT_BLOCK, ATT_BLOCK)
            qp = att_ref[r, pl.ds(q0, ATT_BLOCK), p * 128:(p + 1) * 128]
            zero = jnp.zeros_like(qp)
            q2 = jnp.concatenate([jnp.where(left, qp, zero), jnp.where(left, zero, qp)], axis=0)
            kcol = ATT_WIDTH + p * 128
            s_c = _dot_nt(q2, att_ref[r, 0:CTX_LEN, kcol:kcol + 128])
            if not windowed:
                return s_c, None, None
            start = pl.multiple_of(jnp.minimum(q0 - ATT_BLOCK, tt - 3 * ATT_BLOCK), ATT_BLOCK)
            s_w = jnp.where(keeps[sub], _dot_nt(q2, att_ref[r, pl.ds(start, 3 * ATT_BLOCK), kcol:kcol + 128]),
                            NEG_INF)
            return s_c, s_w, start

        def softmax(p, s_c, s_w, start):
            sink_col = jnp.where(row2 < ATT_BLOCK, sink_ref[2 * p], sink_ref[2 * p + 1]) * LOG2E
            m = jnp.maximum(jnp.max(s_c, axis=-1, keepdims=True), sink_col)
            if windowed:
                m = jnp.maximum(m, jnp.max(s_w, axis=-1, keepdims=True))
            e_c = jnp.exp2(s_c - m)
            den = jnp.sum(e_c, axis=-1, keepdims=True) + jnp.exp2(sink_col - m)
            e_w = None
            if windowed:
                e_w = jnp.exp2(s_w - m)
                den = den + jnp.sum(e_w, axis=-1, keepdims=True)
                e_w = e_w.astype(BF16)
            return e_c.astype(BF16), e_w, den, start

        def values(r, sub, p, e_c, e_w, den, start):
            vcol = 2 * ATT_WIDTH + p * 128
            pv = _dot(e_c, att_ref[r, 0:CTX_LEN, vcol:vcol + 128])
            if windowed:
                pv = pv + _dot(e_w, att_ref[r, pl.ds(start, 3 * ATT_BLOCK), vcol:vcol + 128])
            o2 = pv / den
            o = jnp.where(left, o2[0:ATT_BLOCK], o2[ATT_BLOCK:2 * ATT_BLOCK])
            o_ref[r, sub * ATT_BLOCK:(sub + 1) * ATT_BLOCK, p * 128:(p + 1) * 128] = o.astype(BF16)

        keeps = []
        if windowed:
            for sub in range(n_sub):
                qi = j * n_sub + sub
                kind = jnp.where(qi == CTX_LEN // ATT_BLOCK, 0, jnp.where(qi == last_q, 2, 1))
                keeps.append(mask_ref[kind] > 0)

        n = len(units)
        sc, pr = {}, {}
        for t in range(n + 2):
            if t < n:
                sc[t] = score(*units[t])
            if 0 <= t - 1 < n:
                pr[t - 1] = softmax(units[t - 1][2], *sc.pop(t - 1))
            if 0 <= t - 2 < n:
                values(*units[t - 2], *pr.pop(t - 2))

    @pl.when(j < n_ctx_blocks)
    def _():
        run(False)

    @pl.when(j >= n_ctx_blocks)
    def _():
        run(True)


def _swa_masks():
    r = np.arange(2 * ATT_BLOCK)[:, None] % ATT_BLOCK
    c = np.arange(3 * ATT_BLOCK)[None, :]
    near = lambda delta: np.abs(r - c + delta) <= WINDOW
    first = near(ATT_BLOCK) & (c >= ATT_BLOCK)
    return jnp.asarray(np.stack([first, near(ATT_BLOCK), near(2 * ATT_BLOCK)]), F32)


def _swa(att, sink):
    bsz, tt, _ = att.shape
    nb = tt // TOK_BLOCK
    return pl.pallas_call(
        functools.partial(_swa_kernel, tt=tt),
        grid=(bsz // SWA_ROWS, nb),
        in_specs=[
            pl.BlockSpec(memory_space=pltpu.SMEM),
            pl.BlockSpec((3, 2 * ATT_BLOCK, 3 * ATT_BLOCK), lambda b, j: (0, 0, 0)),
            pl.BlockSpec((SWA_ROWS, tt, 3 * ATT_WIDTH), lambda b, j: (b, 0, 0)),
        ],
        out_specs=pl.BlockSpec((SWA_ROWS, TOK_BLOCK, ATT_WIDTH), lambda b, j: (b, j, 0)),
        out_shape=jax.ShapeDtypeStruct((bsz, tt, ATT_WIDTH), BF16),
        compiler_params=pltpu.CompilerParams(
            dimension_semantics=("arbitrary", "arbitrary"), vmem_limit_bytes=VMEM_LIMIT),
        name="swa",
    )(sink, _swa_masks(), att)


def _s5_prep_kernel(lr_ref, li_ref, ls_ref, br_ref, bi_ref, cr_ref, ci_ref,
                    a16_ref, mre_ref, mim_ref, rre_ref, rim_ref, k_ref):
    rows = S5_LC * S5_GROUP
    rowg = lax.broadcasted_iota(jnp.int32, (rows, S5_NS), 0) >> 4
    colg = lax.broadcasted_iota(jnp.int32, (rows, S5_NS), 1) >> 6
    same_group = rowg == colg
    taps_t = []
    for d in range(2):
        lr = jnp.minimum(lr_ref[d], -1e-4)
        li = li_ref[d]
        dt = jnp.exp(ls_ref[d])
        mag = jnp.exp(lr * dt)
        ar = mag * jnp.cos(li * dt)
        ai = mag * jnp.sin(li * dt)
        den = lr * lr + li * li
        fr = ((ar - 1.0) * lr + ai * li) / den
        fi = (ai * lr - (ar - 1.0) * li) / den
        br = br_ref[d]
        bi = bi_ref[d]
        bbr = fr * br - fi * bi
        bbi = fr * bi + fi * br
        cr = cr_ref[d]
        ci = ci_ref[d]
        pr = jnp.ones_like(ar)
        pi = jnp.zeros_like(ar)
        cp_re, cp_im = [], []
        for tau in range(S5_LC + 1):
            cpr = cr * pr - ci * pi
            cpi = cr * pi + ci * pr
            if tau < S5_LC:
                mre_ref[d, tau] = pr * bbr - pi * bbi
                mim_ref[d, tau] = pr * bbi + pi * bbr
                cp_re.append(cpr)
                cp_im.append(cpi)
            if tau >= 1:
                rre_ref[d, tau - 1] = cpr
                rim_ref[d, tau - 1] = -cpi
            if tau == S5_LC:
                a16_ref[d, 0:1, :] = pr
                a16_ref[d, 1:2, :] = pi
            pr, pi = pr * ar - pi * ai, pr * ai + pi * ar
        bd_re = jnp.where(same_group, jnp.concatenate([bbr] * S5_GROUPS, axis=0), 0.0)
        bd_im = jnp.where(same_group, jnp.concatenate([bbi] * S5_GROUPS, axis=0), 0.0)
        if d == 1:
            cp_re, cp_im = cp_re[::-1], cp_im[::-1]
        hp = lax.Precision.HIGHEST
        dims = (((1,), (1,)), ((), ()))
        taps_t.append(
            lax.dot_general(bd_re, jnp.concatenate(cp_re, axis=0), dims, precision=hp, preferred_element_type=F32)
            - lax.dot_general(bd_im, jnp.concatenate(cp_im, axis=0), dims, precision=hp,
                              preferred_element_type=F32))
    kf, kb_rev = taps_t
    tap0 = lax.broadcasted_iota(jnp.int32, kf.shape, 1) < S5_GROUP
    k_ref[0] = jnp.where(tap0, kf + pltpu.roll(kb_rev, S5_GROUP, 1), kf)
    k_ref[1] = kb_rev


def _s5_tables(fwd, bwd):
    n_layers = fwd[0].shape[0]
    both = lambda i: jnp.stack([fwd[i], bwd[i]], axis=1)
    lam_re = both(0).reshape(n_layers, 2, 1, S5_NS)
    lam_im = both(1).reshape(n_layers, 2, 1, S5_NS)
    log_step = jnp.repeat(both(2), S5_STATE, axis=-1).reshape(n_layers, 2, 1, S5_NS)
    b_hn = lambda t: t.transpose(0, 1, 4, 2, 3).reshape(n_layers, 2, S5_GROUP, S5_NS)
    c_hn = lambda t: t.transpose(0, 1, 3, 2, 4).reshape(n_layers, 2, S5_GROUP, S5_NS)
    vec = pl.BlockSpec((None, 2, 1, S5_NS), lambda l: (l, 0, 0, 0))
    mat = pl.BlockSpec((None, 2, S5_GROUP, S5_NS), lambda l: (l, 0, 0, 0))
    tab = pl.BlockSpec((None, 2, S5_LC, S5_GROUP, S5_NS), lambda l: (l, 0, 0, 0, 0))
    taps = S5_LC * S5_GROUP
    a16, mre, mim, rre, rim, k = pl.pallas_call(
        _s5_prep_kernel,
        grid=(n_layers,),
        in_specs=[vec, vec, vec, mat, mat, mat, mat],
        out_specs=[
            pl.BlockSpec((None, 2, 2, S5_NS), lambda l: (l, 0, 0, 0)),
            tab, tab, tab, tab,
            pl.BlockSpec((None, 2, S5_WIDTH, taps), lambda l: (l, 0, 0, 0)),
        ],
        out_shape=[
            jax.ShapeDtypeStruct((n_layers, 2, 2, S5_NS), F32),
            jax.ShapeDtypeStruct((n_layers, 2, S5_LC, S5_GROUP, S5_NS), F32),
            jax.ShapeDtypeStruct((n_layers, 2, S5_LC, S5_GROUP, S5_NS), F32),
            jax.ShapeDtypeStruct((n_layers, 2, S5_LC, S5_GROUP, S5_NS), F32),
            jax.ShapeDtypeStruct((n_layers, 2, S5_LC, S5_GROUP, S5_NS), F32),
            jax.ShapeDtypeStruct((n_layers, 2, S5_WIDTH, taps), F32),
        ],
        compiler_params=pltpu.CompilerParams(dimension_semantics=("arbitrary",), vmem_limit_bytes=VMEM_LIMIT),
        name="s5_prep",
    )(lam_re, lam_im, log_step, b_hn(both(3)), b_hn(both(4)), c_hn(both(5)), c_hn(both(6)))

    n_pairs = S5_GROUPS // 2
    lane_group = np.arange(128) // S5_STATE

    def pair_rows(t, flip):
        t = t[:, ::-1] if flip else t
        t = t.reshape(n_layers, S5_LC, S5_GROUP, n_pairs, 128).transpose(0, 3, 1, 2, 4)
        own = jnp.asarray(lane_group[None, :] == np.arange(2)[:, None])
        t = jnp.where(own[None, None, :, None, None, :], t[:, :, None], 0.0)
        return t.reshape(n_layers, n_pairs, 2 * taps, 128)

    m_f = jnp.concatenate([pair_rows(mre[:, 0], True), pair_rows(mim[:, 0], True)], axis=-1).astype(BF16)
    m_b = jnp.concatenate([pair_rows(mre[:, 1], False), pair_rows(mim[:, 1], False)], axis=-1).astype(BF16)

    r_t = jnp.concatenate([pair_rows(rre[:, 0], False), pair_rows(rim[:, 0], False),
                           pair_rows(rre[:, 1], True), pair_rows(rim[:, 1], True)], axis=-1).astype(BF16)

    zeros = jnp.zeros((n_layers, S5_WIDTH, taps), F32)
    f2 = jnp.concatenate([zeros, k[:, 0]], axis=-1)
    b2 = jnp.concatenate([k[:, 1], zeros], axis=-1)
    t_f = jnp.stack([f2[..., taps - S5_GROUP * st:2 * taps - S5_GROUP * st] for st in range(S5_LC)], axis=2)
    t_b = jnp.stack([b2[..., S5_GROUP * (S5_LC - 1 - st):S5_GROUP * (S5_LC - 1 - st) + taps]
                     for st in range(S5_LC)], axis=2)
    later = jnp.asarray(np.arange(taps)[None, :] // S5_GROUP >= np.arange(S5_LC)[:, None])
    toe = jnp.where(later[None, None], t_f, t_b)
    toe = toe.reshape(n_layers, S5_GROUPS, S5_GROUP, S5_LC, taps).transpose(0, 1, 3, 2, 4)
    toe = toe.reshape(n_layers, S5_GROUPS, taps, taps).astype(BF16)
    return a16, m_f, m_b, r_t, toe


def _lane_block_transpose(cols):
    lane = lax.broadcasted_iota(jnp.int32, cols[0].shape, 1)
    out = [None] * 32
    for ah in range(2):
        for bh in range(2):
            v = [cols[(ah * 8 + al) * 2 + bh] for al in range(8)]
            for kbit in range(3):
                width = 16 << kbit
                low = ((lane >> (4 + kbit)) & 1) == 0
                nxt = list(v)
                for i in range(8):
                    if i & (1 << kbit):
                        continue
                    lo_v, hi_v = v[i], v[i | (1 << kbit)]
                    nxt[i] = jnp.where(low, lo_v, pltpu.roll(hi_v, width, 1))
                    nxt[i | (1 << kbit)] = jnp.where(low, pltpu.roll(lo_v, 128 - width, 1), hi_v)
                v = nxt
            for bl in range(8):
                out[(bh * 8 + bl) * 2 + ah] = v[bl]
    return out


def _s5_increments(ush, m_ref, d_re, d_im):
    for p in range(S5_GROUPS // 2):
        dp = _dot(ush[:, p * 512:(p + 1) * 512], m_ref[p])
        d_re[p] = dp[:, 0:128]
        d_im[p] = dp[:, 128:256]


def _s5_recurrence(a16_ref, d_re, d_im, x_re, x_im, sr_ref, si_ref, order, bsz):
    ar = a16_ref[0:1, :]
    ai = a16_ref[1:2, :]
    sr = sr_ref[...]
    si = si_ref[...]
    n_slabs = S5_NS // 128
    gather = lambda ref, rows: jnp.concatenate([ref[p, rows, :] for p in range(n_slabs)], axis=1)
    for c in order:
        rows = pl.ds(c, bsz, stride=CH_BLOCK)
        for p in range(n_slabs):
            x_re[p, rows, :] = sr[:, p * 128:(p + 1) * 128]
            x_im[p, rows, :] = si[:, p * 128:(p + 1) * 128]
        sr, si = ar * sr - ai * si + gather(d_re, rows), ar * si + ai * sr + gather(d_im, rows)
    sr_ref[...] = sr
    si_ref[...] = si


def _s5_fwd_kernel(uc_ref, a16_ref, m_ref, ush_ref, xin_ref, d_re, d_im, x_re, x_im, sr_ref, si_ref, *, bsz):
    @pl.when(pl.program_id(0) == 0)
    def _():
        sr_ref[...] = jnp.zeros_like(sr_ref)
        si_ref[...] = jnp.zeros_like(si_ref)

    rows = bsz * CH_BLOCK
    ub = uc_ref[...].reshape(rows, S5_CW).astype(BF16)
    packed = pltpu.bitcast(ub, jnp.uint32)
    cols = _lane_block_transpose([packed[:, v * 128:(v + 1) * 128] for v in range(32)])
    ush = pltpu.bitcast(jnp.concatenate(cols, axis=1), BF16)
    ush_ref[...] = ush.reshape(bsz, CH_BLOCK, S5_CW)
    _s5_increments(ush, m_ref, d_re, d_im)
    _s5_recurrence(a16_ref, d_re, d_im, x_re, x_im, sr_ref, si_ref, range(CH_BLOCK), bsz)
    for p in range(S5_NS // 128):
        xin_ref[:, :, p * 128:(p + 1) * 128] = x_re[p].astype(BF16).reshape(bsz, CH_BLOCK, 128)
        xin_ref[:, :, S5_NS + p * 128:S5_NS + (p + 1) * 128] = x_im[p].astype(BF16).reshape(bsz, CH_BLOCK, 128)


def _s5_bwd_kernel(ush_ref, xf_ref, a16_ref, m_ref, toe_ref, r_ref, y_ref, d_re, d_im, x_re, x_im, sr_ref, si_ref,
                   *, bsz):
    @pl.when(pl.program_id(0) == 0)
    def _():
        sr_ref[...] = jnp.zeros_like(sr_ref)
        si_ref[...] = jnp.zeros_like(si_ref)

    rows = bsz * CH_BLOCK
    ush = ush_ref[...].reshape(rows, S5_CW)
    _s5_increments(ush, m_ref, d_re, d_im)
    _s5_recurrence(a16_ref, d_re, d_im, x_re, x_im, sr_ref, si_ref, reversed(range(CH_BLOCK)), bsz)
    xf = xf_ref[...].reshape(rows, 2 * S5_NS)
    ycols = []
    for p in range(S5_GROUPS // 2):
        lanes = slice(p * 128, (p + 1) * 128)
        xcat = jnp.concatenate([xf[:, lanes], xf[:, S5_NS + p * 128:S5_NS + (p + 1) * 128],
                                x_re[p].astype(BF16), x_im[p].astype(BF16)], axis=1)
        carry = _dot_nt(xcat, r_ref[p])
        for g2 in range(2):
            g = 2 * p + g2
            yg = carry[:, g2 * 256:(g2 + 1) * 256] + _dot(ush[:, g * 256:(g + 1) * 256], toe_ref[g])
            ycols += [yg[:, 0:128], yg[:, 128:256]]
    ycols = _lane_block_transpose(ycols)
    y_ref[...] = jnp.concatenate(ycols, axis=1).reshape(bsz, CH_BLOCK, S5_CW)


def _s5(u_c, tables, layer):
    a16, m_f, m_b, r_t, toe = tables
    bsz, n_rows, _ = u_c.shape
    nb = n_rows // CH_BLOCK
    n_ctx = CTX_LEN // TOK_BLOCK
    rows = bsz * CH_BLOCK
    blk3 = lambda w: (bsz, CH_BLOCK, w)
    scratch = [pltpu.VMEM((S5_NS // 128, rows, 128), F32)] * 4 + [pltpu.VMEM((bsz, S5_NS), F32)] * 2
    params = pltpu.CompilerParams(dimension_semantics=("arbitrary",), vmem_limit_bytes=VMEM_LIMIT)
    ush, xin_f = pl.pallas_call(
        functools.partial(_s5_fwd_kernel, bsz=bsz),
        grid=(nb,),
        in_specs=[
            pl.BlockSpec(blk3(S5_CW), lambda j: (0, j, 0)),
            pl.BlockSpec((None, None, 2, S5_NS), lambda j: (layer, 0, 0, 0)),
            pl.BlockSpec((None, S5_GROUPS // 2, 512, 256), lambda j: (layer, 0, 0, 0)),
        ],
        out_specs=[
            pl.BlockSpec(blk3(S5_CW), lambda j: (0, j, 0)),
            pl.BlockSpec(blk3(2 * S5_NS), lambda j: (0, j, 0)),
        ],
        out_shape=[
            jax.ShapeDtypeStruct((bsz, n_rows, S5_CW), BF16),
            jax.ShapeDtypeStruct((bsz, n_rows, 2 * S5_NS), BF16),
        ],
        scratch_shapes=scratch,
        compiler_params=params,
        name="s5_fwd",
    )(u_c, a16, m_f)
    back = lambda j: (0, _scan_block(1, j, n_ctx, nb), 0)
    return pl.pallas_call(
        functools.partial(_s5_bwd_kernel, bsz=bsz),
        grid=(nb,),
        in_specs=[
            pl.BlockSpec(blk3(S5_CW), back),
            pl.BlockSpec(blk3(2 * S5_NS), back),
            pl.BlockSpec((None, None, 2, S5_NS), lambda j: (layer, 1, 0, 0)),
            pl.BlockSpec((None, S5_GROUPS // 2, 512, 256), lambda j: (layer, 0, 0, 0)),
            pl.BlockSpec((None, S5_GROUPS, 256, 256), lambda j: (layer, 0, 0, 0)),
            pl.BlockSpec((None, S5_GROUPS // 2, 512, 512), lambda j: (layer, 0, 0, 0)),
        ],
        out_specs=pl.BlockSpec(blk3(S5_CW), back),
        out_shape=jax.ShapeDtypeStruct((bsz, n_rows, S5_CW), F32),
        scratch_shapes=scratch,
        compiler_params=params,
        name="s5_bwd",
    )(ush, xin_f, a16, m_b, toe, r_t)


def _out_kernel(xc_ref, x_ref, mod_ref, gof_ref, gob_ref, g_ref, att_ref, y_ref, uc_ref, gnw_ref, hm_ref, d_ref,
                gluw_ref, glub_ref, wo_ref, n2_ref, w1_ref, w2_ref, fn_ref, o_ref, ys_ref,
                *, first_block, final, split):
    j = pl.program_id(1) + first_block

    def mixed(r):
        m = mod_ref[r]
        mrow = jnp.where(j > 0, m[1:2, :], m[0:1, :])
        g1 = mrow[:, 2 * D_MODEL:3 * D_MODEL]
        sh2 = mrow[:, 3 * D_MODEL:4 * D_MODEL]
        sc2 = mrow[:, 4 * D_MODEL:5 * D_MODEL]
        g2 = mrow[:, 5 * D_MODEL:6 * D_MODEL]

        o = gof_ref[r] + gob_ref[r]
        ms = _dot((o * o).astype(BF16), hm_ref[...]) * (1.0 / GLA_DV)
        a = o * lax.rsqrt(ms + EPS) * gnw_ref[...] * _silu(g_ref[r])

        yc = y_ref[r] + d_ref[...] * uc_ref[r]
        for st in range(S5_LC):
            for half in range(S5_WIDTH // 128):
                lane0 = st * S5_WIDTH + half * 128
                ys_ref[r, half, pl.ds(st, CH_BLOCK, stride=S5_LC), :] = yc[:, lane0:lane0 + 128]
        yy = jnp.concatenate([ys_ref[r, half] for half in range(S5_WIDTH // 128)], axis=1)
        ge = 0.5 * yy * (1.0 + jnp.tanh(math.sqrt(2.0 / math.pi) * (yy + 0.044715 * (yy * yy * yy))))
        z = _dot(ge.astype(BF16), gluw_ref[...]) + glub_ref[...]
        s = z[:, 0:S5_WIDTH] * _sigmoid(z[:, S5_WIDTH:2 * S5_WIDTH])

        proj = _dot(jnp.concatenate([a.astype(BF16), att_ref[r], s.astype(BF16)], axis=1), wo_ref[...])
        x_in = jnp.where(j == 0, xc_ref[r], x_ref[r]) if split else x_ref[r]
        x1 = x_in + g1 * proj
        y2 = x1 * lax.rsqrt(jnp.mean(x1 * x1, axis=-1, keepdims=True) + EPS) * n2_ref[...]
        return x1, (y2 * (1.0 + sc2) + sh2).astype(BF16), g2

    def finish(r, x1, g2, mlp):
        x2 = x1 + g2 * mlp
        if final:
            x2 = x2 * lax.rsqrt(jnp.mean(x2 * x2, axis=-1, keepdims=True) + EPS) * fn_ref[...]
        o_ref[r] = x2

    cur = mixed(0)
    for r in range(OUT_ROWS):
        x1, h2, g2 = cur
        hid = jnp.maximum(_dot(h2, w1_ref[...]), 0.0)
        if r + 1 < OUT_ROWS:
            cur = mixed(r + 1)
        mlp = _dot((hid * hid).astype(BF16), w2_ref[...])
        finish(r, x1, g2, mlp)


def _outproj(x_ctx, x_rest, lat_skip, modsel, gla_f, gla_b, gg, att_o, y_c, u_c, gnw, hm, s5d, gluw, glub, wo, n2w,
             w1, w2, fnw, layer, final):
    bsz, tt, _ = gg.shape
    nb = tt // TOK_BLOCK
    first = CTX_LEN // TOK_BLOCK if final else 0
    nsteps = nb - first
    const2 = lambda b, j: (0, 0)
    lyr3 = lambda b, j: (layer, 0, 0)
    single = pl.Buffered(1)
    return pl.pallas_call(
        functools.partial(_out_kernel, first_block=first, final=final, split=lat_skip > 0),
        grid=(bsz // OUT_ROWS, nsteps),
        in_specs=_stream_specs(lat_skip, first, rows=OUT_ROWS) + [
            pl.BlockSpec((OUT_ROWS, 2, 6 * D_MODEL), lambda b, j: (b, 0, 0)),
            pl.BlockSpec((OUT_ROWS, TOK_BLOCK, GLA_WIDTH), lambda b, j: (b, j + first, 0)),
            pl.BlockSpec((OUT_ROWS, TOK_BLOCK, GLA_WIDTH), lambda b, j: (b, j + first, 0)),
            pl.BlockSpec((OUT_ROWS, TOK_BLOCK, GLA_WIDTH), lambda b, j: (b, j + first, 0)),
            pl.BlockSpec((OUT_ROWS, TOK_BLOCK, ATT_WIDTH), lambda b, j: (b, j + first, 0)),
            pl.BlockSpec((OUT_ROWS, CH_BLOCK, S5_CW), lambda b, j: (b, j + first, 0)),
            pl.BlockSpec((OUT_ROWS, CH_BLOCK, S5_CW), lambda b, j: (b, j + first, 0)),
            pl.BlockSpec((1, GLA_WIDTH), const2),
            pl.BlockSpec((GLA_WIDTH, GLA_WIDTH), const2),
            pl.BlockSpec((1, S5_CW), const2),
            pl.BlockSpec((None, S5_WIDTH, 2 * S5_WIDTH), lyr3, pipeline_mode=single),
            pl.BlockSpec((1, 2 * S5_WIDTH), const2),
            pl.BlockSpec((None, D_MODEL, D_MODEL), lyr3, pipeline_mode=single),
            pl.BlockSpec((1, D_MODEL), const2),
            pl.BlockSpec((None, D_MODEL, D_FF), lyr3, pipeline_mode=single),
            pl.BlockSpec((None, D_FF, D_MODEL), lyr3, pipeline_mode=single),
            pl.BlockSpec((1, D_MODEL), const2),
        ],
        out_specs=pl.BlockSpec((OUT_ROWS, TOK_BLOCK, D_MODEL), lambda b, j: (b, j, 0)),
        out_shape=jax.ShapeDtypeStruct((bsz, nsteps * TOK_BLOCK, D_MODEL), F32),
        scratch_shapes=[pltpu.VMEM((OUT_ROWS, S5_WIDTH // 128, TOK_BLOCK, 128), F32)],
        compiler_params=pltpu.CompilerParams(
            dimension_semantics=("arbitrary", "arbitrary"), vmem_limit_bytes=VMEM_LIMIT),
        name="outproj_mlp",
    )(x_ctx, x_rest, modsel, gla_f, gla_b, gg, att_o, y_c, u_c, gnw, hm, s5d, gluw, glub, wo, n2w, w1, w2, fnw)


def _rope_tables(tt):
    n_lat = tt - CTX_LEN
    rows = n_lat // GRID_W
    row = jnp.repeat(jnp.arange(rows, dtype=F32), GRID_W)
    col = jnp.tile(jnp.arange(GRID_W, dtype=F32), rows)
    n_freq = ATT_HD // 4
    inv_freq = ROPE_BASE ** (-jnp.arange(n_freq, dtype=F32) / n_freq)
    ang_r = row[:, None] * inv_freq
    ang_c = col[:, None] * inv_freq
    ang = jnp.concatenate([ang_r, ang_r, ang_c, ang_c], axis=-1)
    cos = jnp.concatenate([jnp.ones((CTX_LEN, ATT_HD), F32), jnp.cos(ang)], axis=0)
    sin = jnp.concatenate([jnp.zeros((CTX_LEN, ATT_HD), F32), jnp.sin(ang)], axis=0)
    up_quarter = (np.arange(ATT_HD) // 16) % 2 == 0
    sa = jnp.where(up_quarter, -sin, 0.0)
    sb = jnp.where(up_quarter, 0.0, sin)
    two = lambda t: jnp.concatenate([t, t], axis=-1)
    return two(cos), two(sa), two(sb)


def _pad_cols(w, width):
    return jnp.pad(w, ((0, 0), (0, 0), (0, width - w.shape[-1])))


def _layout_w_in(w_in):
    offs = np.cumsum([0, GLA_KW, GLA_KW, GLA_WIDTH, GLA_WIDTH, GLA_RANK, GLA_RANK, ATT_WIDTH, ATT_KVW, ATT_KVW,
                      S5_WIDTH])
    q, k, v, g, zf, zb, aq, ak, av, u = [w_in[:, :, offs[i]:offs[i + 1]] for i in range(10)]
    z = jnp.concatenate([zf, zb], axis=-1)
    cols = [v, g, _pad_cols(q, GLA_KPAD), _pad_cols(k, GLA_KPAD), _pad_cols(z, 128), aq, ak, av, u]
    return jnp.concatenate(cols, axis=-1).astype(BF16)


def kernel(x, c, ctx, c_ctx, w_mod, b_mod, norm1_w, norm2_w, w_in, gla_wa_f, gla_ba_f, gla_wa_b, gla_ba_b,
           gla_norm_w, attn_sink, s5_lam_re_f, s5_lam_im_f, s5_log_step_f, s5_b_re_f, s5_b_im_f, s5_c_re_f,
           s5_c_im_f, s5_lam_re_b, s5_lam_im_b, s5_log_step_b, s5_b_re_b, s5_b_im_b, s5_c_re_b, s5_c_im_b,
           s5_d, glu_w, glu_b, w_out, mlp_w1, mlp_w2, final_norm_w):
    bsz, seq, _ = x.shape
    n_layers = w_mod.shape[0]
    tt = CTX_LEN + seq
    assert all(bsz % rows == 0 for rows in (8, IN_ROWS, GLA_ROWS, SWA_ROWS, OUT_ROWS))
    assert seq % TOK_BLOCK == 0 and seq >= 2 * TOK_BLOCK and ctx.shape[1] == CTX_LEN

    mod_rows = -(-(bsz + 1) // 8) * 8
    cvec = jnp.zeros((mod_rows, D_MODEL), F32).at[:bsz].set(c).at[bsz].set(c_ctx)
    mod = _modulation(cvec, w_mod, b_mod)
    mod_ctx = jnp.broadcast_to(mod[:, bsz][:, None], (n_layers, bsz, 6 * D_MODEL))
    modsel = jnp.stack([mod_ctx, mod[:, :bsz]], axis=2)

    w_in_p = _layout_w_in(w_in)
    wa_cat = jnp.zeros((n_layers, 128, 2 * GLA_KPAD), F32)
    wa_cat = wa_cat.at[:, 0:GLA_RANK, 0:GLA_KW].set(gla_wa_f)
    wa_cat = wa_cat.at[:, GLA_RANK:2 * GLA_RANK, GLA_KPAD:GLA_KPAD + GLA_KW].set(gla_wa_b).astype(BF16)
    ba_cat = jnp.zeros((n_layers, 1, 2 * GLA_KPAD), F32)
    ba_cat = ba_cat.at[:, 0, 0:GLA_KW].set(gla_ba_f).at[:, 0, GLA_KPAD:GLA_KPAD + GLA_KW].set(gla_ba_b)
    qscale = jnp.ones((1, 2 * GLA_KPAD), F32).at[:, 0:GLA_KW].set(GLA_DK ** -0.5)
    cos_t, sa_t, sb_t = _rope_tables(tt)
    gnw = jnp.tile(gla_norm_w, (1, GLA_HEADS))[:, None, :]
    head = np.arange(GLA_WIDTH) // GLA_DV
    hm = jnp.asarray(head[:, None] == head[None, :], BF16)
    wo_b = w_out.astype(BF16)
    w1_b = mlp_w1.astype(BF16)
    w2_b = mlp_w2.astype(BF16)
    gluw_b = glu_w.astype(BF16)

    s5_tab = _s5_tables(
        (s5_lam_re_f, s5_lam_im_f, s5_log_step_f, s5_b_re_f, s5_b_im_f, s5_c_re_f, s5_c_im_f),
        (s5_lam_re_b, s5_lam_im_b, s5_log_step_b, s5_b_re_b, s5_b_im_b, s5_c_re_b, s5_c_im_b))
    s5_dt = jnp.tile(s5_d, (1, S5_LC))[:, None, :]

    stream = (ctx, x, CTX_LEN // TOK_BLOCK)
    for l in range(n_layers):
        final = l == n_layers - 1
        gv, gg, gqk, gla, att, u_c = _inproj(*stream, modsel[l], norm1_w[l][None], w_in_p, wa_cat, ba_cat[l],
                                             qscale, cos_t, sa_t, sb_t, l)
        gla_f, gla_b = _gla(gqk, gv, gla)
        att_o = _swa(att, attn_sink[l])
        y_c = _s5(u_c, s5_tab, l)
        xs = _outproj(*stream, modsel[l], gla_f, gla_b, gg, att_o, y_c, u_c,
                      gnw[l], hm, s5_dt[l], gluw_b, glu_b[l][None], wo_b, norm2_w[l][None], w1_b, w2_b,
                      final_norm_w[None], l, final)
        stream = (xs, xs, 0)
    return xs
```

```python
import functools
import math

import jax
import jax.numpy as jnp
import numpy as np
from jax import lax
from jax.experimental import pallas as pl
from jax.experimental.pallas import tpu as pltpu

F32 = jnp.float32
BF16 = jnp.bfloat16

D_MODEL = 1024
D_FF = 4 * D_MODEL
CTX_LEN = 256
GRID_W = 64
EPS = 1e-6
NEG_INF = -1e30
LOG2E = math.log2(math.e)

GLA_HEADS = 4
GLA_DV = 96
GLA_DK = 48
GLA_WIDTH = GLA_HEADS * GLA_DV
GLA_KW = GLA_HEADS * GLA_DK
GLA_KPAD = 256
GLA_RANK = 16
GLA_TAU = 16.0
GLA_CHUNK = 64

ATT_HD = 64
ATT_HEADS = 6
ATT_KV_HEADS = 2
ATT_WIDTH = ATT_HEADS * ATT_HD
ATT_KVW = ATT_KV_HEADS * ATT_HD
WINDOW = 128
ATT_BLOCK = 128
ROPE_BASE = 10000.0

S5_WIDTH = 256
S5_GROUP = 16
S5_GROUPS = 16
S5_STATE = 64
S5_NS = S5_GROUPS * S5_STATE
S5_LC = 16
S5_CW = S5_LC * S5_WIDTH

TOK_BLOCK = 256
CH_BLOCK = TOK_BLOCK // S5_LC
GLA_ROWS = 4
SWA_ROWS = 2
IN_ROWS = 4
OUT_ROWS = 2

C_V, C_G, C_Q, C_K, C_Z, C_AQ, C_AK, C_AV, C_U = 0, 384, 768, 1024, 1280, 1408, 1792, 1920, 2048
IN_PAD = 2304
C_QK_END = C_K + GLA_KPAD

V7X_VMEM_BYTES = 64 * 1024 * 1024
VMEM_LIMIT = V7X_VMEM_BYTES - 8 * 1024 * 1024


def _sigmoid(x):
    return 1.0 / (1.0 + jnp.exp(-x))


def _silu(x):
    return x * _sigmoid(x)


def _dot(a, b):
    return jnp.dot(a, b, preferred_element_type=F32)


def _dot_nt(a, b):
    return lax.dot_general(a, b, (((1,), (1,)), ((), ())), preferred_element_type=F32)


def _dot_tn(a, b):
    return lax.dot_general(a, b, (((0,), (0,)), ((), ())), preferred_element_type=F32)


def _mod_kernel(c_ref, w_ref, b_ref, o_ref):
    a = _silu(c_ref[...]).astype(BF16)
    o_ref[...] = _dot(a, w_ref[...].astype(BF16)) + b_ref[...]


def _modulation(cvec, w_mod, b_mod):
    n_layers = w_mod.shape[0]
    rows = cvec.shape[0]
    tn = 1536
    return pl.pallas_call(
        _mod_kernel,
        grid=(n_layers, 6 * D_MODEL // tn),
        in_specs=[
            pl.BlockSpec((rows, D_MODEL), lambda l, n: (0, 0)),
            pl.BlockSpec((None, D_MODEL, tn), lambda l, n: (l, 0, n)),
            pl.BlockSpec((None, 1, tn), lambda l, n: (l, 0, n)),
        ],
        out_specs=pl.BlockSpec((None, rows, tn), lambda l, n: (l, 0, n)),
        out_shape=jax.ShapeDtypeStruct((n_layers, rows, 6 * D_MODEL), F32),
        compiler_params=pltpu.CompilerParams(
            dimension_semantics=("arbitrary", "arbitrary"), vmem_limit_bytes=VMEM_LIMIT),
        name="modulation",
    )(cvec, w_mod, b_mod.reshape(n_layers, 1, 6 * D_MODEL))


def _inproj_kernel(xc_ref, x_ref, mod_ref, n1_ref, w_ref, wa_ref, ba_ref, qs_ref, cos_ref, sa_ref, sb_ref,
                   gv_ref, gg_ref, gql_ref, att_ref, uc_ref, us_ref, *, split):
    j = pl.program_id(1)
    cos = cos_ref[...]
    sa = sa_ref[...]
    sb = sb_ref[...]

    def normed(r):
        xf = jnp.where(j == 0, xc_ref[r], x_ref[r]) if split else x_ref[r]
        y = xf * lax.rsqrt(jnp.mean(xf * xf, axis=-1, keepdims=True) + EPS) * n1_ref[...]
        m = mod_ref[r]
        mrow = jnp.where(j > 0, m[1:2, :], m[0:1, :])
        sh1 = mrow[:, 0:D_MODEL]
        sc1 = mrow[:, D_MODEL:2 * D_MODEL]
        return (y * (1.0 + sc1) + sh1).astype(BF16)

    def rope(t, reps):
        w = t.shape[-1]
        c3 = jnp.concatenate([cos] * reps, axis=-1) if reps > 1 else cos
        a3 = jnp.concatenate([sa] * reps, axis=-1) if reps > 1 else sa
        b3 = jnp.concatenate([sb] * reps, axis=-1) if reps > 1 else sb
        up = pltpu.roll(t, w - 16, 1)
        dn = pltpu.roll(t, 16, 1)
        return t * c3 + up * a3 + dn * b3

    def tail_gla(r, p):
        gv_ref[r] = p[:, C_V:C_V + GLA_WIDTH].astype(BF16)
        gg_ref[r] = p[:, C_G:C_G + GLA_WIDTH]
        gql_ref[r, :, 0:2 * GLA_KPAD] = p[:, C_Q:C_QK_END] * qs_ref[...]

    def tail_rest(r, p):
        off = C_QK_END
        z = p[:, C_Z - off:C_Z - off + 128].astype(BF16)
        zg = _dot(z, wa_ref[...]) + ba_ref[...]
        gql_ref[r, :, 2 * GLA_KPAD:4 * GLA_KPAD] = (
            (jnp.minimum(zg, 0.0) - jnp.log1p(jnp.exp(-jnp.abs(zg)))) * (1.0 / GLA_TAU))

        aq = rope(p[:, C_AQ - off:C_AQ - off + ATT_WIDTH], 3) * (ATT_HD ** -0.5 * LOG2E)
        ak = rope(p[:, C_AK - off:C_AK - off + ATT_KVW], 1)
        av = p[:, C_AV - off:C_AV - off + ATT_KVW]
        left = lax.broadcasted_iota(jnp.int32, ak.shape, 1) < ATT_HD

        def expand(t):
            sw = pltpu.roll(t, ATT_HD, 1)
            return [jnp.where(left, t, sw), t, jnp.where(left, sw, t)]

        att_ref[r] = jnp.concatenate([aq] + expand(ak) + expand(av), axis=-1).astype(BF16)
        for half in range(S5_WIDTH // 128):
            us_ref[r, half] = p[:, C_U - off + half * 128:C_U - off + (half + 1) * 128]
        for st in range(S5_LC):
            for half in range(S5_WIDTH // 128):
                lane0 = st * S5_WIDTH + half * 128
                uc_ref[r, :, lane0:lane0 + 128] = us_ref[r, half, pl.ds(st, CH_BLOCK, stride=S5_LC), :]

    h = normed(0)
    prev_rest = None
    for r in range(IN_ROWS):
        p_gla = _dot(h, w_ref[:, 0:C_QK_END])
        if prev_rest is not None:
            tail_rest(r - 1, prev_rest)
        p_rest = _dot(h, w_ref[:, C_QK_END:IN_PAD])
        if r + 1 < IN_ROWS:
            h = normed(r + 1)
        tail_gla(r, p_gla)
        prev_rest = p_rest
    tail_rest(IN_ROWS - 1, prev_rest)


def _stream_specs(lat_skip, first=0, rows=None):
    return [pl.BlockSpec((rows, TOK_BLOCK, D_MODEL), lambda b, j: (b, 0, 0)),
            pl.BlockSpec((rows, TOK_BLOCK, D_MODEL), lambda b, j: (b, jnp.maximum(j + first - lat_skip, 0), 0))]


def _inproj(x_ctx, x_rest, lat_skip, modsel, n1w, w_in_p, wa_cat, ba_cat, qscale, cos_t, sa_t, sb_t, layer):
    bsz = x_ctx.shape[0]
    tt = cos_t.shape[0]
    nb = tt // TOK_BLOCK
    const = lambda b, j: (0, 0)
    lyr3 = lambda b, j: (layer, 0, 0)
    tok = lambda w: pl.BlockSpec((IN_ROWS, TOK_BLOCK, w), lambda b, j: (b, j, 0))
    return pl.pallas_call(
        functools.partial(_inproj_kernel, split=lat_skip > 0),
        grid=(bsz // IN_ROWS, nb),
        in_specs=_stream_specs(lat_skip, rows=IN_ROWS) + [
            pl.BlockSpec((IN_ROWS, 2, 6 * D_MODEL), lambda b, j: (b, 0, 0)),
            pl.BlockSpec((1, D_MODEL), const),
            pl.BlockSpec((None, D_MODEL, IN_PAD), lyr3),
            pl.BlockSpec((None, 128, 2 * GLA_KPAD), lyr3),
            pl.BlockSpec((1, 2 * GLA_KPAD), const),
            pl.BlockSpec((1, 2 * GLA_KPAD), const),
            pl.BlockSpec((TOK_BLOCK, 128), lambda b, j: (j, 0)),
            pl.BlockSpec((TOK_BLOCK, 128), lambda b, j: (j, 0)),
            pl.BlockSpec((TOK_BLOCK, 128), lambda b, j: (j, 0)),
        ],
        out_specs=[tok(GLA_WIDTH), tok(GLA_WIDTH), tok(4 * GLA_KPAD), tok(3 * ATT_WIDTH),
                   pl.BlockSpec((IN_ROWS, CH_BLOCK, S5_CW), lambda b, j: (b, j, 0))],
        out_shape=[
            jax.ShapeDtypeStruct((bsz, tt, GLA_WIDTH), BF16),
            jax.ShapeDtypeStruct((bsz, tt, GLA_WIDTH), F32),
            jax.ShapeDtypeStruct((bsz, tt, 4 * GLA_KPAD), F32),
            jax.ShapeDtypeStruct((bsz, tt, 3 * ATT_WIDTH), BF16),
            jax.ShapeDtypeStruct((bsz, tt // S5_LC, S5_CW), F32),
        ],
        scratch_shapes=[pltpu.VMEM((IN_ROWS, S5_WIDTH // 128, TOK_BLOCK, 128), F32)],
        compiler_params=pltpu.CompilerParams(
            dimension_semantics=("arbitrary", "arbitrary"), vmem_limit_bytes=VMEM_LIMIT),
        name="inproj",
    )(x_ctx, x_rest, modsel, n1w, w_in_p, wa_cat, ba_cat, qscale, cos_t, sa_t, sb_t)


def _gla_kernel(qkf_ref, vf_ref, laf_ref, qkb_ref, vb_ref, lab_ref, tri_ref, tril_ref, hm_ref, vm_ref, bd_ref,
                of_ref, ob_ref, stf_ref, stb_ref):
    j = pl.program_id(1)

    @pl.when(j == 0)
    def _():
        stf_ref[...] = jnp.zeros_like(stf_ref)
        stb_ref[...] = jnp.zeros_like(stb_ref)

    n_chunks = TOK_BLOCK // GLA_CHUNK
    chunk = lambda t, c: t[c * GLA_CHUNK:(c + 1) * GLA_CHUNK]
    dirs = []
    for r in range(GLA_ROWS):
        dirs.append(((qkf_ref.at[r], vf_ref.at[r], laf_ref.at[r]), 0, GLA_CHUNK - 1, of_ref.at[r],
                     stf_ref.at[r], list(range(n_chunks))))
        dirs.append(((qkb_ref.at[r], vb_ref.at[r], lab_ref.at[r]), 1, 0, ob_ref.at[r],
                     stb_ref.at[r], list(reversed(range(n_chunks)))))

    cum = []
    for (_, _, la_ref), d, _, _, _, _ in dirs:
        la = la_ref[...]
        hi = la.astype(BF16)
        lo = (la - hi.astype(F32)).astype(BF16)
        cum.append(_dot(tri_ref[d], hi) + _dot(tri_ref[d], lo))

    ops = []
    for ((qk_ref, v_ref, _), d, last_row, _, _, _), b in zip(dirs, cum):
        q = qk_ref[:, 0:GLA_KPAD]
        k = qk_ref[:, GLA_KPAD:2 * GLA_KPAD]
        vb = v_ref[...]
        bl = [chunk(b, c)[last_row:last_row + 1, :] for c in range(n_chunks)]
        blx = jnp.concatenate([jnp.broadcast_to(t, (GLA_CHUNK, GLA_KPAD)) for t in bl], axis=0)
        qb = (q * jnp.exp(b)).astype(BF16)
        kb = (k * jnp.exp(-b)).astype(BF16)
        kd = (k * jnp.exp(blx - b)).astype(BF16)
        ops.append((qb, kb, kd, vb, bl))

    def chunk_scores(si, c):
        qb, kb = ops[si][0], ops[si][1]
        keep = tril_ref[dirs[si][1]] > 0
        kst = jnp.concatenate([chunk(kb, c) * hm_ref[h] for h in range(GLA_HEADS)], axis=0)
        return jnp.where(keep, _dot_nt(chunk(qb, c), kst), 0.0).astype(BF16)

    bdmask = bd_ref[...]
    states = [d[4][...] for d in dirs]
    nxt = [chunk_scores(si, d[5][0]) for si, d in enumerate(dirs)]
    for i in range(n_chunks):
        cur, nxt = nxt, [None] * len(dirs)
        for si, ((_, _, _, o_ref, _, order), (qb, _, kd, vb, bl)) in enumerate(zip(dirs, ops)):
            c = order[i]
            vc = chunk(vb, c)
            inc = _dot_tn(vc, chunk(kd, c)) * bdmask
            if i + 1 < n_chunks:
                nxt[si] = chunk_scores(si, order[i + 1])
            vbd = jnp.concatenate([vc * vm_ref[h] for h in range(GLA_HEADS)], axis=0)
            o_ref[c * GLA_CHUNK:(c + 1) * GLA_CHUNK, :] = (
                _dot(cur[si], vbd) + _dot_nt(chunk(qb, c), states[si].astype(BF16)))
            states[si] = states[si] * jnp.exp(bl[c]) + inc
    for d, st in zip(dirs, states):
        d[4][...] = st


def _gla_masks():
    r = np.arange(GLA_CHUNK)
    lower = (r[None, :] <= r[:, None]).astype(np.float32)
    tri1 = np.stack([lower, lower.T])
    n_chunks = TOK_BLOCK // GLA_CHUNK
    tri = np.stack([np.kron(np.eye(n_chunks, dtype=np.float32), t) for t in tri1])
    tril = np.tile(tri1, (1, 1, GLA_HEADS))
    klane = np.arange(GLA_KPAD)
    vlane = np.arange(GLA_WIDTH)
    hm = np.stack([np.broadcast_to((klane // GLA_DK) == h, (GLA_CHUNK, GLA_KPAD)) for h in range(GLA_HEADS)])
    vm = np.stack([np.broadcast_to((vlane // GLA_DV) == h, (GLA_CHUNK, GLA_WIDTH)) for h in range(GLA_HEADS)])
    bd = ((vlane[:, None] // GLA_DV) == (klane[None, :] // GLA_DK)).astype(np.float32)
    return (jnp.asarray(tri, BF16), jnp.asarray(tril, F32), jnp.asarray(hm, BF16), jnp.asarray(vm, BF16),
            jnp.asarray(bd))


def _scan_block(d, j, n_ctx, n_all):
    bwd = jnp.where(j < n_ctx, n_ctx - 1 - j, n_all + n_ctx - 1 - j)
    return jnp.where(d == 0, j, bwd)


def _gla(gql, gv):
    bsz, tt, _ = gv.shape
    nb = tt // TOK_BLOCK
    n_ctx = CTX_LEN // TOK_BLOCK
    tri, tril, hm, vm, bd = _gla_masks()
    const2 = lambda b, j: (0, 0)
    const3 = lambda b, j: (0, 0, 0)
    fwd = lambda b, j: (b, j, 0)
    bwd = lambda b, j: (b, _scan_block(1, j, n_ctx, nb), 0)
    fwd_gate = lambda b, j: (b, j, 2)
    bwd_gate = lambda b, j: (b, _scan_block(1, j, n_ctx, nb), 3)
    tok = lambda w, imap: pl.BlockSpec((GLA_ROWS, TOK_BLOCK, w), imap)
    return pl.pallas_call(
        _gla_kernel,
        grid=(bsz // GLA_ROWS, nb),
        in_specs=[
            tok(2 * GLA_KPAD, fwd),
            tok(GLA_WIDTH, fwd),
            tok(GLA_KPAD, fwd_gate),
            tok(2 * GLA_KPAD, bwd),
            tok(GLA_WIDTH, bwd),
            tok(GLA_KPAD, bwd_gate),
            pl.BlockSpec((2, TOK_BLOCK, TOK_BLOCK), const3),
            pl.BlockSpec((2, GLA_CHUNK, GLA_HEADS * GLA_CHUNK), const3),
            pl.BlockSpec((GLA_HEADS, GLA_CHUNK, GLA_KPAD), const3),
            pl.BlockSpec((GLA_HEADS, GLA_CHUNK, GLA_WIDTH), const3),
            pl.BlockSpec((GLA_WIDTH, GLA_KPAD), const2),
        ],
        out_specs=[tok(GLA_WIDTH, fwd), tok(GLA_WIDTH, bwd)],
        out_shape=[jax.ShapeDtypeStruct((bsz, tt, GLA_WIDTH), F32)] * 2,
        scratch_shapes=[pltpu.VMEM((GLA_ROWS, GLA_WIDTH, GLA_KPAD), F32)] * 2,
        compiler_params=pltpu.CompilerParams(
            dimension_semantics=("arbitrary", "arbitrary"), vmem_limit_bytes=VMEM_LIMIT),
        name="gla",
    )(gql, gv, gql, gql, gv, gql, tri, tril, hm, vm, bd)


def _swa_kernel(sink_ref, mask_ref, att_ref, o_ref, *, tt):
    j = pl.program_id(1)
    n_sub = TOK_BLOCK // ATT_BLOCK
    n_ctx_blocks = CTX_LEN // TOK_BLOCK
    last_q = tt // ATT_BLOCK - 1
    lane = lax.broadcasted_iota(jnp.int32, (ATT_BLOCK, 2 * ATT_HD), 1)
    left = lane < ATT_HD
    row2 = lax.broadcasted_iota(jnp.int32, (2 * ATT_BLOCK, 1), 0)
    units = [(r, sub, p) for r in range(SWA_ROWS) for sub in range(n_sub) for p in range(ATT_HEADS // 2)]

    def run(windowed):
        def score(r, sub, p):
            q0 = pl.multiple_of(j * TOK_BLOCK + sub * ATT_BLOCK, ATT_BLOCK)
            qp = att_ref[r, pl.ds(q0, ATT_BLOCK), p * 128:(p + 1) * 128]
            zero = jnp.zeros_like(qp)
            q2 = jnp.concatenate([jnp.where(left, qp, zero), jnp.where(left, zero, qp)], axis=0)
            kcol = ATT_WIDTH + p * 128
            s_c = _dot_nt(q2, att_ref[r, 0:CTX_LEN, kcol:kcol + 128])
            if not windowed:
                return s_c, None, None
            start = pl.multiple_of(jnp.minimum(q0 - ATT_BLOCK, tt - 3 * ATT_BLOCK), ATT_BLOCK)
            s_w = jnp.where(keeps[sub], _dot_nt(q2, att_ref[r, pl.ds(start, 3 * ATT_BLOCK), kcol:kcol + 128]),
                            NEG_INF)
            return s_c, s_w, start

        def softmax(p, s_c, s_w, start):
            sink_col = jnp.where(row2 < ATT_BLOCK, sink_ref[2 * p], sink_ref[2 * p + 1]) * LOG2E
            m = jnp.maximum(jnp.max(s_c, axis=-1, keepdims=True), sink_col)
            if windowed:
                m = jnp.maximum(m, jnp.max(s_w, axis=-1, keepdims=True))
            e_c = jnp.exp2(s_c - m)
            den = jnp.sum(e_c, axis=-1, keepdims=True) + jnp.exp2(sink_col - m)
            e_w = None
            if windowed:
                e_w = jnp.exp2(s_w - m)
                den = den + jnp.sum(e_w, axis=-1, keepdims=True)
                e_w = e_w.astype(BF16)
            return e_c.astype(BF16), e_w, den, start

        def values(r, sub, p, e_c, e_w, den, start):
            vcol = 2 * ATT_WIDTH + p * 128
            pv = _dot(e_c, att_ref[r, 0:CTX_LEN, vcol:vcol + 128])
            if windowed:
                pv = pv + _dot(e_w, att_ref[r, pl.ds(start, 3 * ATT_BLOCK), vcol:vcol + 128])
            o2 = pv / den
            o = jnp.where(left, o2[0:ATT_BLOCK], o2[ATT_BLOCK:2 * ATT_BLOCK])
            o_ref[r, sub * ATT_BLOCK:(sub + 1) * ATT_BLOCK, p * 128:(p + 1) * 128] = o.astype(BF16)

        keeps = []
        if windowed:
            for sub in range(n_sub):
                qi = j * n_sub + sub
                kind = jnp.where(qi == CTX_LEN // ATT_BLOCK, 0, jnp.where(qi == last_q, 2, 1))
                keeps.append(mask_ref[kind] > 0)

        n = len(units)
        sc, pr = {}, {}
        for t in range(n + 2):
            if t < n:
                sc[t] = score(*units[t])
            if 0 <= t - 1 < n:
                pr[t - 1] = softmax(units[t - 1][2], *sc.pop(t - 1))
            if 0 <= t - 2 < n:
                values(*units[t - 2], *pr.pop(t - 2))

    @pl.when(j < n_ctx_blocks)
    def _():
        run(False)

    @pl.when(j >= n_ctx_blocks)
    def _():
        run(True)


def _swa_masks():
    r = np.arange(2 * ATT_BLOCK)[:, None] % ATT_BLOCK
    c = np.arange(3 * ATT_BLOCK)[None, :]
    near = lambda delta: np.abs(r - c + delta) <= WINDOW
    first = near(ATT_BLOCK) & (c >= ATT_BLOCK)
    return jnp.asarray(np.stack([first, near(ATT_BLOCK), near(2 * ATT_BLOCK)]), F32)


def _swa(att, sink):
    bsz, tt, _ = att.shape
    nb = tt // TOK_BLOCK
    return pl.pallas_call(
        functools.partial(_swa_kernel, tt=tt),
        grid=(bsz // SWA_ROWS, nb),
        in_specs=[
            pl.BlockSpec(memory_space=pltpu.SMEM),
            pl.BlockSpec((3, 2 * ATT_BLOCK, 3 * ATT_BLOCK), lambda b, j: (0, 0, 0)),
            pl.BlockSpec((SWA_ROWS, tt, 3 * ATT_WIDTH), lambda b, j: (b, 0, 0)),
        ],
        out_specs=pl.BlockSpec((SWA_ROWS, TOK_BLOCK, ATT_WIDTH), lambda b, j: (b, j, 0)),
        out_shape=jax.ShapeDtypeStruct((bsz, tt, ATT_WIDTH), BF16),
        compiler_params=pltpu.CompilerParams(
            dimension_semantics=("arbitrary", "arbitrary"), vmem_limit_bytes=VMEM_LIMIT),
        name="swa",
    )(sink, _swa_masks(), att)


def _s5_prep_kernel(lr_ref, li_ref, ls_ref, br_ref, bi_ref, cr_ref, ci_ref,
                    a16_ref, mre_ref, mim_ref, rre_ref, rim_ref, k_ref):
    rows = S5_LC * S5_GROUP
    rowg = lax.broadcasted_iota(jnp.int32, (rows, S5_NS), 0) >> 4
    colg = lax.broadcasted_iota(jnp.int32, (rows, S5_NS), 1) >> 6
    same_group = rowg == colg
    taps_t = []
    for d in range(2):
        lr = jnp.minimum(lr_ref[d], -1e-4)
        li = li_ref[d]
        dt = jnp.exp(ls_ref[d])
        mag = jnp.exp(lr * dt)
        ar = mag * jnp.cos(li * dt)
        ai = mag * jnp.sin(li * dt)
        den = lr * lr + li * li
        fr = ((ar - 1.0) * lr + ai * li) / den
        fi = (ai * lr - (ar - 1.0) * li) / den
        br = br_ref[d]
        bi = bi_ref[d]
        bbr = fr * br - fi * bi
        bbi = fr * bi + fi * br
        cr = cr_ref[d]
        ci = ci_ref[d]
        pr = jnp.ones_like(ar)
        pi = jnp.zeros_like(ar)
        cp_re, cp_im = [], []
        for tau in range(S5_LC + 1):
            cpr = cr * pr - ci * pi
            cpi = cr * pi + ci * pr
            if tau < S5_LC:
                mre_ref[d, tau] = pr * bbr - pi * bbi
                mim_ref[d, tau] = pr * bbi + pi * bbr
                cp_re.append(cpr)
                cp_im.append(cpi)
            if tau >= 1:
                rre_ref[d, tau - 1] = cpr
                rim_ref[d, tau - 1] = -cpi
            if tau == S5_LC:
                a16_ref[d, 0:1, :] = pr
                a16_ref[d, 1:2, :] = pi
            pr, pi = pr * ar - pi * ai, pr * ai + pi * ar
        bd_re = jnp.where(same_group, jnp.concatenate([bbr] * S5_GROUPS, axis=0), 0.0)
        bd_im = jnp.where(same_group, jnp.concatenate([bbi] * S5_GROUPS, axis=0), 0.0)
        if d == 1:
            cp_re, cp_im = cp_re[::-1], cp_im[::-1]
        hp = lax.Precision.HIGHEST
        dims = (((1,), (1,)), ((), ()))
        taps_t.append(
            lax.dot_general(bd_re, jnp.concatenate(cp_re, axis=0), dims, precision=hp, preferred_element_type=F32)
            - lax.dot_general(bd_im, jnp.concatenate(cp_im, axis=0), dims, precision=hp,
                              preferred_element_type=F32))
    kf, kb_rev = taps_t
    tap0 = lax.broadcasted_iota(jnp.int32, kf.shape, 1) < S5_GROUP
    k_ref[0] = jnp.where(tap0, kf + pltpu.roll(kb_rev, S5_GROUP, 1), kf)
    k_ref[1] = kb_rev


def _s5_tables(fwd, bwd):
    n_layers = fwd[0].shape[0]
    both = lambda i: jnp.stack([fwd[i], bwd[i]], axis=1)
    lam_re = both(0).reshape(n_layers, 2, 1, S5_NS)
    lam_im = both(1).reshape(n_layers, 2, 1, S5_NS)
    log_step = jnp.repeat(both(2), S5_STATE, axis=-1).reshape(n_layers, 2, 1, S5_NS)
    b_hn = lambda t: t.transpose(0, 1, 4, 2, 3).reshape(n_layers, 2, S5_GROUP, S5_NS)
    c_hn = lambda t: t.transpose(0, 1, 3, 2, 4).reshape(n_layers, 2, S5_GROUP, S5_NS)
    vec = pl.BlockSpec((None, 2, 1, S5_NS), lambda l: (l, 0, 0, 0))
    mat = pl.BlockSpec((None, 2, S5_GROUP, S5_NS), lambda l: (l, 0, 0, 0))
    tab = pl.BlockSpec((None, 2, S5_LC, S5_GROUP, S5_NS), lambda l: (l, 0, 0, 0, 0))
    taps = S5_LC * S5_GROUP
    a16, mre, mim, rre, rim, k = pl.pallas_call(
        _s5_prep_kernel,
        grid=(n_layers,),
        in_specs=[vec, vec, vec, mat, mat, mat, mat],
        out_specs=[
            pl.BlockSpec((None, 2, 2, S5_NS), lambda l: (l, 0, 0, 0)),
            tab, tab, tab, tab,
            pl.BlockSpec((None, 2, S5_WIDTH, taps), lambda l: (l, 0, 0, 0)),
        ],
        out_shape=[
            jax.ShapeDtypeStruct((n_layers, 2, 2, S5_NS), F32),
            jax.ShapeDtypeStruct((n_layers, 2, S5_LC, S5_GROUP, S5_NS), F32),
            jax.ShapeDtypeStruct((n_layers, 2, S5_LC, S5_GROUP, S5_NS), F32),
            jax.ShapeDtypeStruct((n_layers, 2, S5_LC, S5_GROUP, S5_NS), F32),
            jax.ShapeDtypeStruct((n_layers, 2, S5_LC, S5_GROUP, S5_NS), F32),
            jax.ShapeDtypeStruct((n_layers, 2, S5_WIDTH, taps), F32),
        ],
        compiler_params=pltpu.CompilerParams(dimension_semantics=("arbitrary",), vmem_limit_bytes=VMEM_LIMIT),
        name="s5_prep",
    )(lam_re, lam_im, log_step, b_hn(both(3)), b_hn(both(4)), c_hn(both(5)), c_hn(both(6)))

    n_pairs = S5_GROUPS // 2
    lane_group = np.arange(128) // S5_STATE

    def pair_rows(t, flip):
        t = t[:, ::-1] if flip else t
        t = t.reshape(n_layers, S5_LC, S5_GROUP, n_pairs, 128).transpose(0, 3, 1, 2, 4)
        own = jnp.asarray(lane_group[None, :] == np.arange(2)[:, None])
        t = jnp.where(own[None, None, :, None, None, :], t[:, :, None], 0.0)
        return t.reshape(n_layers, n_pairs, 2 * taps, 128)

    m_f = jnp.concatenate([pair_rows(mre[:, 0], True), pair_rows(mim[:, 0], True)], axis=-1).astype(BF16)
    m_b = jnp.concatenate([pair_rows(mre[:, 1], False), pair_rows(mim[:, 1], False)], axis=-1).astype(BF16)

    r_t = jnp.concatenate([pair_rows(rre[:, 0], False), pair_rows(rim[:, 0], False),
                           pair_rows(rre[:, 1], True), pair_rows(rim[:, 1], True)], axis=-1).astype(BF16)

    zeros = jnp.zeros((n_layers, S5_WIDTH, taps), F32)
    f2 = jnp.concatenate([zeros, k[:, 0]], axis=-1)
    b2 = jnp.concatenate([k[:, 1], zeros], axis=-1)
    t_f = jnp.stack([f2[..., taps - S5_GROUP * st:2 * taps - S5_GROUP * st] for st in range(S5_LC)], axis=2)
    t_b = jnp.stack([b2[..., S5_GROUP * (S5_LC - 1 - st):S5_GROUP * (S5_LC - 1 - st) + taps]
                     for st in range(S5_LC)], axis=2)
    later = jnp.asarray(np.arange(taps)[None, :] // S5_GROUP >= np.arange(S5_LC)[:, None])
    toe = jnp.where(later[None, None], t_f, t_b)
    toe = toe.reshape(n_layers, S5_GROUPS, S5_GROUP, S5_LC, taps).transpose(0, 1, 3, 2, 4)
    toe = toe.reshape(n_layers, S5_GROUPS, taps, taps).astype(BF16)
    return a16, m_f, m_b, r_t, toe


def _lane_block_transpose(cols):
    lane = lax.broadcasted_iota(jnp.int32, cols[0].shape, 1)
    out = [None] * 32
    for ah in range(2):
        for bh in range(2):
            v = [cols[(ah * 8 + al) * 2 + bh] for al in range(8)]
            for kbit in range(3):
                width = 16 << kbit
                low = ((lane >> (4 + kbit)) & 1) == 0
                nxt = list(v)
                for i in range(8):
                    if i & (1 << kbit):
                        continue
                    lo_v, hi_v = v[i], v[i | (1 << kbit)]
                    nxt[i] = jnp.where(low, lo_v, pltpu.roll(hi_v, width, 1))
                    nxt[i | (1 << kbit)] = jnp.where(low, pltpu.roll(lo_v, 128 - width, 1), hi_v)
                v = nxt
            for bl in range(8):
                out[(bh * 8 + bl) * 2 + ah] = v[bl]
    return out


def _s5_increments(ush, m_ref, d_re, d_im):
    for p in range(S5_GROUPS // 2):
        dp = _dot(ush[:, p * 512:(p + 1) * 512], m_ref[p])
        d_re[p] = dp[:, 0:128]
        d_im[p] = dp[:, 128:256]


def _s5_recurrence(a16_ref, d_re, d_im, x_re, x_im, sr_ref, si_ref, order, bsz):
    ar = a16_ref[0:1, :]
    ai = a16_ref[1:2, :]
    sr = sr_ref[...]
    si = si_ref[...]
    n_slabs = S5_NS // 128
    gather = lambda ref, rows: jnp.concatenate([ref[p, rows, :] for p in range(n_slabs)], axis=1)
    for c in order:
        rows = pl.ds(c, bsz, stride=CH_BLOCK)
        for p in range(n_slabs):
            x_re[p, rows, :] = sr[:, p * 128:(p + 1) * 128]
            x_im[p, rows, :] = si[:, p * 128:(p + 1) * 128]
        sr, si = ar * sr - ai * si + gather(d_re, rows), ar * si + ai * sr + gather(d_im, rows)
    sr_ref[...] = sr
    si_ref[...] = si


def _s5_fwd_kernel(uc_ref, a16_ref, m_ref, ush_ref, xin_ref, d_re, d_im, x_re, x_im, sr_ref, si_ref, *, bsz):
    @pl.when(pl.program_id(0) == 0)
    def _():
        sr_ref[...] = jnp.zeros_like(sr_ref)
        si_ref[...] = jnp.zeros_like(si_ref)

    rows = bsz * CH_BLOCK
    ub = uc_ref[...].reshape(rows, S5_CW).astype(BF16)
    packed = pltpu.bitcast(ub, jnp.uint32)
    cols = _lane_block_transpose([packed[:, v * 128:(v + 1) * 128] for v in range(32)])
    ush = pltpu.bitcast(jnp.concatenate(cols, axis=1), BF16)
    ush_ref[...] = ush.reshape(bsz, CH_BLOCK, S5_CW)
    _s5_increments(ush, m_ref, d_re, d_im)
    _s5_recurrence(a16_ref, d_re, d_im, x_re, x_im, sr_ref, si_ref, range(CH_BLOCK), bsz)
    for p in range(S5_NS // 128):
        xin_ref[:, :, p * 128:(p + 1) * 128] = x_re[p].astype(BF16).reshape(bsz, CH_BLOCK, 128)
        xin_ref[:, :, S5_NS + p * 128:S5_NS + (p + 1) * 128] = x_im[p].astype(BF16).reshape(bsz, CH_BLOCK, 128)


def _s5_bwd_kernel(ush_ref, xf_ref, a16_ref, m_ref, toe_ref, r_ref, y_ref, d_re, d_im, x_re, x_im, sr_ref, si_ref,
                   *, bsz):
    @pl.when(pl.program_id(0) == 0)
    def _():
        sr_ref[...] = jnp.zeros_like(sr_ref)
        si_ref[...] = jnp.zeros_like(si_ref)

    rows = bsz * CH_BLOCK
    ush = ush_ref[...].reshape(rows, S5_CW)
    _s5_increments(ush, m_ref, d_re, d_im)
    _s5_recurrence(a16_ref, d_re, d_im, x_re, x_im, sr_ref, si_ref, reversed(range(CH_BLOCK)), bsz)
    xf = xf_ref[...].reshape(rows, 2 * S5_NS)
    ycols = []
    for p in range(S5_GROUPS // 2):
        lanes = slice(p * 128, (p + 1) * 128)
        xcat = jnp.concatenate([xf[:, lanes], xf[:, S5_NS + p * 128:S5_NS + (p + 1) * 128],
                                x_re[p].astype(BF16), x_im[p].astype(BF16)], axis=1)
        carry = _dot_nt(xcat, r_ref[p])
        for g2 in range(2):
            g = 2 * p + g2
            yg = carry[:, g2 * 256:(g2 + 1) * 256] + _dot(ush[:, g * 256:(g + 1) * 256], toe_ref[g])
            ycols += [yg[:, 0:128], yg[:, 128:256]]
    ycols = _lane_block_transpose(ycols)
    y_ref[...] = jnp.concatenate(ycols, axis=1).reshape(bsz, CH_BLOCK, S5_CW)


def _s5(u_c, tables, layer):
    a16, m_f, m_b, r_t, toe = tables
    bsz, n_rows, _ = u_c.shape
    nb = n_rows // CH_BLOCK
    n_ctx = CTX_LEN // TOK_BLOCK
    rows = bsz * CH_BLOCK
    blk3 = lambda w: (bsz, CH_BLOCK, w)
    scratch = [pltpu.VMEM((S5_NS // 128, rows, 128), F32)] * 4 + [pltpu.VMEM((bsz, S5_NS), F32)] * 2
    params = pltpu.CompilerParams(dimension_semantics=("arbitrary",), vmem_limit_bytes=VMEM_LIMIT)
    ush, xin_f = pl.pallas_call(
        functools.partial(_s5_fwd_kernel, bsz=bsz),
        grid=(nb,),
        in_specs=[
            pl.BlockSpec(blk3(S5_CW), lambda j: (0, j, 0)),
            pl.BlockSpec((None, None, 2, S5_NS), lambda j: (layer, 0, 0, 0)),
            pl.BlockSpec((None, S5_GROUPS // 2, 512, 256), lambda j: (layer, 0, 0, 0)),
        ],
        out_specs=[
            pl.BlockSpec(blk3(S5_CW), lambda j: (0, j, 0)),
            pl.BlockSpec(blk3(2 * S5_NS), lambda j: (0, j, 0)),
        ],
        out_shape=[
            jax.ShapeDtypeStruct((bsz, n_rows, S5_CW), BF16),
            jax.ShapeDtypeStruct((bsz, n_rows, 2 * S5_NS), BF16),
        ],
        scratch_shapes=scratch,
        compiler_params=params,
        name="s5_fwd",
    )(u_c, a16, m_f)
    back = lambda j: (0, _scan_block(1, j, n_ctx, nb), 0)
    return pl.pallas_call(
        functools.partial(_s5_bwd_kernel, bsz=bsz),
        grid=(nb,),
        in_specs=[
            pl.BlockSpec(blk3(S5_CW), back),
            pl.BlockSpec(blk3(2 * S5_NS), back),
            pl.BlockSpec((None, None, 2, S5_NS), lambda j: (layer, 1, 0, 0)),
            pl.BlockSpec((None, S5_GROUPS // 2, 512, 256), lambda j: (layer, 0, 0, 0)),
            pl.BlockSpec((None, S5_GROUPS, 256, 256), lambda j: (layer, 0, 0, 0)),
            pl.BlockSpec((None, S5_GROUPS // 2, 512, 512), lambda j: (layer, 0, 0, 0)),
        ],
        out_specs=pl.BlockSpec(blk3(S5_CW), back),
        out_shape=jax.ShapeDtypeStruct((bsz, n_rows, S5_CW), F32),
        scratch_shapes=scratch,
        compiler_params=params,
        name="s5_bwd",
    )(ush, xin_f, a16, m_b, toe, r_t)


def _out_kernel(xc_ref, x_ref, mod_ref, gof_ref, gob_ref, g_ref, att_ref, y_ref, uc_ref, gnw_ref, hm_ref, d_ref,
                gluw_ref, glub_ref, wo_ref, n2_ref, w1_ref, w2_ref, fn_ref, o_ref, ys_ref,
                *, first_block, final, split):
    j = pl.program_id(1) + first_block

    def mixed(r):
        m = mod_ref[r]
        mrow = jnp.where(j > 0, m[1:2, :], m[0:1, :])
        g1 = mrow[:, 2 * D_MODEL:3 * D_MODEL]
        sh2 = mrow[:, 3 * D_MODEL:4 * D_MODEL]
        sc2 = mrow[:, 4 * D_MODEL:5 * D_MODEL]
        g2 = mrow[:, 5 * D_MODEL:6 * D_MODEL]

        o = gof_ref[r] + gob_ref[r]
        ms = _dot((o * o).astype(BF16), hm_ref[...]) * (1.0 / GLA_DV)
        a = o * lax.rsqrt(ms + EPS) * gnw_ref[...] * _silu(g_ref[r])

        yc = y_ref[r] + d_ref[...] * uc_ref[r]
        for st in range(S5_LC):
            for half in range(S5_WIDTH // 128):
                lane0 = st * S5_WIDTH + half * 128
                ys_ref[r, half, pl.ds(st, CH_BLOCK, stride=S5_LC), :] = yc[:, lane0:lane0 + 128]
        yy = jnp.concatenate([ys_ref[r, half] for half in range(S5_WIDTH // 128)], axis=1)
        ge = 0.5 * yy * (1.0 + jnp.tanh(math.sqrt(2.0 / math.pi) * (yy + 0.044715 * (yy * yy * yy))))
        z = _dot(ge.astype(BF16), gluw_ref[...]) + glub_ref[...]
        s = z[:, 0:S5_WIDTH] * _sigmoid(z[:, S5_WIDTH:2 * S5_WIDTH])

        proj = _dot(jnp.concatenate([a.astype(BF16), att_ref[r], s.astype(BF16)], axis=1), wo_ref[...])
        x_in = jnp.where(j == 0, xc_ref[r], x_ref[r]) if split else x_ref[r]
        x1 = x_in + g1 * proj
        y2 = x1 * lax.rsqrt(jnp.mean(x1 * x1, axis=-1, keepdims=True) + EPS) * n2_ref[...]
        return x1, (y2 * (1.0 + sc2) + sh2).astype(BF16), g2

    def finish(r, x1, g2, mlp):
        x2 = x1 + g2 * mlp
        if final:
            x2 = x2 * lax.rsqrt(jnp.mean(x2 * x2, axis=-1, keepdims=True) + EPS) * fn_ref[...]
        o_ref[r] = x2

    cur = mixed(0)
    for r in range(OUT_ROWS):
        x1, h2, g2 = cur
        hid = jnp.maximum(_dot(h2, w1_ref[...]), 0.0)
        if r + 1 < OUT_ROWS:
            cur = mixed(r + 1)
        mlp = _dot((hid * hid).astype(BF16), w2_ref[...])
        finish(r, x1, g2, mlp)


def _outproj(x_ctx, x_rest, lat_skip, modsel, gla_f, gla_b, gg, att_o, y_c, u_c, gnw, hm, s5d, gluw, glub, wo, n2w,
             w1, w2, fnw, layer, final):
    bsz, tt, _ = gg.shape
    nb = tt // TOK_BLOCK
    first = CTX_LEN // TOK_BLOCK if final else 0
    nsteps = nb - first
    const2 = lambda b, j: (0, 0)
    lyr3 = lambda b, j: (layer, 0, 0)
    single = pl.Buffered(1)
    return pl.pallas_call(
        functools.partial(_out_kernel, first_block=first, final=final, split=lat_skip > 0),
        grid=(bsz // OUT_ROWS, nsteps),
        in_specs=_stream_specs(lat_skip, first, rows=OUT_ROWS) + [
            pl.BlockSpec((OUT_ROWS, 2, 6 * D_MODEL), lambda b, j: (b, 0, 0)),
            pl.BlockSpec((OUT_ROWS, TOK_BLOCK, GLA_WIDTH), lambda b, j: (b, j + first, 0)),
            pl.BlockSpec((OUT_ROWS, TOK_BLOCK, GLA_WIDTH), lambda b, j: (b, j + first, 0)),
            pl.BlockSpec((OUT_ROWS, TOK_BLOCK, GLA_WIDTH), lambda b, j: (b, j + first, 0)),
            pl.BlockSpec((OUT_ROWS, TOK_BLOCK, ATT_WIDTH), lambda b, j: (b, j + first, 0)),
            pl.BlockSpec((OUT_ROWS, CH_BLOCK, S5_CW), lambda b, j: (b, j + first, 0)),
            pl.BlockSpec((OUT_ROWS, CH_BLOCK, S5_CW), lambda b, j: (b, j + first, 0)),
            pl.BlockSpec((1, GLA_WIDTH), const2),
            pl.BlockSpec((GLA_WIDTH, GLA_WIDTH), const2),
            pl.BlockSpec((1, S5_CW), const2),
            pl.BlockSpec((None, S5_WIDTH, 2 * S5_WIDTH), lyr3, pipeline_mode=single),
            pl.BlockSpec((1, 2 * S5_WIDTH), const2),
            pl.BlockSpec((None, D_MODEL, D_MODEL), lyr3, pipeline_mode=single),
            pl.BlockSpec((1, D_MODEL), const2),
            pl.BlockSpec((None, D_MODEL, D_FF), lyr3, pipeline_mode=single),
            pl.BlockSpec((None, D_FF, D_MODEL), lyr3, pipeline_mode=single),
            pl.BlockSpec((1, D_MODEL), const2),
        ],
        out_specs=pl.BlockSpec((OUT_ROWS, TOK_BLOCK, D_MODEL), lambda b, j: (b, j, 0)),
        out_shape=jax.ShapeDtypeStruct((bsz, nsteps * TOK_BLOCK, D_MODEL), F32),
        scratch_shapes=[pltpu.VMEM((OUT_ROWS, S5_WIDTH // 128, TOK_BLOCK, 128), F32)],
        compiler_params=pltpu.CompilerParams(
            dimension_semantics=("arbitrary", "arbitrary"), vmem_limit_bytes=VMEM_LIMIT),
        name="outproj_mlp",
    )(x_ctx, x_rest, modsel, gla_f, gla_b, gg, att_o, y_c, u_c, gnw, hm, s5d, gluw, glub, wo, n2w, w1, w2, fnw)


def _rope_tables(tt):
    n_lat = tt - CTX_LEN
    rows = n_lat // GRID_W
    row = jnp.repeat(jnp.arange(rows, dtype=F32), GRID_W)
    col = jnp.tile(jnp.arange(GRID_W, dtype=F32), rows)
    n_freq = ATT_HD // 4
    inv_freq = ROPE_BASE ** (-jnp.arange(n_freq, dtype=F32) / n_freq)
    ang_r = row[:, None] * inv_freq
    ang_c = col[:, None] * inv_freq
    ang = jnp.concatenate([ang_r, ang_r, ang_c, ang_c], axis=-1)
    cos = jnp.concatenate([jnp.ones((CTX_LEN, ATT_HD), F32), jnp.cos(ang)], axis=0)
    sin = jnp.concatenate([jnp.zeros((CTX_LEN, ATT_HD), F32), jnp.sin(ang)], axis=0)
    up_quarter = (np.arange(ATT_HD) // 16) % 2 == 0
    sa = jnp.where(up_quarter, -sin, 0.0)
    sb = jnp.where(up_quarter, 0.0, sin)
    two = lambda t: jnp.concatenate([t, t], axis=-1)
    return two(cos), two(sa), two(sb)


def _pad_cols(w, width):
    return jnp.pad(w, ((0, 0), (0, 0), (0, width - w.shape[-1])))


def _layout_w_in(w_in):
    offs = np.cumsum([0, GLA_KW, GLA_KW, GLA_WIDTH, GLA_WIDTH, GLA_RANK, GLA_RANK, ATT_WIDTH, ATT_KVW, ATT_KVW,
                      S5_WIDTH])
    q, k, v, g, zf, zb, aq, ak, av, u = [w_in[:, :, offs[i]:offs[i + 1]] for i in range(10)]
    z = jnp.concatenate([zf, zb], axis=-1)
    cols = [v, g, _pad_cols(q, GLA_KPAD), _pad_cols(k, GLA_KPAD), _pad_cols(z, 128), aq, ak, av, u]
    return jnp.concatenate(cols, axis=-1).astype(BF16)


def kernel(x, c, ctx, c_ctx, w_mod, b_mod, norm1_w, norm2_w, w_in, gla_wa_f, gla_ba_f, gla_wa_b, gla_ba_b,
           gla_norm_w, attn_sink, s5_lam_re_f, s5_lam_im_f, s5_log_step_f, s5_b_re_f, s5_b_im_f, s5_c_re_f,
           s5_c_im_f, s5_lam_re_b, s5_lam_im_b, s5_log_step_b, s5_b_re_b, s5_b_im_b, s5_c_re_b, s5_c_im_b,
           s5_d, glu_w, glu_b, w_out, mlp_w1, mlp_w2, final_norm_w):
    bsz, seq, _ = x.shape
    n_layers = w_mod.shape[0]
    tt = CTX_LEN + seq
    assert all(bsz % rows == 0 for rows in (8, IN_ROWS, GLA_ROWS, SWA_ROWS, OUT_ROWS))
    assert seq % TOK_BLOCK == 0 and seq >= 2 * TOK_BLOCK and ctx.shape[1] == CTX_LEN

    mod_rows = -(-(bsz + 1) // 8) * 8
    cvec = jnp.zeros((mod_rows, D_MODEL), F32).at[:bsz].set(c).at[bsz].set(c_ctx)
    mod = _modulation(cvec, w_mod, b_mod)
    mod_ctx = jnp.broadcast_to(mod[:, bsz][:, None], (n_layers, bsz, 6 * D_MODEL))
    modsel = jnp.stack([mod_ctx, mod[:, :bsz]], axis=2)

    w_in_p = _layout_w_in(w_in)
    wa_cat = jnp.zeros((n_layers, 128, 2 * GLA_KPAD), F32)
    wa_cat = wa_cat.at[:, 0:GLA_RANK, 0:GLA_KW].set(gla_wa_f)
    wa_cat = wa_cat.at[:, GLA_RANK:2 * GLA_RANK, GLA_KPAD:GLA_KPAD + GLA_KW].set(gla_wa_b).astype(BF16)
    ba_cat = jnp.zeros((n_layers, 1, 2 * GLA_KPAD), F32)
    ba_cat = ba_cat.at[:, 0, 0:GLA_KW].set(gla_ba_f).at[:, 0, GLA_KPAD:GLA_KPAD + GLA_KW].set(gla_ba_b)
    qscale = jnp.ones((1, 2 * GLA_KPAD), F32).at[:, 0:GLA_KW].set(GLA_DK ** -0.5)
    cos_t, sa_t, sb_t = _rope_tables(tt)
    gnw = jnp.tile(gla_norm_w, (1, GLA_HEADS))[:, None, :]
    head = np.arange(GLA_WIDTH) // GLA_DV
    hm = jnp.asarray(head[:, None] == head[None, :], BF16)
    wo_b = w_out.astype(BF16)
    w1_b = mlp_w1.astype(BF16)
    w2_b = mlp_w2.astype(BF16)
    gluw_b = glu_w.astype(BF16)

    s5_tab = _s5_tables(
        (s5_lam_re_f, s5_lam_im_f, s5_log_step_f, s5_b_re_f, s5_b_im_f, s5_c_re_f, s5_c_im_f),
        (s5_lam_re_b, s5_lam_im_b, s5_log_step_b, s5_b_re_b, s5_b_im_b, s5_c_re_b, s5_c_im_b))
    s5_dt = jnp.tile(s5_d, (1, S5_LC))[:, None, :]

    stream = (ctx, x, CTX_LEN // TOK_BLOCK)
    for l in range(n_layers):
        final = l == n_layers - 1
        gv, gg, gql, att, u_c = _inproj(*stream, modsel[l], norm1_w[l][None], w_in_p, wa_cat, ba_cat[l],
                                             qscale, cos_t, sa_t, sb_t, l)
        gla_f, gla_b = _gla(gql, gv)
        att_o = _swa(att, attn_sink[l])
        y_c = _s5(u_c, s5_tab, l)
        xs = _outproj(*stream, modsel[l], gla_f, gla_b, gg, att_o, y_c, u_c,
                      gnw[l], hm, s5_dt[l], gluw_b, glu_b[l][None], wo_b, norm2_w[l][None], w1_b, w2_b,
                      final_norm_w[None], l, final)
        stream = (xs, xs, 0)
    return xs
```

```python
import functools
import math

import jax
import jax.numpy as jnp
import numpy as np
from jax import lax
from jax.experimental import pallas as pl
from jax.experimental.pallas import tpu as pltpu

F32 = jnp.float32
BF16 = jnp.bfloat16

D_MODEL = 1024
D_FF = 4 * D_MODEL
CTX_LEN = 256
GRID_W = 64
EPS = 1e-6
NEG_INF = -1e30
LOG2E = math.log2(math.e)

GLA_HEADS = 4
GLA_DV = 96
GLA_DK = 48
GLA_WIDTH = GLA_HEADS * GLA_DV
GLA_KW = GLA_HEADS * GLA_DK
GLA_KPAD = 256
GLA_RANK = 16
GLA_TAU = 16.0
GLA_CHUNK = 64

ATT_HD = 64
ATT_HEADS = 6
ATT_KV_HEADS = 2
ATT_WIDTH = ATT_HEADS * ATT_HD
ATT_KVW = ATT_KV_HEADS * ATT_HD
WINDOW = 128
ATT_BLOCK = 128
ROPE_BASE = 10000.0

S5_WIDTH = 256
S5_GROUP = 16
S5_GROUPS = 16
S5_STATE = 64
S5_NS = S5_GROUPS * S5_STATE
S5_LC = 16
S5_CW = S5_LC * S5_WIDTH

TOK_BLOCK = 256
CH_BLOCK = TOK_BLOCK // S5_LC
GLA_ROWS = 4
SWA_ROWS = 2
IN_ROWS = 4
OUT_ROWS = 2

C_V, C_G, C_Q, C_K, C_Z, C_AQ, C_AK, C_AV, C_U = 0, 384, 768, 1024, 1280, 1408, 1792, 1920, 2048
IN_PAD = 2304
C_QK_END = C_K + GLA_KPAD

V7X_VMEM_BYTES = 64 * 1024 * 1024
VMEM_LIMIT = V7X_VMEM_BYTES - 8 * 1024 * 1024


def _sigmoid(x):
    return 1.0 / (1.0 + jnp.exp(-x))


def _silu(x):
    return x * _sigmoid(x)


def _dot(a, b):
    return jnp.dot(a, b, preferred_element_type=F32)


def _dot_nt(a, b):
    return lax.dot_general(a, b, (((1,), (1,)), ((), ())), preferred_element_type=F32)


def _dot_tn(a, b):
    return lax.dot_general(a, b, (((0,), (0,)), ((), ())), preferred_element_type=F32)


def _mod_kernel(c_ref, w_ref, b_ref, o_ref):
    a = _silu(c_ref[...]).astype(BF16)
    o_ref[...] = _dot(a, w_ref[...].astype(BF16)) + b_ref[...]


def _modulation(cvec, w_mod, b_mod):
    n_layers = w_mod.shape[0]
    rows = cvec.shape[0]
    tn = 1536
    return pl.pallas_call(
        _mod_kernel,
        grid=(n_layers, 6 * D_MODEL // tn),
        in_specs=[
            pl.BlockSpec((rows, D_MODEL), lambda l, n: (0, 0)),
            pl.BlockSpec((None, D_MODEL, tn), lambda l, n: (l, 0, n)),
            pl.BlockSpec((None, 1, tn), lambda l, n: (l, 0, n)),
        ],
        out_specs=pl.BlockSpec((None, rows, tn), lambda l, n: (l, 0, n)),
        out_shape=jax.ShapeDtypeStruct((n_layers, rows, 6 * D_MODEL), F32),
        compiler_params=pltpu.CompilerParams(
            dimension_semantics=("arbitrary", "arbitrary"), vmem_limit_bytes=VMEM_LIMIT),
        name="modulation",
    )(cvec, w_mod, b_mod.reshape(n_layers, 1, 6 * D_MODEL))


def _inproj_kernel(xc_ref, x_ref, mod_ref, n1_ref, w_ref, wa_ref, ba_ref, qs_ref, cos_ref, sa_ref, sb_ref,
                   gv_ref, gg_ref, gqk_ref, gla_ref, att_ref, uc_ref, us_ref, *, split):
    j = pl.program_id(1)
    cos = cos_ref[...]
    sa = sa_ref[...]
    sb = sb_ref[...]

    def normed(r):
        xf = jnp.where(j == 0, xc_ref[r], x_ref[r]) if split else x_ref[r]
        y = xf * lax.rsqrt(jnp.mean(xf * xf, axis=-1, keepdims=True) + EPS) * n1_ref[...]
        m = mod_ref[r]
        mrow = jnp.where(j > 0, m[1:2, :], m[0:1, :])
        sh1 = mrow[:, 0:D_MODEL]
        sc1 = mrow[:, D_MODEL:2 * D_MODEL]
        return (y * (1.0 + sc1) + sh1).astype(BF16)

    def rope(t, reps):
        w = t.shape[-1]
        c3 = jnp.concatenate([cos] * reps, axis=-1) if reps > 1 else cos
        a3 = jnp.concatenate([sa] * reps, axis=-1) if reps > 1 else sa
        b3 = jnp.concatenate([sb] * reps, axis=-1) if reps > 1 else sb
        up = pltpu.roll(t, w - 16, 1)
        dn = pltpu.roll(t, 16, 1)
        return t * c3 + up * a3 + dn * b3

    def tail_gla(r, p):
        gv_ref[r] = p[:, C_V:C_V + GLA_WIDTH].astype(BF16)
        gg_ref[r] = p[:, C_G:C_G + GLA_WIDTH]
        gqk_ref[r] = p[:, C_Q:C_QK_END] * qs_ref[...]

    def tail_rest(r, p):
        off = C_QK_END
        z = p[:, C_Z - off:C_Z - off + 128].astype(BF16)
        zg = _dot(z, wa_ref[...]) + ba_ref[...]
        gla_ref[r] = (jnp.minimum(zg, 0.0) - jnp.log1p(jnp.exp(-jnp.abs(zg)))) * (1.0 / GLA_TAU)

        aq = rope(p[:, C_AQ - off:C_AQ - off + ATT_WIDTH], 3) * (ATT_HD ** -0.5 * LOG2E)
        ak = rope(p[:, C_AK - off:C_AK - off + ATT_KVW], 1)
        av = p[:, C_AV - off:C_AV - off + ATT_KVW]
        left = lax.broadcasted_iota(jnp.int32, ak.shape, 1) < ATT_HD

        def expand(t):
            sw = pltpu.roll(t, ATT_HD, 1)
            return [jnp.where(left, t, sw), t, jnp.where(left, sw, t)]

        att_ref[r] = jnp.concatenate([aq] + expand(ak) + expand(av), axis=-1).astype(BF16)
        for half in range(S5_WIDTH // 128):
            us_ref[r, half] = p[:, C_U - off + half * 128:C_U - off + (half + 1) * 128]
        for st in range(S5_LC):
            for half in range(S5_WIDTH // 128):
                lane0 = st * S5_WIDTH + half * 128
                uc_ref[r, :, lane0:lane0 + 128] = us_ref[r, half, pl.ds(st, CH_BLOCK, stride=S5_LC), :]

    h = normed(0)
    prev_rest = None
    for r in range(IN_ROWS):
        p_gla = _dot(h, w_ref[:, 0:C_QK_END])
        if prev_rest is not None:
            tail_rest(r - 1, prev_rest)
        p_rest = _dot(h, w_ref[:, C_QK_END:IN_PAD])
        if r + 1 < IN_ROWS:
            h = normed(r + 1)
        tail_gla(r, p_gla)
        prev_rest = p_rest
    tail_rest(IN_ROWS - 1, prev_rest)


def _stream_specs(lat_skip, first=0, rows=None):
    return [pl.BlockSpec((rows, TOK_BLOCK, D_MODEL), lambda b, j: (b, 0, 0)),
            pl.BlockSpec((rows, TOK_BLOCK, D_MODEL), lambda b, j: (b, jnp.maximum(j + first - lat_skip, 0), 0))]


def _inproj(x_ctx, x_rest, lat_skip, modsel, n1w, w_in_p, wa_cat, ba_cat, qscale, cos_t, sa_t, sb_t, layer):
    bsz = x_ctx.shape[0]
    tt = cos_t.shape[0]
    nb = tt // TOK_BLOCK
    const = lambda b, j: (0, 0)
    lyr3 = lambda b, j: (layer, 0, 0)
    tok = lambda w: pl.BlockSpec((IN_ROWS, TOK_BLOCK, w), lambda b, j: (b, j, 0))
    return pl.pallas_call(
        functools.partial(_inproj_kernel, split=lat_skip > 0),
        grid=(bsz // IN_ROWS, nb),
        in_specs=_stream_specs(lat_skip, rows=IN_ROWS) + [
            pl.BlockSpec((IN_ROWS, 2, 6 * D_MODEL), lambda b, j: (b, 0, 0)),
            pl.BlockSpec((1, D_MODEL), const),
            pl.BlockSpec((None, D_MODEL, IN_PAD), lyr3),
            pl.BlockSpec((None, 128, 2 * GLA_KPAD), lyr3),
            pl.BlockSpec((1, 2 * GLA_KPAD), const),
            pl.BlockSpec((1, 2 * GLA_KPAD), const),
            pl.BlockSpec((TOK_BLOCK, 128), lambda b, j: (j, 0)),
            pl.BlockSpec((TOK_BLOCK, 128), lambda b, j: (j, 0)),
            pl.BlockSpec((TOK_BLOCK, 128), lambda b, j: (j, 0)),
        ],
        out_specs=[tok(GLA_WIDTH), tok(GLA_WIDTH), tok(2 * GLA_KPAD), tok(2 * GLA_KPAD), tok(3 * ATT_WIDTH),
                   pl.BlockSpec((IN_ROWS, CH_BLOCK, S5_CW), lambda b, j: (b, j, 0))],
        out_shape=[
            jax.ShapeDtypeStruct((bsz, tt, GLA_WIDTH), BF16),
            jax.ShapeDtypeStruct((bsz, tt, GLA_WIDTH), F32),
            jax.ShapeDtypeStruct((bsz, tt, 2 * GLA_KPAD), F32),
            jax.ShapeDtypeStruct((bsz, tt, 2 * GLA_KPAD), F32),
            jax.ShapeDtypeStruct((bsz, tt, 3 * ATT_WIDTH), BF16),
            jax.ShapeDtypeStruct((bsz, tt // S5_LC, S5_CW), F32),
        ],
        scratch_shapes=[pltpu.VMEM((IN_ROWS, S5_WIDTH // 128, TOK_BLOCK, 128), F32)],
        compiler_params=pltpu.CompilerParams(
            dimension_semantics=("arbitrary", "arbitrary"), vmem_limit_bytes=VMEM_LIMIT),
        name="inproj",
    )(x_ctx, x_rest, modsel, n1w, w_in_p, wa_cat, ba_cat, qscale, cos_t, sa_t, sb_t)


def _gla_kernel(qkf_ref, vf_ref, laf_ref, qkb_ref, vb_ref, lab_ref, tri_ref, tril_ref, hm_ref, vm_ref, bd_ref,
                of_ref, ob_ref, stf_ref, stb_ref):
    j = pl.program_id(1)

    @pl.when(j == 0)
    def _():
        stf_ref[...] = jnp.zeros_like(stf_ref)
        stb_ref[...] = jnp.zeros_like(stb_ref)

    n_chunks = TOK_BLOCK // GLA_CHUNK
    chunk = lambda t, c: t[c * GLA_CHUNK:(c + 1) * GLA_CHUNK]
    dirs = []
    for r in range(GLA_ROWS):
        dirs.append(((qkf_ref.at[r], vf_ref.at[r], laf_ref.at[r]), 0, GLA_CHUNK - 1, of_ref.at[r],
                     stf_ref.at[r], list(range(n_chunks))))
        dirs.append(((qkb_ref.at[r], vb_ref.at[r], lab_ref.at[r]), 1, 0, ob_ref.at[r],
                     stb_ref.at[r], list(reversed(range(n_chunks)))))

    cum = []
    for (_, _, la_ref), d, _, _, _, _ in dirs:
        la = la_ref[...]
        hi = la.astype(BF16)
        lo = (la - hi.astype(F32)).astype(BF16)
        cum.append(_dot(tri_ref[d], hi) + _dot(tri_ref[d], lo))

    ops = []
    for ((qk_ref, v_ref, _), d, last_row, _, _, _), b in zip(dirs, cum):
        q = qk_ref[:, 0:GLA_KPAD]
        k = qk_ref[:, GLA_KPAD:2 * GLA_KPAD]
        vb = v_ref[...]
        bl = [chunk(b, c)[last_row:last_row + 1, :] for c in range(n_chunks)]
        blx = jnp.concatenate([jnp.broadcast_to(t, (GLA_CHUNK, GLA_KPAD)) for t in bl], axis=0)
        qb = (q * jnp.exp(b)).astype(BF16)
        kb = (k * jnp.exp(-b)).astype(BF16)
        kd = (k * jnp.exp(blx - b)).astype(BF16)
        ops.append((qb, kb, kd, vb, bl))

    def chunk_scores(si, c):
        qb, kb = ops[si][0], ops[si][1]
        keep = tril_ref[dirs[si][1]] > 0
        kst = jnp.concatenate([chunk(kb, c) * hm_ref[h] for h in range(GLA_HEADS)], axis=0)
        return jnp.where(keep, _dot_nt(chunk(qb, c), kst), 0.0).astype(BF16)

    bdmask = bd_ref[...]
    states = [d[4][...] for d in dirs]
    nxt = [chunk_scores(si, d[5][0]) for si, d in enumerate(dirs)]
    for i in range(n_chunks):
        cur, nxt = nxt, [None] * len(dirs)
        for si, ((_, _, _, o_ref, _, order), (qb, _, kd, vb, bl)) in enumerate(zip(dirs, ops)):
            c = order[i]
            vc = chunk(vb, c)
            inc = _dot_tn(vc, chunk(kd, c)) * bdmask
            if i + 1 < n_chunks:
                nxt[si] = chunk_scores(si, order[i + 1])
            vbd = jnp.concatenate([vc * vm_ref[h] for h in range(GLA_HEADS)], axis=0)
            o_ref[c * GLA_CHUNK:(c + 1) * GLA_CHUNK, :] = (
                _dot(cur[si], vbd) + _dot_nt(chunk(qb, c), states[si].astype(BF16)))
            states[si] = states[si] * jnp.exp(bl[c]) + inc
    for d, st in zip(dirs, states):
        d[4][...] = st


def _gla_masks():
    r = np.arange(GLA_CHUNK)
    lower = (r[None, :] <= r[:, None]).astype(np.float32)
    tri1 = np.stack([lower, lower.T])
    n_chunks = TOK_BLOCK // GLA_CHUNK
    tri = np.stack([np.kron(np.eye(n_chunks, dtype=np.float32), t) for t in tri1])
    tril = np.tile(tri1, (1, 1, GLA_HEADS))
    klane = np.arange(GLA_KPAD)
    vlane = np.arange(GLA_WIDTH)
    hm = np.stack([np.broadcast_to((klane // GLA_DK) == h, (GLA_CHUNK, GLA_KPAD)) for h in range(GLA_HEADS)])
    vm = np.stack([np.broadcast_to((vlane // GLA_DV) == h, (GLA_CHUNK, GLA_WIDTH)) for h in range(GLA_HEADS)])
    bd = ((vlane[:, None] // GLA_DV) == (klane[None, :] // GLA_DK)).astype(np.float32)
    return (jnp.asarray(tri, BF16), jnp.asarray(tril, F32), jnp.asarray(hm, BF16), jnp.asarray(vm, BF16),
            jnp.asarray(bd))


def _scan_block(d, j, n_ctx, n_all):
    bwd = jnp.where(j < n_ctx, n_ctx - 1 - j, n_all + n_ctx - 1 - j)
    return jnp.where(d == 0, j, bwd)


def _gla(gqk, gv, gla):
    bsz, tt, _ = gv.shape
    nb = tt // TOK_BLOCK
    n_ctx = CTX_LEN // TOK_BLOCK
    tri, tril, hm, vm, bd = _gla_masks()
    const2 = lambda b, j: (0, 0)
    const3 = lambda b, j: (0, 0, 0)
    fwd = lambda b, j: (b, j, 0)
    bwd = lambda b, j: (b, _scan_block(1, j, n_ctx, nb), 0)
    bwd_gate = lambda b, j: (b, _scan_block(1, j, n_ctx, nb), 1)
    tok = lambda w, imap: pl.BlockSpec((GLA_ROWS, TOK_BLOCK, w), imap)
    return pl.pallas_call(
        _gla_kernel,
        grid=(bsz // GLA_ROWS, nb),
        in_specs=[
            tok(2 * GLA_KPAD, fwd),
            tok(GLA_WIDTH, fwd),
            tok(GLA_KPAD, fwd),
            tok(2 * GLA_KPAD, bwd),
            tok(GLA_WIDTH, bwd),
            tok(GLA_KPAD, bwd_gate),
            pl.BlockSpec((2, TOK_BLOCK, TOK_BLOCK), const3),
            pl.BlockSpec((2, GLA_CHUNK, GLA_HEADS * GLA_CHUNK), const3),
            pl.BlockSpec((GLA_HEADS, GLA_CHUNK, GLA_KPAD), const3),
            pl.BlockSpec((GLA_HEADS, GLA_CHUNK, GLA_WIDTH), const3),
            pl.BlockSpec((GLA_WIDTH, GLA_KPAD), const2),
        ],
        out_specs=[tok(GLA_WIDTH, fwd), tok(GLA_WIDTH, bwd)],
        out_shape=[jax.ShapeDtypeStruct((bsz, tt, GLA_WIDTH), F32)] * 2,
        scratch_shapes=[pltpu.VMEM((GLA_ROWS, GLA_WIDTH, GLA_KPAD), F32)] * 2,
        compiler_params=pltpu.CompilerParams(
            dimension_semantics=("arbitrary", "arbitrary"), vmem_limit_bytes=VMEM_LIMIT),
        name="gla",
    )(gqk, gv, gla, gqk, gv, gla, tri, tril, hm, vm, bd)


def _swa_kernel(sink_ref, mask_ref, att_ref, o_ref, *, tt, first_block):
    j = pl.program_id(1) + first_block
    n_sub = TOK_BLOCK // ATT_BLOCK
    n_ctx_blocks = CTX_LEN // TOK_BLOCK
    last_q = tt // ATT_BLOCK - 1
    lane = lax.broadcasted_iota(jnp.int32, (ATT_BLOCK, 2 * ATT_HD), 1)
    left = lane < ATT_HD
    row2 = lax.broadcasted_iota(jnp.int32, (2 * ATT_BLOCK, 1), 0)
    units = [(r, sub, p) for r in range(SWA_ROWS) for sub in range(n_sub) for p in range(ATT_HEADS // 2)]

    def run(windowed):
        def score(r, sub, p):
            q0 = pl.multiple_of(j * TOK_BLOCK + sub * ATT_BLOCK, ATT_BLOCK)
            qp = att_ref[r, pl.ds(q0, ATT_BLOCK), p * 128:(p + 1) * 128]
            zero = jnp.zeros_like(qp)
            q2 = jnp.concatenate([jnp.where(left, qp, zero), jnp.where(left, zero, qp)], axis=0)
            kcol = ATT_WIDTH + p * 128
            s_c = _dot_nt(q2, att_ref[r, 0:CTX_LEN, kcol:kcol + 128])
            if not windowed:
                return s_c, None, None
            start = pl.multiple_of(jnp.minimum(q0 - ATT_BLOCK, tt - 3 * ATT_BLOCK), ATT_BLOCK)
            s_w = jnp.where(keeps[sub], _dot_nt(q2, att_ref[r, pl.ds(start, 3 * ATT_BLOCK), kcol:kcol + 128]),
                            NEG_INF)
            return s_c, s_w, start

        def softmax(p, s_c, s_w, start):
            sink_col = jnp.where(row2 < ATT_BLOCK, sink_ref[2 * p], sink_ref[2 * p + 1]) * LOG2E
            m = jnp.maximum(jnp.max(s_c, axis=-1, keepdims=True), sink_col)
            if windowed:
                m = jnp.maximum(m, jnp.max(s_w, axis=-1, keepdims=True))
            e_c = jnp.exp2(s_c - m)
            den = jnp.sum(e_c, axis=-1, keepdims=True) + jnp.exp2(sink_col - m)
            e_w = None
            if windowed:
                e_w = jnp.exp2(s_w - m)
                den = den + jnp.sum(e_w, axis=-1, keepdims=True)
                e_w = e_w.astype(BF16)
            return e_c.astype(BF16), e_w, den, start

        def values(r, sub, p, e_c, e_w, den, start):
            vcol = 2 * ATT_WIDTH + p * 128
            pv = _dot(e_c, att_ref[r, 0:CTX_LEN, vcol:vcol + 128])
            if windowed:
                pv = pv + _dot(e_w, att_ref[r, pl.ds(start, 3 * ATT_BLOCK), vcol:vcol + 128])
            o2 = pv / den
            o = jnp.where(left, o2[0:ATT_BLOCK], o2[ATT_BLOCK:2 * ATT_BLOCK])
            o_ref[r, sub * ATT_BLOCK:(sub + 1) * ATT_BLOCK, p * 128:(p + 1) * 128] = o.astype(BF16)

        keeps = []
        if windowed:
            for sub in range(n_sub):
                qi = j * n_sub + sub
                kind = jnp.where(qi == CTX_LEN // ATT_BLOCK, 0, jnp.where(qi == last_q, 2, 1))
                keeps.append(mask_ref[kind] > 0)

        n = len(units)
        sc, pr = {}, {}
        for t in range(n + 2):
            if t < n:
                sc[t] = score(*units[t])
            if 0 <= t - 1 < n:
                pr[t - 1] = softmax(units[t - 1][2], *sc.pop(t - 1))
            if 0 <= t - 2 < n:
                values(*units[t - 2], *pr.pop(t - 2))

    @pl.when(j < n_ctx_blocks)
    def _():
        run(False)

    @pl.when(j >= n_ctx_blocks)
    def _():
        run(True)


def _swa_masks():
    r = np.arange(2 * ATT_BLOCK)[:, None] % ATT_BLOCK
    c = np.arange(3 * ATT_BLOCK)[None, :]
    near = lambda delta: np.abs(r - c + delta) <= WINDOW
    first = near(ATT_BLOCK) & (c >= ATT_BLOCK)
    return jnp.asarray(np.stack([first, near(ATT_BLOCK), near(2 * ATT_BLOCK)]), F32)


def _swa(att, sink, final):
    bsz, tt, _ = att.shape
    nb = tt // TOK_BLOCK
    first = CTX_LEN // TOK_BLOCK if final else 0
    return pl.pallas_call(
        functools.partial(_swa_kernel, tt=tt, first_block=first),
        grid=(bsz // SWA_ROWS, nb - first),
        in_specs=[
            pl.BlockSpec(memory_space=pltpu.SMEM),
            pl.BlockSpec((3, 2 * ATT_BLOCK, 3 * ATT_BLOCK), lambda b, j: (0, 0, 0)),
            pl.BlockSpec((SWA_ROWS, tt, 3 * ATT_WIDTH), lambda b, j: (b, 0, 0)),
        ],
        out_specs=pl.BlockSpec((SWA_ROWS, TOK_BLOCK, ATT_WIDTH), lambda b, j: (b, j + first, 0)),
        out_shape=jax.ShapeDtypeStruct((bsz, tt, ATT_WIDTH), BF16),
        compiler_params=pltpu.CompilerParams(
            dimension_semantics=("arbitrary", "arbitrary"), vmem_limit_bytes=VMEM_LIMIT),
        name="swa",
    )(sink, _swa_masks(), att)


def _s5_prep_kernel(lr_ref, li_ref, ls_ref, br_ref, bi_ref, cr_ref, ci_ref,
                    a16_ref, mre_ref, mim_ref, rre_ref, rim_ref, k_ref):
    rows = S5_LC * S5_GROUP
    rowg = lax.broadcasted_iota(jnp.int32, (rows, S5_NS), 0) >> 4
    colg = lax.broadcasted_iota(jnp.int32, (rows, S5_NS), 1) >> 6
    same_group = rowg == colg
    taps_t = []
    for d in range(2):
        lr = jnp.minimum(lr_ref[d], -1e-4)
        li = li_ref[d]
        dt = jnp.exp(ls_ref[d])
        mag = jnp.exp(lr * dt)
        ar = mag * jnp.cos(li * dt)
        ai = mag * jnp.sin(li * dt)
        den = lr * lr + li * li
        fr = ((ar - 1.0) * lr + ai * li) / den
        fi = (ai * lr - (ar - 1.0) * li) / den
        br = br_ref[d]
        bi = bi_ref[d]
        bbr = fr * br - fi * bi
        bbi = fr * bi + fi * br
        cr = cr_ref[d]
        ci = ci_ref[d]
        pr = jnp.ones_like(ar)
        pi = jnp.zeros_like(ar)
        cp_re, cp_im = [], []
        for tau in range(S5_LC + 1):
            cpr = cr * pr - ci * pi
            cpi = cr * pi + ci * pr
            if tau < S5_LC:
                mre_ref[d, tau] = pr * bbr - pi * bbi
                mim_ref[d, tau] = pr * bbi + pi * bbr
                cp_re.append(cpr)
                cp_im.append(cpi)
            if tau >= 1:
                rre_ref[d, tau - 1] = cpr
                rim_ref[d, tau - 1] = -cpi
            if tau == S5_LC:
                a16_ref[d, 0:1, :] = pr
                a16_ref[d, 1:2, :] = pi
            pr, pi = pr * ar - pi * ai, pr * ai + pi * ar
        bd_re = jnp.where(same_group, jnp.concatenate([bbr] * S5_GROUPS, axis=0), 0.0)
        bd_im = jnp.where(same_group, jnp.concatenate([bbi] * S5_GROUPS, axis=0), 0.0)
        if d == 1:
            cp_re, cp_im = cp_re[::-1], cp_im[::-1]
        hp = lax.Precision.HIGHEST
        dims = (((1,), (1,)), ((), ()))
        taps_t.append(
            lax.dot_general(bd_re, jnp.concatenate(cp_re, axis=0), dims, precision=hp, preferred_element_type=F32)
            - lax.dot_general(bd_im, jnp.concatenate(cp_im, axis=0), dims, precision=hp,
                              preferred_element_type=F32))
    kf, kb_rev = taps_t
    tap0 = lax.broadcasted_iota(jnp.int32, kf.shape, 1) < S5_GROUP
    k_ref[0] = jnp.where(tap0, kf + pltpu.roll(kb_rev, S5_GROUP, 1), kf)
    k_ref[1] = kb_rev


def _s5_tables(fwd, bwd):
    n_layers = fwd[0].shape[0]
    both = lambda i: jnp.stack([fwd[i], bwd[i]], axis=1)
    lam_re = both(0).reshape(n_layers, 2, 1, S5_NS)
    lam_im = both(1).reshape(n_layers, 2, 1, S5_NS)
    log_step = jnp.repeat(both(2), S5_STATE, axis=-1).reshape(n_layers, 2, 1, S5_NS)
    b_hn = lambda t: t.transpose(0, 1, 4, 2, 3).reshape(n_layers, 2, S5_GROUP, S5_NS)
    c_hn = lambda t: t.transpose(0, 1, 3, 2, 4).reshape(n_layers, 2, S5_GROUP, S5_NS)
    vec = pl.BlockSpec((None, 2, 1, S5_NS), lambda l: (l, 0, 0, 0))
    mat = pl.BlockSpec((None, 2, S5_GROUP, S5_NS), lambda l: (l, 0, 0, 0))
    tab = pl.BlockSpec((None, 2, S5_LC, S5_GROUP, S5_NS), lambda l: (l, 0, 0, 0, 0))
    taps = S5_LC * S5_GROUP
    a16, mre, mim, rre, rim, k = pl.pallas_call(
        _s5_prep_kernel,
        grid=(n_layers,),
        in_specs=[vec, vec, vec, mat, mat, mat, mat],
        out_specs=[
            pl.BlockSpec((None, 2, 2, S5_NS), lambda l: (l, 0, 0, 0)),
            tab, tab, tab, tab,
            pl.BlockSpec((None, 2, S5_WIDTH, taps), lambda l: (l, 0, 0, 0)),
        ],
        out_shape=[
            jax.ShapeDtypeStruct((n_layers, 2, 2, S5_NS), F32),
            jax.ShapeDtypeStruct((n_layers, 2, S5_LC, S5_GROUP, S5_NS), F32),
            jax.ShapeDtypeStruct((n_layers, 2, S5_LC, S5_GROUP, S5_NS), F32),
            jax.ShapeDtypeStruct((n_layers, 2, S5_LC, S5_GROUP, S5_NS), F32),
            jax.ShapeDtypeStruct((n_layers, 2, S5_LC, S5_GROUP, S5_NS), F32),
            jax.ShapeDtypeStruct((n_layers, 2, S5_WIDTH, taps), F32),
        ],
        compiler_params=pltpu.CompilerParams(dimension_semantics=("arbitrary",), vmem_limit_bytes=VMEM_LIMIT),
        name="s5_prep",
    )(lam_re, lam_im, log_step, b_hn(both(3)), b_hn(both(4)), c_hn(both(5)), c_hn(both(6)))

    n_pairs = S5_GROUPS // 2
    lane_group = np.arange(128) // S5_STATE

    def pair_rows(t, flip):
        t = t[:, ::-1] if flip else t
        t = t.reshape(n_layers, S5_LC, S5_GROUP, n_pairs, 128).transpose(0, 3, 1, 2, 4)
        own = jnp.asarray(lane_group[None, :] == np.arange(2)[:, None])
        t = jnp.where(own[None, None, :, None, None, :], t[:, :, None], 0.0)
        return t.reshape(n_layers, n_pairs, 2 * taps, 128)

    m_f = jnp.concatenate([pair_rows(mre[:, 0], True), pair_rows(mim[:, 0], True)], axis=-1).astype(BF16)
    m_b = jnp.concatenate([pair_rows(mre[:, 1], False), pair_rows(mim[:, 1], False)], axis=-1).astype(BF16)

    r_t = jnp.concatenate([pair_rows(rre[:, 0], False), pair_rows(rim[:, 0], False),
                           pair_rows(rre[:, 1], True), pair_rows(rim[:, 1], True)], axis=-1).astype(BF16)

    zeros = jnp.zeros((n_layers, S5_WIDTH, taps), F32)
    f2 = jnp.concatenate([zeros, k[:, 0]], axis=-1)
    b2 = jnp.concatenate([k[:, 1], zeros], axis=-1)
    t_f = jnp.stack([f2[..., taps - S5_GROUP * st:2 * taps - S5_GROUP * st] for st in range(S5_LC)], axis=2)
    t_b = jnp.stack([b2[..., S5_GROUP * (S5_LC - 1 - st):S5_GROUP * (S5_LC - 1 - st) + taps]
                     for st in range(S5_LC)], axis=2)
    later = jnp.asarray(np.arange(taps)[None, :] // S5_GROUP >= np.arange(S5_LC)[:, None])
    toe = jnp.where(later[None, None], t_f, t_b)
    toe = toe.reshape(n_layers, S5_GROUPS, S5_GROUP, S5_LC, taps).transpose(0, 1, 3, 2, 4)
    toe = toe.reshape(n_layers, S5_GROUPS, taps, taps).astype(BF16)
    return a16, m_f, m_b, r_t, toe


def _lane_block_transpose(cols):
    lane = lax.broadcasted_iota(jnp.int32, cols[0].shape, 1)
    out = [None] * 32
    for ah in range(2):
        for bh in range(2):
            v = [cols[(ah * 8 + al) * 2 + bh] for al in range(8)]
            for kbit in range(3):
                width = 16 << kbit
                low = ((lane >> (4 + kbit)) & 1) == 0
                nxt = list(v)
                for i in range(8):
                    if i & (1 << kbit):
                        continue
                    lo_v, hi_v = v[i], v[i | (1 << kbit)]
                    nxt[i] = jnp.where(low, lo_v, pltpu.roll(hi_v, width, 1))
                    nxt[i | (1 << kbit)] = jnp.where(low, pltpu.roll(lo_v, 128 - width, 1), hi_v)
                v = nxt
            for bl in range(8):
                out[(bh * 8 + bl) * 2 + ah] = v[bl]
    return out


def _s5_increments(ush, m_ref, d_re, d_im):
    for p in range(S5_GROUPS // 2):
        dp = _dot(ush[:, p * 512:(p + 1) * 512], m_ref[p])
        d_re[p] = dp[:, 0:128]
        d_im[p] = dp[:, 128:256]


def _s5_recurrence(a16_ref, d_re, d_im, x_re, x_im, sr_ref, si_ref, order, bsz):
    ar = a16_ref[0:1, :]
    ai = a16_ref[1:2, :]
    sr = sr_ref[...]
    si = si_ref[...]
    n_slabs = S5_NS // 128
    gather = lambda ref, rows: jnp.concatenate([ref[p, rows, :] for p in range(n_slabs)], axis=1)
    for c in order:
        rows = pl.ds(c, bsz, stride=CH_BLOCK)
        for p in range(n_slabs):
            x_re[p, rows, :] = sr[:, p * 128:(p + 1) * 128]
            x_im[p, rows, :] = si[:, p * 128:(p + 1) * 128]
        sr, si = ar * sr - ai * si + gather(d_re, rows), ar * si + ai * sr + gather(d_im, rows)
    sr_ref[...] = sr
    si_ref[...] = si


def _s5_fwd_kernel(uc_ref, a16_ref, m_ref, ush_ref, xin_ref, d_re, d_im, x_re, x_im, sr_ref, si_ref, *, bsz):
    @pl.when(pl.program_id(0) == 0)
    def _():
        sr_ref[...] = jnp.zeros_like(sr_ref)
        si_ref[...] = jnp.zeros_like(si_ref)

    rows = bsz * CH_BLOCK
    ub = uc_ref[...].reshape(rows, S5_CW).astype(BF16)
    packed = pltpu.bitcast(ub, jnp.uint32)
    cols = _lane_block_transpose([packed[:, v * 128:(v + 1) * 128] for v in range(32)])
    ush = pltpu.bitcast(jnp.concatenate(cols, axis=1), BF16)
    ush_ref[...] = ush.reshape(bsz, CH_BLOCK, S5_CW)
    _s5_increments(ush, m_ref, d_re, d_im)
    _s5_recurrence(a16_ref, d_re, d_im, x_re, x_im, sr_ref, si_ref, range(CH_BLOCK), bsz)
    for p in range(S5_NS // 128):
        xin_ref[:, :, p * 128:(p + 1) * 128] = x_re[p].astype(BF16).reshape(bsz, CH_BLOCK, 128)
        xin_ref[:, :, S5_NS + p * 128:S5_NS + (p + 1) * 128] = x_im[p].astype(BF16).reshape(bsz, CH_BLOCK, 128)


def _s5_bwd_kernel(ush_ref, xf_ref, a16_ref, m_ref, toe_ref, r_ref, y_ref, d_re, d_im, x_re, x_im, sr_ref, si_ref,
                   *, bsz):
    @pl.when(pl.program_id(0) == 0)
    def _():
        sr_ref[...] = jnp.zeros_like(sr_ref)
        si_ref[...] = jnp.zeros_like(si_ref)

    rows = bsz * CH_BLOCK
    ush = ush_ref[...].reshape(rows, S5_CW)
    _s5_increments(ush, m_ref, d_re, d_im)
    _s5_recurrence(a16_ref, d_re, d_im, x_re, x_im, sr_ref, si_ref, reversed(range(CH_BLOCK)), bsz)
    xf = xf_ref[...].reshape(rows, 2 * S5_NS)
    ycols = []
    for p in range(S5_GROUPS // 2):
        lanes = slice(p * 128, (p + 1) * 128)
        xcat = jnp.concatenate([xf[:, lanes], xf[:, S5_NS + p * 128:S5_NS + (p + 1) * 128],
                                x_re[p].astype(BF16), x_im[p].astype(BF16)], axis=1)
        carry = _dot_nt(xcat, r_ref[p])
        for g2 in range(2):
            g = 2 * p + g2
            yg = carry[:, g2 * 256:(g2 + 1) * 256] + _dot(ush[:, g * 256:(g + 1) * 256], toe_ref[g])
            ycols += [yg[:, 0:128], yg[:, 128:256]]
    ycols = _lane_block_transpose(ycols)
    y_ref[...] = jnp.concatenate(ycols, axis=1).reshape(bsz, CH_BLOCK, S5_CW)


def _s5(u_c, tables, layer):
    a16, m_f, m_b, r_t, toe = tables
    bsz, n_rows, _ = u_c.shape
    nb = n_rows // CH_BLOCK
    n_ctx = CTX_LEN // TOK_BLOCK
    rows = bsz * CH_BLOCK
    blk3 = lambda w: (bsz, CH_BLOCK, w)
    scratch = [pltpu.VMEM((S5_NS // 128, rows, 128), F32)] * 4 + [pltpu.VMEM((bsz, S5_NS), F32)] * 2
    params = pltpu.CompilerParams(dimension_semantics=("arbitrary",), vmem_limit_bytes=VMEM_LIMIT)
    ush, xin_f = pl.pallas_call(
        functools.partial(_s5_fwd_kernel, bsz=bsz),
        grid=(nb,),
        in_specs=[
            pl.BlockSpec(blk3(S5_CW), lambda j: (0, j, 0)),
            pl.BlockSpec((None, None, 2, S5_NS), lambda j: (layer, 0, 0, 0)),
            pl.BlockSpec((None, S5_GROUPS // 2, 512, 256), lambda j: (layer, 0, 0, 0)),
        ],
        out_specs=[
            pl.BlockSpec(blk3(S5_CW), lambda j: (0, j, 0)),
            pl.BlockSpec(blk3(2 * S5_NS), lambda j: (0, j, 0)),
        ],
        out_shape=[
            jax.ShapeDtypeStruct((bsz, n_rows, S5_CW), BF16),
            jax.ShapeDtypeStruct((bsz, n_rows, 2 * S5_NS), BF16),
        ],
        scratch_shapes=scratch,
        compiler_params=params,
        name="s5_fwd",
    )(u_c, a16, m_f)
    back = lambda j: (0, _scan_block(1, j, n_ctx, nb), 0)
    return pl.pallas_call(
        functools.partial(_s5_bwd_kernel, bsz=bsz),
        grid=(nb,),
        in_specs=[
            pl.BlockSpec(blk3(S5_CW), back),
            pl.BlockSpec(blk3(2 * S5_NS), back),
            pl.BlockSpec((None, None, 2, S5_NS), lambda j: (layer, 1, 0, 0)),
            pl.BlockSpec((None, S5_GROUPS // 2, 512, 256), lambda j: (layer, 0, 0, 0)),
            pl.BlockSpec((None, S5_GROUPS, 256, 256), lambda j: (layer, 0, 0, 0)),
            pl.BlockSpec((None, S5_GROUPS // 2, 512, 512), lambda j: (layer, 0, 0, 0)),
        ],
        out_specs=pl.BlockSpec(blk3(S5_CW), back),
        out_shape=jax.ShapeDtypeStruct((bsz, n_rows, S5_CW), F32),
        scratch_shapes=scratch,
        compiler_params=params,
        name="s5_bwd",
    )(ush, xin_f, a16, m_b, toe, r_t)


def _out_kernel(xc_ref, x_ref, mod_ref, gof_ref, gob_ref, g_ref, att_ref, y_ref, uc_ref, gnw_ref, hm_ref, d_ref,
                gluw_ref, glub_ref, wo_ref, n2_ref, w1_ref, w2_ref, fn_ref, o_ref, ys_ref,
                *, first_block, final, split):
    j = pl.program_id(1) + first_block

    def mixed(r):
        m = mod_ref[r]
        mrow = jnp.where(j > 0, m[1:2, :], m[0:1, :])
        g1 = mrow[:, 2 * D_MODEL:3 * D_MODEL]
        sh2 = mrow[:, 3 * D_MODEL:4 * D_MODEL]
        sc2 = mrow[:, 4 * D_MODEL:5 * D_MODEL]
        g2 = mrow[:, 5 * D_MODEL:6 * D_MODEL]

        o = gof_ref[r] + gob_ref[r]
        ms = _dot((o * o).astype(BF16), hm_ref[...]) * (1.0 / GLA_DV)
        a = o * lax.rsqrt(ms + EPS) * gnw_ref[...] * _silu(g_ref[r])

        yc = y_ref[r] + d_ref[...] * uc_ref[r]
        for st in range(S5_LC):
            for half in range(S5_WIDTH // 128):
                lane0 = st * S5_WIDTH + half * 128
                ys_ref[r, half, pl.ds(st, CH_BLOCK, stride=S5_LC), :] = yc[:, lane0:lane0 + 128]
        yy = jnp.concatenate([ys_ref[r, half] for half in range(S5_WIDTH // 128)], axis=1)
        ge = 0.5 * yy * (1.0 + jnp.tanh(math.sqrt(2.0 / math.pi) * (yy + 0.044715 * (yy * yy * yy))))
        z = _dot(ge.astype(BF16), gluw_ref[...]) + glub_ref[...]
        s = z[:, 0:S5_WIDTH] * _sigmoid(z[:, S5_WIDTH:2 * S5_WIDTH])

        proj = _dot(jnp.concatenate([a.astype(BF16), att_ref[r], s.astype(BF16)], axis=1), wo_ref[...])
        x_in = jnp.where(j == 0, xc_ref[r], x_ref[r]) if split else x_ref[r]
        x1 = x_in + g1 * proj
        y2 = x1 * lax.rsqrt(jnp.mean(x1 * x1, axis=-1, keepdims=True) + EPS) * n2_ref[...]
        return x1, (y2 * (1.0 + sc2) + sh2).astype(BF16), g2

    def finish(r, x1, g2, mlp):
        x2 = x1 + g2 * mlp
        if final:
            x2 = x2 * lax.rsqrt(jnp.mean(x2 * x2, axis=-1, keepdims=True) + EPS) * fn_ref[...]
        o_ref[r] = x2

    cur = mixed(0)
    for r in range(OUT_ROWS):
        x1, h2, g2 = cur
        hid = jnp.maximum(_dot(h2, w1_ref[...]), 0.0)
        if r + 1 < OUT_ROWS:
            cur = mixed(r + 1)
        mlp = _dot((hid * hid).astype(BF16), w2_ref[...])
        finish(r, x1, g2, mlp)


def _outproj(x_ctx, x_rest, lat_skip, modsel, gla_f, gla_b, gg, att_o, y_c, u_c, gnw, hm, s5d, gluw, glub, wo, n2w,
             w1, w2, fnw, layer, final):
    bsz, tt, _ = gg.shape
    nb = tt // TOK_BLOCK
    first = CTX_LEN // TOK_BLOCK if final else 0
    nsteps = nb - first
    const2 = lambda b, j: (0, 0)
    lyr3 = lambda b, j: (layer, 0, 0)
    single = pl.Buffered(1)
    return pl.pallas_call(
        functools.partial(_out_kernel, first_block=first, final=final, split=lat_skip > 0),
        grid=(bsz // OUT_ROWS, nsteps),
        in_specs=_stream_specs(lat_skip, first, rows=OUT_ROWS) + [
            pl.BlockSpec((OUT_ROWS, 2, 6 * D_MODEL), lambda b, j: (b, 0, 0)),
            pl.BlockSpec((OUT_ROWS, TOK_BLOCK, GLA_WIDTH), lambda b, j: (b, j + first, 0)),
            pl.BlockSpec((OUT_ROWS, TOK_BLOCK, GLA_WIDTH), lambda b, j: (b, j + first, 0)),
            pl.BlockSpec((OUT_ROWS, TOK_BLOCK, GLA_WIDTH), lambda b, j: (b, j + first, 0)),
            pl.BlockSpec((OUT_ROWS, TOK_BLOCK, ATT_WIDTH), lambda b, j: (b, j + first, 0)),
            pl.BlockSpec((OUT_ROWS, CH_BLOCK, S5_CW), lambda b, j: (b, j + first, 0)),
            pl.BlockSpec((OUT_ROWS, CH_BLOCK, S5_CW), lambda b, j: (b, j + first, 0)),
            pl.BlockSpec((1, GLA_WIDTH), const2),
            pl.BlockSpec((GLA_WIDTH, GLA_WIDTH), const2),
            pl.BlockSpec((1, S5_CW), const2),
            pl.BlockSpec((None, S5_WIDTH, 2 * S5_WIDTH), lyr3, pipeline_mode=single),
            pl.BlockSpec((1, 2 * S5_WIDTH), const2),
            pl.BlockSpec((None, D_MODEL, D_MODEL), lyr3, pipeline_mode=single),
            pl.BlockSpec((1, D_MODEL), const2),
            pl.BlockSpec((None, D_MODEL, D_FF), lyr3, pipeline_mode=single),
            pl.BlockSpec((None, D_FF, D_MODEL), lyr3, pipeline_mode=single),
            pl.BlockSpec((1, D_MODEL), const2),
        ],
        out_specs=pl.BlockSpec((OUT_ROWS, TOK_BLOCK, D_MODEL), lambda b, j: (b, j, 0)),
        out_shape=jax.ShapeDtypeStruct((bsz, nsteps * TOK_BLOCK, D_MODEL), F32),
        scratch_shapes=[pltpu.VMEM((OUT_ROWS, S5_WIDTH // 128, TOK_BLOCK, 128), F32)],
        compiler_params=pltpu.CompilerParams(
            dimension_semantics=("arbitrary", "arbitrary"), vmem_limit_bytes=VMEM_LIMIT),
        name="outproj_mlp",
    )(x_ctx, x_rest, modsel, gla_f, gla_b, gg, att_o, y_c, u_c, gnw, hm, s5d, gluw, glub, wo, n2w, w1, w2, fnw)


def _rope_tables(tt):
    n_lat = tt - CTX_LEN
    rows = n_lat // GRID_W
    row = jnp.repeat(jnp.arange(rows, dtype=F32), GRID_W)
    col = jnp.tile(jnp.arange(GRID_W, dtype=F32), rows)
    n_freq = ATT_HD // 4
    inv_freq = ROPE_BASE ** (-jnp.arange(n_freq, dtype=F32) / n_freq)
    ang_r = row[:, None] * inv_freq
    ang_c = col[:, None] * inv_freq
    ang = jnp.concatenate([ang_r, ang_r, ang_c, ang_c], axis=-1)
    cos = jnp.concatenate([jnp.ones((CTX_LEN, ATT_HD), F32), jnp.cos(ang)], axis=0)
    sin = jnp.concatenate([jnp.zeros((CTX_LEN, ATT_HD), F32), jnp.sin(ang)], axis=0)
    up_quarter = (np.arange(ATT_HD) // 16) % 2 == 0
    sa = jnp.where(up_quarter, -sin, 0.0)
    sb = jnp.where(up_quarter, 0.0, sin)
    two = lambda t: jnp.concatenate([t, t], axis=-1)
    return two(cos), two(sa), two(sb)


def _pad_cols(w, width):
    return jnp.pad(w, ((0, 0), (0, 0), (0, width - w.shape[-1])))


def _layout_w_in(w_in):
    offs = np.cumsum([0, GLA_KW, GLA_KW, GLA_WIDTH, GLA_WIDTH, GLA_RANK, GLA_RANK, ATT_WIDTH, ATT_KVW, ATT_KVW,
                      S5_WIDTH])
    q, k, v, g, zf, zb, aq, ak, av, u = [w_in[:, :, offs[i]:offs[i + 1]] for i in range(10)]
    z = jnp.concatenate([zf, zb], axis=-1)
    cols = [v, g, _pad_cols(q, GLA_KPAD), _pad_cols(k, GLA_KPAD), _pad_cols(z, 128), aq, ak, av, u]
    return jnp.concatenate(cols, axis=-1).astype(BF16)


def kernel(x, c, ctx, c_ctx, w_mod, b_mod, norm1_w, norm2_w, w_in, gla_wa_f, gla_ba_f, gla_wa_b, gla_ba_b,
           gla_norm_w, attn_sink, s5_lam_re_f, s5_lam_im_f, s5_log_step_f, s5_b_re_f, s5_b_im_f, s5_c_re_f,
           s5_c_im_f, s5_lam_re_b, s5_lam_im_b, s5_log_step_b, s5_b_re_b, s5_b_im_b, s5_c_re_b, s5_c_im_b,
           s5_d, glu_w, glu_b, w_out, mlp_w1, mlp_w2, final_norm_w):
    bsz, seq, _ = x.shape
    n_layers = w_mod.shape[0]
    tt = CTX_LEN + seq
    assert all(bsz % rows == 0 for rows in (8, IN_ROWS, GLA_ROWS, SWA_ROWS, OUT_ROWS))
    assert seq % TOK_BLOCK == 0 and seq >= 2 * TOK_BLOCK and ctx.shape[1] == CTX_LEN

    mod_rows = -(-(bsz + 1) // 8) * 8
    cvec = jnp.zeros((mod_rows, D_MODEL), F32).at[:bsz].set(c).at[bsz].set(c_ctx)
    mod = _modulation(cvec, w_mod, b_mod)
    mod_ctx = jnp.broadcast_to(mod[:, bsz][:, None], (n_layers, bsz, 6 * D_MODEL))
    modsel = jnp.stack([mod_ctx, mod[:, :bsz]], axis=2)

    w_in_p = _layout_w_in(w_in)
    wa_cat = jnp.zeros((n_layers, 128, 2 * GLA_KPAD), F32)
    wa_cat = wa_cat.at[:, 0:GLA_RANK, 0:GLA_KW].set(gla_wa_f)
    wa_cat = wa_cat.at[:, GLA_RANK:2 * GLA_RANK, GLA_KPAD:GLA_KPAD + GLA_KW].set(gla_wa_b).astype(BF16)
    ba_cat = jnp.zeros((n_layers, 1, 2 * GLA_KPAD), F32)
    ba_cat = ba_cat.at[:, 0, 0:GLA_KW].set(gla_ba_f).at[:, 0, GLA_KPAD:GLA_KPAD + GLA_KW].set(gla_ba_b)
    qscale = jnp.ones((1, 2 * GLA_KPAD), F32).at[:, 0:GLA_KW].set(GLA_DK ** -0.5)
    cos_t, sa_t, sb_t = _rope_tables(tt)
    gnw = jnp.tile(gla_norm_w, (1, GLA_HEADS))[:, None, :]
    head = np.arange(GLA_WIDTH) // GLA_DV
    hm = jnp.asarray(head[:, None] == head[None, :], BF16)
    wo_b = w_out.astype(BF16)
    w1_b = mlp_w1.astype(BF16)
    w2_b = mlp_w2.astype(BF16)
    gluw_b = glu_w.astype(BF16)

    s5_tab = _s5_tables(
        (s5_lam_re_f, s5_lam_im_f, s5_log_step_f, s5_b_re_f, s5_b_im_f, s5_c_re_f, s5_c_im_f),
        (s5_lam_re_b, s5_lam_im_b, s5_log_step_b, s5_b_re_b, s5_b_im_b, s5_c_re_b, s5_c_im_b))
    s5_dt = jnp.tile(s5_d, (1, S5_LC))[:, None, :]

    stream = (ctx, x, CTX_LEN // TOK_BLOCK)
    for l in range(n_layers):
        final = l == n_layers - 1
        gv, gg, gqk, gla, att, u_c = _inproj(*stream, modsel[l], norm1_w[l][None], w_in_p, wa_cat, ba_cat[l],
                                             qscale, cos_t, sa_t, sb_t, l)
        gla_f, gla_b = _gla(gqk, gv, gla)
        att_o = _swa(att, attn_sink[l], final)
        y_c = _s5(u_c, s5_tab, l)
        xs = _outproj(*stream, modsel[l], gla_f, gla_b, gg, att_o, y_c, u_c,
                      gnw[l], hm, s5_dt[l], gluw_b, glu_b[l][None], wo_b, norm2_w[l][None], w1_b, w2_b,
                      final_norm_w[None], l, final)
        stream = (xs, xs, 0)
    return xs
```
